```python
import math
import jax
import jax.numpy as jnp
from jax import lax
import numpy as np

D_MODEL = 1024
BATCH = 4
SEQ = 4096
DEPTH = 1

NSA_HEADS = 8
NSA_KV_GROUPS = 2
NSA_HPG = NSA_HEADS // NSA_KV_GROUPS
HEAD_DIM = 64
CMP_BLOCK = 32
CMP_STRIDE = 16
CMP_HIDDEN = 128
SLC_BLOCK = 64
SLC_TOPN = 16
WINDOW = 512
Q_BLOCK = 128
GLA_HEADS = 4
GLA_DK = 64
GLA_DV = 128
GLA_GATE_RANK = 16
GLA_TAU = 16.0
GLA_CHUNK = 64
REL_BUCKETS = 32
REL_MAX_EXACT = REL_BUCKETS // 2
REL_MAX_DIST = 128
N_GROUPS = 4
EXPERTS_PER_GROUP = 8
N_EXPERTS = N_GROUPS * EXPERTS_PER_GROUP
TOPK_IN_GROUP = 2
EXPERT_FF = 256
EPS = 1e-6

NSA_WIDTH = NSA_HEADS * HEAD_DIM
KV_WIDTH = NSA_KV_GROUPS * HEAD_DIM
GLA_K_WIDTH = GLA_HEADS * GLA_DK
GLA_V_WIDTH = GLA_HEADS * GLA_DV
IN_SPLITS = (NSA_WIDTH, KV_WIDTH, KV_WIDTH, KV_WIDTH, KV_WIDTH, KV_WIDTH, KV_WIDTH,
             3 * NSA_HEADS, GLA_K_WIDTH, GLA_K_WIDTH, GLA_V_WIDTH, GLA_GATE_RANK,
             GLA_V_WIDTH, D_MODEL, D_MODEL)

kernel_name = "hybrid_nsa_gla_hmoe_block"


def rmsnorm(x, g):
    xf = x.astype(jnp.float32)
    y = xf * lax.rsqrt(jnp.mean(xf * xf, axis=-1, keepdims=True) + EPS)
    return (y * g.astype(jnp.float32)).astype(x.dtype)


def masked_softmax(s, mask):
    s = jnp.where(mask, s.astype(jnp.float32), -jnp.inf)
    m = jnp.max(s, axis=-1, keepdims=True)
    m = jnp.where(jnp.isfinite(m), m, 0.0)
    e = jnp.where(mask, jnp.exp(s - m), 0.0)
    return e / jnp.maximum(jnp.sum(e, axis=-1, keepdims=True), jnp.finfo(jnp.float32).tiny)


def t5_bucket(rel):
    n = jnp.maximum(rel, 0)
    nf = jnp.maximum(n, 1).astype(jnp.float32)
    large = REL_MAX_EXACT + (jnp.log(nf / REL_MAX_EXACT) / math.log(REL_MAX_DIST / REL_MAX_EXACT)
                             * (REL_BUCKETS - REL_MAX_EXACT)).astype(jnp.int32)
    return jnp.where(n < REL_MAX_EXACT, n, jnp.minimum(large, REL_BUCKETS - 1))


def compress(t, pe, w1, w2):
    B, G, S, dh = t.shape
    sub = t.reshape(B, G, S // CMP_STRIDE, CMP_STRIDE, dh)
    blocks = jnp.concatenate([sub[:, :, :-1], sub[:, :, 1:]], axis=3) + pe
    flat = blocks.reshape(B, G, blocks.shape[2], CMP_BLOCK * dh)
    return jax.nn.gelu(flat @ w1) @ w2


def block_overlap(n_cmp, n_slc):
    ci = np.arange(n_cmp)[:, None] * CMP_STRIDE
    sj = np.arange(n_slc)[None, :] * SLC_BLOCK
    return ((ci < sj + SLC_BLOCK) & (ci + CMP_BLOCK > sj)).astype(np.float32)


def selected_attention(qh, ks, vs, idx, tbl):
    B, G, HPG, S, dh = qh.shape
    n_top = idx.shape[-1]
    nqb = S // Q_BLOCK
    kb = ks.reshape(B, G, S // SLC_BLOCK, SLC_BLOCK, dh)
    vb = vs.reshape(B, G, S // SLC_BLOCK, SLC_BLOCK, dh)

    def one_group(q, kb_g, vb_g, idx_g, tpos, tbl_g):
        kg = kb_g[idx_g]
        vg = vb_g[idx_g]
        s = jnp.einsum('hqd,qnld->hqnl', q, kg)
        kpos = idx_g[..., None] * SLC_BLOCK + jnp.arange(SLC_BLOCK)
        rel = tpos[:, None, None] - kpos
        s = s + tbl_g[:, t5_bucket(rel)]
        tq = idx_g.shape[0]
        p = masked_softmax(s.reshape(HPG, tq, n_top * SLC_BLOCK), (rel >= 0).reshape(tq, n_top * SLC_BLOCK))
        return jnp.einsum('hqm,qmd->hqd', p.astype(vg.dtype), vg.reshape(tq, n_top * SLC_BLOCK, dh))

    per_g = jax.vmap(one_group, in_axes=(0, 0, 0, 0, None, 0))
    per_bg = jax.vmap(per_g, in_axes=(0, 0, 0, 0, None, None))
    q_blocks = qh.reshape(B, G, HPG, nqb, Q_BLOCK, dh).transpose(3, 0, 1, 2, 4, 5)
    idx_blocks = idx.reshape(B, G, nqb, Q_BLOCK, n_top).transpose(2, 0, 1, 3, 4)
    t_blocks = jnp.arange(S).reshape(nqb, Q_BLOCK)
    out = lax.map(lambda a: per_bg(a[0], kb, vb, a[1], a[2], tbl), (q_blocks, idx_blocks, t_blocks))
    return out.transpose(1, 2, 3, 0, 4, 5).reshape(B, G, HPG, S, dh)


def window_attention(qh, kw, vw, tbl):
    B, G, HPG, S, dh = qh.shape
    nqb = S // Q_BLOCK
    nprev = WINDOW // Q_BLOCK
    span = (nprev + 1) * Q_BLOCK

    def band(t):
        tp = jnp.pad(t, ((0, 0), (0, 0), (WINDOW, 0), (0, 0))).reshape(B, G, nqb + nprev, Q_BLOCK, dh)
        return jnp.concatenate([tp[:, :, j:j + nqb] for j in range(nprev + 1)], axis=3)

    kband, vband = band(kw), band(vw)
    qb = qh.reshape(B, G, HPG, nqb, Q_BLOCK, dh)
    s = jnp.einsum('bghnqd,bgnkd->bghnqk', qb, kband)
    ql = jnp.arange(Q_BLOCK)[:, None]
    kl = jnp.arange(span)[None, :]
    rel = ql + WINDOW - kl
    kabs = jnp.arange(nqb)[:, None, None] * Q_BLOCK - WINDOW + kl[None]
    mask = (rel >= 0) & (rel < WINDOW) & (kabs >= 0)
    s = s + tbl[:, :, t5_bucket(rel)][:, :, None]
    p = masked_softmax(s, mask)
    o = jnp.einsum('bghnqk,bgnkd->bghnqd', p.astype(vband.dtype), vband)
    return o.reshape(B, G, HPG, S, dh)


def nsa_mixer(q, kc, vc, ks, vs, kw, vw, gates, pe_k, ck1, ck2, pe_v, cv1, cv2, rel_bias):
    B, S, _ = q.shape
    G, HPG, dh = NSA_KV_GROUPS, NSA_HPG, HEAD_DIM
    qh = q.reshape(B, S, G, HPG, dh).transpose(0, 2, 3, 1, 4) * (dh ** -0.5)

    def heads_kv(t):
        return t.reshape(B, S, G, dh).transpose(0, 2, 1, 3)

    tbl = rel_bias.T.reshape(G, HPG, REL_BUCKETS)
    tpos = jnp.arange(S)

    kcb = compress(heads_kv(kc), pe_k, ck1, ck2)
    vcb = compress(heads_kv(vc), pe_v, cv1, cv2)
    n_cmp = kcb.shape[2]
    cend = jnp.arange(n_cmp) * CMP_STRIDE + CMP_BLOCK - 1
    rel_c = tpos[:, None] - cend[None, :]
    s_c = jnp.einsum('bghsd,bgcd->bghsc', qh, kcb) + tbl[:, :, t5_bucket(rel_c)]
    p_c = masked_softmax(s_c, rel_c >= 0)
    o_c = jnp.einsum('bghsc,bgcd->bghsd', p_c.astype(vcb.dtype), vcb)

    n_slc = S // SLC_BLOCK
    n_top = min(SLC_TOPN, n_slc)
    imp = jnp.einsum('bghsc,cj->bgsj', p_c, jnp.asarray(block_overlap(n_cmp, n_slc)))
    j = jnp.arange(n_slc)[None, :]
    tb = (tpos // SLC_BLOCK)[:, None]
    forced = (j == 0) | (j == tb) | (j == tb - 1)
    future = j * SLC_BLOCK > tpos[:, None]
    imp = jnp.where(forced, 1e30, jnp.where(future, -1e30, imp))
    _, idx = lax.top_k(imp, n_top)
    o_s = selected_attention(qh, heads_kv(ks), heads_kv(vs), idx, tbl)

    o_w = window_attention(qh, heads_kv(kw), heads_kv(vw), tbl)

    g = jax.nn.sigmoid(gates.astype(jnp.float32)).reshape(B, S, 3, G, HPG).transpose(2, 0, 3, 4, 1)[..., None]
    o = g[0] * o_c + g[1] * o_s + g[2] * o_w
    return o.transpose(0, 3, 1, 2, 4).reshape(B, S, NSA_WIDTH).astype(q.dtype)


def gla_mixer(q, k, v, a_lr, r, w_alpha, b_alpha, norm_g):
    B, S, _ = q.shape
    H, dk, dv, C = GLA_HEADS, GLA_DK, GLA_DV, GLA_CHUNK
    nc = S // C
    f32 = jnp.float32
    log_a = jax.nn.log_sigmoid((a_lr @ w_alpha + b_alpha).astype(f32)) / GLA_TAU

    def chunks(t, d):
        return t.astype(f32).reshape(B, nc, C, H, d).transpose(0, 3, 1, 2, 4)

    qc = chunks(q, dk) * (dk ** -0.5)
    kc = chunks(k, dk)
    vc = chunks(v, dv)
    bc = jnp.cumsum(chunks(log_a, dk), axis=3)
    b_last = bc[:, :, :, -1:]
    q_in = qc * jnp.exp(bc)
    k_in = kc * jnp.exp(-bc)
    causal = jnp.tril(jnp.ones((C, C), dtype=bool))
    attn = jnp.where(causal, jnp.einsum('bhncd,bhnjd->bhncj', q_in, k_in), 0.0)
    o_intra = jnp.einsum('bhncj,bhnjv->bhncv', attn, vc)
    chunk_state = jnp.einsum('bhncd,bhncv->bhndv', kc * jnp.exp(b_last - bc), vc)
    decay = jnp.exp(b_last[:, :, :, 0])

    def step(state, xs):
        dec, upd = xs
        return dec[..., None] * state + upd, state

    _, prev = lax.scan(step, jnp.zeros((B, H, dk, dv), f32),
                       (decay.transpose(2, 0, 1, 3), chunk_state.transpose(2, 0, 1, 3, 4)))
    prev = prev.transpose(1, 2, 0, 3, 4)
    o = o_intra + jnp.einsum('bhncd,bhndv->bhncv', q_in, prev)
    o = o.transpose(0, 2, 3, 1, 4).reshape(B, S, H, dv)
    o = o * lax.rsqrt(jnp.mean(o * o, axis=-1, keepdims=True) + EPS)
    o = o.reshape(B, S, H * dv) * norm_g.astype(f32) * jax.nn.silu(r.astype(f32))
    return o.astype(q.dtype)


def hier_moe(h, w_rg, b_rg, w_re, b_re, w_gate, w_up, w_down):
    B, S, D = h.shape
    T = B * S
    t = h.reshape(T, D)
    p_group = jax.nn.softmax((t @ w_rg + b_rg).astype(jnp.float32), axis=-1)
    g_sel = jnp.argmax(p_group, axis=-1)
    g_prob = jnp.max(p_group, axis=-1)
    e_logits = (t @ w_re + b_re).astype(jnp.float32).reshape(T, N_GROUPS, EXPERTS_PER_GROUP)
    e_sel = jnp.take_along_axis(e_logits, g_sel[:, None, None], axis=1)[:, 0]
    top_v, top_i = lax.top_k(e_sel, TOPK_IN_GROUP)
    top_w = jax.nn.softmax(top_v, axis=-1) * g_prob[:, None]
    w_local = jnp.einsum('tk,tke->te', top_w, jax.nn.one_hot(top_i, EXPERTS_PER_GROUP, dtype=jnp.float32))
    combine = jax.nn.one_hot(g_sel, N_GROUPS, dtype=jnp.float32)[:, :, None] * w_local[:, None, :]
    wg = w_gate.reshape(N_GROUPS, EXPERTS_PER_GROUP, D, EXPERT_FF)
    wu = w_up.reshape(N_GROUPS, EXPERTS_PER_GROUP, D, EXPERT_FF)
    wd = w_down.reshape(N_GROUPS, EXPERTS_PER_GROUP, EXPERT_FF, D)
    y = jnp.zeros((T, D), dtype=t.dtype)
    for gi in range(N_GROUPS):
        a = jnp.einsum('td,edf->tef', t, wg[gi])
        u = jnp.einsum('td,edf->tef', t, wu[gi])
        hid = jax.nn.silu(a) * u * combine[:, gi, :, None].astype(t.dtype)
        y = y + jnp.einsum('tef,efd->td', hid, wd[gi])
    return y.reshape(B, S, D)


def setup_inputs(seed: int = 0) -> dict:
    key = jax.random.key(seed)
    ks = jax.random.split(key, 26)
    f32 = jnp.float32
    L = DEPTH
    n_in = sum(IN_SPLITS)

    def nrm(k, shape, scale):
        return jax.random.normal(k, shape, f32) * scale

    return {
        "x": nrm(ks[0], (BATCH, SEQ, D_MODEL), 1.0),
        "g_mix": 1.0 + nrm(ks[1], (L, D_MODEL), 0.02),
        "w_in": nrm(ks[2], (L, D_MODEL, n_in), D_MODEL ** -0.5),
        "nsa_pe_k": nrm(ks[3], (L, CMP_BLOCK, HEAD_DIM), 0.1),
        "nsa_cmp_k_w1": nrm(ks[4], (L, CMP_BLOCK * HEAD_DIM, CMP_HIDDEN), (CMP_BLOCK * HEAD_DIM) ** -0.5),
        "nsa_cmp_k_w2": nrm(ks[5], (L, CMP_HIDDEN, HEAD_DIM), CMP_HIDDEN ** -0.5),
        "nsa_pe_v": nrm(ks[6], (L, CMP_BLOCK, HEAD_DIM), 0.1),
        "nsa_cmp_v_w1": nrm(ks[7], (L, CMP_BLOCK * HEAD_DIM, CMP_HIDDEN), (CMP_BLOCK * HEAD_DIM) ** -0.5),
        "nsa_cmp_v_w2": nrm(ks[8], (L, CMP_HIDDEN, HEAD_DIM), CMP_HIDDEN ** -0.5),
        "rel_bias": nrm(ks[9], (REL_BUCKETS, NSA_HEADS), 0.2),
        "gla_w_alpha": nrm(ks[10], (L, GLA_GATE_RANK, GLA_K_WIDTH), GLA_GATE_RANK ** -0.5),
        "gla_b_alpha": nrm(ks[11], (L, GLA_K_WIDTH), 0.01),
        "gla_norm_g": 1.0 + nrm(ks[12], (L, GLA_V_WIDTH), 0.02),
        "w_branch_a": nrm(ks[13], (L, NSA_WIDTH, D_MODEL), NSA_WIDTH ** -0.5),
        "w_branch_b": nrm(ks[14], (L, GLA_V_WIDTH, D_MODEL), GLA_V_WIDTH ** -0.5),
        "w_out": nrm(ks[15], (L, D_MODEL, D_MODEL), D_MODEL ** -0.5),
        "g_ffn": 1.0 + nrm(ks[16], (L, D_MODEL), 0.02),
        "w_router_group": nrm(ks[17], (L, D_MODEL, N_GROUPS), D_MODEL ** -0.5),
        "b_router_group": nrm(ks[18], (L, N_GROUPS), 0.01),
        "w_router_expert": nrm(ks[19], (L, D_MODEL, N_EXPERTS), D_MODEL ** -0.5),
        "b_router_expert": nrm(ks[20], (L, N_EXPERTS), 0.01),
        "w_exp_gate": nrm(ks[21], (L, N_EXPERTS, D_MODEL, EXPERT_FF), D_MODEL ** -0.5),
        "w_exp_up": nrm(ks[22], (L, N_EXPERTS, D_MODEL, EXPERT_FF), D_MODEL ** -0.5),
        "w_exp_down": nrm(ks[23], (L, N_EXPERTS, EXPERT_FF, D_MODEL), EXPERT_FF ** -0.5),
        "g_final": 1.0 + nrm(ks[24], (D_MODEL,), 0.02),
    }


def reference(x, g_mix, w_in, nsa_pe_k, nsa_cmp_k_w1, nsa_cmp_k_w2, nsa_pe_v, nsa_cmp_v_w1,
              nsa_cmp_v_w2, rel_bias, gla_w_alpha, gla_b_alpha, gla_norm_g, w_branch_a, w_branch_b,
              w_out, g_ffn, w_router_group, b_router_group, w_router_expert, b_router_expert,
              w_exp_gate, w_exp_up, w_exp_down, g_final):
    split_at = np.cumsum(IN_SPLITS)[:-1].tolist()
    for l in range(DEPTH):
        h = rmsnorm(x, g_mix[l])
        proj = h @ w_in[l]
        (q_a, k_cmp, v_cmp, k_slc, v_slc, k_win, v_win, gate_a,
         q_b, k_b, v_b, a_b, r_b, mg_a, mg_b) = jnp.split(proj, split_at, axis=-1)
        y_a = nsa_mixer(q_a, k_cmp, v_cmp, k_slc, v_slc, k_win, v_win, gate_a,
                        nsa_pe_k[l], nsa_cmp_k_w1[l], nsa_cmp_k_w2[l],
                        nsa_pe_v[l], nsa_cmp_v_w1[l], nsa_cmp_v_w2[l], rel_bias)
        y_b = gla_mixer(q_b, k_b, v_b, a_b, r_b, gla_w_alpha[l], gla_b_alpha[l], gla_norm_g[l])
        merged = jax.nn.sigmoid(mg_a) * (y_a @ w_branch_a[l]) + jax.nn.sigmoid(mg_b) * (y_b @ w_branch_b[l])
        x = x + merged @ w_out[l]
        x = x + hier_moe(rmsnorm(x, g_ffn[l]), w_router_group[l], b_router_group[l],
                         w_router_expert[l], b_router_expert[l],
                         w_exp_gate[l], w_exp_up[l], w_exp_down[l])
    return rmsnorm(x, g_final)
```

```python
import functools
import math

import numpy as np
import jax
import jax.numpy as jnp
from jax import lax
from jax.experimental import pallas as pl
from jax.experimental.pallas import tpu as pltpu

F32 = jnp.float32
BF16 = jnp.bfloat16

NSA_HEADS = 8
NSA_GROUPS = 2
HPG = NSA_HEADS // NSA_GROUPS
DH = 64
CMP_BLOCK = 32
CMP_STRIDE = 16
CMP_HIDDEN = 128
SLC_BLOCK = 64
SLC_TOPN = 16
WINDOW = 512
GLA_HEADS = 4
GLA_DK = 64
GLA_DV = 128
GLA_RANK = 16
GLA_TAU = 16.0
GLA_CHUNK = 64
REL_BUCKETS = 32
REL_MAX_EXACT = REL_BUCKETS // 2
REL_MAX_DIST = 128
N_GROUPS = 4
EPG = 8
N_EXPERTS = N_GROUPS * EPG
EXPERT_FF = 256
EPS = 1e-6

LANES = 128
SUBLANES = 8
VMEM_LIMIT = 56 * 1024 * 1024

NEG = -1e30
BIG = float(2.0 ** 100)
QT = 128
SLC_PAD = 128
WIN_PAD = 512
TE = 256
MOE_TM = 512


def _nt(a, b):
    return lax.dot_general(a, b, (((1,), (1,)), ((), ())), preferred_element_type=F32)


def _nn(a, b):
    return jnp.dot(a, b, preferred_element_type=F32)


def _split3(x):
    a = x.astype(BF16)
    r = x - a.astype(F32)
    b = r.astype(BF16)
    c = (r - b.astype(F32)).astype(BF16)
    return a, b, c


def _t5_bucket_np(rel):
    n = np.maximum(rel, 0)
    nf = np.maximum(n, 1).astype(np.float32)
    large = REL_MAX_EXACT + (np.log(nf / np.float32(REL_MAX_EXACT)) / np.float32(math.log(REL_MAX_DIST / REL_MAX_EXACT))
                             * np.float32(REL_BUCKETS - REL_MAX_EXACT)).astype(np.int32)
    return np.where(n < REL_MAX_EXACT, n, np.minimum(large, REL_BUCKETS - 1)).astype(np.int32)


def _inproj_kernel(x_ref, g_ref, w_ref, wt_ref, oq, okslc, okwin, okv, omisc, oqkb, ovb, orb, omg,
                   ovts, ovtw, ovtb, *, tm, seq):
    x = x_ref[0]
    ms = jnp.mean(x * x, axis=-1, keepdims=True)
    h = (x * lax.rsqrt(ms + EPS) * g_ref[...]).astype(BF16)

    def mm(a, b):
        return _nn(h, w_ref[:, a:b])

    oq[0] = mm(0, 1024).astype(BF16)
    ks = mm(1024, 1280)
    row = lax.broadcasted_iota(jnp.int32, (tm, 256), 0) + pl.program_id(1) * tm
    lane = lax.broadcasted_iota(jnp.int32, (tm, 256), 1) % LANES
    onehot = jnp.where(lane - DH == row // SLC_BLOCK, 1.0, 0.0)
    okslc[0] = (ks + onehot).astype(BF16)
    okwin[0] = mm(1280, 1536).astype(BF16)
    okv[0] = mm(1536, 1792)
    omisc[0] = mm(1792, 1920)
    oqkb[0] = mm(1920, 2432)
    ovb[0] = mm(2432, 2944).astype(BF16)
    orb[0] = mm(2944, 3456)
    omg[0] = mm(3456, 5504)
    vt = _nt(wt_ref[...], h)
    ovts[0] = vt[0:128].astype(BF16)
    ovtw[0] = vt[128:256].astype(BF16)
    ovtb[0] = vt[256:768].astype(BF16)


def _pack_inproj_weights(w_in):
    o = np.cumsum([0, 512, 128, 128, 128, 128, 128, 128, 24, 256, 256, 512, 16, 512, 1024, 1024])
    (q_a, k_cmp, v_cmp, k_slc, v_slc, k_win, v_win, gate_a, q_b, k_b, v_b, a_b, r_b, mg_a, mg_b) = [
        w_in[:, o[i]:o[i + 1]] for i in range(15)]
    D = w_in.shape[0]
    z64 = jnp.zeros((D, DH), w_in.dtype)
    qcols = []
    for hd in range(NSA_HEADS):
        qcols += [q_a[:, hd * DH:(hd + 1) * DH] * (DH ** -0.5), z64]
    kslc = [k_slc[:, :DH], z64, k_slc[:, DH:], z64]
    kwin = [k_win[:, :DH], z64, k_win[:, DH:], z64]
    ga = gate_a.reshape(D, 3, NSA_GROUPS, HPG)
    z4 = jnp.zeros((D, 4), w_in.dtype)
    misc = [ga[:, :, 0, :].reshape(D, 12), z4, ga[:, :, 1, :].reshape(D, 12), z4, a_b,
            jnp.zeros((D, LANES - 48), w_in.dtype)]
    w = jnp.concatenate(qcols + kslc + kwin + [k_cmp, v_cmp] + misc + [q_b, k_b, v_b, r_b, mg_a, mg_b], axis=1)
    wt = jnp.concatenate([v_slc, v_win, v_b], axis=1).T
    return w.astype(BF16), wt.astype(BF16)


def _inproj(x, g_mix, w, wt, tm=512):
    B, S, D = x.shape
    nw = w.shape[1]
    widths = [(1024, BF16), (256, BF16), (256, BF16), (256, F32), (128, F32), (512, F32), (512, BF16),
              (512, F32), (2048, F32)]
    out_shape = [jax.ShapeDtypeStruct((B, S, n), dt) for n, dt in widths]
    out_specs = [pl.BlockSpec((1, tm, n), lambda b, i: (b, i, 0)) for n, _ in widths]
    for rows in (128, 128, 512):
        out_shape.append(jax.ShapeDtypeStruct((B, rows, S), BF16))
        out_specs.append(pl.BlockSpec((1, rows, tm), lambda b, i: (b, 0, i)))
    return pl.pallas_call(
        functools.partial(_inproj_kernel, tm=tm, seq=S),
        out_shape=out_shape,
        grid=(B, S // tm),
        in_specs=[
            pl.BlockSpec((1, tm, D), lambda b, i: (b, i, 0)),
            pl.BlockSpec((1, D), lambda b, i: (0, 0)),
            pl.BlockSpec((D, nw), lambda b, i: (0, 0), pipeline_mode=pl.Buffered(1)),
            pl.BlockSpec((768, D), lambda b, i: (0, 0), pipeline_mode=pl.Buffered(1)),
        ],
        out_specs=out_specs,
        compiler_params=pltpu.CompilerParams(dimension_semantics=("parallel", "parallel"),
                                             vmem_limit_bytes=VMEM_LIMIT),
        name="inproj",
    )(x, g_mix.reshape(1, D), w, wt)


def _gelu_tanh(x):
    return 0.5 * x * (1.0 + jnp.tanh(math.sqrt(2.0 / math.pi) * (x + 0.044715 * (x * x * x))))


def _compress_kernel(xk_ref, xv_ref, pek_ref, pev_ref, w1k_ref, w1v_ref, w2k_ref, w2vt_ref, ok_ref, ovt_ref):
    nsub = xk_ref.shape[2]

    def hidden(x_ref, pe_ref, w1_ref):
        sub = x_ref[0, 0]
        top = _nn((sub + pe_ref[0:1, :]).astype(BF16), w1_ref[0])
        bot = _nn((sub + pe_ref[1:2, :]).astype(BF16), w1_ref[1])
        bot = pltpu.roll(bot, shift=nsub - 1, axis=0)
        return _gelu_tanh(top + bot).astype(BF16)

    ok_ref[0, 0] = _nn(hidden(xk_ref, pek_ref, w1k_ref), w2k_ref[...]).astype(BF16)
    ovt_ref[0, 0] = _nt(w2vt_ref[...], hidden(xv_ref, pev_ref, w1v_ref)).astype(BF16)


def _compress(kv_cmp, pe_k, w1k, w2k, pe_v, w1v, w2v):
    B, S, _ = kv_cmp.shape
    nsub = S // CMP_STRIDE
    flat = CMP_STRIDE * DH
    xs = kv_cmp.reshape(B, nsub, CMP_STRIDE, 2, NSA_GROUPS, DH).transpose(0, 3, 4, 1, 2, 5)
    xs = xs.reshape(B, 2, NSA_GROUPS, nsub, flat)
    xk, xv = xs[:, 0], xs[:, 1]

    def prep(pe, w1):
        return pe.reshape(2, flat), w1.reshape(2, flat, CMP_HIDDEN).astype(BF16)

    pek, w1k2 = prep(pe_k, w1k)
    pev, w1v2 = prep(pe_v, w1v)
    w2kp = jnp.concatenate([w2k, jnp.zeros_like(w2k)], axis=1).astype(BF16)
    w2vt = w2v.T.astype(BF16)
    xspec = pl.BlockSpec((1, 1, nsub, flat), lambda b, g: (b, g, 0, 0))
    full = lambda shp: pl.BlockSpec(shp, lambda b, g: (0,) * len(shp))
    return pl.pallas_call(
        _compress_kernel,
        out_shape=[jax.ShapeDtypeStruct((B, NSA_GROUPS, nsub, LANES), BF16),
                   jax.ShapeDtypeStruct((B, NSA_GROUPS, DH, nsub), BF16)],
        grid=(B, NSA_GROUPS),
        in_specs=[xspec, xspec, full((2, flat)), full((2, flat)), full((2, flat, CMP_HIDDEN)),
                  full((2, flat, CMP_HIDDEN)), full((CMP_HIDDEN, LANES)), full((DH, CMP_HIDDEN))],
        out_specs=[pl.BlockSpec((1, 1, nsub, LANES), lambda b, g: (b, g, 0, 0)),
                   pl.BlockSpec((1, 1, DH, nsub), lambda b, g: (b, g, 0, 0))],
        compiler_params=pltpu.CompilerParams(dimension_semantics=("parallel", "parallel"),
                                             vmem_limit_bytes=VMEM_LIMIT),
        name="compress",
    )(xk, xv, pek, pev, w1k2, w1v2, w2kp, w2vt)


def _nsa_bias_tables(rel_bias, seq):
    nq_lanes = HPG * QT
    tbl = rel_bias.T.reshape(NSA_GROUPS, HPG, REL_BUCKETS)
    ql = np.arange(QT)

    def lookup(rel):
        bk = _t5_bucket_np(rel)
        vals = tbl[:, :, bk]
        vals = jnp.where(jnp.asarray(rel >= 0)[None, None], vals, NEG)
        return vals.transpose(0, 2, 1, 3).reshape(NSA_GROUPS, rel.shape[0], nq_lanes)

    near = lookup(ql[None, :] + QT - np.arange(2 * QT)[:, None])
    ncmp = seq // CMP_STRIDE
    y = np.arange(2 * ncmp)
    cmpb = lookup(ql[None, :] - CMP_STRIDE * (y[:, None] - ncmp) - (CMP_BLOCK - 1))
    far = jnp.repeat(tbl[:, :, REL_BUCKETS - 1], QT, axis=1).reshape(NSA_GROUPS, 1, nq_lanes)
    xw = np.arange(3 * QT)
    wmask = (xw[:, None] >= QT) | (xw[:, None] > ql[None, :])
    wmask = np.tile(wmask, (1, HPG))
    winfar = jnp.where(jnp.asarray(wmask)[None], far, NEG)
    return near, cmpb, far, winfar


def _nsa_kernel(q_ref, kslc_ref, vtslc_ref, kwin_ref, vtwin_ref, kcb_ref, vcbt_ref, misc_ref,
                near_ref, cmpb_ref, far_ref, winfar_ref, ovt_ref, eye_ref, eye4_ref,
                out_ref, q0_s, qfar_s, qnear_s, qwin_s, *, ncmp, nslc):
    g = pl.program_id(1)
    qt = pl.program_id(2)
    nql = HPG * QT

    q0 = jnp.concatenate([q_ref[0, :, h * LANES:(h + 1) * LANES] for h in range(HPG)], axis=0)
    q0_s[...] = q0

    off = pl.multiple_of(ncmp - (QT // CMP_STRIDE) * qt, SUBLANES)
    bc = cmpb_ref[0, pl.ds(off, ncmp), :]
    sc = _nt(kcb_ref[0, 0], q0) + bc
    vis = bc > 0.5 * NEG
    mc = jnp.max(sc, axis=0, keepdims=True)
    ec = jnp.where(vis, jnp.exp(sc - mc), 0.0)
    den = jnp.maximum(jnp.sum(ec, axis=0, keepdims=True), jnp.finfo(F32).tiny)
    pc = ec * (1.0 / den)
    o_c = _nn(vcbt_ref[0, 0], pc.astype(BF16))

    psum = pc[:, 0:QT]
    for h in range(1, HPG):
        psum = psum + pc[:, h * QT:(h + 1) * QT]
    p1, p2, p3 = _split3(psum)
    ovt = ovt_ref[...]
    imp = _nn(ovt, p1) + _nn(ovt, p2) + _nn(ovt, p3)
    jidx = lax.broadcasted_iota(jnp.int32, (nslc, QT), 0)
    tq = qt * QT + lax.broadcasted_iota(jnp.int32, (nslc, QT), 1)
    tb = tq // SLC_BLOCK
    forced = (jidx == 0) | (jidx == tb) | (jidx == tb - 1)
    future = jidx * SLC_BLOCK > tq
    imp = jnp.where(forced, 1e30, jnp.where(future, -1e30, imp))
    rank = jnp.zeros((nslc, QT), F32)
    for jp in range(nslc):
        row = imp[jp:jp + 1, :]
        tie = jnp.where(jidx > jp, 1.0, 0.0)
        rank = rank + jnp.where(row > imp, 1.0, jnp.where(row == imp, tie, 0.0))
    sel = rank < float(min(SLC_TOPN, nslc))
    sel_near = jnp.where(sel, 1.0, 0.0)
    sel_far = jnp.where(jidx < 2 * (qt - 1), sel_near, 0.0)
    ones_lo = jnp.ones((DH, QT), F32)
    eye = eye_ref[...]

    def aug(sel01):
        rows = [ones_lo, sel01]
        if LANES - DH - nslc:
            rows.append(jnp.ones((LANES - DH - nslc, QT), F32))
        m01 = _nt(eye, jnp.concatenate(rows, axis=0).astype(BF16))
        return jnp.concatenate([((m01 - 1.0) * BIG).astype(BF16)] * HPG, axis=0)

    qfar_s[...] = q0 + aug(sel_far)
    qnear_s[...] = q0 + aug(sel_near)
    lane = lax.broadcasted_iota(jnp.int32, (HPG * QT, LANES), 1)
    qwin_s[...] = q0 + jnp.where(lane >= DH, -BIG, 0.0).astype(BF16)

    def flash(carry, k_chunk, vt_chunk, qa, bias):
        m, l, acc = carry
        s = _nt(k_chunk, qa) + bias
        m_new = jnp.maximum(m, jnp.max(s, axis=0, keepdims=True))
        alpha = jnp.exp(m - m_new)
        p = jnp.exp(s - m_new)
        l = alpha * l + jnp.sum(p, axis=0, keepdims=True)
        acc = alpha * acc + _nn(vt_chunk, p.astype(BF16))
        return m_new, l, acc

    init = (jnp.full((1, nql), NEG, F32), jnp.zeros((1, nql), F32), jnp.zeros((DH, nql), F32))
    far_b = far_ref[0]
    near_b = near_ref[0]

    def far_body(f, carry):
        start = pl.multiple_of(QT * qt - 2 * QT * f - 2 * QT, QT)
        return flash(carry, kslc_ref[0, pl.ds(start, 2 * QT), :], vtslc_ref[0, :, pl.ds(start, 2 * QT)],
                     qfar_s[...], far_b)

    carry = lax.fori_loop(0, qt // 2, far_body, init)
    ns = pl.multiple_of(QT * qt, QT)
    m, l, acc = flash(carry, kslc_ref[0, pl.ds(ns, 2 * QT), :], vtslc_ref[0, :, pl.ds(ns, 2 * QT)],
                      qnear_s[...], near_b)
    o_s = acc * (1.0 / l)

    carry = flash(init, kwin_ref[0, pl.ds(ns, 3 * QT), :], vtwin_ref[0, :, pl.ds(ns, 3 * QT)],
                  qwin_s[...], winfar_ref[0])
    nw = pl.multiple_of(QT * qt + 3 * QT, QT)
    m, l, acc = flash(carry, kwin_ref[0, pl.ds(nw, 2 * QT), :], vtwin_ref[0, :, pl.ds(nw, 2 * QT)],
                      qwin_s[...], near_b)
    o_w = acc * (1.0 / l)

    gt = jax.nn.sigmoid(misc_ref[0]).T
    gsel = jnp.where(g == 0, gt[0:16], gt[16:32])

    def gate_row(br):
        return jnp.concatenate([gsel[br * HPG + h:br * HPG + h + 1, :] for h in range(HPG)], axis=1)

    o = gate_row(0) * o_c + gate_row(1) * o_s + gate_row(2) * o_w
    ob = o.astype(BF16)
    head = lax.broadcasted_iota(jnp.int32, (DH, nql), 1) // QT
    blocks = jnp.concatenate([jnp.where(head == h, ob, jnp.zeros_like(ob)) for h in range(HPG)], axis=0)
    out_ref[0] = _nt(eye4_ref[...], blocks).astype(BF16)


def _nsa(q, kslc, vtslc, kwin, vtwin, kcb, vcbt, misc, rel_bias):
    B, S, _ = q.shape
    G = NSA_GROUPS
    nq = S // QT
    ncmp = S // CMP_STRIDE
    nslc = S // SLC_BLOCK
    nql = HPG * QT
    near, cmpb, far, winfar = _nsa_bias_tables(rel_bias, S)
    kpad_s = jnp.concatenate([jnp.zeros((DH,), BF16), jnp.ones((DH,), BF16)] * G)
    kslc_p = jnp.concatenate([jnp.broadcast_to(kpad_s, (B, SLC_PAD, G * LANES)), kslc], axis=1)
    kwin_p = jnp.concatenate([jnp.broadcast_to(kpad_s, (B, WIN_PAD, G * LANES)), kwin], axis=1)
    vtslc_p = jnp.pad(vtslc, ((0, 0), (0, 0), (SLC_PAD, 0)))
    vtwin_p = jnp.pad(vtwin, ((0, 0), (0, 0), (WIN_PAD, 0)))
    ci = np.arange(ncmp)[None, :] * CMP_STRIDE
    sj = np.arange(nslc)[:, None] * SLC_BLOCK
    ovt = jnp.asarray(((ci < sj + SLC_BLOCK) & (ci + CMP_BLOCK > sj)).astype(np.float32), BF16)
    eye = jnp.eye(QT, dtype=BF16)
    eye4 = jnp.tile(eye, (1, HPG))
    kern = functools.partial(_nsa_kernel, ncmp=ncmp, nslc=nslc)
    bg = lambda shp, f: pl.BlockSpec(shp, f)
    return pl.pallas_call(
        kern,
        out_shape=jax.ShapeDtypeStruct((B, S, NSA_HEADS * DH), BF16),
        grid=(B, G, nq),
        in_specs=[
            bg((1, QT, HPG * LANES), lambda b, g, i: (b, i, g)),
            bg((1, SLC_PAD + S, LANES), lambda b, g, i: (b, 0, g)),
            bg((1, DH, SLC_PAD + S), lambda b, g, i: (b, g, 0)),
            bg((1, WIN_PAD + S, LANES), lambda b, g, i: (b, 0, g)),
            bg((1, DH, WIN_PAD + S), lambda b, g, i: (b, g, 0)),
            bg((1, 1, ncmp, LANES), lambda b, g, i: (b, g, 0, 0)),
            bg((1, 1, DH, ncmp), lambda b, g, i: (b, g, 0, 0)),
            bg((1, QT, LANES), lambda b, g, i: (b, i, 0)),
            bg((1, 2 * QT, nql), lambda b, g, i: (g, 0, 0)),
            bg((1, 2 * ncmp, nql), lambda b, g, i: (g, 0, 0)),
            bg((1, 1, nql), lambda b, g, i: (g, 0, 0)),
            bg((1, 3 * QT, nql), lambda b, g, i: (g, 0, 0)),
            bg((nslc, ncmp), lambda b, g, i: (0, 0)),
            bg((QT, QT), lambda b, g, i: (0, 0)),
            bg((QT, nql), lambda b, g, i: (0, 0)),
        ],
        out_specs=pl.BlockSpec((1, QT, HPG * DH), lambda b, g, i: (b, i, g)),
        scratch_shapes=[pltpu.VMEM((nql, LANES), BF16)] * 4,
        compiler_params=pltpu.CompilerParams(dimension_semantics=("parallel", "parallel", "arbitrary"),
                                             vmem_limit_bytes=VMEM_LIMIT),
        name="nsa",
    )(q, kslc_p, vtslc_p, kwin_p, vtwin_p, kcb, vcbt, misc, near, cmpb, far, winfar, ovt, eye, eye4)


def _gla_kernel(qk_ref, v_ref, vt_ref, misc_ref, r_ref, wal_ref, bal_ref, ng_ref, cum_ref, out_ref,
                state_s, o_s, *, ct):
    H, dk, dv, C = GLA_HEADS, GLA_DK, GLA_DV, GLA_CHUNK
    kw = H * dk

    @pl.when(pl.program_id(1) == 0)
    def _():
        state_s[...] = jnp.zeros_like(state_s)

    z = _nn(misc_ref[0].astype(BF16), wal_ref[...]) + bal_ref[...]
    log_a = (jnp.minimum(z, 0.0) - jnp.log1p(jnp.exp(-jnp.abs(z)))) * (1.0 / GLA_TAU)
    cum = cum_ref[...]
    a1, a2, a3 = _split3(log_a)
    cs = _nn(cum, a1) + _nn(cum, a2) + _nn(cum, a3)
    bc, bl = cs[:ct], cs[ct:]
    q = qk_ref[0, :, :kw]
    k = qk_ref[0, :, kw:]
    q_in = (q * (dk ** -0.5)) * jnp.exp(bc)
    k_in = (k * jnp.exp(-bc)).astype(BF16)
    k_st = k * jnp.exp(bl - bc)
    decay = jnp.exp(bl)
    lane_head = lax.broadcasted_iota(jnp.int32, (C, kw), 1) // dk
    rr = lax.broadcasted_iota(jnp.int32, (H * C, C), 0) % C
    cc = lax.broadcasted_iota(jnp.int32, (H * C, C), 1)
    causal = rr >= cc
    pair_row = lax.broadcasted_iota(jnp.int32, (2 * C, kw), 0) // C
    pair_head = lax.broadcasted_iota(jnp.int32, (2 * C, kw), 1) // dk

    for c in range(ct // C):
        r0 = c * C
        qc = q_in[r0:r0 + C]
        qcb = qc.astype(BF16)
        q_heads = jnp.concatenate([jnp.where(lane_head == h, qc, 0.0) for h in range(H)], axis=0).astype(BF16)
        attn = jnp.where(causal, _nt(q_heads, k_in[r0:r0 + C]), 0.0).astype(BF16)
        p0 = (c // 2) * 2 * C
        kst_pair = k_st[p0:p0 + 2 * C]
        dec = decay[r0:r0 + 1]
        for h in range(H):
            st = state_s[h]
            o = _nn(attn[h * C:(h + 1) * C], v_ref[0, r0:r0 + C, h * dv:(h + 1) * dv])
            o = o + _nt(qcb, st.astype(BF16))
            o_s[r0:r0 + C, h * dv:(h + 1) * dv] = o
            kst_h = jnp.where((pair_row == c % 2) & (pair_head == h), kst_pair, 0.0).astype(BF16)
            state_s[h] = st * dec + _nn(vt_ref[0, h * dv:(h + 1) * dv, p0:p0 + 2 * C], kst_h)

    for h in range(H):
        oh = o_s[:, h * dv:(h + 1) * dv]
        ms = jnp.mean(oh * oh, axis=-1, keepdims=True)
        r = r_ref[0, :, h * dv:(h + 1) * dv]
        y = oh * lax.rsqrt(ms + EPS) * ng_ref[:, h * dv:(h + 1) * dv] * (r * jax.nn.sigmoid(r))
        out_ref[0, :, h * dv:(h + 1) * dv] = y.astype(BF16)


def _gla(qkb, vb, vtb, misc, rb, w_alpha, b_alpha, norm_g, ct=512):
    B, S, _ = qkb.shape
    H, dk, dv, C = GLA_HEADS, GLA_DK, GLA_DV, GLA_CHUNK
    kw, vw = H * dk, H * dv
    wal = jnp.zeros((LANES, kw), F32).at[32:32 + GLA_RANK].set(w_alpha).astype(BF16)
    r = np.arange(ct)
    tri = (r[:, None] // C == r[None, :] // C) & (r[:, None] >= r[None, :])
    tot = r[:, None] // C == r[None, :] // C
    cum = jnp.asarray(np.concatenate([tri, tot], axis=0).astype(np.float32), BF16)
    full = lambda shp: pl.BlockSpec(shp, lambda b, i: (0,) * len(shp))
    return pl.pallas_call(
        functools.partial(_gla_kernel, ct=ct),
        out_shape=jax.ShapeDtypeStruct((B, S, vw), BF16),
        grid=(B, S // ct),
        in_specs=[
            pl.BlockSpec((1, ct, 2 * kw), lambda b, i: (b, i, 0)),
            pl.BlockSpec((1, ct, vw), lambda b, i: (b, i, 0)),
            pl.BlockSpec((1, vw, ct), lambda b, i: (b, 0, i)),
            pl.BlockSpec((1, ct, LANES), lambda b, i: (b, i, 0)),
            pl.BlockSpec((1, ct, vw), lambda b, i: (b, i, 0)),
            full((LANES, kw)), full((1, kw)), full((1, vw)), full((2 * ct, ct)),
        ],
        out_specs=pl.BlockSpec((1, ct, vw), lambda b, i: (b, i, 0)),
        scratch_shapes=[pltpu.VMEM((H, dv, kw), F32), pltpu.VMEM((ct, vw), F32)],
        compiler_params=pltpu.CompilerParams(dimension_semantics=("parallel", "arbitrary"),
                                             vmem_limit_bytes=VMEM_LIMIT),
        name="gla",
    )(qkb, vb, vtb, misc, rb, wal, b_alpha.reshape(1, kw), norm_g.reshape(1, vw), cum)


ROUTER_ROWS = 128
EXPERT_ROW0 = 8


def _outproj_kernel(ya_ref, yb_ref, mg_ref, x_ref, wa_ref, wb_ref, wo_ref, gf_ref, wrh_ref, wrl_ref, br_ref,
                    tri_ref, x1_ref, xw_ref, ri_ref, rw_ref, cnt_ref, carry_s, *, tm):
    D = x_ref.shape[1]

    @pl.when(pl.program_id(0) == 0)
    def _():
        carry_s[...] = jnp.zeros_like(carry_s)

    ma = _nn(ya_ref[...], wa_ref[...])
    mb = _nn(yb_ref[...], wb_ref[...])
    merged = jax.nn.sigmoid(mg_ref[:, :D]) * ma + jax.nn.sigmoid(mg_ref[:, D:]) * mb
    x1 = x_ref[...] + _nn(merged.astype(BF16), wo_ref[...])
    x1_ref[...] = x1
    ms = jnp.mean(x1 * x1, axis=-1, keepdims=True)
    h2 = x1 * lax.rsqrt(ms + EPS) * gf_ref[...]
    hi = h2.astype(BF16)
    hif = hi.astype(F32)
    lo = (h2 - hif).astype(BF16)
    u = pltpu.bitcast(hif, jnp.uint32)
    xw_ref[...] = u[:, :D // 2] | (u[:, D // 2:] >> 16)

    lg = _nt(wrh_ref[...], hi) + _nt(wrh_ref[...], lo) + _nt(wrl_ref[...], hi) + br_ref[...]
    row8 = lax.broadcasted_iota(jnp.int32, (SUBLANES, tm), 0)
    gl = jnp.where(row8 < N_GROUPS, lg[0:SUBLANES], NEG)
    gmax = jnp.max(gl, axis=0, keepdims=True)
    g_sel = jnp.min(jnp.where(gl == gmax, row8, SUBLANES), axis=0, keepdims=True)
    g_prob = 1.0 / jnp.sum(jnp.where(row8 < N_GROUPS, jnp.exp(gl - gmax), 0.0), axis=0, keepdims=True)
    e_sel = jnp.zeros((EPG, tm), F32)
    for gi in range(N_GROUPS):
        r0 = EXPERT_ROW0 + gi * EPG
        e_sel = e_sel + jnp.where(g_sel == gi, lg[r0:r0 + EPG], 0.0)
    v1 = jnp.max(e_sel, axis=0, keepdims=True)
    i1 = jnp.min(jnp.where(e_sel == v1, row8, EPG), axis=0, keepdims=True)
    rest = jnp.where(row8 == i1, -jnp.inf, e_sel)
    v2 = jnp.max(rest, axis=0, keepdims=True)
    i2 = jnp.min(jnp.where(rest == v2, row8, EPG), axis=0, keepdims=True)
    t = jnp.exp(v2 - v1)
    w1 = g_prob / (1.0 + t)
    w2 = g_prob * t / (1.0 + t)
    e1 = g_sel * EPG + i1
    e2 = g_sel * EPG + i2

    rowe = lax.broadcasted_iota(jnp.int32, (N_EXPERTS, tm), 0)
    oh1 = rowe == e1
    oh2 = rowe == e2
    oh = jnp.where(oh1, 1.0, 0.0) + jnp.where(oh2, 1.0, 0.0)
    pre = _nn(oh.astype(BF16), tri_ref[...]) + carry_s[:, 0:1]
    rank1 = jnp.sum(jnp.where(oh1, pre, 0.0), axis=0, keepdims=True)
    rank2 = jnp.sum(jnp.where(oh2, pre, 0.0), axis=0, keepdims=True)
    carry_s[...] = carry_s[...] + jnp.sum(oh, axis=1, keepdims=True)
    cnt_ref[...] = carry_s[...].astype(jnp.int32)

    zi = jnp.zeros((SUBLANES - 4, tm), jnp.int32)
    ri_ref[0] = jnp.concatenate([e1, e2, rank1.astype(jnp.int32), rank2.astype(jnp.int32), zi], axis=0)
    rw_ref[0] = jnp.concatenate([w1, w2, jnp.zeros((SUBLANES - 2, tm), F32)], axis=0)


def _outproj(ya, yb, mg, x, wa, wb, wo, g_ffn, w_rg, b_rg, w_re, b_re, tm=MOE_TM):
    T, D = x.shape
    nt = T // tm
    wr = jnp.zeros((ROUTER_ROWS, D), F32).at[0:N_GROUPS].set(w_rg.T).at[EXPERT_ROW0:EXPERT_ROW0 + N_EXPERTS].set(w_re.T)
    wrh = wr.astype(BF16)
    wrl = (wr - wrh.astype(F32)).astype(BF16)
    br = jnp.zeros((ROUTER_ROWS, 1), F32).at[0:N_GROUPS, 0].set(b_rg).at[EXPERT_ROW0:EXPERT_ROW0 + N_EXPERTS, 0].set(b_re)
    r = np.arange(tm)
    tri = jnp.asarray((r[:, None] < r[None, :]).astype(np.float32), BF16)
    row = lambda n: pl.BlockSpec((tm, n), lambda i: (i, 0))
    full = lambda shp: pl.BlockSpec(shp, lambda i: (0,) * len(shp))
    return pl.pallas_call(
        functools.partial(_outproj_kernel, tm=tm),
        out_shape=[jax.ShapeDtypeStruct((T, D), F32), jax.ShapeDtypeStruct((T, D // 2), jnp.uint32),
                   jax.ShapeDtypeStruct((nt, SUBLANES, tm), jnp.int32), jax.ShapeDtypeStruct((nt, SUBLANES, tm), F32),
                   jax.ShapeDtypeStruct((N_EXPERTS, LANES), jnp.int32)],
        grid=(nt,),
        in_specs=[row(ya.shape[1]), row(yb.shape[1]), row(2 * D), row(D),
                  full(wa.shape), full(wb.shape), full(wo.shape), full((1, D)),
                  full((ROUTER_ROWS, D)), full((ROUTER_ROWS, D)), full((ROUTER_ROWS, 1)), full((tm, tm))],
        out_specs=[row(D), row(D // 2),
                   pl.BlockSpec((1, SUBLANES, tm), lambda i: (i, 0, 0)),
                   pl.BlockSpec((1, SUBLANES, tm), lambda i: (i, 0, 0)),
                   full((N_EXPERTS, LANES))],
        scratch_shapes=[pltpu.VMEM((N_EXPERTS, LANES), F32)],
        compiler_params=pltpu.CompilerParams(dimension_semantics=("arbitrary",), vmem_limit_bytes=VMEM_LIMIT),
        name="outproj",
    )(ya, yb, mg, x, wa, wb, wo, g_ffn.reshape(1, D), wrh, wrl, br, tri)


def _dispatch_kernel(pos_ref, xw_ref, zero_ref, xs_ref, sem, *, tm):
    del zero_ref

    def row_copy(r, k):
        return pltpu.make_async_copy(xw_ref.at[pl.ds(r, 1), :], xs_ref.at[pl.ds(pos_ref[0, k, r], 1), :], sem)

    def start(r, c):
        row_copy(r, 0).start()
        row_copy(r, 1).start()
        return c

    lax.fori_loop(0, tm, start, 0)

    def wait(r, c):
        row_copy(r, 0).wait()
        row_copy(r, 1).wait()
        return c

    lax.fori_loop(0, tm, wait, 0)


def _dispatch(pos, xw, n_rows, tm=MOE_TM):
    T, W = xw.shape
    nt = T // tm
    zeros = jnp.zeros((n_rows, W), xw.dtype)
    return pl.pallas_call(
        functools.partial(_dispatch_kernel, tm=tm),
        out_shape=jax.ShapeDtypeStruct((n_rows, W), xw.dtype),
        grid=(nt,),
        in_specs=[pl.BlockSpec((1, 2, tm), lambda i: (i, 0, 0), memory_space=pltpu.SMEM),
                  pl.BlockSpec((tm, W), lambda i: (i, 0)),
                  pl.BlockSpec(memory_space=pl.ANY)],
        out_specs=pl.BlockSpec(memory_space=pl.ANY),
        scratch_shapes=[pltpu.SemaphoreType.DMA(())],
        input_output_aliases={2: 0},
        compiler_params=pltpu.CompilerParams(dimension_semantics=("arbitrary",)),
        name="dispatch",
    )(pos, xw, zeros)


def _experts_kernel(te_ref, nv_ref, xs_ref, wg_ref, wu_ref, wd_ref, out_ref):
    i = pl.program_id(0)

    @pl.when(i < nv_ref[0])
    def _():
        w = xs_ref[...]
        xh = pltpu.bitcast(w & jnp.uint32(0xFFFF0000), F32)
        xl = pltpu.bitcast(w << 16, F32)
        x = jnp.concatenate([xh, xl], axis=1).astype(BF16)
        a = _nn(x, wg_ref[0].astype(BF16))
        u = _nn(x, wu_ref[0].astype(BF16))
        hid = (a * jax.nn.sigmoid(a)) * u
        out_ref[...] = _nn(hid.astype(BF16), wd_ref[0].astype(BF16))

    @pl.when(i >= nv_ref[0])
    def _():
        out_ref[...] = jnp.zeros_like(out_ref)


def _experts(tile_expert, n_valid, xs, w_gate, w_up, w_down):
    n_rows, W = xs.shape
    D = 2 * W
    n_tiles = n_rows // TE
    last = lambda i, nv: jnp.minimum(i, nv[0] - 1)
    return pl.pallas_call(
        _experts_kernel,
        out_shape=jax.ShapeDtypeStruct((n_rows, D), F32),
        grid_spec=pltpu.PrefetchScalarGridSpec(
            num_scalar_prefetch=2,
            grid=(n_tiles,),
            in_specs=[pl.BlockSpec((TE, W), lambda i, te, nv: (last(i, nv), 0)),
                      pl.BlockSpec((1, D, EXPERT_FF), lambda i, te, nv: (te[last(i, nv)], 0, 0)),
                      pl.BlockSpec((1, D, EXPERT_FF), lambda i, te, nv: (te[last(i, nv)], 0, 0)),
                      pl.BlockSpec((1, EXPERT_FF, D), lambda i, te, nv: (te[last(i, nv)], 0, 0))],
            out_specs=pl.BlockSpec((TE, D), lambda i, te, nv: (i, 0)),
        ),
        compiler_params=pltpu.CompilerParams(dimension_semantics=("arbitrary",), vmem_limit_bytes=VMEM_LIMIT),
        name="experts",
    )(tile_expert, n_valid, xs, w_gate, w_up, w_down)


def _combine_kernel(pos_ref, ys_ref, x1_ref, rw_ref, gfin_ref, out_ref, buf, sem, *, tm, apply_norm):
    def row_copy(r, k):
        return pltpu.make_async_copy(ys_ref.at[pl.ds(pos_ref[0, k, r], 1), :], buf.at[k, pl.ds(r, 1), :], sem)

    def start(r, c):
        row_copy(r, 0).start()
        row_copy(r, 1).start()
        return c

    lax.fori_loop(0, tm, start, 0)

    def wait(r, c):
        row_copy(r, 0).wait()
        row_copy(r, 1).wait()
        return c

    lax.fori_loop(0, tm, wait, 0)
    wts = jnp.concatenate([rw_ref[0], jnp.zeros((LANES - SUBLANES, tm), F32)], axis=0).T
    y = x1_ref[...] + wts[:, 0:1] * buf[0] + wts[:, 1:2] * buf[1]
    if apply_norm:
        ms = jnp.mean(y * y, axis=-1, keepdims=True)
        y = y * lax.rsqrt(ms + EPS) * gfin_ref[...]
    out_ref[...] = y


def _combine(pos, ys, x1, rw, g_final, apply_norm, tm=MOE_TM):
    T, D = x1.shape
    nt = T // tm
    return pl.pallas_call(
        functools.partial(_combine_kernel, tm=tm, apply_norm=apply_norm),
        out_shape=jax.ShapeDtypeStruct((T, D), F32),
        grid=(nt,),
        in_specs=[pl.BlockSpec((1, 2, tm), lambda i: (i, 0, 0), memory_space=pltpu.SMEM),
                  pl.BlockSpec(memory_space=pl.ANY),
                  pl.BlockSpec((tm, D), lambda i: (i, 0)),
                  pl.BlockSpec((1, SUBLANES, tm), lambda i: (i, 0, 0)),
                  pl.BlockSpec((1, D), lambda i: (0, 0))],
        out_specs=pl.BlockSpec((tm, D), lambda i: (i, 0)),
        scratch_shapes=[pltpu.VMEM((2, tm, D), F32), pltpu.SemaphoreType.DMA(())],
        compiler_params=pltpu.CompilerParams(dimension_semantics=("arbitrary",), vmem_limit_bytes=VMEM_LIMIT),
        name="combine",
    )(pos, ys, x1, rw, g_final.reshape(1, D))


def _moe_plan(ri, counts, T):
    n_tiles_max = (2 * T) // TE + N_EXPERTS
    cnt = counts[:, 0]
    tiles = (cnt + TE - 1) // TE
    tile_end = jnp.cumsum(tiles)
    row0 = (tile_end - tiles) * TE
    e = ri[:, 0:2, :]
    pos = row0[e] + ri[:, 2:4, :]
    tile_expert = jnp.minimum(jnp.searchsorted(tile_end, jnp.arange(n_tiles_max), side="right"),
                              N_EXPERTS - 1).astype(jnp.int32)
    return pos.astype(jnp.int32), tile_expert, tile_end[-1:].astype(jnp.int32), n_tiles_max * TE


def kernel(x, g_mix, w_in, nsa_pe_k, nsa_cmp_k_w1, nsa_cmp_k_w2, nsa_pe_v, nsa_cmp_v_w1, nsa_cmp_v_w2, rel_bias,
           gla_w_alpha, gla_b_alpha, gla_norm_g, w_branch_a, w_branch_b, w_out, g_ffn, w_router_group,
           b_router_group, w_router_expert, b_router_expert, w_exp_gate, w_exp_up, w_exp_down, g_final):
    B, S, D = x.shape
    T = B * S
    for l in range(w_in.shape[0]):
        w, wt = _pack_inproj_weights(w_in[l])
        (q, kslc, kwin, kvc, misc, qkb, vb, rb, mg, vts, vtw, vtb) = _inproj(x, g_mix[l], w, wt)
        kcb, vcbt = _compress(kvc, nsa_pe_k[l], nsa_cmp_k_w1[l], nsa_cmp_k_w2[l],
                              nsa_pe_v[l], nsa_cmp_v_w1[l], nsa_cmp_v_w2[l])
        ya = _nsa(q, kslc, vts, kwin, vtw, kcb, vcbt, misc, rel_bias)
        yb = _gla(qkb, vb, vtb, misc, rb, gla_w_alpha[l], gla_b_alpha[l], gla_norm_g[l])
        x1, xw, ri, rw, counts = _outproj(
            ya.reshape(T, -1), yb.reshape(T, -1), mg.reshape(T, -1), x.reshape(T, D),
            w_branch_a[l].astype(BF16), w_branch_b[l].astype(BF16), w_out[l].astype(BF16), g_ffn[l],
            w_router_group[l], b_router_group[l], w_router_expert[l], b_router_expert[l])
        pos, tile_expert, n_valid, n_rows = _moe_plan(ri, counts, T)
        xs = _dispatch(pos, xw, n_rows)
        ys = _experts(tile_expert, n_valid, xs, w_exp_gate[l], w_exp_up[l], w_exp_down[l])
        last_layer = l == w_in.shape[0] - 1
        x = _combine(pos, ys, x1, rw, g_final, apply_norm=last_layer).reshape(B, S, D)
    return x
```

```python
import functools
import math

import numpy as np
import jax
import jax.numpy as jnp
from jax import lax
from jax.experimental import pallas as pl
from jax.experimental.pallas import tpu as pltpu

F32 = jnp.float32
BF16 = jnp.bfloat16

NSA_HEADS = 8
NSA_GROUPS = 2
HPG = NSA_HEADS // NSA_GROUPS
DH = 64
CMP_BLOCK = 32
CMP_STRIDE = 16
CMP_HIDDEN = 128
SLC_BLOCK = 64
SLC_TOPN = 16
WINDOW = 512
GLA_HEADS = 4
GLA_DK = 64
GLA_DV = 128
GLA_RANK = 16
GLA_TAU = 16.0
GLA_CHUNK = 64
REL_BUCKETS = 32
REL_MAX_EXACT = REL_BUCKETS // 2
REL_MAX_DIST = 128
N_GROUPS = 4
EPG = 8
N_EXPERTS = N_GROUPS * EPG
EXPERT_FF = 256
EPS = 1e-6

LANES = 128
SUBLANES = 8
VMEM_LIMIT = 56 * 1024 * 1024

LOG2E = math.log2(math.e)
NEG = -1e30
BIG = float(2.0 ** 100)
QT = 128
SLC_PAD = 128
WIN_PAD = 512
TE = 256
MOE_TM = 512


def _nt(a, b):
    return lax.dot_general(a, b, (((1,), (1,)), ((), ())), preferred_element_type=F32)


def _nn(a, b):
    return jnp.dot(a, b, preferred_element_type=F32)


def _split3(x):
    a = x.astype(BF16)
    r = x - a.astype(F32)
    b = r.astype(BF16)
    c = (r - b.astype(F32)).astype(BF16)
    return a, b, c


def _t5_bucket_np(rel):
    n = np.maximum(rel, 0)
    nf = np.maximum(n, 1).astype(np.float32)
    large = REL_MAX_EXACT + (np.log(nf / np.float32(REL_MAX_EXACT)) / np.float32(math.log(REL_MAX_DIST / REL_MAX_EXACT))
                             * np.float32(REL_BUCKETS - REL_MAX_EXACT)).astype(np.int32)
    return np.where(n < REL_MAX_EXACT, n, np.minimum(large, REL_BUCKETS - 1)).astype(np.int32)


def _inproj_kernel(x_ref, g_ref, w_ref, wt_ref, oq, okslc, okwin, okv, omisc, oqkb, ovb, orb, omg,
                   ovts, ovtw, ovtb, *, tm, seq):
    x = x_ref[0]
    ms = jnp.mean(x * x, axis=-1, keepdims=True)
    h = (x * lax.rsqrt(ms + EPS) * g_ref[...]).astype(BF16)

    def mm(a, b):
        return _nn(h, w_ref[:, a:b])

    oq[0] = mm(0, 1024).astype(BF16)
    ks = mm(1024, 1280)
    row = lax.broadcasted_iota(jnp.int32, (tm, 256), 0) + pl.program_id(1) * tm
    lane = lax.broadcasted_iota(jnp.int32, (tm, 256), 1) % LANES
    onehot = jnp.where(lane - DH == row // SLC_BLOCK, 1.0, 0.0)
    okslc[0] = (ks + onehot).astype(BF16)
    okwin[0] = mm(1280, 1536).astype(BF16)
    okv[0] = mm(1536, 1792)
    omisc[0] = mm(1792, 1920)
    oqkb[0] = mm(1920, 2432)
    ovb[0] = mm(2432, 2944).astype(BF16)
    orb[0] = mm(2944, 3456)
    omg[0] = mm(3456, 5504)
    vt = _nt(wt_ref[...], h)
    ovts[0] = vt[0:128].astype(BF16)
    ovtw[0] = vt[128:256].astype(BF16)
    ovtb[0] = vt[256:768].astype(BF16)


def _pack_inproj_weights(w_in):
    o = np.cumsum([0, 512, 128, 128, 128, 128, 128, 128, 24, 256, 256, 512, 16, 512, 1024, 1024])
    (q_a, k_cmp, v_cmp, k_slc, v_slc, k_win, v_win, gate_a, q_b, k_b, v_b, a_b, r_b, mg_a, mg_b) = [
        w_in[:, o[i]:o[i + 1]] for i in range(15)]
    D = w_in.shape[0]
    z64 = jnp.zeros((D, DH), w_in.dtype)
    qcols = []
    for hd in range(NSA_HEADS):
        qcols += [q_a[:, hd * DH:(hd + 1) * DH] * (DH ** -0.5 * LOG2E), z64]
    kslc = [k_slc[:, :DH], z64, k_slc[:, DH:], z64]
    kwin = [k_win[:, :DH], z64, k_win[:, DH:], z64]
    ga = gate_a.reshape(D, 3, NSA_GROUPS, HPG)
    z4 = jnp.zeros((D, 4), w_in.dtype)
    misc = [ga[:, :, 0, :].reshape(D, 12), z4, ga[:, :, 1, :].reshape(D, 12), z4, a_b,
            jnp.zeros((D, LANES - 48), w_in.dtype)]
    w = jnp.concatenate(qcols + kslc + kwin + [k_cmp, v_cmp] + misc + [q_b, k_b, v_b, r_b, mg_a, mg_b], axis=1)
    wt = jnp.concatenate([v_slc, v_win, v_b], axis=1).T
    return w.astype(BF16), wt.astype(BF16)


def _inproj(x, g_mix, w, wt, tm=512):
    B, S, D = x.shape
    nw = w.shape[1]
    widths = [(1024, BF16), (256, BF16), (256, BF16), (256, F32), (128, F32), (512, F32), (512, BF16),
              (512, F32), (2048, F32)]
    out_shape = [jax.ShapeDtypeStruct((B, S, n), dt) for n, dt in widths]
    out_specs = [pl.BlockSpec((1, tm, n), lambda b, i: (b, i, 0)) for n, _ in widths]
    for rows in (128, 128, 512):
        out_shape.append(jax.ShapeDtypeStruct((B, rows, S), BF16))
        out_specs.append(pl.BlockSpec((1, rows, tm), lambda b, i: (b, 0, i)))
    return pl.pallas_call(
        functools.partial(_inproj_kernel, tm=tm, seq=S),
        out_shape=out_shape,
        grid=(B, S // tm),
        in_specs=[
            pl.BlockSpec((1, tm, D), lambda b, i: (b, i, 0)),
            pl.BlockSpec((1, D), lambda b, i: (0, 0)),
            pl.BlockSpec((D, nw), lambda b, i: (0, 0), pipeline_mode=pl.Buffered(1)),
            pl.BlockSpec((768, D), lambda b, i: (0, 0), pipeline_mode=pl.Buffered(1)),
        ],
        out_specs=out_specs,
        compiler_params=pltpu.CompilerParams(dimension_semantics=("parallel", "parallel"),
                                             vmem_limit_bytes=VMEM_LIMIT),
        name="inproj",
    )(x, g_mix.reshape(1, D), w, wt)


def _gelu_tanh(x):
    return 0.5 * x * (1.0 + jnp.tanh(math.sqrt(2.0 / math.pi) * (x + 0.044715 * (x * x * x))))


def _compress_kernel(xk_ref, xv_ref, pek_ref, pev_ref, w1k_ref, w1v_ref, w2k_ref, w2vt_ref, ok_ref, ovt_ref):
    nsub = xk_ref.shape[2]

    def hidden(x_ref, pe_ref, w1_ref):
        sub = x_ref[0, 0]
        top = _nn((sub + pe_ref[0:1, :]).astype(BF16), w1_ref[0])
        bot = _nn((sub + pe_ref[1:2, :]).astype(BF16), w1_ref[1])
        bot = pltpu.roll(bot, shift=nsub - 1, axis=0)
        return _gelu_tanh(top + bot).astype(BF16)

    ok_ref[0, 0] = _nn(hidden(xk_ref, pek_ref, w1k_ref), w2k_ref[...]).astype(BF16)
    ovt_ref[0, 0] = _nt(w2vt_ref[...], hidden(xv_ref, pev_ref, w1v_ref)).astype(BF16)


def _compress(kv_cmp, pe_k, w1k, w2k, pe_v, w1v, w2v):
    B, S, _ = kv_cmp.shape
    nsub = S // CMP_STRIDE
    flat = CMP_STRIDE * DH
    xs = kv_cmp.reshape(B, nsub, CMP_STRIDE, 2, NSA_GROUPS, DH).transpose(0, 3, 4, 1, 2, 5)
    xs = xs.reshape(B, 2, NSA_GROUPS, nsub, flat)
    xk, xv = xs[:, 0], xs[:, 1]

    def prep(pe, w1):
        return pe.reshape(2, flat), w1.reshape(2, flat, CMP_HIDDEN).astype(BF16)

    pek, w1k2 = prep(pe_k, w1k)
    pev, w1v2 = prep(pe_v, w1v)
    w2kp = jnp.concatenate([w2k, jnp.zeros_like(w2k)], axis=1).astype(BF16)
    w2vt = w2v.T.astype(BF16)
    xspec = pl.BlockSpec((1, 1, nsub, flat), lambda b, g: (b, g, 0, 0))
    full = lambda shp: pl.BlockSpec(shp, lambda b, g: (0,) * len(shp))
    return pl.pallas_call(
        _compress_kernel,
        out_shape=[jax.ShapeDtypeStruct((B, NSA_GROUPS, nsub, LANES), BF16),
                   jax.ShapeDtypeStruct((B, NSA_GROUPS, DH, nsub), BF16)],
        grid=(B, NSA_GROUPS),
        in_specs=[xspec, xspec, full((2, flat)), full((2, flat)), full((2, flat, CMP_HIDDEN)),
                  full((2, flat, CMP_HIDDEN)), full((CMP_HIDDEN, LANES)), full((DH, CMP_HIDDEN))],
        out_specs=[pl.BlockSpec((1, 1, nsub, LANES), lambda b, g: (b, g, 0, 0)),
                   pl.BlockSpec((1, 1, DH, nsub), lambda b, g: (b, g, 0, 0))],
        compiler_params=pltpu.CompilerParams(dimension_semantics=("parallel", "parallel"),
                                             vmem_limit_bytes=VMEM_LIMIT),
        name="compress",
    )(xk, xv, pek, pev, w1k2, w1v2, w2kp, w2vt)


def _bias_kernel(tbl_ref, bkn_ref, bkc_ref, near_ref, cmpb_ref):
    g = pl.program_id(0)
    for h in range(HPG):
        hd = g * HPG + h

        def lookup(bk):
            acc = jnp.full(bk.shape, NEG, F32)
            for b in range(REL_BUCKETS):
                acc = jnp.where(bk == b, tbl_ref[hd, b], acc)
            return acc

        vn = lookup(bkn_ref[...])
        near_ref[0, :, h * QT:(h + 1) * QT] = jnp.where(vn > 0.5 * NEG, (vn - tbl_ref[hd, REL_BUCKETS - 1]) * LOG2E, NEG)
        vc = lookup(bkc_ref[...])
        cmpb_ref[0, :, h * QT:(h + 1) * QT] = jnp.where(vc > 0.5 * NEG, vc * LOG2E, NEG)


def _nsa_bias_tables(rel_bias, seq):
    ql = np.arange(QT)
    ncmp = seq // CMP_STRIDE

    def buckets(rel):
        return jnp.asarray(np.where(rel >= 0, _t5_bucket_np(rel), -1).astype(np.int32))

    bkn = buckets(ql[None, :] + QT - np.arange(2 * QT)[:, None])
    y = np.arange(2 * ncmp)
    bkc = buckets(ql[None, :] - CMP_STRIDE * (y[:, None] - ncmp) - (CMP_BLOCK - 1))
    nql = HPG * QT
    return pl.pallas_call(
        _bias_kernel,
        out_shape=[jax.ShapeDtypeStruct((NSA_GROUPS, 2 * QT, nql), F32),
                   jax.ShapeDtypeStruct((NSA_GROUPS, 2 * ncmp, nql), F32)],
        grid=(NSA_GROUPS,),
        in_specs=[pl.BlockSpec(memory_space=pltpu.SMEM),
                  pl.BlockSpec((2 * QT, QT), lambda g: (0, 0)),
                  pl.BlockSpec((2 * ncmp, QT), lambda g: (0, 0))],
        out_specs=[pl.BlockSpec((1, 2 * QT, nql), lambda g: (g, 0, 0)),
                   pl.BlockSpec((1, 2 * ncmp, nql), lambda g: (g, 0, 0))],
        compiler_params=pltpu.CompilerParams(dimension_semantics=("parallel",)),
        name="t5bias",
    )(rel_bias.T, bkn, bkc)


def _nsa_kernel(q_ref, kslc_ref, vtslc_ref, kwin_ref, vtwin_ref, kcb_ref, vcbt_ref, misc_ref,
                near_ref, cmpb_ref, wmask_ref, ovt_ref, eye_ref, eye4_ref,
                out_ref, qaug_s, s0_s, s1_s, *, ncmp, nslc):
    g = pl.program_id(1)
    qt = pl.program_id(2)
    nql = HPG * QT
    ck = 2 * QT

    q0 = jnp.concatenate([q_ref[0, :, h * LANES:(h + 1) * LANES] for h in range(HPG)], axis=0)

    def flash(carry, s, vt_chunk):
        m, l, acc = carry
        m_new = jnp.maximum(m, jnp.max(s, axis=0, keepdims=True))
        alpha = jnp.exp2(m - m_new)
        p = jnp.exp2(s - m_new)
        l = alpha * l + jnp.sum(p, axis=0, keepdims=True)
        acc = alpha * acc + _nn(vt_chunk, p.astype(BF16))
        return m_new, l, acc

    init = (jnp.full((1, nql), NEG, F32), jnp.zeros((1, nql), F32), jnp.zeros((DH, nql), F32))
    near_b = near_ref[0]
    ns = pl.multiple_of(QT * qt, QT)

    lane = lax.broadcasted_iota(jnp.int32, (nql, LANES), 1)
    qwin = q0 + jnp.where(lane >= DH, -BIG, 0.0).astype(BF16)
    sw = _nt(kwin_ref[0, pl.ds(ns, 3 * QT), :], qwin)
    sw = jnp.concatenate([sw[:QT] + wmask_ref[...], sw[QT:]], axis=0)
    carry = flash(init, sw, vtwin_ref[0, :, pl.ds(ns, 3 * QT)])
    nw = pl.multiple_of(QT * qt + 3 * QT, QT)
    sw = _nt(kwin_ref[0, pl.ds(nw, ck), :], qwin) + near_b
    m, l, acc = flash(carry, sw, vtwin_ref[0, :, pl.ds(nw, ck)])
    o_w = acc * (1.0 / l)

    off = pl.multiple_of(ncmp - (QT // CMP_STRIDE) * qt, SUBLANES)
    bc = cmpb_ref[0, pl.ds(off, ncmp), :]
    sc = _nt(kcb_ref[0, 0], q0) + bc
    vis = bc > 0.5 * NEG
    mc = jnp.max(sc, axis=0, keepdims=True)
    ec = jnp.where(vis, jnp.exp2(sc - mc), 0.0)
    den = jnp.maximum(jnp.sum(ec, axis=0, keepdims=True), jnp.finfo(F32).tiny)
    pc = ec * (1.0 / den)
    o_c = _nn(vcbt_ref[0, 0], pc.astype(BF16))

    psum = pc[:, 0:QT]
    for h in range(1, HPG):
        psum = psum + pc[:, h * QT:(h + 1) * QT]
    p1, p2, p3 = _split3(psum)
    ovt = ovt_ref[...]
    imp = _nn(ovt, p1) + _nn(ovt, p2) + _nn(ovt, p3)
    jidx = lax.broadcasted_iota(jnp.int32, (nslc, QT), 0)
    tq = qt * QT + lax.broadcasted_iota(jnp.int32, (nslc, QT), 1)
    tb = tq // SLC_BLOCK
    forced = (jidx == 0) | (jidx == tb) | (jidx == tb - 1)
    future = jidx * SLC_BLOCK > tq
    imp = jnp.where(forced, 1e30, jnp.where(future, -1e30, imp))
    ngrp = nslc // SUBLANES
    grp = [imp[SUBLANES * v:SUBLANES * (v + 1)] for v in range(ngrp)]
    cnt = [jnp.zeros((SUBLANES, QT), F32) for _ in range(ngrp)]
    sub = lax.broadcasted_iota(jnp.int32, (SUBLANES, QT), 0)
    for jp in range(nslc):
        v0, r0 = divmod(jp, SUBLANES)
        row = jnp.broadcast_to(imp[jp:jp + 1, :], (SUBLANES, QT))
        for v in range(ngrp):
            if v < v0:
                inc = jnp.where(row > grp[v], 1.0, 0.0)
            elif v > v0:
                inc = jnp.where(row >= grp[v], 1.0, 0.0)
            else:
                inc = jnp.where(sub > r0, jnp.where(row >= grp[v], 1.0, 0.0), jnp.where(row > grp[v], 1.0, 0.0))
            cnt[v] = cnt[v] + inc
    rank = jnp.concatenate(cnt, axis=0)
    sel = rank < float(min(SLC_TOPN, nslc))
    sel_near = jnp.where(sel, 1.0, 0.0)
    sel_far = jnp.where(jidx < 2 * (qt - 1), sel_near, 0.0)
    ones_lo = jnp.ones((DH, QT), F32)
    eye = eye_ref[...]

    def aug(sel01):
        rows = [ones_lo, sel01]
        if LANES - DH - nslc:
            rows.append(jnp.ones((LANES - DH - nslc, QT), F32))
        m01 = _nt(eye, jnp.concatenate(rows, axis=0).astype(BF16))
        return jnp.concatenate([((m01 - 1.0) * BIG).astype(BF16)] * HPG, axis=0)

    qaug_s[0] = q0 + aug(sel_far)
    qaug_s[1] = q0 + aug(sel_near)

    n_far = qt // 2
    n_chunks = n_far + 1

    def rows(i):
        return pl.multiple_of(QT * qt - ck * i, QT)

    def scores(i):
        i = jnp.minimum(i, n_far)
        return _nt(kslc_ref[0, pl.ds(rows(i), ck), :], qaug_s[jnp.where(i == 0, 1, 0)])

    def vt(i):
        return vtslc_ref[0, :, pl.ds(rows(i), ck)]

    s0_s[...] = scores(0) + near_b

    def pair_body(p, carry):
        i = 2 * p
        s1_s[...] = scores(i + 1)
        carry = flash(carry, s0_s[...], vt(i))
        s0_s[...] = scores(i + 2)
        return flash(carry, s1_s[...], vt(i + 1))

    carry = lax.fori_loop(0, n_chunks // 2, pair_body, init)
    m, l, acc = lax.cond(n_chunks % 2 == 1, lambda c: flash(c, s0_s[...], vt(n_far)), lambda c: c, carry)
    o_s = acc * (1.0 / l)

    gt = jax.nn.sigmoid(misc_ref[0]).T
    gsel = jnp.where(g == 0, gt[0:16], gt[16:32])

    def gate_row(br):
        return jnp.concatenate([gsel[br * HPG + h:br * HPG + h + 1, :] for h in range(HPG)], axis=1)

    o = gate_row(0) * o_c + gate_row(1) * o_s + gate_row(2) * o_w
    ob = o.astype(BF16)
    head = lax.broadcasted_iota(jnp.int32, (DH, nql), 1) // QT
    blocks = jnp.concatenate([jnp.where(head == h, ob, jnp.zeros_like(ob)) for h in range(HPG)], axis=0)
    out_ref[0] = _nt(eye4_ref[...], blocks).astype(BF16)


def _nsa(q, kslc, vtslc, kwin, vtwin, kcb, vcbt, misc, rel_bias):
    B, S, _ = q.shape
    G = NSA_GROUPS
    nq = S // QT
    ncmp = S // CMP_STRIDE
    nslc = S // SLC_BLOCK
    nql = HPG * QT
    near, cmpb = _nsa_bias_tables(rel_bias, S)
    wmask = jnp.asarray(np.tile(np.where(np.arange(QT)[:, None] > np.arange(QT)[None, :], 0.0, NEG), (1, HPG)), F32)
    kpad_s = jnp.concatenate([jnp.zeros((DH,), BF16), jnp.ones((DH,), BF16)] * G)
    kslc_p = jnp.concatenate([jnp.broadcast_to(kpad_s, (B, SLC_PAD, G * LANES)), kslc], axis=1)
    kwin_p = jnp.concatenate([jnp.broadcast_to(kpad_s, (B, WIN_PAD, G * LANES)), kwin], axis=1)
    vtslc_p = jnp.pad(vtslc, ((0, 0), (0, 0), (SLC_PAD, 0)))
    vtwin_p = jnp.pad(vtwin, ((0, 0), (0, 0), (WIN_PAD, 0)))
    ci = np.arange(ncmp)[None, :] * CMP_STRIDE
    sj = np.arange(nslc)[:, None] * SLC_BLOCK
    ovt = jnp.asarray(((ci < sj + SLC_BLOCK) & (ci + CMP_BLOCK > sj)).astype(np.float32), BF16)
    eye = jnp.eye(QT, dtype=BF16)
    eye4 = jnp.tile(eye, (1, HPG))
    kern = functools.partial(_nsa_kernel, ncmp=ncmp, nslc=nslc)
    bg = lambda shp, f: pl.BlockSpec(shp, f)
    return pl.pallas_call(
        kern,
        out_shape=jax.ShapeDtypeStruct((B, S, NSA_HEADS * DH), BF16),
        grid=(B, G, nq),
        in_specs=[
            bg((1, QT, HPG * LANES), lambda b, g, i: (b, i, g)),
            bg((1, SLC_PAD + S, LANES), lambda b, g, i: (b, 0, g)),
            bg((1, DH, SLC_PAD + S), lambda b, g, i: (b, g, 0)),
            bg((1, WIN_PAD + S, LANES), lambda b, g, i: (b, 0, g)),
            bg((1, DH, WIN_PAD + S), lambda b, g, i: (b, g, 0)),
            bg((1, 1, ncmp, LANES), lambda b, g, i: (b, g, 0, 0)),
            bg((1, 1, DH, ncmp), lambda b, g, i: (b, g, 0, 0)),
            bg((1, QT, LANES), lambda b, g, i: (b, i, 0)),
            bg((1, 2 * QT, nql), lambda b, g, i: (g, 0, 0)),
            bg((1, 2 * ncmp, nql), lambda b, g, i: (g, 0, 0)),
            bg((QT, nql), lambda b, g, i: (0, 0)),
            bg((nslc, ncmp), lambda b, g, i: (0, 0)),
            bg((QT, QT), lambda b, g, i: (0, 0)),
            bg((QT, nql), lambda b, g, i: (0, 0)),
        ],
        out_specs=pl.BlockSpec((1, QT, HPG * DH), lambda b, g, i: (b, i, g)),
        scratch_shapes=[pltpu.VMEM((2, nql, LANES), BF16), pltpu.VMEM((2 * QT, nql), F32),
                        pltpu.VMEM((2 * QT, nql), F32)],
        compiler_params=pltpu.CompilerParams(dimension_semantics=("parallel", "parallel", "arbitrary"),
                                             vmem_limit_bytes=VMEM_LIMIT),
        name="nsa",
    )(q, kslc_p, vtslc_p, kwin_p, vtwin_p, kcb, vcbt, misc, near, cmpb, wmask, ovt, eye, eye4)


def _gla_kernel(qk_ref, v_ref, vt_ref, misc_ref, r_ref, wal_ref, bal_ref, ng_ref, cum_ref, out_ref,
                state_s, o_s, *, ct):
    H, dk, dv, C = GLA_HEADS, GLA_DK, GLA_DV, GLA_CHUNK
    kw = H * dk

    @pl.when(pl.program_id(1) == 0)
    def _():
        state_s[...] = jnp.zeros_like(state_s)

    z = _nn(misc_ref[0].astype(BF16), wal_ref[...]) + bal_ref[...]
    log_a = (jnp.minimum(z, 0.0) - jnp.log1p(jnp.exp(-jnp.abs(z)))) * (1.0 / GLA_TAU)
    cum = cum_ref[...]
    a1, a2, a3 = _split3(log_a)
    cs = _nn(cum, a1) + _nn(cum, a2) + _nn(cum, a3)
    bc, bl = cs[:ct], cs[ct:]
    q = qk_ref[0, :, :kw]
    k = qk_ref[0, :, kw:]
    q_in = (q * (dk ** -0.5)) * jnp.exp(bc)
    k_in = (k * jnp.exp(-bc)).astype(BF16)
    k_st = k * jnp.exp(bl - bc)
    decay = jnp.exp(bl)
    lane_head = lax.broadcasted_iota(jnp.int32, (C, kw), 1) // dk
    rr = lax.broadcasted_iota(jnp.int32, (H * C, C), 0) % C
    cc = lax.broadcasted_iota(jnp.int32, (H * C, C), 1)
    causal = rr >= cc
    pair_row = lax.broadcasted_iota(jnp.int32, (2 * C, kw), 0) // C
    pair_head = lax.broadcasted_iota(jnp.int32, (2 * C, kw), 1) // dk

    for c in range(ct // C):
        r0 = c * C
        qc = q_in[r0:r0 + C]
        qcb = qc.astype(BF16)
        q_heads = jnp.concatenate([jnp.where(lane_head == h, qc, 0.0) for h in range(H)], axis=0).astype(BF16)
        attn = jnp.where(causal, _nt(q_heads, k_in[r0:r0 + C]), 0.0).astype(BF16)
        p0 = (c // 2) * 2 * C
        kst_pair = k_st[p0:p0 + 2 * C]
        dec = decay[r0:r0 + 1]
        for h in range(H):
            st = state_s[h]
            o = _nn(attn[h * C:(h + 1) * C], v_ref[0, r0:r0 + C, h * dv:(h + 1) * dv])
            o = o + _nt(qcb, st.astype(BF16))
            o_s[r0:r0 + C, h * dv:(h + 1) * dv] = o
            kst_h = jnp.where((pair_row == c % 2) & (pair_head == h), kst_pair, 0.0).astype(BF16)
            state_s[h] = st * dec + _nn(vt_ref[0, h * dv:(h + 1) * dv, p0:p0 + 2 * C], kst_h)

    for h in range(H):
        oh = o_s[:, h * dv:(h + 1) * dv]
        ms = jnp.mean(oh * oh, axis=-1, keepdims=True)
        r = r_ref[0, :, h * dv:(h + 1) * dv]
        y = oh * lax.rsqrt(ms + EPS) * ng_ref[:, h * dv:(h + 1) * dv] * (r * jax.nn.sigmoid(r))
        out_ref[0, :, h * dv:(h + 1) * dv] = y.astype(BF16)


def _gla(qkb, vb, vtb, misc, rb, w_alpha, b_alpha, norm_g, ct=512):
    B, S, _ = qkb.shape
    H, dk, dv, C = GLA_HEADS, GLA_DK, GLA_DV, GLA_CHUNK
    kw, vw = H * dk, H * dv
    wal = jnp.zeros((LANES, kw), F32).at[32:32 + GLA_RANK].set(w_alpha).astype(BF16)
    r = np.arange(ct)
    tri = (r[:, None] // C == r[None, :] // C) & (r[:, None] >= r[None, :])
    tot = r[:, None] // C == r[None, :] // C
    cum = jnp.asarray(np.concatenate([tri, tot], axis=0).astype(np.float32), BF16)
    full = lambda shp: pl.BlockSpec(shp, lambda b, i: (0,) * len(shp))
    return pl.pallas_call(
        functools.partial(_gla_kernel, ct=ct),
        out_shape=jax.ShapeDtypeStruct((B, S, vw), BF16),
        grid=(B, S // ct),
        in_specs=[
            pl.BlockSpec((1, ct, 2 * kw), lambda b, i: (b, i, 0)),
            pl.BlockSpec((1, ct, vw), lambda b, i: (b, i, 0)),
            pl.BlockSpec((1, vw, ct), lambda b, i: (b, 0, i)),
            pl.BlockSpec((1, ct, LANES), lambda b, i: (b, i, 0)),
            pl.BlockSpec((1, ct, vw), lambda b, i: (b, i, 0)),
            full((LANES, kw)), full((1, kw)), full((1, vw)), full((2 * ct, ct)),
        ],
        out_specs=pl.BlockSpec((1, ct, vw), lambda b, i: (b, i, 0)),
        scratch_shapes=[pltpu.VMEM((H, dv, kw), F32), pltpu.VMEM((ct, vw), F32)],
        compiler_params=pltpu.CompilerParams(dimension_semantics=("parallel", "arbitrary"),
                                             vmem_limit_bytes=VMEM_LIMIT),
        name="gla",
    )(qkb, vb, vtb, misc, rb, wal, b_alpha.reshape(1, kw), norm_g.reshape(1, vw), cum)


ROUTER_ROWS = 128
EXPERT_ROW0 = 8


def _outproj_kernel(ya_ref, yb_ref, mg_ref, x_ref, wa_ref, wb_ref, wo_ref, gf_ref, wrh_ref, wrl_ref, br_ref,
                    tri_ref, x1_ref, xw_ref, ri_ref, rw_ref, cnt_ref, carry_s, *, tm):
    D = x_ref.shape[1]

    @pl.when(pl.program_id(0) == 0)
    def _():
        carry_s[...] = jnp.zeros_like(carry_s)

    ma = _nn(ya_ref[...], wa_ref[...])
    mb = _nn(yb_ref[...], wb_ref[...])
    merged = jax.nn.sigmoid(mg_ref[:, :D]) * ma + jax.nn.sigmoid(mg_ref[:, D:]) * mb
    x1 = x_ref[...] + _nn(merged.astype(BF16), wo_ref[...])
    x1_ref[...] = x1
    ms = jnp.mean(x1 * x1, axis=-1, keepdims=True)
    h2 = x1 * lax.rsqrt(ms + EPS) * gf_ref[...]
    hi = h2.astype(BF16)
    hif = hi.astype(F32)
    lo = (h2 - hif).astype(BF16)
    u = pltpu.bitcast(hif, jnp.uint32)
    xw_ref[...] = u[:, :D // 2] | (u[:, D // 2:] >> 16)

    lg = _nt(wrh_ref[...], hi) + _nt(wrh_ref[...], lo) + _nt(wrl_ref[...], hi) + br_ref[...]
    row8 = lax.broadcasted_iota(jnp.int32, (SUBLANES, tm), 0)
    gl = jnp.where(row8 < N_GROUPS, lg[0:SUBLANES], NEG)
    gmax = jnp.max(gl, axis=0, keepdims=True)
    g_sel = jnp.min(jnp.where(gl == gmax, row8, SUBLANES), axis=0, keepdims=True)
    g_prob = 1.0 / jnp.sum(jnp.where(row8 < N_GROUPS, jnp.exp(gl - gmax), 0.0), axis=0, keepdims=True)
    e_sel = jnp.zeros((EPG, tm), F32)
    for gi in range(N_GROUPS):
        r0 = EXPERT_ROW0 + gi * EPG
        e_sel = e_sel + jnp.where(g_sel == gi, lg[r0:r0 + EPG], 0.0)
    v1 = jnp.max(e_sel, axis=0, keepdims=True)
    i1 = jnp.min(jnp.where(e_sel == v1, row8, EPG), axis=0, keepdims=True)
    rest = jnp.where(row8 == i1, -jnp.inf, e_sel)
    v2 = jnp.max(rest, axis=0, keepdims=True)
    i2 = jnp.min(jnp.where(rest == v2, row8, EPG), axis=0, keepdims=True)
    t = jnp.exp(v2 - v1)
    w1 = g_prob / (1.0 + t)
    w2 = g_prob * t / (1.0 + t)
    e1 = g_sel * EPG + i1
    e2 = g_sel * EPG + i2

    rowe = lax.broadcasted_iota(jnp.int32, (N_EXPERTS, tm), 0)
    oh1 = rowe == e1
    oh2 = rowe == e2
    oh = jnp.where(oh1, 1.0, 0.0) + jnp.where(oh2, 1.0, 0.0)
    pre = _nn(oh.astype(BF16), tri_ref[...]) + carry_s[:, 0:1]
    rank1 = jnp.sum(jnp.where(oh1, pre, 0.0), axis=0, keepdims=True)
    rank2 = jnp.sum(jnp.where(oh2, pre, 0.0), axis=0, keepdims=True)
    carry_s[...] = carry_s[...] + jnp.sum(oh, axis=1, keepdims=True)
    cnt_ref[...] = carry_s[...].astype(jnp.int32)

    zi = jnp.zeros((SUBLANES - 4, tm), jnp.int32)
    ri_ref[0] = jnp.concatenate([e1, e2, rank1.astype(jnp.int32), rank2.astype(jnp.int32), zi], axis=0)
    rw_ref[0] = jnp.concatenate([w1, w2, jnp.zeros((SUBLANES - 2, tm), F32)], axis=0)


def _outproj(ya, yb, mg, x, wa, wb, wo, g_ffn, w_rg, b_rg, w_re, b_re, tm=MOE_TM):
    T, D = x.shape
    nt = T // tm
    wr = jnp.zeros((ROUTER_ROWS, D), F32).at[0:N_GROUPS].set(w_rg.T).at[EXPERT_ROW0:EXPERT_ROW0 + N_EXPERTS].set(w_re.T)
    wrh = wr.astype(BF16)
    wrl = (wr - wrh.astype(F32)).astype(BF16)
    br = jnp.zeros((ROUTER_ROWS, 1), F32).at[0:N_GROUPS, 0].set(b_rg).at[EXPERT_ROW0:EXPERT_ROW0 + N_EXPERTS, 0].set(b_re)
    r = np.arange(tm)
    tri = jnp.asarray((r[:, None] < r[None, :]).astype(np.float32), BF16)
    row = lambda n: pl.BlockSpec((tm, n), lambda i: (i, 0))
    full = lambda shp: pl.BlockSpec(shp, lambda i: (0,) * len(shp))
    return pl.pallas_call(
        functools.partial(_outproj_kernel, tm=tm),
        out_shape=[jax.ShapeDtypeStruct((T, D), F32), jax.ShapeDtypeStruct((T, D // 2), jnp.uint32),
                   jax.ShapeDtypeStruct((nt, SUBLANES, tm), jnp.int32), jax.ShapeDtypeStruct((nt, SUBLANES, tm), F32),
                   jax.ShapeDtypeStruct((N_EXPERTS, LANES), jnp.int32)],
        grid=(nt,),
        in_specs=[row(ya.shape[1]), row(yb.shape[1]), row(2 * D), row(D),
                  full(wa.shape), full(wb.shape), full(wo.shape), full((1, D)),
                  full((ROUTER_ROWS, D)), full((ROUTER_ROWS, D)), full((ROUTER_ROWS, 1)), full((tm, tm))],
        out_specs=[row(D), row(D // 2),
                   pl.BlockSpec((1, SUBLANES, tm), lambda i: (i, 0, 0)),
                   pl.BlockSpec((1, SUBLANES, tm), lambda i: (i, 0, 0)),
                   full((N_EXPERTS, LANES))],
        scratch_shapes=[pltpu.VMEM((N_EXPERTS, LANES), F32)],
        compiler_params=pltpu.CompilerParams(dimension_semantics=("arbitrary",), vmem_limit_bytes=VMEM_LIMIT),
        name="outproj",
    )(ya, yb, mg, x, wa, wb, wo, g_ffn.reshape(1, D), wrh, wrl, br, tri)


def _dispatch_kernel(pos_ref, xw_ref, zero_ref, xs_ref, sem, *, tm):
    del zero_ref

    def row_copy(r, k):
        return pltpu.make_async_copy(xw_ref.at[pl.ds(r, 1), :], xs_ref.at[pl.ds(pos_ref[0, k, r], 1), :], sem)

    def start(r, c):
        row_copy(r, 0).start()
        row_copy(r, 1).start()
        return c

    lax.fori_loop(0, tm, start, 0)

    def wait(r, c):
        row_copy(r, 0).wait()
        row_copy(r, 1).wait()
        return c

    lax.fori_loop(0, tm, wait, 0)


def _dispatch(pos, xw, n_rows, tm=MOE_TM):
    T, W = xw.shape
    nt = T // tm
    zeros = jnp.zeros((n_rows, W), xw.dtype)
    return pl.pallas_call(
        functools.partial(_dispatch_kernel, tm=tm),
        out_shape=jax.ShapeDtypeStruct((n_rows, W), xw.dtype),
        grid=(nt,),
        in_specs=[pl.BlockSpec((1, 2, tm), lambda i: (i, 0, 0), memory_space=pltpu.SMEM),
                  pl.BlockSpec((tm, W), lambda i: (i, 0)),
                  pl.BlockSpec(memory_space=pl.ANY)],
        out_specs=pl.BlockSpec(memory_space=pl.ANY),
        scratch_shapes=[pltpu.SemaphoreType.DMA(())],
        input_output_aliases={2: 0},
        compiler_params=pltpu.CompilerParams(dimension_semantics=("arbitrary",)),
        name="dispatch",
    )(pos, xw, zeros)


def _experts_kernel(te_ref, nv_ref, xs_ref, wg_ref, wu_ref, wd_ref, out_ref):
    i = pl.program_id(0)

    @pl.when(i < nv_ref[0])
    def _():
        w = xs_ref[...]
        xh = pltpu.bitcast(w & jnp.uint32(0xFFFF0000), F32)
        xl = pltpu.bitcast(w << 16, F32)
        x = jnp.concatenate([xh, xl], axis=1).astype(BF16)
        a = _nn(x, wg_ref[0].astype(BF16))
        u = _nn(x, wu_ref[0].astype(BF16))
        hid = (a * jax.nn.sigmoid(a)) * u
        out_ref[...] = _nn(hid.astype(BF16), wd_ref[0].astype(BF16))

    @pl.when(i >= nv_ref[0])
    def _():
        out_ref[...] = jnp.zeros_like(out_ref)


def _experts(tile_expert, n_valid, xs, w_gate, w_up, w_down):
    n_rows, W = xs.shape
    D = 2 * W
    n_tiles = n_rows // TE
    last = lambda i, nv: jnp.minimum(i, nv[0] - 1)
    return pl.pallas_call(
        _experts_kernel,
        out_shape=jax.ShapeDtypeStruct((n_rows, D), F32),
        grid_spec=pltpu.PrefetchScalarGridSpec(
            num_scalar_prefetch=2,
            grid=(n_tiles,),
            in_specs=[pl.BlockSpec((TE, W), lambda i, te, nv: (last(i, nv), 0)),
                      pl.BlockSpec((1, D, EXPERT_FF), lambda i, te, nv: (te[last(i, nv)], 0, 0)),
                      pl.BlockSpec((1, D, EXPERT_FF), lambda i, te, nv: (te[last(i, nv)], 0, 0)),
                      pl.BlockSpec((1, EXPERT_FF, D), lambda i, te, nv: (te[last(i, nv)], 0, 0))],
            out_specs=pl.BlockSpec((TE, D), lambda i, te, nv: (i, 0)),
        ),
        compiler_params=pltpu.CompilerParams(dimension_semantics=("arbitrary",), vmem_limit_bytes=VMEM_LIMIT),
        name="experts",
    )(tile_expert, n_valid, xs, w_gate, w_up, w_down)


def _combine_kernel(pos_ref, ys_ref, x1_ref, rw_ref, gfin_ref, out_ref, buf, sem, *, tm, apply_norm):
    def row_copy(r, k):
        return pltpu.make_async_copy(ys_ref.at[pl.ds(pos_ref[0, k, r], 1), :], buf.at[k, pl.ds(r, 1), :], sem)

    def start(r, c):
        row_copy(r, 0).start()
        row_copy(r, 1).start()
        return c

    lax.fori_loop(0, tm, start, 0)

    def wait(r, c):
        row_copy(r, 0).wait()
        row_copy(r, 1).wait()
        return c

    lax.fori_loop(0, tm, wait, 0)
    wts = jnp.concatenate([rw_ref[0], jnp.zeros((LANES - SUBLANES, tm), F32)], axis=0).T
    y = x1_ref[...] + wts[:, 0:1] * buf[0] + wts[:, 1:2] * buf[1]
    if apply_norm:
        ms = jnp.mean(y * y, axis=-1, keepdims=True)
        y = y * lax.rsqrt(ms + EPS) * gfin_ref[...]
    out_ref[...] = y


def _combine(pos, ys, x1, rw, g_final, apply_norm, tm=MOE_TM):
    T, D = x1.shape
    nt = T // tm
    return pl.pallas_call(
        functools.partial(_combine_kernel, tm=tm, apply_norm=apply_norm),
        out_shape=jax.ShapeDtypeStruct((T, D), F32),
        grid=(nt,),
        in_specs=[pl.BlockSpec((1, 2, tm), lambda i: (i, 0, 0), memory_space=pltpu.SMEM),
                  pl.BlockSpec(memory_space=pl.ANY),
                  pl.BlockSpec((tm, D), lambda i: (i, 0)),
                  pl.BlockSpec((1, SUBLANES, tm), lambda i: (i, 0, 0)),
                  pl.BlockSpec((1, D), lambda i: (0, 0))],
        out_specs=pl.BlockSpec((tm, D), lambda i: (i, 0)),
        scratch_shapes=[pltpu.VMEM((2, tm, D), F32), pltpu.SemaphoreType.DMA(())],
        compiler_params=pltpu.CompilerParams(dimension_semantics=("arbitrary",), vmem_limit_bytes=VMEM_LIMIT),
        name="combine",
    )(pos, ys, x1, rw, g_final.reshape(1, D))


def _moe_plan(ri, counts, T):
    n_tiles_max = (2 * T) // TE + N_EXPERTS
    cnt = counts[:, 0]
    tiles = (cnt + TE - 1) // TE
    ids = jnp.arange(N_EXPERTS)
    tile_end = jnp.sum(jnp.where(ids[None, :] <= ids[:, None], tiles[None, :], 0), axis=1)
    row0 = (tile_end - tiles) * TE
    e = ri[:, 0:2, :]
    pos = jnp.sum(jnp.where(e[..., None] == ids, row0, 0), axis=-1) + ri[:, 2:4, :]
    tile_expert = jnp.minimum(jnp.sum(tile_end[None, :] <= jnp.arange(n_tiles_max)[:, None], axis=1),
                              N_EXPERTS - 1).astype(jnp.int32)
    return pos.astype(jnp.int32), tile_expert, tile_end[-1:].astype(jnp.int32), n_tiles_max * TE


def kernel(x, g_mix, w_in, nsa_pe_k, nsa_cmp_k_w1, nsa_cmp_k_w2, nsa_pe_v, nsa_cmp_v_w1, nsa_cmp_v_w2, rel_bias,
           gla_w_alpha, gla_b_alpha, gla_norm_g, w_branch_a, w_branch_b, w_out, g_ffn, w_router_group,
           b_router_group, w_router_expert, b_router_expert, w_exp_gate, w_exp_up, w_exp_down, g_final):
    B, S, D = x.shape
    T = B * S
    for l in range(w_in.shape[0]):
        w, wt = _pack_inproj_weights(w_in[l])
        (q, kslc, kwin, kvc, misc, qkb, vb, rb, mg, vts, vtw, vtb) = _inproj(x, g_mix[l], w, wt)
        kcb, vcbt = _compress(kvc, nsa_pe_k[l], nsa_cmp_k_w1[l], nsa_cmp_k_w2[l],
                              nsa_pe_v[l], nsa_cmp_v_w1[l], nsa_cmp_v_w2[l])
        ya = _nsa(q, kslc, vts, kwin, vtw, kcb, vcbt, misc, rel_bias)
        yb = _gla(qkb, vb, vtb, misc, rb, gla_w_alpha[l], gla_b_alpha[l], gla_norm_g[l])
        x1, xw, ri, rw, counts = _outproj(
            ya.reshape(T, -1), yb.reshape(T, -1), mg.reshape(T, -1), x.reshape(T, D),
            w_branch_a[l].astype(BF16), w_branch_b[l].astype(BF16), w_out[l].astype(BF16), g_ffn[l],
            w_router_group[l], b_router_group[l], w_router_expert[l], b_router_expert[l])
        pos, tile_expert, n_valid, n_rows = _moe_plan(ri, counts, T)
        xs = _dispatch(pos, xw, n_rows)
        ys = _experts(tile_expert, n_valid, xs, w_exp_gate[l], w_exp_up[l], w_exp_down[l])
        last_layer = l == w_in.shape[0] - 1
        x = _combine(pos, ys, x1, rw, g_final, apply_norm=last_layer).reshape(B, S, D)
    return x
```

```python
import functools
import math

import numpy as np
import jax
import jax.numpy as jnp
from jax import lax
from jax.experimental import pallas as pl
from jax.experimental.pallas import tpu as pltpu

F32 = jnp.float32
BF16 = jnp.bfloat16

NSA_HEADS = 8
NSA_GROUPS = 2
HPG = NSA_HEADS // NSA_GROUPS
DH = 64
CMP_BLOCK = 32
CMP_STRIDE = 16
CMP_HIDDEN = 128
SLC_BLOCK = 64
SLC_TOPN = 16
WINDOW = 512
GLA_HEADS = 4
GLA_DK = 64
GLA_DV = 128
GLA_RANK = 16
GLA_TAU = 16.0
GLA_CHUNK = 64
REL_BUCKETS = 32
REL_MAX_EXACT = REL_BUCKETS // 2
REL_MAX_DIST = 128
N_GROUPS = 4
EPG = 8
N_EXPERTS = N_GROUPS * EPG
EXPERT_FF = 256
EPS = 1e-6

LANES = 128
SUBLANES = 8
VMEM_LIMIT = 56 * 1024 * 1024

LOG2E = math.log2(math.e)
NEG = -1e30
BIG = float(2.0 ** 100)
QT = 128
SLC_PAD = 128
WIN_PAD = 512
TE = 256
MOE_TM = 512


def _nt(a, b):
    return lax.dot_general(a, b, (((1,), (1,)), ((), ())), preferred_element_type=F32)


def _nn(a, b):
    return jnp.dot(a, b, preferred_element_type=F32)


def _split3(x):
    a = x.astype(BF16)
    r = x - a.astype(F32)
    b = r.astype(BF16)
    c = (r - b.astype(F32)).astype(BF16)
    return a, b, c


def _t5_bucket_np(rel):
    n = np.maximum(rel, 0)
    nf = np.maximum(n, 1).astype(np.float32)
    large = REL_MAX_EXACT + (np.log(nf / np.float32(REL_MAX_EXACT)) / np.float32(math.log(REL_MAX_DIST / REL_MAX_EXACT))
                             * np.float32(REL_BUCKETS - REL_MAX_EXACT)).astype(np.int32)
    return np.where(n < REL_MAX_EXACT, n, np.minimum(large, REL_BUCKETS - 1)).astype(np.int32)


def _inproj_kernel(x_ref, g_ref, w_ref, wt_ref, oq, okslc, okwin, okv, omisc, oqkb, ovb, orb, omg,
                   ovts, ovtw, ovtb, *, tm, seq):
    x = x_ref[0]
    ms = jnp.mean(x * x, axis=-1, keepdims=True)
    h = (x * lax.rsqrt(ms + EPS) * g_ref[...]).astype(BF16)

    def mm(a, b):
        return _nn(h, w_ref[:, a:b])

    oq[0] = mm(0, 1024).astype(BF16)
    ks = mm(1024, 1280)
    row = lax.broadcasted_iota(jnp.int32, (tm, 256), 0) + pl.program_id(1) * tm
    lane = lax.broadcasted_iota(jnp.int32, (tm, 256), 1) % LANES
    onehot = jnp.where(lane - DH == row // SLC_BLOCK, 1.0, 0.0)
    okslc[0] = (ks + onehot).astype(BF16)
    okwin[0] = mm(1280, 1536).astype(BF16)
    okv[0] = mm(1536, 1792)
    omisc[0] = mm(1792, 1920)
    oqkb[0] = mm(1920, 2432)
    ovb[0] = mm(2432, 2944).astype(BF16)
    orb[0] = mm(2944, 3456)
    omg[0] = mm(3456, 5504)
    vt = _nt(wt_ref[...], h)
    ovts[0] = vt[0:128].astype(BF16)
    ovtw[0] = vt[128:256].astype(BF16)
    ovtb[0] = vt[256:768].astype(BF16)


def _pack_inproj_weights(w_in):
    o = np.cumsum([0, 512, 128, 128, 128, 128, 128, 128, 24, 256, 256, 512, 16, 512, 1024, 1024])
    (q_a, k_cmp, v_cmp, k_slc, v_slc, k_win, v_win, gate_a, q_b, k_b, v_b, a_b, r_b, mg_a, mg_b) = [
        w_in[:, o[i]:o[i + 1]] for i in range(15)]
    D = w_in.shape[0]
    z64 = jnp.zeros((D, DH), w_in.dtype)
    qcols = []
    for hd in range(NSA_HEADS):
        qcols += [q_a[:, hd * DH:(hd + 1) * DH] * (DH ** -0.5 * LOG2E), z64]
    kslc = [k_slc[:, :DH], z64, k_slc[:, DH:], z64]
    kwin = [k_win[:, :DH], z64, k_win[:, DH:], z64]
    ga = gate_a.reshape(D, 3, NSA_GROUPS, HPG)
    z4 = jnp.zeros((D, 4), w_in.dtype)
    misc = [ga[:, :, 0, :].reshape(D, 12), z4, ga[:, :, 1, :].reshape(D, 12), z4, a_b,
            jnp.zeros((D, LANES - 48), w_in.dtype)]
    w = jnp.concatenate(qcols + kslc + kwin + [k_cmp, v_cmp] + misc + [q_b, k_b, v_b, r_b, mg_a, mg_b], axis=1)
    wt = jnp.concatenate([v_slc, v_win, v_b], axis=1).T
    return w.astype(BF16), wt.astype(BF16)


def _inproj(x, g_mix, w, wt, tm=512):
    B, S, D = x.shape
    nw = w.shape[1]
    widths = [(1024, BF16), (256, BF16), (256, BF16), (256, F32), (128, F32), (512, F32), (512, BF16),
              (512, F32), (2048, F32)]
    out_shape = [jax.ShapeDtypeStruct((B, S, n), dt) for n, dt in widths]
    out_specs = [pl.BlockSpec((1, tm, n), lambda b, i: (b, i, 0)) for n, _ in widths]
    for rows in (128, 128, 512):
        out_shape.append(jax.ShapeDtypeStruct((B, rows, S), BF16))
        out_specs.append(pl.BlockSpec((1, rows, tm), lambda b, i: (b, 0, i)))
    return pl.pallas_call(
        functools.partial(_inproj_kernel, tm=tm, seq=S),
        out_shape=out_shape,
        grid=(B, S // tm),
        in_specs=[
            pl.BlockSpec((1, tm, D), lambda b, i: (b, i, 0)),
            pl.BlockSpec((1, D), lambda b, i: (0, 0)),
            pl.BlockSpec((D, nw), lambda b, i: (0, 0), pipeline_mode=pl.Buffered(1)),
            pl.BlockSpec((768, D), lambda b, i: (0, 0), pipeline_mode=pl.Buffered(1)),
        ],
        out_specs=out_specs,
        compiler_params=pltpu.CompilerParams(dimension_semantics=("parallel", "parallel"),
                                             vmem_limit_bytes=VMEM_LIMIT),
        name="inproj",
    )(x, g_mix.reshape(1, D), w, wt)


def _gelu_tanh(x):
    return 0.5 * x * (1.0 + jnp.tanh(math.sqrt(2.0 / math.pi) * (x + 0.044715 * (x * x * x))))


def _compress_kernel(xk_ref, xv_ref, pe_ref, w1_ref, w2k_ref, w2vt_ref, ok_ref, ovt_ref, *, nsub):
    for kind, x_ref in enumerate((xk_ref, xv_ref)):
        top = jnp.zeros((nsub, 2 * CMP_HIDDEN), F32)
        bot = jnp.zeros((nsub, 2 * CMP_HIDDEN), F32)
        for r in range(CMP_STRIDE):
            xr = x_ref[0, pl.ds(r, nsub, stride=CMP_STRIDE), :]
            top = top + _nn((xr + pe_ref[kind, 0, r:r + 1, :]).astype(BF16), w1_ref[kind, 0, r])
            bot = bot + _nn((xr + pe_ref[kind, 1, r:r + 1, :]).astype(BF16), w1_ref[kind, 1, r])
        hid = _gelu_tanh(top + pltpu.roll(bot, shift=nsub - 1, axis=0)).astype(BF16)
        for g in range(NSA_GROUPS):
            hg = hid[:, g * CMP_HIDDEN:(g + 1) * CMP_HIDDEN]
            if kind == 0:
                ok_ref[0, g] = _nn(hg, w2k_ref[...]).astype(BF16)
            else:
                ovt_ref[0, g] = _nt(w2vt_ref[...], hg).astype(BF16)


def _compress(kv_cmp, pe_k, w1k, w2k, pe_v, w1v, w2v):
    B, S, _ = kv_cmp.shape
    nsub = S // CMP_STRIDE
    G = NSA_GROUPS

    def prep(pe, w1):
        pe_t = jnp.tile(pe.reshape(2, CMP_STRIDE, DH), (1, 1, G))
        a = w1.reshape(2, CMP_STRIDE, DH, CMP_HIDDEN)
        z = jnp.zeros_like(a)
        w = jnp.concatenate([jnp.concatenate([a, z], axis=3), jnp.concatenate([z, a], axis=3)], axis=2)
        return pe_t, w.astype(BF16)

    pek, w1kb = prep(pe_k, w1k)
    pev, w1vb = prep(pe_v, w1v)
    pe = jnp.stack([pek, pev])
    w1 = jnp.stack([w1kb, w1vb])
    w2kp = jnp.concatenate([w2k, jnp.zeros_like(w2k)], axis=1).astype(BF16)
    w2vt = w2v.T.astype(BF16)
    full = lambda shp: pl.BlockSpec(shp, lambda b: (0,) * len(shp))
    return pl.pallas_call(
        functools.partial(_compress_kernel, nsub=nsub),
        out_shape=[jax.ShapeDtypeStruct((B, G, nsub, LANES), BF16),
                   jax.ShapeDtypeStruct((B, G, DH, nsub), BF16)],
        grid=(B,),
        in_specs=[pl.BlockSpec((1, S, G * DH), lambda b: (b, 0, 0)), pl.BlockSpec((1, S, G * DH), lambda b: (b, 0, 1)),
                  full(pe.shape), full(w1.shape), full((CMP_HIDDEN, LANES)), full((DH, CMP_HIDDEN))],
        out_specs=[pl.BlockSpec((1, G, nsub, LANES), lambda b: (b, 0, 0, 0)),
                   pl.BlockSpec((1, G, DH, nsub), lambda b: (b, 0, 0, 0))],
        compiler_params=pltpu.CompilerParams(dimension_semantics=("parallel",), vmem_limit_bytes=VMEM_LIMIT),
        name="compress",
    )(kv_cmp, kv_cmp, pe, w1, w2kp, w2vt)


def _bias_kernel(tbl_ref, bkn_ref, bkc_ref, near_ref, cmpb_ref):
    g = pl.program_id(0)
    for h in range(HPG):
        hd = g * HPG + h

        def lookup(bk):
            acc = jnp.full(bk.shape, NEG, F32)
            for b in range(REL_BUCKETS):
                acc = jnp.where(bk == b, tbl_ref[hd, b], acc)
            return acc

        vn = lookup(bkn_ref[...])
        near_ref[0, :, h * QT:(h + 1) * QT] = jnp.where(vn > 0.5 * NEG, (vn - tbl_ref[hd, REL_BUCKETS - 1]) * LOG2E, NEG)
        vc = lookup(bkc_ref[...])
        cmpb_ref[0, :, h * QT:(h + 1) * QT] = jnp.where(vc > 0.5 * NEG, vc * LOG2E, NEG)


def _nsa_bias_tables(rel_bias, seq):
    ql = np.arange(QT)
    ncmp = seq // CMP_STRIDE

    def buckets(rel):
        return jnp.asarray(np.where(rel >= 0, _t5_bucket_np(rel), -1).astype(np.int32))

    bkn = buckets(ql[None, :] + QT - np.arange(2 * QT)[:, None])
    y = np.arange(2 * ncmp)
    bkc = buckets(ql[None, :] - CMP_STRIDE * (y[:, None] - ncmp) - (CMP_BLOCK - 1))
    nql = HPG * QT
    return pl.pallas_call(
        _bias_kernel,
        out_shape=[jax.ShapeDtypeStruct((NSA_GROUPS, 2 * QT, nql), F32),
                   jax.ShapeDtypeStruct((NSA_GROUPS, 2 * ncmp, nql), F32)],
        grid=(NSA_GROUPS,),
        in_specs=[pl.BlockSpec(memory_space=pltpu.SMEM),
                  pl.BlockSpec((2 * QT, QT), lambda g: (0, 0)),
                  pl.BlockSpec((2 * ncmp, QT), lambda g: (0, 0))],
        out_specs=[pl.BlockSpec((1, 2 * QT, nql), lambda g: (g, 0, 0)),
                   pl.BlockSpec((1, 2 * ncmp, nql), lambda g: (g, 0, 0))],
        compiler_params=pltpu.CompilerParams(dimension_semantics=("parallel",)),
        name="t5bias",
    )(rel_bias.T, bkn, bkc)


def _nsa_kernel(q_ref, kslc_ref, vtslc_ref, kwin_ref, vtwin_ref, kcb_ref, vcbt_ref, misc_ref,
                near_ref, cmpb_ref, wmask_ref, ovt_ref, eye_ref, eye4_ref,
                out_ref, qaug_s, s0_s, s1_s, *, ncmp, nslc):
    G = NSA_GROUPS
    qt = pl.program_id(1)
    nql = HPG * QT
    ck = 2 * QT

    def flash(carry, s, vt_chunk):
        m, l, acc = carry
        m_new = jnp.maximum(m, jnp.max(s, axis=0, keepdims=True))
        alpha = jnp.exp2(m - m_new)
        p = jnp.exp2(s - m_new)
        l = alpha * l + jnp.sum(p, axis=0, keepdims=True)
        acc = alpha * acc + _nn(vt_chunk, p.astype(BF16))
        return m_new, l, acc

    init = (jnp.full((1, nql), NEG, F32), jnp.zeros((1, nql), F32), jnp.zeros((DH, nql), F32))
    ns = pl.multiple_of(QT * qt, QT)
    nw = pl.multiple_of(QT * qt + 3 * QT, QT)
    off = pl.multiple_of(ncmp - (QT // CMP_STRIDE) * qt, SUBLANES)
    lane = lax.broadcasted_iota(jnp.int32, (nql, LANES), 1)
    win_aug = jnp.where(lane >= DH, -BIG, 0.0).astype(BF16)
    jidx = lax.broadcasted_iota(jnp.int32, (nslc, QT), 0)
    tq = qt * QT + lax.broadcasted_iota(jnp.int32, (nslc, QT), 1)
    tb = tq // SLC_BLOCK
    forced = (jidx == 0) | (jidx == tb) | (jidx == tb - 1)
    future = jidx * SLC_BLOCK > tq
    sub = lax.broadcasted_iota(jnp.int32, (SUBLANES, QT), 0)
    ones_lo = jnp.ones((DH, QT), F32)
    eye = eye_ref[...]
    ovt = ovt_ref[...]
    ngrp = nslc // SUBLANES

    def aug(sel01):
        rows = [ones_lo, sel01]
        if LANES - DH - nslc:
            rows.append(jnp.ones((LANES - DH - nslc, QT), F32))
        m01 = _nt(eye, jnp.concatenate(rows, axis=0).astype(BF16))
        return jnp.concatenate([((m01 - 1.0) * BIG).astype(BF16)] * HPG, axis=0)

    o_cw = []
    for g in range(G):
        kl = slice(g * LANES, (g + 1) * LANES)
        vr = slice(g * DH, (g + 1) * DH)
        q0 = jnp.concatenate([q_ref[0, :, (g * HPG + h) * LANES:(g * HPG + h + 1) * LANES] for h in range(HPG)],
                             axis=0)
        near_b = near_ref[g]

        qwin = q0 + win_aug
        sw = _nt(kwin_ref[0, pl.ds(ns, 3 * QT), kl], qwin)
        sw = jnp.concatenate([sw[:QT] + wmask_ref[...], sw[QT:]], axis=0)
        carry = flash(init, sw, vtwin_ref[0, vr, pl.ds(ns, 3 * QT)])
        sw = _nt(kwin_ref[0, pl.ds(nw, ck), kl], qwin) + near_b
        m, l, acc = flash(carry, sw, vtwin_ref[0, vr, pl.ds(nw, ck)])
        o_w = acc * (1.0 / l)

        bc = cmpb_ref[g, pl.ds(off, ncmp), :]
        sc = _nt(kcb_ref[0, g], q0) + bc
        vis = bc > 0.5 * NEG
        mc = jnp.max(sc, axis=0, keepdims=True)
        ec = jnp.where(vis, jnp.exp2(sc - mc), 0.0)
        den = jnp.maximum(jnp.sum(ec, axis=0, keepdims=True), jnp.finfo(F32).tiny)
        pc = ec * (1.0 / den)
        o_c = _nn(vcbt_ref[0, g], pc.astype(BF16))
        o_cw.append((o_c, o_w))

        psum = pc[:, 0:QT]
        for h in range(1, HPG):
            psum = psum + pc[:, h * QT:(h + 1) * QT]
        p1, p2, p3 = _split3(psum)
        imp = _nn(ovt, p1) + _nn(ovt, p2) + _nn(ovt, p3)
        imp = jnp.where(forced, 1e30, jnp.where(future, -1e30, imp))
        grp = [imp[SUBLANES * v:SUBLANES * (v + 1)] for v in range(ngrp)]
        cnt = [jnp.zeros((SUBLANES, QT), F32) for _ in range(ngrp)]
        for jp in range(nslc):
            v0, r0 = divmod(jp, SUBLANES)
            row = jnp.broadcast_to(imp[jp:jp + 1, :], (SUBLANES, QT))
            for v in range(ngrp):
                if v < v0:
                    inc = jnp.where(row > grp[v], 1.0, 0.0)
                elif v > v0:
                    inc = jnp.where(row >= grp[v], 1.0, 0.0)
                else:
                    inc = jnp.where(sub > r0, jnp.where(row >= grp[v], 1.0, 0.0), jnp.where(row > grp[v], 1.0, 0.0))
                cnt[v] = cnt[v] + inc
        sel = jnp.concatenate(cnt, axis=0) < float(min(SLC_TOPN, nslc))
        sel_near = jnp.where(sel, 1.0, 0.0)
        sel_far = jnp.where(jidx < 2 * (qt - 1), sel_near, 0.0)
        qaug_s[g, 0] = q0 + aug(sel_far)
        qaug_s[g, 1] = q0 + aug(sel_near)

    n_far = qt // 2
    n_chunks = n_far + 1

    def rows(i):
        return pl.multiple_of(QT * qt - ck * i, QT)

    def scores(g, i):
        i = jnp.minimum(i, n_far)
        return _nt(kslc_ref[0, pl.ds(rows(i), ck), g * LANES:(g + 1) * LANES], qaug_s[g, jnp.where(i == 0, 1, 0)])

    def vt(g, i):
        return vtslc_ref[0, g * DH:(g + 1) * DH, pl.ds(rows(i), ck)]

    for g in range(G):
        s0_s[g] = scores(g, 0) + near_ref[g]

    def pair_body(p, carry):
        i = 2 * p
        for g in range(G):
            s1_s[g] = scores(g, i + 1)
        carry = tuple(flash(carry[g], s0_s[g], vt(g, i)) for g in range(G))
        for g in range(G):
            s0_s[g] = scores(g, i + 2)
        return tuple(flash(carry[g], s1_s[g], vt(g, i + 1)) for g in range(G))

    carry = lax.fori_loop(0, n_chunks // 2, pair_body, (init,) * G)
    carry = lax.cond(n_chunks % 2 == 1,
                     lambda c: tuple(flash(c[g], s0_s[g], vt(g, n_far)) for g in range(G)),
                     lambda c: c, carry)

    gt = jax.nn.sigmoid(misc_ref[0]).T
    head = lax.broadcasted_iota(jnp.int32, (DH, nql), 1) // QT
    for g in range(G):
        o_c, o_w = o_cw[g]
        m, l, acc = carry[g]
        o_s = acc * (1.0 / l)

        def gate_row(br):
            r0 = 16 * g + br * HPG
            return jnp.concatenate([gt[r0 + h:r0 + h + 1, :] for h in range(HPG)], axis=1)

        o = gate_row(0) * o_c + gate_row(1) * o_s + gate_row(2) * o_w
        ob = o.astype(BF16)
        blocks = jnp.concatenate([jnp.where(head == h, ob, jnp.zeros_like(ob)) for h in range(HPG)], axis=0)
        out_ref[0, :, g * HPG * DH:(g + 1) * HPG * DH] = _nt(eye4_ref[...], blocks).astype(BF16)


def _nsa(q, kslc, vtslc, kwin, vtwin, kcb, vcbt, misc, rel_bias):
    B, S, _ = q.shape
    G = NSA_GROUPS
    nq = S // QT
    ncmp = S // CMP_STRIDE
    nslc = S // SLC_BLOCK
    nql = HPG * QT
    near, cmpb = _nsa_bias_tables(rel_bias, S)
    wmask = jnp.asarray(np.tile(np.where(np.arange(QT)[:, None] > np.arange(QT)[None, :], 0.0, NEG), (1, HPG)), F32)
    kpad_s = jnp.concatenate([jnp.zeros((DH,), BF16), jnp.ones((DH,), BF16)] * G)
    kslc_p = jnp.concatenate([jnp.broadcast_to(kpad_s, (B, SLC_PAD, G * LANES)), kslc], axis=1)
    kwin_p = jnp.concatenate([jnp.broadcast_to(kpad_s, (B, WIN_PAD, G * LANES)), kwin], axis=1)
    vtslc_p = jnp.pad(vtslc, ((0, 0), (0, 0), (SLC_PAD, 0)))
    vtwin_p = jnp.pad(vtwin, ((0, 0), (0, 0), (WIN_PAD, 0)))
    ci = np.arange(ncmp)[None, :] * CMP_STRIDE
    sj = np.arange(nslc)[:, None] * SLC_BLOCK
    ovt = jnp.asarray(((ci < sj + SLC_BLOCK) & (ci + CMP_BLOCK > sj)).astype(np.float32), BF16)
    eye = jnp.eye(QT, dtype=BF16)
    eye4 = jnp.tile(eye, (1, HPG))
    kern = functools.partial(_nsa_kernel, ncmp=ncmp, nslc=nslc)
    per_b = lambda shp: pl.BlockSpec(shp, lambda b, i: (b,) + (0,) * (len(shp) - 1))
    full = lambda shp: pl.BlockSpec(shp, lambda b, i: (0,) * len(shp))
    return pl.pallas_call(
        kern,
        out_shape=jax.ShapeDtypeStruct((B, S, NSA_HEADS * DH), BF16),
        grid=(B, nq),
        in_specs=[
            pl.BlockSpec((1, QT, NSA_HEADS * LANES), lambda b, i: (b, i, 0)),
            per_b((1, SLC_PAD + S, G * LANES)),
            per_b((1, G * DH, SLC_PAD + S)),
            per_b((1, WIN_PAD + S, G * LANES)),
            per_b((1, G * DH, WIN_PAD + S)),
            per_b((1, G, ncmp, LANES)),
            per_b((1, G, DH, ncmp)),
            pl.BlockSpec((1, QT, LANES), lambda b, i: (b, i, 0)),
            full((G, 2 * QT, nql)),
            full((G, 2 * ncmp, nql)),
            full((QT, nql)),
            full((nslc, ncmp)),
            full((QT, QT)),
            full((QT, nql)),
        ],
        out_specs=pl.BlockSpec((1, QT, NSA_HEADS * DH), lambda b, i: (b, i, 0)),
        scratch_shapes=[pltpu.VMEM((G, 2, nql, LANES), BF16), pltpu.VMEM((G, 2 * QT, nql), F32),
                        pltpu.VMEM((G, 2 * QT, nql), F32)],
        compiler_params=pltpu.CompilerParams(dimension_semantics=("parallel", "arbitrary"),
                                             vmem_limit_bytes=VMEM_LIMIT),
        name="nsa",
    )(q, kslc_p, vtslc_p, kwin_p, vtwin_p, kcb, vcbt, misc, near, cmpb, wmask, ovt, eye, eye4)


def _gla_kernel(qk_ref, v_ref, vt_ref, misc_ref, r_ref, wal_ref, bal_ref, ng_ref, cum_ref, out_ref,
                state_s, o_s, *, ct):
    H, dk, dv, C = GLA_HEADS, GLA_DK, GLA_DV, GLA_CHUNK
    kw = H * dk

    @pl.when(pl.program_id(1) == 0)
    def _():
        state_s[...] = jnp.zeros_like(state_s)

    z = _nn(misc_ref[0].astype(BF16), wal_ref[...]) + bal_ref[...]
    log_a = (jnp.minimum(z, 0.0) - jnp.log1p(jnp.exp(-jnp.abs(z)))) * (1.0 / GLA_TAU)
    cum = cum_ref[...]
    a1, a2, a3 = _split3(log_a)
    cs = _nn(cum, a1) + _nn(cum, a2) + _nn(cum, a3)
    bc, bl = cs[:ct], cs[ct:]
    q = qk_ref[0, :, :kw]
    k = qk_ref[0, :, kw:]
    q_in = (q * (dk ** -0.5)) * jnp.exp(bc)
    k_in = (k * jnp.exp(-bc)).astype(BF16)
    k_st = k * jnp.exp(bl - bc)
    decay = jnp.exp(bl)
    lane_head = lax.broadcasted_iota(jnp.int32, (C, kw), 1) // dk
    rr = lax.broadcasted_iota(jnp.int32, (H * C, C), 0) % C
    cc = lax.broadcasted_iota(jnp.int32, (H * C, C), 1)
    causal = rr >= cc
    pair_row = lax.broadcasted_iota(jnp.int32, (2 * C, kw), 0) // C
    pair_head = lax.broadcasted_iota(jnp.int32, (2 * C, kw), 1) // dk

    for c in range(ct // C):
        r0 = c * C
        qc = q_in[r0:r0 + C]
        qcb = qc.astype(BF16)
        q_heads = jnp.concatenate([jnp.where(lane_head == h, qc, 0.0) for h in range(H)], axis=0).astype(BF16)
        attn = jnp.where(causal, _nt(q_heads, k_in[r0:r0 + C]), 0.0).astype(BF16)
        p0 = (c // 2) * 2 * C
        kst_pair = k_st[p0:p0 + 2 * C]
        dec = decay[r0:r0 + 1]
        for h in range(H):
            st = state_s[h]
            o = _nn(attn[h * C:(h + 1) * C], v_ref[0, r0:r0 + C, h * dv:(h + 1) * dv])
            o = o + _nt(qcb, st.astype(BF16))
            o_s[r0:r0 + C, h * dv:(h + 1) * dv] = o
            kst_h = jnp.where((pair_row == c % 2) & (pair_head == h), kst_pair, 0.0).astype(BF16)
            state_s[h] = st * dec + _nn(vt_ref[0, h * dv:(h + 1) * dv, p0:p0 + 2 * C], kst_h)

    for h in range(H):
        oh = o_s[:, h * dv:(h + 1) * dv]
        ms = jnp.mean(oh * oh, axis=-1, keepdims=True)
        r = r_ref[0, :, h * dv:(h + 1) * dv]
        y = oh * lax.rsqrt(ms + EPS) * ng_ref[:, h * dv:(h + 1) * dv] * (r * jax.nn.sigmoid(r))
        out_ref[0, :, h * dv:(h + 1) * dv] = y.astype(BF16)


def _gla(qkb, vb, vtb, misc, rb, w_alpha, b_alpha, norm_g, ct=512):
    B, S, _ = qkb.shape
    H, dk, dv, C = GLA_HEADS, GLA_DK, GLA_DV, GLA_CHUNK
    kw, vw = H * dk, H * dv
    wal = jnp.zeros((LANES, kw), F32).at[32:32 + GLA_RANK].set(w_alpha).astype(BF16)
    r = np.arange(ct)
    tri = (r[:, None] // C == r[None, :] // C) & (r[:, None] >= r[None, :])
    tot = r[:, None] // C == r[None, :] // C
    cum = jnp.asarray(np.concatenate([tri, tot], axis=0).astype(np.float32), BF16)
    full = lambda shp: pl.BlockSpec(shp, lambda b, i: (0,) * len(shp))
    return pl.pallas_call(
        functools.partial(_gla_kernel, ct=ct),
        out_shape=jax.ShapeDtypeStruct((B, S, vw), BF16),
        grid=(B, S // ct),
        in_specs=[
            pl.BlockSpec((1, ct, 2 * kw), lambda b, i: (b, i, 0)),
            pl.BlockSpec((1, ct, vw), lambda b, i: (b, i, 0)),
            pl.BlockSpec((1, vw, ct), lambda b, i: (b, 0, i)),
            pl.BlockSpec((1, ct, LANES), lambda b, i: (b, i, 0)),
            pl.BlockSpec((1, ct, vw), lambda b, i: (b, i, 0)),
            full((LANES, kw)), full((1, kw)), full((1, vw)), full((2 * ct, ct)),
        ],
        out_specs=pl.BlockSpec((1, ct, vw), lambda b, i: (b, i, 0)),
        scratch_shapes=[pltpu.VMEM((H, dv, kw), F32), pltpu.VMEM((ct, vw), F32)],
        compiler_params=pltpu.CompilerParams(dimension_semantics=("parallel", "arbitrary"),
                                             vmem_limit_bytes=VMEM_LIMIT),
        name="gla",
    )(qkb, vb, vtb, misc, rb, wal, b_alpha.reshape(1, kw), norm_g.reshape(1, vw), cum)


ROUTER_ROWS = 128
EXPERT_ROW0 = 8


def _local_rows(tm):
    return -(-(2 * tm + N_EXPERTS * (SUBLANES - 1)) // LANES) * LANES


def _pack_bf16_pairs(x):
    u = pltpu.bitcast(x, jnp.uint32)
    w = x.shape[1] // 2
    return u[:, :w] | (u[:, w:] >> 16)


def _unpack_bf16_pairs(w):
    xh = pltpu.bitcast(w & jnp.uint32(0xFFFF0000), F32)
    xl = pltpu.bitcast(w << 16, F32)
    return jnp.concatenate([xh, xl], axis=1).astype(BF16)


def _outproj_kernel(ya_ref, yb_ref, mg_ref, x_ref, wa_ref, wb_ref, wo_ref, gf_ref, wrh_ref, wrl_ref, br_ref,
                    tri_ref, ltri_ref, x1_ref, xsl_ref, seg_ref, rw_ref, cnt_ref, carry_s, *, tm, ls):
    D = x_ref.shape[1]

    @pl.when(pl.program_id(0) == 0)
    def _():
        carry_s[...] = jnp.zeros_like(carry_s)

    ma = _nn(ya_ref[...], wa_ref[...])
    mb = _nn(yb_ref[...], wb_ref[...])
    merged = jax.nn.sigmoid(mg_ref[:, :D]) * ma + jax.nn.sigmoid(mg_ref[:, D:]) * mb
    x1 = x_ref[...] + _nn(merged.astype(BF16), wo_ref[...])
    x1_ref[...] = x1
    ms = jnp.mean(x1 * x1, axis=-1, keepdims=True)
    h2 = x1 * lax.rsqrt(ms + EPS) * gf_ref[...]
    hi = h2.astype(BF16)
    lo = (h2 - hi.astype(F32)).astype(BF16)
    lg = _nt(wrh_ref[...], hi) + _nt(wrh_ref[...], lo) + _nt(wrl_ref[...], hi) + br_ref[...]
    row8 = lax.broadcasted_iota(jnp.int32, (SUBLANES, tm), 0)
    gl = jnp.where(row8 < N_GROUPS, lg[0:SUBLANES], NEG)
    gmax = jnp.max(gl, axis=0, keepdims=True)
    g_sel = jnp.min(jnp.where(gl == gmax, row8, SUBLANES), axis=0, keepdims=True)
    g_prob = 1.0 / jnp.sum(jnp.where(row8 < N_GROUPS, jnp.exp(gl - gmax), 0.0), axis=0, keepdims=True)
    e_sel = jnp.zeros((EPG, tm), F32)
    for gi in range(N_GROUPS):
        r0 = EXPERT_ROW0 + gi * EPG
        e_sel = e_sel + jnp.where(g_sel == gi, lg[r0:r0 + EPG], 0.0)
    v1 = jnp.max(e_sel, axis=0, keepdims=True)
    i1 = jnp.min(jnp.where(e_sel == v1, row8, EPG), axis=0, keepdims=True)
    rest = jnp.where(row8 == i1, -jnp.inf, e_sel)
    v2 = jnp.max(rest, axis=0, keepdims=True)
    i2 = jnp.min(jnp.where(rest == v2, row8, EPG), axis=0, keepdims=True)
    t = jnp.exp(v2 - v1)
    w1 = g_prob / (1.0 + t)
    w2 = g_prob * t / (1.0 + t)
    e1 = g_sel * EPG + i1
    e2 = g_sel * EPG + i2

    rowe = lax.broadcasted_iota(jnp.int32, (N_EXPERTS, tm), 0)
    oh1 = rowe == e1
    oh2 = rowe == e2
    oh = jnp.where(oh1, 1.0, 0.0) + jnp.where(oh2, 1.0, 0.0)
    pre = _nn(oh.astype(BF16), tri_ref[...])
    cnt = jnp.sum(oh, axis=1, keepdims=True)
    cnt8 = jnp.floor((cnt + (SUBLANES - 1)) * (1.0 / SUBLANES)) * SUBLANES
    c_b = jnp.broadcast_to(cnt8, (N_EXPERTS, LANES))
    c_hi = jnp.floor(c_b * (1.0 / 16.0))
    c_lo = c_b - 16.0 * c_hi
    base = 16.0 * _nn(ltri_ref[...], c_hi.astype(BF16)) + _nn(ltri_ref[...], c_lo.astype(BF16))
    loc = pre + base[:, 0:1]
    lpos1 = jnp.sum(jnp.where(oh1, loc, 0.0), axis=0, keepdims=True)
    lpos2 = jnp.sum(jnp.where(oh2, loc, 0.0), axis=0, keepdims=True)
    lane = lax.broadcasted_iota(jnp.int32, (N_EXPERTS, LANES), 1)
    seg_ref[0] = jnp.where(lane == 0, base, jnp.where(lane == 1, c_b, carry_s[...])).astype(jnp.int32)
    carry_s[...] = carry_s[...] + c_b
    cnt_ref[...] = carry_s[...].astype(jnp.int32)
    rw_ref[0] = jnp.concatenate([w1, w2, lpos1, lpos2, jnp.zeros((SUBLANES - 4, tm), F32)], axis=0)

    srow = lax.broadcasted_iota(jnp.int32, (ls, tm), 0)
    perm = jnp.where(srow == lpos1.astype(jnp.int32), 1.0, jnp.where(srow == lpos2.astype(jnp.int32), 1.0, 0.0))
    xsorted = _nn(perm.astype(BF16), hi)
    xsl_ref[0] = _pack_bf16_pairs(xsorted)


def _outproj(ya, yb, mg, x, wa, wb, wo, g_ffn, w_rg, b_rg, w_re, b_re, tm=MOE_TM):
    T, D = x.shape
    nt = T // tm
    wr = jnp.zeros((ROUTER_ROWS, D), F32).at[0:N_GROUPS].set(w_rg.T).at[EXPERT_ROW0:EXPERT_ROW0 + N_EXPERTS].set(w_re.T)
    wrh = wr.astype(BF16)
    wrl = (wr - wrh.astype(F32)).astype(BF16)
    br = jnp.zeros((ROUTER_ROWS, 1), F32).at[0:N_GROUPS, 0].set(b_rg).at[EXPERT_ROW0:EXPERT_ROW0 + N_EXPERTS, 0].set(b_re)
    r = np.arange(tm)
    tri = jnp.asarray((r[:, None] < r[None, :]).astype(np.float32), BF16)
    re = np.arange(N_EXPERTS)
    ltri = jnp.asarray((re[None, :] < re[:, None]).astype(np.float32), BF16)
    ls = _local_rows(tm)
    row = lambda n: pl.BlockSpec((tm, n), lambda i: (i, 0))
    full = lambda shp: pl.BlockSpec(shp, lambda i: (0,) * len(shp))
    return pl.pallas_call(
        functools.partial(_outproj_kernel, tm=tm, ls=ls),
        out_shape=[jax.ShapeDtypeStruct((T, D), F32), jax.ShapeDtypeStruct((nt, ls, D // 2), jnp.uint32),
                   jax.ShapeDtypeStruct((nt, N_EXPERTS, LANES), jnp.int32),
                   jax.ShapeDtypeStruct((nt, SUBLANES, tm), F32),
                   jax.ShapeDtypeStruct((N_EXPERTS, LANES), jnp.int32)],
        grid=(nt,),
        in_specs=[row(ya.shape[1]), row(yb.shape[1]), row(2 * D), row(D),
                  full(wa.shape), full(wb.shape), full(wo.shape), full((1, D)),
                  full((ROUTER_ROWS, D)), full((ROUTER_ROWS, D)), full((ROUTER_ROWS, 1)), full((tm, tm)),
                  full((N_EXPERTS, N_EXPERTS))],
        out_specs=[row(D), pl.BlockSpec((1, ls, D // 2), lambda i: (i, 0, 0)),
                   pl.BlockSpec((1, N_EXPERTS, LANES), lambda i: (i, 0, 0)),
                   pl.BlockSpec((1, SUBLANES, tm), lambda i: (i, 0, 0)),
                   full((N_EXPERTS, LANES))],
        scratch_shapes=[pltpu.VMEM((N_EXPERTS, LANES), F32)],
        compiler_params=pltpu.CompilerParams(dimension_semantics=("arbitrary",), vmem_limit_bytes=VMEM_LIMIT),
        name="outproj",
    )(ya, yb, mg, x, wa, wb, wo, g_ffn.reshape(1, D), wrh, wrl, br, tri, ltri)


SEG_FIELDS = 4
SEG_BITS = 7
TAIL_BITS = 5


def _segment_copies(n8, bits, make_copy, wait):
    off = 0
    for bit in reversed(range(bits)):
        rows = SUBLANES << bit
        take = (n8 >> bit) & 1

        @pl.when(take == 1)
        def _(off=off, rows=rows):
            cp = make_copy(off, rows)
            if wait:
                cp.wait()
            else:
                cp.start()

        off = off + take * rows


def _dispatch_kernel(seg_ref, tail_ref, xsl_ref, xs_ref, zero_s, sem):
    i = pl.program_id(0)

    def segments(wait):
        def body(e, c):
            s0 = (i * N_EXPERTS + e) * SEG_FIELDS
            src0 = pl.multiple_of(seg_ref[s0], SUBLANES)
            dst0 = pl.multiple_of(seg_ref[s0 + 2], SUBLANES)

            def make_copy(off, rows):
                return pltpu.make_async_copy(xsl_ref.at[i, pl.ds(pl.multiple_of(src0 + off, SUBLANES), rows), :],
                                             xs_ref.at[pl.ds(pl.multiple_of(dst0 + off, SUBLANES), rows), :], sem)

            _segment_copies(seg_ref[s0 + 1], SEG_BITS, make_copy, wait)
            return c

        lax.fori_loop(0, N_EXPERTS, body, 0)

    def tails(wait):
        def body(e, c):
            dst0 = pl.multiple_of(tail_ref[2 * e], SUBLANES)

            def make_copy(off, rows):
                return pltpu.make_async_copy(zero_s.at[pl.ds(0, rows), :],
                                             xs_ref.at[pl.ds(pl.multiple_of(dst0 + off, SUBLANES), rows), :], sem)

            _segment_copies(tail_ref[2 * e + 1], TAIL_BITS, make_copy, wait)
            return c

        lax.fori_loop(0, N_EXPERTS, body, 0)

    def unused_tiles(wait):
        def body(t, c):
            cp = pltpu.make_async_copy(zero_s, xs_ref.at[pl.ds(pl.multiple_of(t * TE, TE), TE), :], sem)
            if wait:
                cp.wait()
            else:
                cp.start()
            return c

        lax.fori_loop(tail_ref[2 * N_EXPERTS], xs_ref.shape[0] // TE, body, 0)

    @pl.when(i == 0)
    def _():
        zero_s[...] = jnp.zeros_like(zero_s)
        tails(False)
        unused_tiles(False)
        tails(True)
        unused_tiles(True)

    segments(False)
    segments(True)


def _dispatch(seg, tail, xsl, n_rows):
    nt, ls, W = xsl.shape
    return pl.pallas_call(
        _dispatch_kernel,
        out_shape=jax.ShapeDtypeStruct((n_rows, W), xsl.dtype),
        grid_spec=pltpu.PrefetchScalarGridSpec(
            num_scalar_prefetch=2,
            grid=(nt,),
            in_specs=[pl.BlockSpec(memory_space=pl.ANY)],
            out_specs=pl.BlockSpec(memory_space=pl.ANY),
            scratch_shapes=[pltpu.VMEM((TE, W), xsl.dtype), pltpu.SemaphoreType.DMA(())],
        ),
        compiler_params=pltpu.CompilerParams(dimension_semantics=("arbitrary",)),
        name="dispatch",
    )(seg, tail, xsl)


def _experts_kernel(te_ref, nv_ref, xs_ref, wg_ref, wu_ref, wd_ref, out_ref):
    i = pl.program_id(0)

    @pl.when(i < nv_ref[0])
    def _():
        x = _unpack_bf16_pairs(xs_ref[...])
        a = _nn(x, wg_ref[0].astype(BF16))
        u = _nn(x, wu_ref[0].astype(BF16))
        hid = (a * jax.nn.sigmoid(a)) * u
        y = _nn(hid.astype(BF16), wd_ref[0].astype(BF16))
        out_ref[...] = _pack_bf16_pairs(y.astype(BF16).astype(F32))

    @pl.when(i >= nv_ref[0])
    def _():
        out_ref[...] = jnp.zeros_like(out_ref)


def _experts(tile_expert, n_valid, xs, w_gate, w_up, w_down):
    n_rows, W = xs.shape
    D = 2 * W
    n_tiles = n_rows // TE
    last = lambda i, nv: jnp.minimum(i, nv[0] - 1)
    return pl.pallas_call(
        _experts_kernel,
        out_shape=jax.ShapeDtypeStruct((n_rows, W), jnp.uint32),
        grid_spec=pltpu.PrefetchScalarGridSpec(
            num_scalar_prefetch=2,
            grid=(n_tiles,),
            in_specs=[pl.BlockSpec((TE, W), lambda i, te, nv: (last(i, nv), 0)),
                      pl.BlockSpec((1, D, EXPERT_FF), lambda i, te, nv: (te[last(i, nv)], 0, 0)),
                      pl.BlockSpec((1, D, EXPERT_FF), lambda i, te, nv: (te[last(i, nv)], 0, 0)),
                      pl.BlockSpec((1, EXPERT_FF, D), lambda i, te, nv: (te[last(i, nv)], 0, 0))],
            out_specs=pl.BlockSpec((TE, W), lambda i, te, nv: (i, 0)),
        ),
        compiler_params=pltpu.CompilerParams(dimension_semantics=("arbitrary",), vmem_limit_bytes=VMEM_LIMIT),
        name="experts",
    )(tile_expert, n_valid, xs, w_gate, w_up, w_down)


def _combine_kernel(seg_ref, ys_ref, x1_ref, rw_ref, gfin_ref, out_ref, buf, sem, *, tm, ls, apply_norm):
    i = pl.program_id(0)

    @pl.when(i == 0)
    def _():
        buf[...] = jnp.zeros_like(buf)

    def segments(wait):
        def body(e, c):
            s0 = (i * N_EXPERTS + e) * SEG_FIELDS
            loc0 = pl.multiple_of(seg_ref[s0], SUBLANES)
            glob0 = pl.multiple_of(seg_ref[s0 + 2], SUBLANES)

            def make_copy(off, rows):
                return pltpu.make_async_copy(ys_ref.at[pl.ds(pl.multiple_of(glob0 + off, SUBLANES), rows), :],
                                             buf.at[pl.ds(pl.multiple_of(loc0 + off, SUBLANES), rows), :], sem)

            _segment_copies(seg_ref[s0 + 1], SEG_BITS, make_copy, wait)
            return c

        lax.fori_loop(0, N_EXPERTS, body, 0)

    segments(False)
    segments(True)
    ysl = _unpack_bf16_pairs(buf[...])
    cols = jnp.concatenate([rw_ref[0], jnp.zeros((LANES - SUBLANES, tm), F32)], axis=0).T
    srow = lax.broadcasted_iota(jnp.int32, (tm, ls), 1)
    y = x1_ref[...]
    for k in range(2):
        pick = jnp.where(srow == cols[:, 2 + k:3 + k].astype(jnp.int32), 1.0, 0.0).astype(BF16)
        y = y + cols[:, k:k + 1] * _nn(pick, ysl)
    if apply_norm:
        ms = jnp.mean(y * y, axis=-1, keepdims=True)
        y = y * lax.rsqrt(ms + EPS) * gfin_ref[...]
    out_ref[...] = y


def _combine(seg, ys, x1, rw, g_final, apply_norm, tm=MOE_TM):
    T, D = x1.shape
    nt = T // tm
    ls = _local_rows(tm)
    return pl.pallas_call(
        functools.partial(_combine_kernel, tm=tm, ls=ls, apply_norm=apply_norm),
        out_shape=jax.ShapeDtypeStruct((T, D), F32),
        grid_spec=pltpu.PrefetchScalarGridSpec(
            num_scalar_prefetch=1,
            grid=(nt,),
            in_specs=[pl.BlockSpec(memory_space=pl.ANY),
                      pl.BlockSpec((tm, D), lambda i, sg: (i, 0)),
                      pl.BlockSpec((1, SUBLANES, tm), lambda i, sg: (i, 0, 0)),
                      pl.BlockSpec((1, D), lambda i, sg: (0, 0))],
            out_specs=pl.BlockSpec((tm, D), lambda i, sg: (i, 0)),
            scratch_shapes=[pltpu.VMEM((ls, D // 2), jnp.uint32), pltpu.SemaphoreType.DMA(())],
        ),
        compiler_params=pltpu.CompilerParams(dimension_semantics=("arbitrary",), vmem_limit_bytes=VMEM_LIMIT),
        name="combine",
    )(seg, ys, x1, rw, g_final.reshape(1, D))


def _moe_plan(seg, counts, T, tm=MOE_TM):
    nt = T // tm
    n_tiles_max = (2 * T + nt * N_EXPERTS * (SUBLANES - 1)) // TE + N_EXPERTS
    total = counts[:, 0]
    tiles = (total + TE - 1) // TE
    ids = jnp.arange(N_EXPERTS)
    tile_end = jnp.sum(jnp.where(ids[None, :] <= ids[:, None], tiles[None, :], 0), axis=1)
    row0 = (tile_end - tiles) * TE
    segtab = jnp.stack([seg[:, :, 0], seg[:, :, 1] // SUBLANES, seg[:, :, 2] + row0[None, :],
                        jnp.zeros_like(seg[:, :, 0])], axis=-1).reshape(-1).astype(jnp.int32)
    tail = jnp.concatenate([jnp.stack([row0 + total, (tiles * TE - total) // SUBLANES], axis=-1).reshape(-1),
                            tile_end[-1:]]).astype(jnp.int32)
    tile_expert = jnp.minimum(jnp.sum(tile_end[None, :] <= jnp.arange(n_tiles_max)[:, None], axis=1),
                              N_EXPERTS - 1).astype(jnp.int32)
    return segtab, tail, tile_expert, tile_end[-1:].astype(jnp.int32), n_tiles_max * TE


def kernel(x, g_mix, w_in, nsa_pe_k, nsa_cmp_k_w1, nsa_cmp_k_w2, nsa_pe_v, nsa_cmp_v_w1, nsa_cmp_v_w2, rel_bias,
           gla_w_alpha, gla_b_alpha, gla_norm_g, w_branch_a, w_branch_b, w_out, g_ffn, w_router_group,
           b_router_group, w_router_expert, b_router_expert, w_exp_gate, w_exp_up, w_exp_down, g_final):
    B, S, D = x.shape
    T = B * S
    for l in range(w_in.shape[0]):
        w, wt = _pack_inproj_weights(w_in[l])
        (q, kslc, kwin, kvc, misc, qkb, vb, rb, mg, vts, vtw, vtb) = _inproj(x, g_mix[l], w, wt)
        kcb, vcbt = _compress(kvc, nsa_pe_k[l], nsa_cmp_k_w1[l], nsa_cmp_k_w2[l],
                              nsa_pe_v[l], nsa_cmp_v_w1[l], nsa_cmp_v_w2[l])
        ya = _nsa(q, kslc, vts, kwin, vtw, kcb, vcbt, misc, rel_bias)
        yb = _gla(qkb, vb, vtb, misc, rb, gla_w_alpha[l], gla_b_alpha[l], gla_norm_g[l])
        x1, xsl, seg, rw, counts = _outproj(
            ya.reshape(T, -1), yb.reshape(T, -1), mg.reshape(T, -1), x.reshape(T, D),
            w_branch_a[l].astype(BF16), w_branch_b[l].astype(BF16), w_out[l].astype(BF16), g_ffn[l],
            w_router_group[l], b_router_group[l], w_router_expert[l], b_router_expert[l])
        segtab, tail, tile_expert, n_valid, n_rows = _moe_plan(seg, counts, T)
        xs = _dispatch(segtab, tail, xsl, n_rows)
        ys = _experts(tile_expert, n_valid, xs, w_exp_gate[l], w_exp_up[l], w_exp_down[l])
        last_layer = l == w_in.shape[0] - 1
        x = _combine(segtab, ys, x1, rw, g_final, apply_norm=last_layer).reshape(B, S, D)
    return x
```

```python
import functools
import math

import numpy as np
import jax
import jax.numpy as jnp
from jax import lax
from jax.experimental import pallas as pl
from jax.experimental.pallas import tpu as pltpu

F32 = jnp.float32
BF16 = jnp.bfloat16

NSA_HEADS = 8
NSA_GROUPS = 2
HPG = NSA_HEADS // NSA_GROUPS
DH = 64
CMP_BLOCK = 32
CMP_STRIDE = 16
CMP_HIDDEN = 128
SLC_BLOCK = 64
SLC_TOPN = 16
WINDOW = 512
GLA_HEADS = 4
GLA_DK = 64
GLA_DV = 128
GLA_RANK = 16
GLA_TAU = 16.0
GLA_CHUNK = 64
REL_BUCKETS = 32
REL_MAX_EXACT = REL_BUCKETS // 2
REL_MAX_DIST = 128
N_GROUPS = 4
EPG = 8
N_EXPERTS = N_GROUPS * EPG
EXPERT_FF = 256
EPS = 1e-6

LANES = 128
SUBLANES = 8
VMEM_LIMIT = 56 * 1024 * 1024

LOG2E = math.log2(math.e)
NEG = -1e30
BIG = float(2.0 ** 100)
QT = 128
SLC_PAD = 128
WIN_PAD = 512
TE = 256
MOE_TM = 512


def _nt(a, b):
    return lax.dot_general(a, b, (((1,), (1,)), ((), ())), preferred_element_type=F32)


def _nn(a, b):
    return jnp.dot(a, b, preferred_element_type=F32)


def _split3(x):
    a = x.astype(BF16)
    r = x - a.astype(F32)
    b = r.astype(BF16)
    c = (r - b.astype(F32)).astype(BF16)
    return a, b, c


def _t5_bucket_np(rel):
    n = np.maximum(rel, 0)
    nf = np.maximum(n, 1).astype(np.float32)
    large = REL_MAX_EXACT + (np.log(nf / np.float32(REL_MAX_EXACT)) / np.float32(math.log(REL_MAX_DIST / REL_MAX_EXACT))
                             * np.float32(REL_BUCKETS - REL_MAX_EXACT)).astype(np.int32)
    return np.where(n < REL_MAX_EXACT, n, np.minimum(large, REL_BUCKETS - 1)).astype(np.int32)


def _inproj_kernel(x_ref, g_ref, w_ref, wt_ref, oq, okslc, okwin, okv, omisc, oqkb, ovb, orb, omg,
                   ovts, ovtw, ovtb, *, tm, seq):
    x = x_ref[0]
    ms = jnp.mean(x * x, axis=-1, keepdims=True)
    h = (x * lax.rsqrt(ms + EPS) * g_ref[...]).astype(BF16)

    def mm(a, b):
        return _nn(h, w_ref[:, a:b])

    oq[0] = mm(0, 1024).astype(BF16)
    ks = mm(1024, 1280)
    row = lax.broadcasted_iota(jnp.int32, (tm, 256), 0) + pl.program_id(1) * tm
    lane = lax.broadcasted_iota(jnp.int32, (tm, 256), 1) % LANES
    onehot = jnp.where(lane - DH == row // SLC_BLOCK, 1.0, 0.0)
    okslc[0] = (ks + onehot).astype(BF16)
    okwin[0] = mm(1280, 1536).astype(BF16)
    okv[0] = mm(1536, 1792)
    omisc[0] = mm(1792, 1920)
    oqkb[0] = mm(1920, 2432)
    ovb[0] = mm(2432, 2944).astype(BF16)
    orb[0] = mm(2944, 3456)
    omg[0] = mm(3456, 5504)
    vt = _nt(wt_ref[...], h)
    ovts[0] = vt[0:128].astype(BF16)
    ovtw[0] = vt[128:256].astype(BF16)
    ovtb[0] = vt[256:768].astype(BF16)


def _pack_inproj_weights(w_in):
    o = np.cumsum([0, 512, 128, 128, 128, 128, 128, 128, 24, 256, 256, 512, 16, 512, 1024, 1024])
    (q_a, k_cmp, v_cmp, k_slc, v_slc, k_win, v_win, gate_a, q_b, k_b, v_b, a_b, r_b, mg_a, mg_b) = [
        w_in[:, o[i]:o[i + 1]] for i in range(15)]
    D = w_in.shape[0]
    z64 = jnp.zeros((D, DH), w_in.dtype)
    qcols = []
    for hd in range(NSA_HEADS):
        qcols += [q_a[:, hd * DH:(hd + 1) * DH] * (DH ** -0.5 * LOG2E), z64]
    kslc = [k_slc[:, :DH], z64, k_slc[:, DH:], z64]
    kwin = [k_win[:, :DH], z64, k_win[:, DH:], z64]
    ga = gate_a.reshape(D, 3, NSA_GROUPS, HPG)
    z4 = jnp.zeros((D, 4), w_in.dtype)
    misc = [ga[:, :, 0, :].reshape(D, 12), z4, ga[:, :, 1, :].reshape(D, 12), z4, a_b,
            jnp.zeros((D, LANES - 48), w_in.dtype)]
    w = jnp.concatenate(qcols + kslc + kwin + [k_cmp, v_cmp] + misc + [q_b, k_b, v_b, r_b, mg_a, mg_b], axis=1)
    wt = jnp.concatenate([v_slc, v_win, v_b], axis=1).T
    return w.astype(BF16), wt.astype(BF16)


def _inproj(x, g_mix, w, wt, tm=512):
    B, S, D = x.shape
    nw = w.shape[1]
    widths = [(1024, BF16), (256, BF16), (256, BF16), (256, F32), (128, F32), (512, F32), (512, BF16),
              (512, F32), (2048, F32)]
    out_shape = [jax.ShapeDtypeStruct((B, S, n), dt) for n, dt in widths]
    out_specs = [pl.BlockSpec((1, tm, n), lambda b, i: (b, i, 0)) for n, _ in widths]
    for rows in (128, 128, 512):
        out_shape.append(jax.ShapeDtypeStruct((B, rows, S), BF16))
        out_specs.append(pl.BlockSpec((1, rows, tm), lambda b, i: (b, 0, i)))
    return pl.pallas_call(
        functools.partial(_inproj_kernel, tm=tm, seq=S),
        out_shape=out_shape,
        grid=(B, S // tm),
        in_specs=[
            pl.BlockSpec((1, tm, D), lambda b, i: (b, i, 0)),
            pl.BlockSpec((1, D), lambda b, i: (0, 0)),
            pl.BlockSpec((D, nw), lambda b, i: (0, 0), pipeline_mode=pl.Buffered(1)),
            pl.BlockSpec((768, D), lambda b, i: (0, 0), pipeline_mode=pl.Buffered(1)),
        ],
        out_specs=out_specs,
        compiler_params=pltpu.CompilerParams(dimension_semantics=("parallel", "parallel"),
                                             vmem_limit_bytes=VMEM_LIMIT),
        name="inproj",
    )(x, g_mix.reshape(1, D), w, wt)


def _gelu_tanh(x):
    return 0.5 * x * (1.0 + jnp.tanh(math.sqrt(2.0 / math.pi) * (x + 0.044715 * (x * x * x))))


def _compress_kernel(xk_ref, xv_ref, pe_ref, w1_ref, w2k_ref, w2vt_ref, ok_ref, ovt_ref, *, nsub):
    for kind, x_ref in enumerate((xk_ref, xv_ref)):
        top = jnp.zeros((nsub, 2 * CMP_HIDDEN), F32)
        bot = jnp.zeros((nsub, 2 * CMP_HIDDEN), F32)
        for r in range(CMP_STRIDE):
            xr = x_ref[0, pl.ds(r, nsub, stride=CMP_STRIDE), :]
            top = top + _nn((xr + pe_ref[kind, 0, r:r + 1, :]).astype(BF16), w1_ref[kind, 0, r])
            bot = bot + _nn((xr + pe_ref[kind, 1, r:r + 1, :]).astype(BF16), w1_ref[kind, 1, r])
        hid = _gelu_tanh(top + pltpu.roll(bot, shift=nsub - 1, axis=0)).astype(BF16)
        for g in range(NSA_GROUPS):
            hg = hid[:, g * CMP_HIDDEN:(g + 1) * CMP_HIDDEN]
            if kind == 0:
                ok_ref[0, g] = _nn(hg, w2k_ref[...]).astype(BF16)
            else:
                ovt_ref[0, g] = _nt(w2vt_ref[...], hg).astype(BF16)


def _compress(kv_cmp, pe_k, w1k, w2k, pe_v, w1v, w2v):
    B, S, _ = kv_cmp.shape
    nsub = S // CMP_STRIDE
    G = NSA_GROUPS

    def prep(pe, w1):
        pe_t = jnp.tile(pe.reshape(2, CMP_STRIDE, DH), (1, 1, G))
        a = w1.reshape(2, CMP_STRIDE, DH, CMP_HIDDEN)
        z = jnp.zeros_like(a)
        w = jnp.concatenate([jnp.concatenate([a, z], axis=3), jnp.concatenate([z, a], axis=3)], axis=2)
        return pe_t, w.astype(BF16)

    pek, w1kb = prep(pe_k, w1k)
    pev, w1vb = prep(pe_v, w1v)
    pe = jnp.stack([pek, pev])
    w1 = jnp.stack([w1kb, w1vb])
    w2kp = jnp.concatenate([w2k, jnp.zeros_like(w2k)], axis=1).astype(BF16)
    w2vt = w2v.T.astype(BF16)
    full = lambda shp: pl.BlockSpec(shp, lambda b: (0,) * len(shp))
    return pl.pallas_call(
        functools.partial(_compress_kernel, nsub=nsub),
        out_shape=[jax.ShapeDtypeStruct((B, G, nsub, LANES), BF16),
                   jax.ShapeDtypeStruct((B, G, DH, nsub), BF16)],
        grid=(B,),
        in_specs=[pl.BlockSpec((1, S, G * DH), lambda b: (b, 0, 0)), pl.BlockSpec((1, S, G * DH), lambda b: (b, 0, 1)),
                  full(pe.shape), full(w1.shape), full((CMP_HIDDEN, LANES)), full((DH, CMP_HIDDEN))],
        out_specs=[pl.BlockSpec((1, G, nsub, LANES), lambda b: (b, 0, 0, 0)),
                   pl.BlockSpec((1, G, DH, nsub), lambda b: (b, 0, 0, 0))],
        compiler_params=pltpu.CompilerParams(dimension_semantics=("parallel",), vmem_limit_bytes=VMEM_LIMIT),
        name="compress",
    )(kv_cmp, kv_cmp, pe, w1, w2kp, w2vt)


def _bias_kernel(tbl_ref, bkn_ref, bkc_ref, near_ref, cmpb_ref):
    g = pl.program_id(0)
    for h in range(HPG):
        hd = g * HPG + h

        def lookup(bk):
            acc = jnp.full(bk.shape, NEG, F32)
            for b in range(REL_BUCKETS):
                acc = jnp.where(bk == b, tbl_ref[hd, b], acc)
            return acc

        vn = lookup(bkn_ref[...])
        near_ref[0, :, h * QT:(h + 1) * QT] = jnp.where(vn > 0.5 * NEG, (vn - tbl_ref[hd, REL_BUCKETS - 1]) * LOG2E, NEG)
        vc = lookup(bkc_ref[...])
        cmpb_ref[0, :, h * QT:(h + 1) * QT] = jnp.where(vc > 0.5 * NEG, vc * LOG2E, NEG)


def _nsa_bias_tables(rel_bias, seq):
    ql = np.arange(QT)
    ncmp = seq // CMP_STRIDE

    def buckets(rel):
        return jnp.asarray(np.where(rel >= 0, _t5_bucket_np(rel), -1).astype(np.int32))

    bkn = buckets(ql[None, :] + QT - np.arange(2 * QT)[:, None])
    y = np.arange(2 * ncmp)
    bkc = buckets(ql[None, :] - CMP_STRIDE * (y[:, None] - ncmp) - (CMP_BLOCK - 1))
    nql = HPG * QT
    return pl.pallas_call(
        _bias_kernel,
        out_shape=[jax.ShapeDtypeStruct((NSA_GROUPS, 2 * QT, nql), F32),
                   jax.ShapeDtypeStruct((NSA_GROUPS, 2 * ncmp, nql), F32)],
        grid=(NSA_GROUPS,),
        in_specs=[pl.BlockSpec(memory_space=pltpu.SMEM),
                  pl.BlockSpec((2 * QT, QT), lambda g: (0, 0)),
                  pl.BlockSpec((2 * ncmp, QT), lambda g: (0, 0))],
        out_specs=[pl.BlockSpec((1, 2 * QT, nql), lambda g: (g, 0, 0)),
                   pl.BlockSpec((1, 2 * ncmp, nql), lambda g: (g, 0, 0))],
        compiler_params=pltpu.CompilerParams(dimension_semantics=("parallel",)),
        name="t5bias",
    )(rel_bias.T, bkn, bkc)


def _nsa_kernel(q_ref, kslc_ref, vtslc_ref, kwin_ref, vtwin_ref, kcb_ref, vcbt_ref, misc_ref,
                near_ref, cmpb_ref, wmask_ref, ovt_ref, eye_ref, eye4_ref,
                out_ref, qaug_s, s0_s, s1_s, *, ncmp, nslc):
    G = NSA_GROUPS
    qt = pl.program_id(1)
    nql = HPG * QT
    ck = 2 * QT

    def flash(carry, s, vt_chunk):
        m, l, acc = carry
        m_new = jnp.maximum(m, jnp.max(s, axis=0, keepdims=True))
        alpha = jnp.exp2(m - m_new)
        p = jnp.exp2(s - m_new)
        l = alpha * l + jnp.sum(p, axis=0, keepdims=True)
        acc = alpha * acc + _nn(vt_chunk, p.astype(BF16))
        return m_new, l, acc

    init = (jnp.full((1, nql), NEG, F32), jnp.zeros((1, nql), F32), jnp.zeros((DH, nql), F32))
    ns = pl.multiple_of(QT * qt, QT)
    nw = pl.multiple_of(QT * qt + 3 * QT, QT)
    off = pl.multiple_of(ncmp - (QT // CMP_STRIDE) * qt, SUBLANES)
    lane = lax.broadcasted_iota(jnp.int32, (nql, LANES), 1)
    win_aug = jnp.where(lane >= DH, -BIG, 0.0).astype(BF16)
    jidx = lax.broadcasted_iota(jnp.int32, (nslc, QT), 0)
    tq = qt * QT + lax.broadcasted_iota(jnp.int32, (nslc, QT), 1)
    tb = tq // SLC_BLOCK
    forced = (jidx == 0) | (jidx == tb) | (jidx == tb - 1)
    future = jidx * SLC_BLOCK > tq
    sub = lax.broadcasted_iota(jnp.int32, (SUBLANES, QT), 0)
    ones_lo = jnp.ones((DH, QT), F32)
    eye = eye_ref[...]
    ovt = ovt_ref[...]
    ngrp = nslc // SUBLANES

    def aug(sel01):
        rows = [ones_lo, sel01]
        if LANES - DH - nslc:
            rows.append(jnp.ones((LANES - DH - nslc, QT), F32))
        m01 = _nt(eye, jnp.concatenate(rows, axis=0).astype(BF16))
        return jnp.concatenate([((m01 - 1.0) * BIG).astype(BF16)] * HPG, axis=0)

    o_cw = []
    for g in range(G):
        kl = slice(g * LANES, (g + 1) * LANES)
        vr = slice(g * DH, (g + 1) * DH)
        q0 = jnp.concatenate([q_ref[0, :, (g * HPG + h) * LANES:(g * HPG + h + 1) * LANES] for h in range(HPG)],
                             axis=0)
        near_b = near_ref[g]

        qwin = q0 + win_aug
        sw = _nt(kwin_ref[0, pl.ds(ns, 3 * QT), kl], qwin)
        sw = jnp.concatenate([sw[:QT] + wmask_ref[...], sw[QT:]], axis=0)
        carry = flash(init, sw, vtwin_ref[0, vr, pl.ds(ns, 3 * QT)])
        sw = _nt(kwin_ref[0, pl.ds(nw, ck), kl], qwin) + near_b
        m, l, acc = flash(carry, sw, vtwin_ref[0, vr, pl.ds(nw, ck)])
        o_w = acc * (1.0 / l)

        bc = cmpb_ref[g, pl.ds(off, ncmp), :]
        sc = _nt(kcb_ref[0, g], q0) + bc
        vis = bc > 0.5 * NEG
        mc = jnp.max(sc, axis=0, keepdims=True)
        ec = jnp.where(vis, jnp.exp2(sc - mc), 0.0)
        den = jnp.maximum(jnp.sum(ec, axis=0, keepdims=True), jnp.finfo(F32).tiny)
        pc = ec * (1.0 / den)
        o_c = _nn(vcbt_ref[0, g], pc.astype(BF16))
        o_cw.append((o_c, o_w))

        psum = pc[:, 0:QT]
        for h in range(1, HPG):
            psum = psum + pc[:, h * QT:(h + 1) * QT]
        p1, p2, p3 = _split3(psum)
        imp = _nn(ovt, p1) + _nn(ovt, p2) + _nn(ovt, p3)
        imp = jnp.where(forced, 1e30, jnp.where(future, -1e30, imp))
        grp = [imp[SUBLANES * v:SUBLANES * (v + 1)] for v in range(ngrp)]
        cnt = [jnp.zeros((SUBLANES, QT), F32) for _ in range(ngrp)]
        for jp in range(nslc):
            v0, r0 = divmod(jp, SUBLANES)
            row = jnp.broadcast_to(imp[jp:jp + 1, :], (SUBLANES, QT))
            for v in range(ngrp):
                if v < v0:
                    inc = jnp.where(row > grp[v], 1.0, 0.0)
                elif v > v0:
                    inc = jnp.where(row >= grp[v], 1.0, 0.0)
                else:
                    inc = jnp.where(sub > r0, jnp.where(row >= grp[v], 1.0, 0.0), jnp.where(row > grp[v], 1.0, 0.0))
                cnt[v] = cnt[v] + inc
        sel = jnp.concatenate(cnt, axis=0) < float(min(SLC_TOPN, nslc))
        sel_near = jnp.where(sel, 1.0, 0.0)
        sel_far = jnp.where(jidx < 2 * (qt - 1), sel_near, 0.0)
        qaug_s[g, 0] = q0 + aug(sel_far)
        qaug_s[g, 1] = q0 + aug(sel_near)

    n_far = qt // 2
    n_chunks = n_far + 1

    def rows(i):
        return pl.multiple_of(QT * qt - ck * i, QT)

    def scores(g, i):
        i = jnp.minimum(i, n_far)
        return _nt(kslc_ref[0, pl.ds(rows(i), ck), g * LANES:(g + 1) * LANES], qaug_s[g, jnp.where(i == 0, 1, 0)])

    def vt(g, i):
        return vtslc_ref[0, g * DH:(g + 1) * DH, pl.ds(rows(i), ck)]

    for g in range(G):
        s0_s[g] = scores(g, 0) + near_ref[g]

    def pair_body(p, carry):
        i = 2 * p
        for g in range(G):
            s1_s[g] = scores(g, i + 1)
        carry = tuple(flash(carry[g], s0_s[g], vt(g, i)) for g in range(G))
        for g in range(G):
            s0_s[g] = scores(g, i + 2)
        return tuple(flash(carry[g], s1_s[g], vt(g, i + 1)) for g in range(G))

    carry = lax.fori_loop(0, n_chunks // 2, pair_body, (init,) * G)
    carry = lax.cond(n_chunks % 2 == 1,
                     lambda c: tuple(flash(c[g], s0_s[g], vt(g, n_far)) for g in range(G)),
                     lambda c: c, carry)

    gt = jax.nn.sigmoid(misc_ref[0]).T
    head = lax.broadcasted_iota(jnp.int32, (DH, nql), 1) // QT
    for g in range(G):
        o_c, o_w = o_cw[g]
        m, l, acc = carry[g]
        o_s = acc * (1.0 / l)

        def gate_row(br):
            r0 = 16 * g + br * HPG
            return jnp.concatenate([gt[r0 + h:r0 + h + 1, :] for h in range(HPG)], axis=1)

        o = gate_row(0) * o_c + gate_row(1) * o_s + gate_row(2) * o_w
        ob = o.astype(BF16)
        blocks = jnp.concatenate([jnp.where(head == h, ob, jnp.zeros_like(ob)) for h in range(HPG)], axis=0)
        out_ref[0, :, g * HPG * DH:(g + 1) * HPG * DH] = _nt(eye4_ref[...], blocks).astype(BF16)


def _nsa(q, kslc, vtslc, kwin, vtwin, kcb, vcbt, misc, rel_bias):
    B, S, _ = q.shape
    G = NSA_GROUPS
    nq = S // QT
    ncmp = S // CMP_STRIDE
    nslc = S // SLC_BLOCK
    nql = HPG * QT
    near, cmpb = _nsa_bias_tables(rel_bias, S)
    wmask = jnp.asarray(np.tile(np.where(np.arange(QT)[:, None] > np.arange(QT)[None, :], 0.0, NEG), (1, HPG)), F32)
    kpad_s = jnp.concatenate([jnp.zeros((DH,), BF16), jnp.ones((DH,), BF16)] * G)
    kslc_p = jnp.concatenate([jnp.broadcast_to(kpad_s, (B, SLC_PAD, G * LANES)), kslc], axis=1)
    kwin_p = jnp.concatenate([jnp.broadcast_to(kpad_s, (B, WIN_PAD, G * LANES)), kwin], axis=1)
    vtslc_p = jnp.pad(vtslc, ((0, 0), (0, 0), (SLC_PAD, 0)))
    vtwin_p = jnp.pad(vtwin, ((0, 0), (0, 0), (WIN_PAD, 0)))
    ci = np.arange(ncmp)[None, :] * CMP_STRIDE
    sj = np.arange(nslc)[:, None] * SLC_BLOCK
    ovt = jnp.asarray(((ci < sj + SLC_BLOCK) & (ci + CMP_BLOCK > sj)).astype(np.float32), BF16)
    eye = jnp.eye(QT, dtype=BF16)
    eye4 = jnp.tile(eye, (1, HPG))
    kern = functools.partial(_nsa_kernel, ncmp=ncmp, nslc=nslc)
    per_b = lambda shp: pl.BlockSpec(shp, lambda b, i: (b,) + (0,) * (len(shp) - 1))
    full = lambda shp: pl.BlockSpec(shp, lambda b, i: (0,) * len(shp))
    return pl.pallas_call(
        kern,
        out_shape=jax.ShapeDtypeStruct((B, S, NSA_HEADS * DH), BF16),
        grid=(B, nq),
        in_specs=[
            pl.BlockSpec((1, QT, NSA_HEADS * LANES), lambda b, i: (b, i, 0)),
            per_b((1, SLC_PAD + S, G * LANES)),
            per_b((1, G * DH, SLC_PAD + S)),
            per_b((1, WIN_PAD + S, G * LANES)),
            per_b((1, G * DH, WIN_PAD + S)),
            per_b((1, G, ncmp, LANES)),
            per_b((1, G, DH, ncmp)),
            pl.BlockSpec((1, QT, LANES), lambda b, i: (b, i, 0)),
            full((G, 2 * QT, nql)),
            full((G, 2 * ncmp, nql)),
            full((QT, nql)),
            full((nslc, ncmp)),
            full((QT, QT)),
            full((QT, nql)),
        ],
        out_specs=pl.BlockSpec((1, QT, NSA_HEADS * DH), lambda b, i: (b, i, 0)),
        scratch_shapes=[pltpu.VMEM((G, 2, nql, LANES), BF16), pltpu.VMEM((G, 2 * QT, nql), F32),
                        pltpu.VMEM((G, 2 * QT, nql), F32)],
        compiler_params=pltpu.CompilerParams(dimension_semantics=("parallel", "arbitrary"),
                                             vmem_limit_bytes=VMEM_LIMIT),
        name="nsa",
    )(q, kslc_p, vtslc_p, kwin_p, vtwin_p, kcb, vcbt, misc, near, cmpb, wmask, ovt, eye, eye4)


def _gla_kernel(qk_ref, v_ref, vt_ref, misc_ref, r_ref, wal_ref, bal_ref, ng_ref, cum_ref, out_ref,
                state_s, o_s, *, ct):
    H, dk, dv, C = GLA_HEADS, GLA_DK, GLA_DV, GLA_CHUNK
    kw = H * dk

    @pl.when(pl.program_id(1) == 0)
    def _():
        state_s[...] = jnp.zeros_like(state_s)

    z = _nn(misc_ref[0].astype(BF16), wal_ref[...]) + bal_ref[...]
    log_a = (jnp.minimum(z, 0.0) - jnp.log1p(jnp.exp(-jnp.abs(z)))) * (1.0 / GLA_TAU)
    cum = cum_ref[...]
    a1, a2, a3 = _split3(log_a)
    cs = _nn(cum, a1) + _nn(cum, a2) + _nn(cum, a3)
    bc, bl = cs[:ct], cs[ct:]
    q = qk_ref[0, :, :kw]
    k = qk_ref[0, :, kw:]
    q_in = (q * (dk ** -0.5)) * jnp.exp(bc)
    k_in = (k * jnp.exp(-bc)).astype(BF16)
    k_st = k * jnp.exp(bl - bc)
    decay = jnp.exp(bl)
    lane_head = lax.broadcasted_iota(jnp.int32, (C, kw), 1) // dk
    rr = lax.broadcasted_iota(jnp.int32, (H * C, C), 0) % C
    cc = lax.broadcasted_iota(jnp.int32, (H * C, C), 1)
    causal = rr >= cc
    pair_row = lax.broadcasted_iota(jnp.int32, (2 * C, kw), 0) // C
    pair_head = lax.broadcasted_iota(jnp.int32, (2 * C, kw), 1) // dk

    for c in range(ct // C):
        r0 = c * C
        qc = q_in[r0:r0 + C]
        qcb = qc.astype(BF16)
        q_heads = jnp.concatenate([jnp.where(lane_head == h, qc, 0.0) for h in range(H)], axis=0).astype(BF16)
        attn = jnp.where(causal, _nt(q_heads, k_in[r0:r0 + C]), 0.0).astype(BF16)
        p0 = (c // 2) * 2 * C
        kst_pair = k_st[p0:p0 + 2 * C]
        dec = decay[r0:r0 + 1]
        for h in range(H):
            st = state_s[h]
            o = _nn(attn[h * C:(h + 1) * C], v_ref[0, r0:r0 + C, h * dv:(h + 1) * dv])
            o = o + _nt(qcb, st.astype(BF16))
            o_s[r0:r0 + C, h * dv:(h + 1) * dv] = o
            kst_h = jnp.where((pair_row == c % 2) & (pair_head == h), kst_pair, 0.0).astype(BF16)
            state_s[h] = st * dec + _nn(vt_ref[0, h * dv:(h + 1) * dv, p0:p0 + 2 * C], kst_h)

    for h in range(H):
        oh = o_s[:, h * dv:(h + 1) * dv]
        ms = jnp.mean(oh * oh, axis=-1, keepdims=True)
        r = r_ref[0, :, h * dv:(h + 1) * dv]
        y = oh * lax.rsqrt(ms + EPS) * ng_ref[:, h * dv:(h + 1) * dv] * (r * jax.nn.sigmoid(r))
        out_ref[0, :, h * dv:(h + 1) * dv] = y.astype(BF16)


def _gla(qkb, vb, vtb, misc, rb, w_alpha, b_alpha, norm_g, ct=512):
    B, S, _ = qkb.shape
    H, dk, dv, C = GLA_HEADS, GLA_DK, GLA_DV, GLA_CHUNK
    kw, vw = H * dk, H * dv
    wal = jnp.zeros((LANES, kw), F32).at[32:32 + GLA_RANK].set(w_alpha).astype(BF16)
    r = np.arange(ct)
    tri = (r[:, None] // C == r[None, :] // C) & (r[:, None] >= r[None, :])
    tot = r[:, None] // C == r[None, :] // C
    cum = jnp.asarray(np.concatenate([tri, tot], axis=0).astype(np.float32), BF16)
    full = lambda shp: pl.BlockSpec(shp, lambda b, i: (0,) * len(shp))
    return pl.pallas_call(
        functools.partial(_gla_kernel, ct=ct),
        out_shape=jax.ShapeDtypeStruct((B, S, vw), BF16),
        grid=(B, S // ct),
        in_specs=[
            pl.BlockSpec((1, ct, 2 * kw), lambda b, i: (b, i, 0)),
            pl.BlockSpec((1, ct, vw), lambda b, i: (b, i, 0)),
            pl.BlockSpec((1, vw, ct), lambda b, i: (b, 0, i)),
            pl.BlockSpec((1, ct, LANES), lambda b, i: (b, i, 0)),
            pl.BlockSpec((1, ct, vw), lambda b, i: (b, i, 0)),
            full((LANES, kw)), full((1, kw)), full((1, vw)), full((2 * ct, ct)),
        ],
        out_specs=pl.BlockSpec((1, ct, vw), lambda b, i: (b, i, 0)),
        scratch_shapes=[pltpu.VMEM((H, dv, kw), F32), pltpu.VMEM((ct, vw), F32)],
        compiler_params=pltpu.CompilerParams(dimension_semantics=("parallel", "arbitrary"),
                                             vmem_limit_bytes=VMEM_LIMIT),
        name="gla",
    )(qkb, vb, vtb, misc, rb, wal, b_alpha.reshape(1, kw), norm_g.reshape(1, vw), cum)


ROUTER_ROWS = 128
EXPERT_ROW0 = 8


def _local_rows(tm):
    return -(-(2 * tm + N_EXPERTS * (SUBLANES - 1)) // LANES) * LANES


def _pack_bf16_pairs(x):
    u = pltpu.bitcast(x, jnp.uint32)
    w = x.shape[1] // 2
    return u[:, :w] | (u[:, w:] >> 16)


def _unpack_bf16_pairs(w):
    xh = pltpu.bitcast(w & jnp.uint32(0xFFFF0000), F32)
    xl = pltpu.bitcast(w << 16, F32)
    return jnp.concatenate([xh, xl], axis=1).astype(BF16)


def _outproj_kernel(ya_ref, yb_ref, mg_ref, x_ref, wa_ref, wb_ref, wo_ref, gf_ref, wrh_ref, wrl_ref, br_ref,
                    tri_ref, ltri_ref, x1_ref, xsl_ref, seg_ref, rw_ref, cnt_ref, carry_s, *, tm, ls):
    D = x_ref.shape[1]

    @pl.when(pl.program_id(0) == 0)
    def _():
        carry_s[...] = jnp.zeros_like(carry_s)

    ma = _nn(ya_ref[...], wa_ref[...])
    mb = _nn(yb_ref[...], wb_ref[...])
    merged = jax.nn.sigmoid(mg_ref[:, :D]) * ma + jax.nn.sigmoid(mg_ref[:, D:]) * mb
    x1 = x_ref[...] + _nn(merged.astype(BF16), wo_ref[...])
    x1_ref[...] = x1
    ms = jnp.mean(x1 * x1, axis=-1, keepdims=True)
    h2 = x1 * lax.rsqrt(ms + EPS) * gf_ref[...]
    hi = h2.astype(BF16)
    lo = (h2 - hi.astype(F32)).astype(BF16)
    lg = _nt(wrh_ref[...], hi) + _nt(wrh_ref[...], lo) + _nt(wrl_ref[...], hi) + br_ref[...]
    row8 = lax.broadcasted_iota(jnp.int32, (SUBLANES, tm), 0)
    gl = jnp.where(row8 < N_GROUPS, lg[0:SUBLANES], NEG)
    gmax = jnp.max(gl, axis=0, keepdims=True)
    g_sel = jnp.min(jnp.where(gl == gmax, row8, SUBLANES), axis=0, keepdims=True)
    g_prob = 1.0 / jnp.sum(jnp.where(row8 < N_GROUPS, jnp.exp(gl - gmax), 0.0), axis=0, keepdims=True)
    e_sel = jnp.zeros((EPG, tm), F32)
    for gi in range(N_GROUPS):
        r0 = EXPERT_ROW0 + gi * EPG
        e_sel = e_sel + jnp.where(g_sel == gi, lg[r0:r0 + EPG], 0.0)
    v1 = jnp.max(e_sel, axis=0, keepdims=True)
    i1 = jnp.min(jnp.where(e_sel == v1, row8, EPG), axis=0, keepdims=True)
    rest = jnp.where(row8 == i1, -jnp.inf, e_sel)
    v2 = jnp.max(rest, axis=0, keepdims=True)
    i2 = jnp.min(jnp.where(rest == v2, row8, EPG), axis=0, keepdims=True)
    t = jnp.exp(v2 - v1)
    w1 = g_prob / (1.0 + t)
    w2 = g_prob * t / (1.0 + t)
    e1 = g_sel * EPG + i1
    e2 = g_sel * EPG + i2

    rowe = lax.broadcasted_iota(jnp.int32, (N_EXPERTS, tm), 0)
    oh1 = rowe == e1
    oh2 = rowe == e2
    oh = jnp.where(oh1, 1.0, 0.0) + jnp.where(oh2, 1.0, 0.0)
    pre = _nn(oh.astype(BF16), tri_ref[...])
    cnt = jnp.sum(oh, axis=1, keepdims=True)
    cnt8 = jnp.floor((cnt + (SUBLANES - 1)) * (1.0 / SUBLANES)) * SUBLANES
    c_b = jnp.broadcast_to(cnt8, (N_EXPERTS, LANES))
    c_hi = jnp.floor(c_b * (1.0 / 16.0))
    c_lo = c_b - 16.0 * c_hi
    base = 16.0 * _nn(ltri_ref[...], c_hi.astype(BF16)) + _nn(ltri_ref[...], c_lo.astype(BF16))
    loc = pre + base[:, 0:1]
    lpos1 = jnp.sum(jnp.where(oh1, loc, 0.0), axis=0, keepdims=True)
    lpos2 = jnp.sum(jnp.where(oh2, loc, 0.0), axis=0, keepdims=True)
    lane = lax.broadcasted_iota(jnp.int32, (N_EXPERTS, LANES), 1)
    seg_ref[0] = jnp.where(lane == 0, base, jnp.where(lane == 1, c_b, carry_s[...])).astype(jnp.int32)
    carry_s[...] = carry_s[...] + c_b
    cnt_ref[...] = carry_s[...].astype(jnp.int32)
    rw_ref[0] = jnp.concatenate([w1, w2, lpos1, lpos2, jnp.zeros((SUBLANES - 4, tm), F32)], axis=0)

    srow = lax.broadcasted_iota(jnp.int32, (ls, tm), 0)
    perm = jnp.where(srow == lpos1.astype(jnp.int32), 1.0, jnp.where(srow == lpos2.astype(jnp.int32), 1.0, 0.0))
    xsorted = _nn(perm.astype(BF16), hi)
    xsl_ref[0] = _pack_bf16_pairs(xsorted)


def _outproj(ya, yb, mg, x, wa, wb, wo, g_ffn, w_rg, b_rg, w_re, b_re, tm=MOE_TM):
    T, D = x.shape
    nt = T // tm
    wr = jnp.zeros((ROUTER_ROWS, D), F32).at[0:N_GROUPS].set(w_rg.T).at[EXPERT_ROW0:EXPERT_ROW0 + N_EXPERTS].set(w_re.T)
    wrh = wr.astype(BF16)
    wrl = (wr - wrh.astype(F32)).astype(BF16)
    br = jnp.zeros((ROUTER_ROWS, 1), F32).at[0:N_GROUPS, 0].set(b_rg).at[EXPERT_ROW0:EXPERT_ROW0 + N_EXPERTS, 0].set(b_re)
    r = np.arange(tm)
    tri = jnp.asarray((r[:, None] < r[None, :]).astype(np.float32), BF16)
    re = np.arange(N_EXPERTS)
    ltri = jnp.asarray((re[None, :] < re[:, None]).astype(np.float32), BF16)
    ls = _local_rows(tm)
    row = lambda n: pl.BlockSpec((tm, n), lambda i: (i, 0))
    full = lambda shp: pl.BlockSpec(shp, lambda i: (0,) * len(shp))
    return pl.pallas_call(
        functools.partial(_outproj_kernel, tm=tm, ls=ls),
        out_shape=[jax.ShapeDtypeStruct((T, D), F32), jax.ShapeDtypeStruct((nt, ls, D // 2), jnp.uint32),
                   jax.ShapeDtypeStruct((nt, N_EXPERTS, LANES), jnp.int32),
                   jax.ShapeDtypeStruct((nt, SUBLANES, tm), F32),
                   jax.ShapeDtypeStruct((N_EXPERTS, LANES), jnp.int32)],
        grid=(nt,),
        in_specs=[row(ya.shape[1]), row(yb.shape[1]), row(2 * D), row(D),
                  full(wa.shape), full(wb.shape), full(wo.shape), full((1, D)),
                  full((ROUTER_ROWS, D)), full((ROUTER_ROWS, D)), full((ROUTER_ROWS, 1)), full((tm, tm)),
                  full((N_EXPERTS, N_EXPERTS))],
        out_specs=[row(D), pl.BlockSpec((1, ls, D // 2), lambda i: (i, 0, 0)),
                   pl.BlockSpec((1, N_EXPERTS, LANES), lambda i: (i, 0, 0)),
                   pl.BlockSpec((1, SUBLANES, tm), lambda i: (i, 0, 0)),
                   full((N_EXPERTS, LANES))],
        scratch_shapes=[pltpu.VMEM((N_EXPERTS, LANES), F32)],
        compiler_params=pltpu.CompilerParams(dimension_semantics=("arbitrary",), vmem_limit_bytes=VMEM_LIMIT),
        name="outproj",
    )(ya, yb, mg, x, wa, wb, wo, g_ffn.reshape(1, D), wrh, wrl, br, tri, ltri)


SEG_FIELDS = 4
SEG_BITS = 7
TAIL_BITS = 5


def _segment_copies(n8, bits, make_copy, wait):
    off = 0
    for bit in reversed(range(bits)):
        rows = SUBLANES << bit
        take = (n8 >> bit) & 1

        @pl.when(take == 1)
        def _(off=off, rows=rows):
            cp = make_copy(off, rows)
            if wait:
                cp.wait()
            else:
                cp.start()

        off = off + take * rows


def _dispatch_kernel(seg_ref, tail_ref, xsl_ref, xs_ref, zero_s, sem):
    i = pl.program_id(0)

    def segments(wait):
        def body(e, c):
            s0 = (i * N_EXPERTS + e) * SEG_FIELDS
            src0 = pl.multiple_of(seg_ref[s0], SUBLANES)
            dst0 = pl.multiple_of(seg_ref[s0 + 2], SUBLANES)

            def make_copy(off, rows):
                return pltpu.make_async_copy(xsl_ref.at[0, pl.ds(pl.multiple_of(src0 + off, SUBLANES), rows), :],
                                             xs_ref.at[pl.ds(pl.multiple_of(dst0 + off, SUBLANES), rows), :], sem)

            _segment_copies(seg_ref[s0 + 1], SEG_BITS, make_copy, wait)
            return c

        lax.fori_loop(0, N_EXPERTS, body, 0)

    def tails(wait):
        def body(e, c):
            dst0 = pl.multiple_of(tail_ref[2 * e], SUBLANES)

            def make_copy(off, rows):
                return pltpu.make_async_copy(zero_s.at[pl.ds(0, rows), :],
                                             xs_ref.at[pl.ds(pl.multiple_of(dst0 + off, SUBLANES), rows), :], sem)

            _segment_copies(tail_ref[2 * e + 1], TAIL_BITS, make_copy, wait)
            return c

        lax.fori_loop(0, N_EXPERTS, body, 0)

    def unused_tiles(wait):
        def body(t, c):
            cp = pltpu.make_async_copy(zero_s, xs_ref.at[pl.ds(pl.multiple_of(t * TE, TE), TE), :], sem)
            if wait:
                cp.wait()
            else:
                cp.start()
            return c

        lax.fori_loop(tail_ref[2 * N_EXPERTS], xs_ref.shape[0] // TE, body, 0)

    @pl.when(i == 0)
    def _():
        zero_s[...] = jnp.zeros_like(zero_s)
        tails(False)
        unused_tiles(False)
        tails(True)
        unused_tiles(True)

    segments(False)
    segments(True)


def _dispatch(seg, tail, xsl, n_rows):
    nt, ls, W = xsl.shape
    return pl.pallas_call(
        _dispatch_kernel,
        out_shape=jax.ShapeDtypeStruct((n_rows, W), xsl.dtype),
        grid_spec=pltpu.PrefetchScalarGridSpec(
            num_scalar_prefetch=2,
            grid=(nt,),
            in_specs=[pl.BlockSpec((1, ls, W), lambda i, sg, tl: (i, 0, 0))],
            out_specs=pl.BlockSpec(memory_space=pl.ANY),
            scratch_shapes=[pltpu.VMEM((TE, W), xsl.dtype), pltpu.SemaphoreType.DMA(())],
        ),
        compiler_params=pltpu.CompilerParams(dimension_semantics=("arbitrary",)),
        name="dispatch",
    )(seg, tail, xsl)


def _experts_kernel(te_ref, nv_ref, xs_ref, wg_ref, wu_ref, wd_ref, out_ref):
    i = pl.program_id(0)

    @pl.when(i < nv_ref[0])
    def _():
        x = _unpack_bf16_pairs(xs_ref[...])
        a = _nn(x, wg_ref[0].astype(BF16))
        u = _nn(x, wu_ref[0].astype(BF16))
        hid = (a * jax.nn.sigmoid(a)) * u
        y = _nn(hid.astype(BF16), wd_ref[0].astype(BF16))
        out_ref[...] = _pack_bf16_pairs(y.astype(BF16).astype(F32))

    @pl.when(i >= nv_ref[0])
    def _():
        out_ref[...] = jnp.zeros_like(out_ref)


def _experts(tile_expert, n_valid, xs, w_gate, w_up, w_down):
    n_rows, W = xs.shape
    D = 2 * W
    n_tiles = n_rows // TE
    last = lambda i, nv: jnp.minimum(i, nv[0] - 1)
    return pl.pallas_call(
        _experts_kernel,
        out_shape=jax.ShapeDtypeStruct((n_rows, W), jnp.uint32),
        grid_spec=pltpu.PrefetchScalarGridSpec(
            num_scalar_prefetch=2,
            grid=(n_tiles,),
            in_specs=[pl.BlockSpec((TE, W), lambda i, te, nv: (last(i, nv), 0)),
                      pl.BlockSpec((1, D, EXPERT_FF), lambda i, te, nv: (te[last(i, nv)], 0, 0)),
                      pl.BlockSpec((1, D, EXPERT_FF), lambda i, te, nv: (te[last(i, nv)], 0, 0)),
                      pl.BlockSpec((1, EXPERT_FF, D), lambda i, te, nv: (te[last(i, nv)], 0, 0))],
            out_specs=pl.BlockSpec((TE, W), lambda i, te, nv: (i, 0)),
        ),
        compiler_params=pltpu.CompilerParams(dimension_semantics=("arbitrary",), vmem_limit_bytes=VMEM_LIMIT),
        name="experts",
    )(tile_expert, n_valid, xs, w_gate, w_up, w_down)


def _combine_kernel(seg_ref, ys_ref, x1_ref, rw_ref, gfin_ref, out_ref, buf, sem, *, tm, ls, apply_norm):
    i = pl.program_id(0)

    @pl.when(i == 0)
    def _():
        buf[...] = jnp.zeros_like(buf)

    def segments(wait):
        def body(e, c):
            s0 = (i * N_EXPERTS + e) * SEG_FIELDS
            loc0 = pl.multiple_of(seg_ref[s0], SUBLANES)
            glob0 = pl.multiple_of(seg_ref[s0 + 2], SUBLANES)

            def make_copy(off, rows):
                return pltpu.make_async_copy(ys_ref.at[pl.ds(pl.multiple_of(glob0 + off, SUBLANES), rows), :],
                                             buf.at[pl.ds(pl.multiple_of(loc0 + off, SUBLANES), rows), :], sem)

            _segment_copies(seg_ref[s0 + 1], SEG_BITS, make_copy, wait)
            return c

        lax.fori_loop(0, N_EXPERTS, body, 0)

    segments(False)
    segments(True)
    ysl = _unpack_bf16_pairs(buf[...])
    cols = jnp.concatenate([rw_ref[0], jnp.zeros((LANES - SUBLANES, tm), F32)], axis=0).T
    srow = lax.broadcasted_iota(jnp.int32, (tm, ls), 1)
    y = x1_ref[...]
    for k in range(2):
        pick = jnp.where(srow == cols[:, 2 + k:3 + k].astype(jnp.int32), 1.0, 0.0).astype(BF16)
        y = y + cols[:, k:k + 1] * _nn(pick, ysl)
    if apply_norm:
        ms = jnp.mean(y * y, axis=-1, keepdims=True)
        y = y * lax.rsqrt(ms + EPS) * gfin_ref[...]
    out_ref[...] = y


def _combine(seg, ys, x1, rw, g_final, apply_norm, tm=MOE_TM):
    T, D = x1.shape
    nt = T // tm
    ls = _local_rows(tm)
    return pl.pallas_call(
        functools.partial(_combine_kernel, tm=tm, ls=ls, apply_norm=apply_norm),
        out_shape=jax.ShapeDtypeStruct((T, D), F32),
        grid_spec=pltpu.PrefetchScalarGridSpec(
            num_scalar_prefetch=1,
            grid=(nt,),
            in_specs=[pl.BlockSpec(memory_space=pl.ANY),
                      pl.BlockSpec((tm, D), lambda i, sg: (i, 0)),
                      pl.BlockSpec((1, SUBLANES, tm), lambda i, sg: (i, 0, 0)),
                      pl.BlockSpec((1, D), lambda i, sg: (0, 0))],
            out_specs=pl.BlockSpec((tm, D), lambda i, sg: (i, 0)),
            scratch_shapes=[pltpu.VMEM((ls, D // 2), jnp.uint32), pltpu.SemaphoreType.DMA(())],
        ),
        compiler_params=pltpu.CompilerParams(dimension_semantics=("arbitrary",), vmem_limit_bytes=VMEM_LIMIT),
        name="combine",
    )(seg, ys, x1, rw, g_final.reshape(1, D))


def _moe_plan(seg, counts, T, tm=MOE_TM):
    nt = T // tm
    n_tiles_max = (2 * T + nt * N_EXPERTS * (SUBLANES - 1)) // TE + N_EXPERTS
    total = counts[:, 0]
    tiles = (total + TE - 1) // TE
    ids = jnp.arange(N_EXPERTS)
    tile_end = jnp.sum(jnp.where(ids[None, :] <= ids[:, None], tiles[None, :], 0), axis=1)
    row0 = (tile_end - tiles) * TE
    segtab = jnp.stack([seg[:, :, 0], seg[:, :, 1] // SUBLANES, seg[:, :, 2] + row0[None, :],
                        jnp.zeros_like(seg[:, :, 0])], axis=-1).reshape(-1).astype(jnp.int32)
    tail = jnp.concatenate([jnp.stack([row0 + total, (tiles * TE - total) // SUBLANES], axis=-1).reshape(-1),
                            tile_end[-1:]]).astype(jnp.int32)
    tile_expert = jnp.minimum(jnp.sum(tile_end[None, :] <= jnp.arange(n_tiles_max)[:, None], axis=1),
                              N_EXPERTS - 1).astype(jnp.int32)
    return segtab, tail, tile_expert, tile_end[-1:].astype(jnp.int32), n_tiles_max * TE


def kernel(x, g_mix, w_in, nsa_pe_k, nsa_cmp_k_w1, nsa_cmp_k_w2, nsa_pe_v, nsa_cmp_v_w1, nsa_cmp_v_w2, rel_bias,
           gla_w_alpha, gla_b_alpha, gla_norm_g, w_branch_a, w_branch_b, w_out, g_ffn, w_router_group,
           b_router_group, w_router_expert, b_router_expert, w_exp_gate, w_exp_up, w_exp_down, g_final):
    B, S, D = x.shape
    T = B * S
    for l in range(w_in.shape[0]):
        w, wt = _pack_inproj_weights(w_in[l])
        (q, kslc, kwin, kvc, misc, qkb, vb, rb, mg, vts, vtw, vtb) = _inproj(x, g_mix[l], w, wt)
        kcb, vcbt = _compress(kvc, nsa_pe_k[l], nsa_cmp_k_w1[l], nsa_cmp_k_w2[l],
                              nsa_pe_v[l], nsa_cmp_v_w1[l], nsa_cmp_v_w2[l])
        ya = _nsa(q, kslc, vts, kwin, vtw, kcb, vcbt, misc, rel_bias)
        yb = _gla(qkb, vb, vtb, misc, rb, gla_w_alpha[l], gla_b_alpha[l], gla_norm_g[l])
        x1, xsl, seg, rw, counts = _outproj(
            ya.reshape(T, -1), yb.reshape(T, -1), mg.reshape(T, -1), x.reshape(T, D),
            w_branch_a[l].astype(BF16), w_branch_b[l].astype(BF16), w_out[l].astype(BF16), g_ffn[l],
            w_router_group[l], b_router_group[l], w_router_expert[l], b_router_expert[l])
        segtab, tail, tile_expert, n_valid, n_rows = _moe_plan(seg, counts, T)
        xs = _dispatch(segtab, tail, xsl, n_rows)
        ys = _experts(tile_expert, n_valid, xs, w_exp_gate[l], w_exp_up[l], w_exp_down[l])
        last_layer = l == w_in.shape[0] - 1
        x = _combine(segtab, ys, x1, rw, g_final, apply_norm=last_layer).reshape(B, S, D)
    return x
```

```python
import functools
import math

import numpy as np
import jax
import jax.numpy as jnp
from jax import lax
from jax.experimental import pallas as pl
from jax.experimental.pallas import tpu as pltpu

F32 = jnp.float32
BF16 = jnp.bfloat16

NSA_HEADS = 8
NSA_GROUPS = 2
HPG = NSA_HEADS // NSA_GROUPS
DH = 64
CMP_BLOCK = 32
CMP_STRIDE = 16
CMP_HIDDEN = 128
SLC_BLOCK = 64
SLC_TOPN = 16
WINDOW = 512
GLA_HEADS = 4
GLA_DK = 64
GLA_DV = 128
GLA_RANK = 16
GLA_TAU = 16.0
GLA_CHUNK = 64
REL_BUCKETS = 32
REL_MAX_EXACT = REL_BUCKETS // 2
REL_MAX_DIST = 128
N_GROUPS = 4
EPG = 8
N_EXPERTS = N_GROUPS * EPG
EXPERT_FF = 256
EPS = 1e-6

LANES = 128
SUBLANES = 8
VMEM_LIMIT = 56 * 1024 * 1024

LOG2E = math.log2(math.e)
NEG = -1e30
BIG = float(2.0 ** 100)
QT = 128
SLC_PAD = 128
WIN_PAD = 512
TE = 512
MOE_TM = 512
VT_ROWS = DH + 16


def _nt(a, b):
    return lax.dot_general(a, b, (((1,), (1,)), ((), ())), preferred_element_type=F32)


def _nn(a, b):
    return jnp.dot(a, b, preferred_element_type=F32)


def _split3(x):
    a = x.astype(BF16)
    r = x - a.astype(F32)
    b = r.astype(BF16)
    c = (r - b.astype(F32)).astype(BF16)
    return a, b, c


def _t5_bucket_np(rel):
    n = np.maximum(rel, 0)
    nf = np.maximum(n, 1).astype(np.float32)
    large = REL_MAX_EXACT + (np.log(nf / np.float32(REL_MAX_EXACT)) / np.float32(math.log(REL_MAX_DIST / REL_MAX_EXACT))
                             * np.float32(REL_BUCKETS - REL_MAX_EXACT)).astype(np.int32)
    return np.where(n < REL_MAX_EXACT, n, np.minimum(large, REL_BUCKETS - 1)).astype(np.int32)


def _inproj_kernel(x_ref, g_ref, w_ref, wt_ref, oq, okslc, okwin, okv, omisc, oqkb, ovb, orb, omg,
                   ovts, ovtw, ovtb, *, tm, seq):
    x = x_ref[0]
    ms = jnp.mean(x * x, axis=-1, keepdims=True)
    h = (x * lax.rsqrt(ms + EPS) * g_ref[...]).astype(BF16)

    def mm(a, b):
        return _nn(h, w_ref[:, a:b])

    oq[0] = mm(0, 1024).astype(BF16)
    ks = mm(1024, 1280)
    row = lax.broadcasted_iota(jnp.int32, (tm, 256), 0) + pl.program_id(1) * tm
    lane = lax.broadcasted_iota(jnp.int32, (tm, 256), 1) % LANES
    onehot = jnp.where(lane - DH == row // SLC_BLOCK, 1.0, 0.0)
    okslc[0] = (ks + onehot).astype(BF16)
    okwin[0] = mm(1280, 1536).astype(BF16)
    okv[0] = mm(1536, 1792)
    omisc[0] = mm(1792, 1920)
    oqkb[0] = mm(1920, 2432)
    ovb[0] = mm(2432, 2944).astype(BF16)
    orb[0] = mm(2944, 3456)
    omg[0] = mm(3456, 5504)
    vt = _nt(wt_ref[...], h)
    nv = NSA_GROUPS * VT_ROWS
    ones_row = jnp.where(lax.broadcasted_iota(jnp.int32, (nv, tm), 0) % VT_ROWS == DH, 1.0, 0.0)
    ovts[0] = (vt[0:nv] + ones_row).astype(BF16)
    ovtw[0] = (vt[nv:2 * nv] + ones_row).astype(BF16)
    ovtb[0] = vt[2 * nv:].astype(BF16)


def _pack_inproj_weights(w_in):
    o = np.cumsum([0, 512, 128, 128, 128, 128, 128, 128, 24, 256, 256, 512, 16, 512, 1024, 1024])
    (q_a, k_cmp, v_cmp, k_slc, v_slc, k_win, v_win, gate_a, q_b, k_b, v_b, a_b, r_b, mg_a, mg_b) = [
        w_in[:, o[i]:o[i + 1]] for i in range(15)]
    D = w_in.shape[0]
    z64 = jnp.zeros((D, DH), w_in.dtype)
    qcols = []
    for hd in range(NSA_HEADS):
        qcols += [q_a[:, hd * DH:(hd + 1) * DH] * (DH ** -0.5 * LOG2E), z64]
    kslc = [k_slc[:, :DH], z64, k_slc[:, DH:], z64]
    kwin = [k_win[:, :DH], z64, k_win[:, DH:], z64]
    ga = gate_a.reshape(D, 3, NSA_GROUPS, HPG)
    z4 = jnp.zeros((D, 4), w_in.dtype)
    misc = [ga[:, :, 0, :].reshape(D, 12), z4, ga[:, :, 1, :].reshape(D, 12), z4, a_b,
            jnp.zeros((D, LANES - 48), w_in.dtype)]
    w = jnp.concatenate(qcols + kslc + kwin + [k_cmp, v_cmp] + misc + [q_b, k_b, v_b, r_b, mg_a, mg_b], axis=1)
    zv = jnp.zeros((D, VT_ROWS - DH), w_in.dtype)
    wt = jnp.concatenate([v_slc[:, :DH], zv, v_slc[:, DH:], zv, v_win[:, :DH], zv, v_win[:, DH:], zv, v_b], axis=1).T
    return w.astype(BF16), wt.astype(BF16)


def _inproj(x, g_mix, w, wt, tm=512):
    B, S, D = x.shape
    nw = w.shape[1]
    widths = [(1024, BF16), (256, BF16), (256, BF16), (256, F32), (128, F32), (512, F32), (512, BF16),
              (512, F32), (2048, F32)]
    out_shape = [jax.ShapeDtypeStruct((B, S, n), dt) for n, dt in widths]
    out_specs = [pl.BlockSpec((1, tm, n), lambda b, i: (b, i, 0)) for n, _ in widths]
    for rows in (NSA_GROUPS * VT_ROWS, NSA_GROUPS * VT_ROWS, 512):
        out_shape.append(jax.ShapeDtypeStruct((B, rows, S), BF16))
        out_specs.append(pl.BlockSpec((1, rows, tm), lambda b, i: (b, 0, i)))
    return pl.pallas_call(
        functools.partial(_inproj_kernel, tm=tm, seq=S),
        out_shape=out_shape,
        grid=(B, S // tm),
        in_specs=[
            pl.BlockSpec((1, tm, D), lambda b, i: (b, i, 0)),
            pl.BlockSpec((1, D), lambda b, i: (0, 0)),
            pl.BlockSpec((D, nw), lambda b, i: (0, 0), pipeline_mode=pl.Buffered(1)),
            pl.BlockSpec(wt.shape, lambda b, i: (0, 0), pipeline_mode=pl.Buffered(1)),
        ],
        out_specs=out_specs,
        compiler_params=pltpu.CompilerParams(dimension_semantics=("parallel", "parallel"),
                                             vmem_limit_bytes=VMEM_LIMIT),
        name="inproj",
    )(x, g_mix.reshape(1, D), w, wt)


def _gelu_tanh(x):
    return 0.5 * x * (1.0 + jnp.tanh(math.sqrt(2.0 / math.pi) * (x + 0.044715 * (x * x * x))))


def _compress_kernel(xk_ref, xv_ref, pe_ref, w1_ref, w2k_ref, w2vt_ref, ok_ref, ovt_ref, *, nsub):
    for kind, x_ref in enumerate((xk_ref, xv_ref)):
        top = jnp.zeros((nsub, 2 * CMP_HIDDEN), F32)
        bot = jnp.zeros((nsub, 2 * CMP_HIDDEN), F32)
        for r in range(CMP_STRIDE):
            xr = x_ref[0, pl.ds(r, nsub, stride=CMP_STRIDE), :]
            top = top + _nn((xr + pe_ref[kind, 0, r:r + 1, :]).astype(BF16), w1_ref[kind, 0, r])
            bot = bot + _nn((xr + pe_ref[kind, 1, r:r + 1, :]).astype(BF16), w1_ref[kind, 1, r])
        hid = _gelu_tanh(top + pltpu.roll(bot, shift=nsub - 1, axis=0)).astype(BF16)
        for g in range(NSA_GROUPS):
            hg = hid[:, g * CMP_HIDDEN:(g + 1) * CMP_HIDDEN]
            if kind == 0:
                ok_ref[0, g] = _nn(hg, w2k_ref[...]).astype(BF16)
            else:
                ovt_ref[0, g] = _nt(w2vt_ref[...], hg).astype(BF16)


def _compress(kv_cmp, pe_k, w1k, w2k, pe_v, w1v, w2v):
    B, S, _ = kv_cmp.shape
    nsub = S // CMP_STRIDE
    G = NSA_GROUPS

    def prep(pe, w1):
        pe_t = jnp.tile(pe.reshape(2, CMP_STRIDE, DH), (1, 1, G))
        a = w1.reshape(2, CMP_STRIDE, DH, CMP_HIDDEN)
        z = jnp.zeros_like(a)
        w = jnp.concatenate([jnp.concatenate([a, z], axis=3), jnp.concatenate([z, a], axis=3)], axis=2)
        return pe_t, w.astype(BF16)

    pek, w1kb = prep(pe_k, w1k)
    pev, w1vb = prep(pe_v, w1v)
    pe = jnp.stack([pek, pev])
    w1 = jnp.stack([w1kb, w1vb])
    w2kp = jnp.concatenate([w2k, jnp.zeros_like(w2k)], axis=1).astype(BF16)
    w2vt = w2v.T.astype(BF16)
    full = lambda shp: pl.BlockSpec(shp, lambda b: (0,) * len(shp))
    return pl.pallas_call(
        functools.partial(_compress_kernel, nsub=nsub),
        out_shape=[jax.ShapeDtypeStruct((B, G, nsub, LANES), BF16),
                   jax.ShapeDtypeStruct((B, G, DH, nsub), BF16)],
        grid=(B,),
        in_specs=[pl.BlockSpec((1, S, G * DH), lambda b: (b, 0, 0)), pl.BlockSpec((1, S, G * DH), lambda b: (b, 0, 1)),
                  full(pe.shape), full(w1.shape), full((CMP_HIDDEN, LANES)), full((DH, CMP_HIDDEN))],
        out_specs=[pl.BlockSpec((1, G, nsub, LANES), lambda b: (b, 0, 0, 0)),
                   pl.BlockSpec((1, G, DH, nsub), lambda b: (b, 0, 0, 0))],
        compiler_params=pltpu.CompilerParams(dimension_semantics=("parallel",), vmem_limit_bytes=VMEM_LIMIT),
        name="compress",
    )(kv_cmp, kv_cmp, pe, w1, w2kp, w2vt)


def _bias_kernel(tbl_ref, bkn_ref, bkc_ref, near_ref, cmpb_ref):
    g = pl.program_id(0)
    for h in range(HPG):
        hd = g * HPG + h

        def lookup(bk):
            acc = jnp.full(bk.shape, NEG, F32)
            for b in range(REL_BUCKETS):
                acc = jnp.where(bk == b, tbl_ref[hd, b], acc)
            return acc

        vn = lookup(bkn_ref[...])
        near_ref[0, :, h * QT:(h + 1) * QT] = jnp.where(vn > 0.5 * NEG, (vn - tbl_ref[hd, REL_BUCKETS - 1]) * LOG2E, NEG)
        vc = lookup(bkc_ref[...])
        cmpb_ref[0, :, h * QT:(h + 1) * QT] = jnp.where(vc > 0.5 * NEG, vc * LOG2E, NEG)


def _nsa_bias_tables(rel_bias, seq):
    ql = np.arange(QT)
    ncmp = seq // CMP_STRIDE

    def buckets(rel):
        return jnp.asarray(np.where(rel >= 0, _t5_bucket_np(rel), -1).astype(np.int32))

    bkn = buckets(ql[None, :] + QT - np.arange(2 * QT)[:, None])
    y = np.arange(2 * ncmp)
    bkc = buckets(ql[None, :] - CMP_STRIDE * (y[:, None] - ncmp) - (CMP_BLOCK - 1))
    nql = HPG * QT
    return pl.pallas_call(
        _bias_kernel,
        out_shape=[jax.ShapeDtypeStruct((NSA_GROUPS, 2 * QT, nql), F32),
                   jax.ShapeDtypeStruct((NSA_GROUPS, 2 * ncmp, nql), F32)],
        grid=(NSA_GROUPS,),
        in_specs=[pl.BlockSpec(memory_space=pltpu.SMEM),
                  pl.BlockSpec((2 * QT, QT), lambda g: (0, 0)),
                  pl.BlockSpec((2 * ncmp, QT), lambda g: (0, 0))],
        out_specs=[pl.BlockSpec((1, 2 * QT, nql), lambda g: (g, 0, 0)),
                   pl.BlockSpec((1, 2 * ncmp, nql), lambda g: (g, 0, 0))],
        compiler_params=pltpu.CompilerParams(dimension_semantics=("parallel",)),
        name="t5bias",
    )(rel_bias.T, bkn, bkc)


def _nsa_kernel(q_ref, kslc_ref, vtslc_ref, kwin_ref, vtwin_ref, kcb_ref, vcbt_ref, misc_ref,
                near_ref, cmpb_ref, wmask_ref, ovt_ref, eye_ref, eye4_ref,
                out_ref, qaug_s, s0_s, s1_s, *, ncmp, nslc):
    G = NSA_GROUPS
    qt = pl.program_id(1)
    nql = HPG * QT
    ck = 2 * QT

    def flash(carry, s, vt_chunk):
        m, acc = carry
        m_new = jnp.maximum(m, jnp.max(s, axis=0, keepdims=True))
        alpha = jnp.exp2(m - m_new)
        p = jnp.exp2((s - m_new).astype(BF16))
        return m_new, alpha * acc + _nn(vt_chunk, p)

    def finish(carry):
        m, acc = carry
        return acc[:DH] * (1.0 / acc[DH:DH + 1])

    init = (jnp.full((1, nql), NEG, F32), jnp.zeros((VT_ROWS, nql), F32))
    ns = pl.multiple_of(QT * qt, QT)
    nw = pl.multiple_of(QT * qt + 3 * QT, QT)
    off = pl.multiple_of(ncmp - (QT // CMP_STRIDE) * qt, SUBLANES)
    lane = lax.broadcasted_iota(jnp.int32, (nql, LANES), 1)
    win_aug = jnp.where(lane >= DH, -BIG, 0.0).astype(BF16)
    jidx = lax.broadcasted_iota(jnp.int32, (nslc, QT), 0)
    tq = qt * QT + lax.broadcasted_iota(jnp.int32, (nslc, QT), 1)
    tb = tq // SLC_BLOCK
    forced = (jidx == 0) | (jidx == tb) | (jidx == tb - 1)
    future = jidx * SLC_BLOCK > tq
    sub = lax.broadcasted_iota(jnp.int32, (SUBLANES, QT), 0)
    ones_lo = jnp.ones((DH, QT), F32)
    eye = eye_ref[...]
    ovt = ovt_ref[...]
    ngrp = nslc // SUBLANES

    def aug(sel01):
        rows = [ones_lo, sel01]
        if LANES - DH - nslc:
            rows.append(jnp.ones((LANES - DH - nslc, QT), F32))
        m01 = _nt(eye, jnp.concatenate(rows, axis=0).astype(BF16))
        return jnp.concatenate([((m01 - 1.0) * BIG).astype(BF16)] * HPG, axis=0)

    gs = range(G)
    kl = [slice(g * LANES, (g + 1) * LANES) for g in gs]
    vr = [slice(g * VT_ROWS, (g + 1) * VT_ROWS) for g in gs]
    q0 = [jnp.concatenate([q_ref[0, :, (g * HPG + h) * LANES:(g * HPG + h + 1) * LANES] for h in range(HPG)], axis=0)
          for g in gs]
    qwin = [q0[g] + win_aug for g in gs]

    bc = [cmpb_ref[g, pl.ds(off, ncmp), :] for g in gs]
    sc = [_nt(kcb_ref[0, g], q0[g]) + bc[g] for g in gs]
    sw = [_nt(kwin_ref[0, pl.ds(ns, 3 * QT), kl[g]], qwin[g]) for g in gs]
    sw = [jnp.concatenate([sw[g][:QT] + wmask_ref[...], sw[g][QT:]], axis=0) for g in gs]

    pc = []
    for g in gs:
        mc = jnp.max(sc[g], axis=0, keepdims=True)
        ec = jnp.where(bc[g] > 0.5 * NEG, jnp.exp2(sc[g] - mc), 0.0)
        den = jnp.maximum(jnp.sum(ec, axis=0, keepdims=True), jnp.finfo(F32).tiny)
        pc.append(ec * (1.0 / den))
    wcar = [flash(init, sw[g], vtwin_ref[0, vr[g], pl.ds(ns, 3 * QT)]) for g in gs]
    o_c = [_nn(vcbt_ref[0, g], pc[g].astype(BF16)) for g in gs]

    imp = []
    for g in gs:
        psum = pc[g][:, 0:QT]
        for h in range(1, HPG):
            psum = psum + pc[g][:, h * QT:(h + 1) * QT]
        p1, p2, p3 = _split3(psum)
        v = _nn(ovt, p1) + _nn(ovt, p2) + _nn(ovt, p3)
        imp.append(jnp.where(forced, 1e30, jnp.where(future, -1e30, v)))
    sw = [_nt(kwin_ref[0, pl.ds(nw, ck), kl[g]], qwin[g]) + near_ref[g] for g in gs]

    grp = [[imp[g][SUBLANES * v:SUBLANES * (v + 1)] for v in range(ngrp)] for g in gs]
    cnt = [[jnp.zeros((SUBLANES, QT), F32) for _ in range(ngrp)] for g in gs]
    for jp in range(nslc):
        v0, r0 = divmod(jp, SUBLANES)
        for g in gs:
            row = jnp.broadcast_to(imp[g][jp:jp + 1, :], (SUBLANES, QT))
            for v in range(ngrp):
                if v < v0:
                    inc = jnp.where(row > grp[g][v], 1.0, 0.0)
                elif v > v0:
                    inc = jnp.where(row >= grp[g][v], 1.0, 0.0)
                else:
                    inc = jnp.where(sub > r0, jnp.where(row >= grp[g][v], 1.0, 0.0),
                                    jnp.where(row > grp[g][v], 1.0, 0.0))
                cnt[g][v] = cnt[g][v] + inc
    o_w = [finish(flash(wcar[g], sw[g], vtwin_ref[0, vr[g], pl.ds(nw, ck)])) for g in gs]
    for g in gs:
        sel = jnp.concatenate(cnt[g], axis=0) < float(min(SLC_TOPN, nslc))
        sel_near = jnp.where(sel, 1.0, 0.0)
        sel_far = jnp.where(jidx < 2 * (qt - 1), sel_near, 0.0)
        qaug_s[g, 0] = q0[g] + aug(sel_far)
        qaug_s[g, 1] = q0[g] + aug(sel_near)

    n_far = qt // 2
    n_chunks = n_far + 1

    def rows(i):
        return pl.multiple_of(QT * qt - ck * i, QT)

    def scores(g, i):
        i = jnp.minimum(i, n_far)
        return _nt(kslc_ref[0, pl.ds(rows(i), ck), g * LANES:(g + 1) * LANES], qaug_s[g, jnp.where(i == 0, 1, 0)])

    def vt(g, i):
        return vtslc_ref[0, g * VT_ROWS:(g + 1) * VT_ROWS, pl.ds(rows(i), ck)]

    for g in range(G):
        s0_s[g] = scores(g, 0) + near_ref[g]

    def pair_body(p, carry):
        i = 2 * p
        for g in range(G):
            s1_s[g] = scores(g, i + 1)
        carry = tuple(flash(carry[g], s0_s[g], vt(g, i)) for g in range(G))
        for g in range(G):
            s0_s[g] = scores(g, i + 2)
        return tuple(flash(carry[g], s1_s[g], vt(g, i + 1)) for g in range(G))

    carry = lax.fori_loop(0, n_chunks // 2, pair_body, (init,) * G)
    carry = lax.cond(n_chunks % 2 == 1,
                     lambda c: tuple(flash(c[g], s0_s[g], vt(g, n_far)) for g in range(G)),
                     lambda c: c, carry)

    gt = jax.nn.sigmoid(misc_ref[0]).T
    head = lax.broadcasted_iota(jnp.int32, (DH, nql), 1) // QT
    for g in range(G):
        o_s = finish(carry[g])

        def gate_row(br):
            r0 = 16 * g + br * HPG
            return jnp.concatenate([gt[r0 + h:r0 + h + 1, :] for h in range(HPG)], axis=1)

        o = gate_row(0) * o_c[g] + gate_row(1) * o_s + gate_row(2) * o_w[g]
        ob = o.astype(BF16)
        blocks = jnp.concatenate([jnp.where(head == h, ob, jnp.zeros_like(ob)) for h in range(HPG)], axis=0)
        out_ref[0, :, g * HPG * DH:(g + 1) * HPG * DH] = _nt(eye4_ref[...], blocks).astype(BF16)


def _nsa(q, kslc, vtslc, kwin, vtwin, kcb, vcbt, misc, rel_bias):
    B, S, _ = q.shape
    G = NSA_GROUPS
    nq = S // QT
    ncmp = S // CMP_STRIDE
    nslc = S // SLC_BLOCK
    nql = HPG * QT
    near, cmpb = _nsa_bias_tables(rel_bias, S)
    wmask = jnp.asarray(np.tile(np.where(np.arange(QT)[:, None] > np.arange(QT)[None, :], 0.0, NEG), (1, HPG)), F32)
    kpad_s = jnp.concatenate([jnp.zeros((DH,), BF16), jnp.ones((DH,), BF16)] * G)
    kslc_p = jnp.concatenate([jnp.broadcast_to(kpad_s, (B, SLC_PAD, G * LANES)), kslc], axis=1)
    kwin_p = jnp.concatenate([jnp.broadcast_to(kpad_s, (B, WIN_PAD, G * LANES)), kwin], axis=1)
    vtslc_p = jnp.pad(vtslc, ((0, 0), (0, 0), (SLC_PAD, 0)))
    vtwin_p = jnp.pad(vtwin, ((0, 0), (0, 0), (WIN_PAD, 0)))
    ci = np.arange(ncmp)[None, :] * CMP_STRIDE
    sj = np.arange(nslc)[:, None] * SLC_BLOCK
    ovt = jnp.asarray(((ci < sj + SLC_BLOCK) & (ci + CMP_BLOCK > sj)).astype(np.float32), BF16)
    eye = jnp.eye(QT, dtype=BF16)
    eye4 = jnp.tile(eye, (1, HPG))
    kern = functools.partial(_nsa_kernel, ncmp=ncmp, nslc=nslc)
    per_b = lambda shp: pl.BlockSpec(shp, lambda b, i: (b,) + (0,) * (len(shp) - 1))
    full = lambda shp: pl.BlockSpec(shp, lambda b, i: (0,) * len(shp))
    return pl.pallas_call(
        kern,
        out_shape=jax.ShapeDtypeStruct((B, S, NSA_HEADS * DH), BF16),
        grid=(B, nq),
        in_specs=[
            pl.BlockSpec((1, QT, NSA_HEADS * LANES), lambda b, i: (b, i, 0)),
            per_b((1, SLC_PAD + S, G * LANES)),
            per_b((1, G * VT_ROWS, SLC_PAD + S)),
            per_b((1, WIN_PAD + S, G * LANES)),
            per_b((1, G * VT_ROWS, WIN_PAD + S)),
            per_b((1, G, ncmp, LANES)),
            per_b((1, G, DH, ncmp)),
            pl.BlockSpec((1, QT, LANES), lambda b, i: (b, i, 0)),
            full((G, 2 * QT, nql)),
            full((G, 2 * ncmp, nql)),
            full((QT, nql)),
            full((nslc, ncmp)),
            full((QT, QT)),
            full((QT, nql)),
        ],
        out_specs=pl.BlockSpec((1, QT, NSA_HEADS * DH), lambda b, i: (b, i, 0)),
        scratch_shapes=[pltpu.VMEM((G, 2, nql, LANES), BF16), pltpu.VMEM((G, 2 * QT, nql), F32),
                        pltpu.VMEM((G, 2 * QT, nql), F32)],
        compiler_params=pltpu.CompilerParams(dimension_semantics=("parallel", "arbitrary"),
                                             vmem_limit_bytes=VMEM_LIMIT),
        name="nsa",
    )(q, kslc_p, vtslc_p, kwin_p, vtwin_p, kcb, vcbt, misc, near, cmpb, wmask, ovt, eye, eye4)


def _gla_kernel(qk_ref, v_ref, vt_ref, misc_ref, r_ref, wal_ref, bal_ref, ng_ref, cum_ref, out_ref,
                state_s, o_s, *, ct):
    H, dk, dv, C = GLA_HEADS, GLA_DK, GLA_DV, GLA_CHUNK
    kw = H * dk

    @pl.when(pl.program_id(1) == 0)
    def _():
        state_s[...] = jnp.zeros_like(state_s)

    z = _nn(misc_ref[0].astype(BF16), wal_ref[...]) + bal_ref[...]
    log_a = (jnp.minimum(z, 0.0) - jnp.log1p(jnp.exp(-jnp.abs(z)))) * (1.0 / GLA_TAU)
    cum = cum_ref[...]
    a1, a2, a3 = _split3(log_a)
    cs = _nn(cum, a1) + _nn(cum, a2) + _nn(cum, a3)
    bc, bl = cs[:ct], cs[ct:]
    q = qk_ref[0, :, :kw]
    k = qk_ref[0, :, kw:]
    q_in = (q * (dk ** -0.5)) * jnp.exp(bc)
    k_in = (k * jnp.exp(-bc)).astype(BF16)
    k_st = k * jnp.exp(bl - bc)
    decay = jnp.exp(bl)
    lane_head = lax.broadcasted_iota(jnp.int32, (C, kw), 1) // dk
    rr = lax.broadcasted_iota(jnp.int32, (H * C, C), 0) % C
    cc = lax.broadcasted_iota(jnp.int32, (H * C, C), 1)
    causal = rr >= cc
    pair_row = lax.broadcasted_iota(jnp.int32, (2 * C, kw), 0) // C
    pair_head = lax.broadcasted_iota(jnp.int32, (2 * C, kw), 1) // dk

    for c in range(ct // C):
        r0 = c * C
        qc = q_in[r0:r0 + C]
        qcb = qc.astype(BF16)
        q_heads = jnp.concatenate([jnp.where(lane_head == h, qc, 0.0) for h in range(H)], axis=0).astype(BF16)
        attn = jnp.where(causal, _nt(q_heads, k_in[r0:r0 + C]), 0.0).astype(BF16)
        p0 = (c // 2) * 2 * C
        kst_pair = k_st[p0:p0 + 2 * C]
        dec = decay[r0:r0 + 1]
        for h in range(H):
            st = state_s[h]
            o = _nn(attn[h * C:(h + 1) * C], v_ref[0, r0:r0 + C, h * dv:(h + 1) * dv])
            o = o + _nt(qcb, st.astype(BF16))
            o_s[r0:r0 + C, h * dv:(h + 1) * dv] = o
            kst_h = jnp.where((pair_row == c % 2) & (pair_head == h), kst_pair, 0.0).astype(BF16)
            state_s[h] = st * dec + _nn(vt_ref[0, h * dv:(h + 1) * dv, p0:p0 + 2 * C], kst_h)

    for h in range(H):
        oh = o_s[:, h * dv:(h + 1) * dv]
        ms = jnp.mean(oh * oh, axis=-1, keepdims=True)
        r = r_ref[0, :, h * dv:(h + 1) * dv]
        y = oh * lax.rsqrt(ms + EPS) * ng_ref[:, h * dv:(h + 1) * dv] * (r * jax.nn.sigmoid(r))
        out_ref[0, :, h * dv:(h + 1) * dv] = y.astype(BF16)


def _gla(qkb, vb, vtb, misc, rb, w_alpha, b_alpha, norm_g, ct=512):
    B, S, _ = qkb.shape
    H, dk, dv, C = GLA_HEADS, GLA_DK, GLA_DV, GLA_CHUNK
    kw, vw = H * dk, H * dv
    wal = jnp.zeros((LANES, kw), F32).at[32:32 + GLA_RANK].set(w_alpha).astype(BF16)
    r = np.arange(ct)
    tri = (r[:, None] // C == r[None, :] // C) & (r[:, None] >= r[None, :])
    tot = r[:, None] // C == r[None, :] // C
    cum = jnp.asarray(np.concatenate([tri, tot], axis=0).astype(np.float32), BF16)
    full = lambda shp: pl.BlockSpec(shp, lambda b, i: (0,) * len(shp))
    return pl.pallas_call(
        functools.partial(_gla_kernel, ct=ct),
        out_shape=jax.ShapeDtypeStruct((B, S, vw), BF16),
        grid=(B, S // ct),
        in_specs=[
            pl.BlockSpec((1, ct, 2 * kw), lambda b, i: (b, i, 0)),
            pl.BlockSpec((1, ct, vw), lambda b, i: (b, i, 0)),
            pl.BlockSpec((1, vw, ct), lambda b, i: (b, 0, i)),
            pl.BlockSpec((1, ct, LANES), lambda b, i: (b, i, 0)),
            pl.BlockSpec((1, ct, vw), lambda b, i: (b, i, 0)),
            full((LANES, kw)), full((1, kw)), full((1, vw)), full((2 * ct, ct)),
        ],
        out_specs=pl.BlockSpec((1, ct, vw), lambda b, i: (b, i, 0)),
        scratch_shapes=[pltpu.VMEM((H, dv, kw), F32), pltpu.VMEM((ct, vw), F32)],
        compiler_params=pltpu.CompilerParams(dimension_semantics=("parallel", "arbitrary"),
                                             vmem_limit_bytes=VMEM_LIMIT),
        name="gla",
    )(qkb, vb, vtb, misc, rb, wal, b_alpha.reshape(1, kw), norm_g.reshape(1, vw), cum)


ROUTER_ROWS = 128
EXPERT_ROW0 = 8


def _local_rows(tm):
    return -(-(2 * tm + N_EXPERTS * (SUBLANES - 1)) // LANES) * LANES


def _pack_bf16_pairs(x):
    u = pltpu.bitcast(x, jnp.uint32)
    w = x.shape[1] // 2
    return u[:, :w] | (u[:, w:] >> 16)


def _unpack_bf16_pairs(w):
    xh = pltpu.bitcast(w & jnp.uint32(0xFFFF0000), F32)
    xl = pltpu.bitcast(w << 16, F32)
    return jnp.concatenate([xh, xl], axis=1).astype(BF16)


def _outproj_kernel(ya_ref, yb_ref, mg_ref, x_ref, wa_ref, wb_ref, wo_ref, gf_ref, wrh_ref, wrl_ref, br_ref,
                    tri_ref, ltri_ref, x1_ref, xsl_ref, seg_ref, rw_ref, cnt_ref, carry_s, *, tm, ls):
    D = x_ref.shape[1]

    @pl.when(pl.program_id(0) == 0)
    def _():
        carry_s[...] = jnp.zeros_like(carry_s)

    ma = _nn(ya_ref[...], wa_ref[...])
    mb = _nn(yb_ref[...], wb_ref[...])
    merged = jax.nn.sigmoid(mg_ref[:, :D]) * ma + jax.nn.sigmoid(mg_ref[:, D:]) * mb
    x1 = x_ref[...] + _nn(merged.astype(BF16), wo_ref[...])
    x1_ref[...] = x1
    ms = jnp.mean(x1 * x1, axis=-1, keepdims=True)
    h2 = x1 * lax.rsqrt(ms + EPS) * gf_ref[...]
    hi = h2.astype(BF16)
    lo = (h2 - hi.astype(F32)).astype(BF16)
    lg = _nt(wrh_ref[...], hi) + _nt(wrh_ref[...], lo) + _nt(wrl_ref[...], hi) + br_ref[...]
    row8 = lax.broadcasted_iota(jnp.int32, (SUBLANES, tm), 0)
    gl = jnp.where(row8 < N_GROUPS, lg[0:SUBLANES], NEG)
    gmax = jnp.max(gl, axis=0, keepdims=True)
    g_sel = jnp.min(jnp.where(gl == gmax, row8, SUBLANES), axis=0, keepdims=True)
    g_prob = 1.0 / jnp.sum(jnp.where(row8 < N_GROUPS, jnp.exp(gl - gmax), 0.0), axis=0, keepdims=True)
    e_sel = jnp.zeros((EPG, tm), F32)
    for gi in range(N_GROUPS):
        r0 = EXPERT_ROW0 + gi * EPG
        e_sel = e_sel + jnp.where(g_sel == gi, lg[r0:r0 + EPG], 0.0)
    v1 = jnp.max(e_sel, axis=0, keepdims=True)
    i1 = jnp.min(jnp.where(e_sel == v1, row8, EPG), axis=0, keepdims=True)
    rest = jnp.where(row8 == i1, -jnp.inf, e_sel)
    v2 = jnp.max(rest, axis=0, keepdims=True)
    i2 = jnp.min(jnp.where(rest == v2, row8, EPG), axis=0, keepdims=True)
    t = jnp.exp(v2 - v1)
    w1 = g_prob / (1.0 + t)
    w2 = g_prob * t / (1.0 + t)
    e1 = g_sel * EPG + i1
    e2 = g_sel * EPG + i2

    rowe = lax.broadcasted_iota(jnp.int32, (N_EXPERTS, tm), 0)
    oh1 = rowe == e1
    oh2 = rowe == e2
    oh = jnp.where(oh1, 1.0, 0.0) + jnp.where(oh2, 1.0, 0.0)
    pre = _nn(oh.astype(BF16), tri_ref[...])
    cnt = jnp.sum(oh, axis=1, keepdims=True)
    cnt8 = jnp.floor((cnt + (SUBLANES - 1)) * (1.0 / SUBLANES)) * SUBLANES
    c_b = jnp.broadcast_to(cnt8, (N_EXPERTS, LANES))
    c_hi = jnp.floor(c_b * (1.0 / 16.0))
    c_lo = c_b - 16.0 * c_hi
    base = 16.0 * _nn(ltri_ref[...], c_hi.astype(BF16)) + _nn(ltri_ref[...], c_lo.astype(BF16))
    loc = pre + base[:, 0:1]
    lpos1 = jnp.sum(jnp.where(oh1, loc, 0.0), axis=0, keepdims=True)
    lpos2 = jnp.sum(jnp.where(oh2, loc, 0.0), axis=0, keepdims=True)
    lane = lax.broadcasted_iota(jnp.int32, (N_EXPERTS, LANES), 1)
    seg_ref[0] = jnp.where(lane == 0, base, jnp.where(lane == 1, c_b, carry_s[...])).astype(jnp.int32)
    carry_s[...] = carry_s[...] + c_b
    cnt_ref[...] = carry_s[...].astype(jnp.int32)
    rw_ref[0] = jnp.concatenate([w1, w2, lpos1, lpos2, jnp.zeros((SUBLANES - 4, tm), F32)], axis=0)

    srow = lax.broadcasted_iota(jnp.int32, (ls, tm), 0)
    perm = jnp.where(srow == lpos1.astype(jnp.int32), 1.0, jnp.where(srow == lpos2.astype(jnp.int32), 1.0, 0.0))
    xsorted = _nn(perm.astype(BF16), hi)
    xsl_ref[0] = _pack_bf16_pairs(xsorted)


def _outproj(ya, yb, mg, x, wa, wb, wo, g_ffn, w_rg, b_rg, w_re, b_re, tm=MOE_TM):
    T, D = x.shape
    nt = T // tm
    wr = jnp.zeros((ROUTER_ROWS, D), F32).at[0:N_GROUPS].set(w_rg.T).at[EXPERT_ROW0:EXPERT_ROW0 + N_EXPERTS].set(w_re.T)
    wrh = wr.astype(BF16)
    wrl = (wr - wrh.astype(F32)).astype(BF16)
    br = jnp.zeros((ROUTER_ROWS, 1), F32).at[0:N_GROUPS, 0].set(b_rg).at[EXPERT_ROW0:EXPERT_ROW0 + N_EXPERTS, 0].set(b_re)
    r = np.arange(tm)
    tri = jnp.asarray((r[:, None] < r[None, :]).astype(np.float32), BF16)
    re = np.arange(N_EXPERTS)
    ltri = jnp.asarray((re[None, :] < re[:, None]).astype(np.float32), BF16)
    ls = _local_rows(tm)
    row = lambda n: pl.BlockSpec((tm, n), lambda i: (i, 0))
    full = lambda shp: pl.BlockSpec(shp, lambda i: (0,) * len(shp))
    return pl.pallas_call(
        functools.partial(_outproj_kernel, tm=tm, ls=ls),
        out_shape=[jax.ShapeDtypeStruct((T, D), F32), jax.ShapeDtypeStruct((nt, ls, D // 2), jnp.uint32),
                   jax.ShapeDtypeStruct((nt, N_EXPERTS, LANES), jnp.int32),
                   jax.ShapeDtypeStruct((nt, SUBLANES, tm), F32),
                   jax.ShapeDtypeStruct((N_EXPERTS, LANES), jnp.int32)],
        grid=(nt,),
        in_specs=[row(ya.shape[1]), row(yb.shape[1]), row(2 * D), row(D),
                  full(wa.shape), full(wb.shape), full(wo.shape), full((1, D)),
                  full((ROUTER_ROWS, D)), full((ROUTER_ROWS, D)), full((ROUTER_ROWS, 1)), full((tm, tm)),
                  full((N_EXPERTS, N_EXPERTS))],
        out_specs=[row(D), pl.BlockSpec((1, ls, D // 2), lambda i: (i, 0, 0)),
                   pl.BlockSpec((1, N_EXPERTS, LANES), lambda i: (i, 0, 0)),
                   pl.BlockSpec((1, SUBLANES, tm), lambda i: (i, 0, 0)),
                   full((N_EXPERTS, LANES))],
        scratch_shapes=[pltpu.VMEM((N_EXPERTS, LANES), F32)],
        compiler_params=pltpu.CompilerParams(dimension_semantics=("arbitrary",), vmem_limit_bytes=VMEM_LIMIT),
        name="outproj",
    )(ya, yb, mg, x, wa, wb, wo, g_ffn.reshape(1, D), wrh, wrl, br, tri, ltri)


SEG_FIELDS = 4
SEG_BITS = 7
TAIL_BITS = (TE // SUBLANES).bit_length() - 1


def _segment_copies(n8, bits, make_copy, wait):
    off = 0
    for bit in reversed(range(bits)):
        rows = SUBLANES << bit
        take = (n8 >> bit) & 1

        @pl.when(take == 1)
        def _(off=off, rows=rows):
            cp = make_copy(off, rows)
            if wait:
                cp.wait()
            else:
                cp.start()

        off = off + take * rows


def _dispatch_kernel(seg_ref, tail_ref, xsl_ref, xs_ref, zero_s, sem):
    i = pl.program_id(0)

    def segments(wait):
        def body(e, c):
            s0 = (i * N_EXPERTS + e) * SEG_FIELDS
            src0 = pl.multiple_of(seg_ref[s0], SUBLANES)
            dst0 = pl.multiple_of(seg_ref[s0 + 2], SUBLANES)

            def make_copy(off, rows):
                return pltpu.make_async_copy(xsl_ref.at[0, pl.ds(pl.multiple_of(src0 + off, SUBLANES), rows), :],
                                             xs_ref.at[pl.ds(pl.multiple_of(dst0 + off, SUBLANES), rows), :], sem)

            _segment_copies(seg_ref[s0 + 1], SEG_BITS, make_copy, wait)
            return c

        lax.fori_loop(0, N_EXPERTS, body, 0)

    def tails(wait):
        def body(e, c):
            dst0 = pl.multiple_of(tail_ref[2 * e], SUBLANES)

            def make_copy(off, rows):
                return pltpu.make_async_copy(zero_s.at[pl.ds(0, rows), :],
                                             xs_ref.at[pl.ds(pl.multiple_of(dst0 + off, SUBLANES), rows), :], sem)

            _segment_copies(tail_ref[2 * e + 1], TAIL_BITS, make_copy, wait)
            return c

        lax.fori_loop(0, N_EXPERTS, body, 0)

    def unused_tiles(wait):
        def body(t, c):
            cp = pltpu.make_async_copy(zero_s, xs_ref.at[pl.ds(pl.multiple_of(t * TE, TE), TE), :], sem)
            if wait:
                cp.wait()
            else:
                cp.start()
            return c

        lax.fori_loop(tail_ref[2 * N_EXPERTS], xs_ref.shape[0] // TE, body, 0)

    @pl.when(i == 0)
    def _():
        zero_s[...] = jnp.zeros_like(zero_s)
        tails(False)
        unused_tiles(False)
        tails(True)
        unused_tiles(True)

    segments(False)
    segments(True)


def _dispatch(seg, tail, xsl, n_rows):
    nt, ls, W = xsl.shape
    return pl.pallas_call(
        _dispatch_kernel,
        out_shape=jax.ShapeDtypeStruct((n_rows, W), xsl.dtype),
        grid_spec=pltpu.PrefetchScalarGridSpec(
            num_scalar_prefetch=2,
            grid=(nt,),
            in_specs=[pl.BlockSpec((1, ls, W), lambda i, sg, tl: (i, 0, 0))],
            out_specs=pl.BlockSpec(memory_space=pl.ANY),
            scratch_shapes=[pltpu.VMEM((TE, W), xsl.dtype), pltpu.SemaphoreType.DMA(())],
        ),
        compiler_params=pltpu.CompilerParams(dimension_semantics=("arbitrary",)),
        name="dispatch",
    )(seg, tail, xsl)


def _experts_kernel(te_ref, nv_ref, xs_ref, wg_ref, wu_ref, wd_ref, out_ref):
    i = pl.program_id(0)

    @pl.when(i < nv_ref[0])
    def _():
        x = _unpack_bf16_pairs(xs_ref[...])
        a = _nn(x, wg_ref[0].astype(BF16))
        u = _nn(x, wu_ref[0].astype(BF16))
        hid = (a * jax.nn.sigmoid(a)) * u
        y = _nn(hid.astype(BF16), wd_ref[0].astype(BF16))
        out_ref[...] = _pack_bf16_pairs(y.astype(BF16).astype(F32))

    @pl.when(i >= nv_ref[0])
    def _():
        out_ref[...] = jnp.zeros_like(out_ref)


def _experts(tile_expert, n_valid, xs, w_gate, w_up, w_down):
    n_rows, W = xs.shape
    D = 2 * W
    n_tiles = n_rows // TE
    last = lambda i, nv: jnp.minimum(i, nv[0] - 1)
    return pl.pallas_call(
        _experts_kernel,
        out_shape=jax.ShapeDtypeStruct((n_rows, W), jnp.uint32),
        grid_spec=pltpu.PrefetchScalarGridSpec(
            num_scalar_prefetch=2,
            grid=(n_tiles,),
            in_specs=[pl.BlockSpec((TE, W), lambda i, te, nv: (last(i, nv), 0)),
                      pl.BlockSpec((1, D, EXPERT_FF), lambda i, te, nv: (te[last(i, nv)], 0, 0)),
                      pl.BlockSpec((1, D, EXPERT_FF), lambda i, te, nv: (te[last(i, nv)], 0, 0)),
                      pl.BlockSpec((1, EXPERT_FF, D), lambda i, te, nv: (te[last(i, nv)], 0, 0))],
            out_specs=pl.BlockSpec((TE, W), lambda i, te, nv: (i, 0)),
        ),
        compiler_params=pltpu.CompilerParams(dimension_semantics=("arbitrary",), vmem_limit_bytes=VMEM_LIMIT),
        name="experts",
    )(tile_expert, n_valid, xs, w_gate, w_up, w_down)


def _combine_kernel(seg_ref, ys_ref, x1_ref, rw_ref, gfin_ref, out_ref, buf, sem, *, tm, ls, apply_norm):
    i = pl.program_id(0)
    nt = pl.num_programs(0)

    def segments(tile, wait):
        slot = tile % 2

        def body(e, c):
            s0 = (tile * N_EXPERTS + e) * SEG_FIELDS
            loc0 = pl.multiple_of(seg_ref[s0], SUBLANES)
            glob0 = pl.multiple_of(seg_ref[s0 + 2], SUBLANES)

            def make_copy(off, rows):
                return pltpu.make_async_copy(ys_ref.at[pl.ds(pl.multiple_of(glob0 + off, SUBLANES), rows), :],
                                             buf.at[slot, pl.ds(pl.multiple_of(loc0 + off, SUBLANES), rows), :],
                                             sem.at[slot])

            _segment_copies(seg_ref[s0 + 1], SEG_BITS, make_copy, wait)
            return c

        lax.fori_loop(0, N_EXPERTS, body, 0)

    @pl.when(i == 0)
    def _():
        buf[...] = jnp.zeros_like(buf)
        segments(i, False)

    @pl.when(i + 1 < nt)
    def _():
        segments(i + 1, False)

    segments(i, True)
    ysl = _unpack_bf16_pairs(buf[i % 2])
    cols = jnp.concatenate([rw_ref[0], jnp.zeros((LANES - SUBLANES, tm), F32)], axis=0).T
    srow = lax.broadcasted_iota(jnp.int32, (tm, ls), 1)
    y = x1_ref[...]
    for k in range(2):
        pick = jnp.where(srow == cols[:, 2 + k:3 + k].astype(jnp.int32), 1.0, 0.0).astype(BF16)
        y = y + cols[:, k:k + 1] * _nn(pick, ysl)
    if apply_norm:
        ms = jnp.mean(y * y, axis=-1, keepdims=True)
        y = y * lax.rsqrt(ms + EPS) * gfin_ref[...]
    out_ref[...] = y


def _combine(seg, ys, x1, rw, g_final, apply_norm, tm=MOE_TM):
    T, D = x1.shape
    nt = T // tm
    ls = _local_rows(tm)
    return pl.pallas_call(
        functools.partial(_combine_kernel, tm=tm, ls=ls, apply_norm=apply_norm),
        out_shape=jax.ShapeDtypeStruct((T, D), F32),
        grid_spec=pltpu.PrefetchScalarGridSpec(
            num_scalar_prefetch=1,
            grid=(nt,),
            in_specs=[pl.BlockSpec(memory_space=pl.ANY),
                      pl.BlockSpec((tm, D), lambda i, sg: (i, 0)),
                      pl.BlockSpec((1, SUBLANES, tm), lambda i, sg: (i, 0, 0)),
                      pl.BlockSpec((1, D), lambda i, sg: (0, 0))],
            out_specs=pl.BlockSpec((tm, D), lambda i, sg: (i, 0)),
            scratch_shapes=[pltpu.VMEM((2, ls, D // 2), jnp.uint32), pltpu.SemaphoreType.DMA((2,))],
        ),
        compiler_params=pltpu.CompilerParams(dimension_semantics=("arbitrary",), vmem_limit_bytes=VMEM_LIMIT),
        name="combine",
    )(seg, ys, x1, rw, g_final.reshape(1, D))


def _moe_plan(seg, counts, T, tm=MOE_TM):
    nt = T // tm
    n_tiles_max = (2 * T + nt * N_EXPERTS * (SUBLANES - 1)) // TE + N_EXPERTS
    total = counts[:, 0]
    tiles = (total + TE - 1) // TE
    ids = jnp.arange(N_EXPERTS)
    tile_end = jnp.sum(jnp.where(ids[None, :] <= ids[:, None], tiles[None, :], 0), axis=1)
    row0 = (tile_end - tiles) * TE
    segtab = jnp.stack([seg[:, :, 0], seg[:, :, 1] // SUBLANES, seg[:, :, 2] + row0[None, :],
                        jnp.zeros_like(seg[:, :, 0])], axis=-1).reshape(-1).astype(jnp.int32)
    tail = jnp.concatenate([jnp.stack([row0 + total, (tiles * TE - total) // SUBLANES], axis=-1).reshape(-1),
                            tile_end[-1:]]).astype(jnp.int32)
    tile_expert = jnp.minimum(jnp.sum(tile_end[None, :] <= jnp.arange(n_tiles_max)[:, None], axis=1),
                              N_EXPERTS - 1).astype(jnp.int32)
    return segtab, tail, tile_expert, tile_end[-1:].astype(jnp.int32), n_tiles_max * TE


def kernel(x, g_mix, w_in, nsa_pe_k, nsa_cmp_k_w1, nsa_cmp_k_w2, nsa_pe_v, nsa_cmp_v_w1, nsa_cmp_v_w2, rel_bias,
           gla_w_alpha, gla_b_alpha, gla_norm_g, w_branch_a, w_branch_b, w_out, g_ffn, w_router_group,
           b_router_group, w_router_expert, b_router_expert, w_exp_gate, w_exp_up, w_exp_down, g_final):
    B, S, D = x.shape
    T = B * S
    for l in range(w_in.shape[0]):
        w, wt = _pack_inproj_weights(w_in[l])
        (q, kslc, kwin, kvc, misc, qkb, vb, rb, mg, vts, vtw, vtb) = _inproj(x, g_mix[l], w, wt)
        kcb, vcbt = _compress(kvc, nsa_pe_k[l], nsa_cmp_k_w1[l], nsa_cmp_k_w2[l],
                              nsa_pe_v[l], nsa_cmp_v_w1[l], nsa_cmp_v_w2[l])
        ya = _nsa(q, kslc, vts, kwin, vtw, kcb, vcbt, misc, rel_bias)
        yb = _gla(qkb, vb, vtb, misc, rb, gla_w_alpha[l], gla_b_alpha[l], gla_norm_g[l])
        x1, xsl, seg, rw, counts = _outproj(
            ya.reshape(T, -1), yb.reshape(T, -1), mg.reshape(T, -1), x.reshape(T, D),
            w_branch_a[l].astype(BF16), w_branch_b[l].astype(BF16), w_out[l].astype(BF16), g_ffn[l],
            w_router_group[l], b_router_group[l], w_router_expert[l], b_router_expert[l])
        segtab, tail, tile_expert, n_valid, n_rows = _moe_plan(seg, counts, T)
        xs = _dispatch(segtab, tail, xsl, n_rows)
        ys = _experts(tile_expert, n_valid, xs, w_exp_gate[l], w_exp_up[l], w_exp_down[l])
        last_layer = l == w_in.shape[0] - 1
        x = _combine(segtab, ys, x1, rw, g_final, apply_norm=last_layer).reshape(B, S, D)
    return x
```

```python
import functools
import math

import numpy as np
import jax
import jax.numpy as jnp
from jax import lax
from jax.experimental import pallas as pl
from jax.experimental.pallas import tpu as pltpu

F32 = jnp.float32
BF16 = jnp.bfloat16

NSA_HEADS = 8
NSA_GROUPS = 2
HPG = NSA_HEADS // NSA_GROUPS
DH = 64
CMP_BLOCK = 32
CMP_STRIDE = 16
CMP_HIDDEN = 128
SLC_BLOCK = 64
SLC_TOPN = 16
WINDOW = 512
GLA_HEADS = 4
GLA_DK = 64
GLA_DV = 128
GLA_RANK = 16
GLA_TAU = 16.0
GLA_CHUNK = 64
REL_BUCKETS = 32
REL_MAX_EXACT = REL_BUCKETS // 2
REL_MAX_DIST = 128
N_GROUPS = 4
EPG = 8
N_EXPERTS = N_GROUPS * EPG
EXPERT_FF = 256
EPS = 1e-6

LANES = 128
SUBLANES = 8
VMEM_LIMIT = 56 * 1024 * 1024

LOG2E = math.log2(math.e)
NEG = -1e30
BIG = float(2.0 ** 100)
QT = 128
SLC_PAD = 128
WIN_PAD = 512
TE = 512
MOE_TM = 512
VT_ROWS = DH + 16


def _nt(a, b):
    return lax.dot_general(a, b, (((1,), (1,)), ((), ())), preferred_element_type=F32)


def _nn(a, b):
    return jnp.dot(a, b, preferred_element_type=F32)


def _split3(x):
    a = x.astype(BF16)
    r = x - a.astype(F32)
    b = r.astype(BF16)
    c = (r - b.astype(F32)).astype(BF16)
    return a, b, c


def _t5_bucket_np(rel):
    n = np.maximum(rel, 0)
    nf = np.maximum(n, 1).astype(np.float32)
    large = REL_MAX_EXACT + (np.log(nf / np.float32(REL_MAX_EXACT)) / np.float32(math.log(REL_MAX_DIST / REL_MAX_EXACT))
                             * np.float32(REL_BUCKETS - REL_MAX_EXACT)).astype(np.int32)
    return np.where(n < REL_MAX_EXACT, n, np.minimum(large, REL_BUCKETS - 1)).astype(np.int32)


def _inproj_kernel(x_ref, g_ref, w_ref, kplace_ref, oq, okslc, okwin, okv, omisc, oqkb, ovb, orb, omg,
                   ovts, ovtw, ovtb, *, tm, seq):
    x = x_ref[0]
    ms = jnp.mean(x * x, axis=-1, keepdims=True)
    h = (x * lax.rsqrt(ms + EPS) * g_ref[...]).astype(BF16)

    def mm(a, b):
        return _nn(h, w_ref[:, a:b])

    oq[0] = mm(0, 512).astype(BF16)
    kv = mm(512, 1280)
    okv[0] = kv[:, 0:256]

    def spread(k):
        return _nn(k.astype(BF16), kplace_ref[...])

    row = lax.broadcasted_iota(jnp.int32, (tm, 256), 0) + pl.program_id(1) * tm
    lane = lax.broadcasted_iota(jnp.int32, (tm, 256), 1) % LANES
    onehot = jnp.where(lane - DH == row // SLC_BLOCK, 1.0, 0.0)
    okslc[0] = (spread(kv[:, 256:384]) + onehot).astype(BF16)
    okwin[0] = spread(kv[:, 512:640]).astype(BF16)

    ones_rows = jnp.where(lax.broadcasted_iota(jnp.int32, (VT_ROWS - DH, tm), 0) == 0, 1.0, 0.0)

    def vt_groups(v):
        t = v.T
        return jnp.concatenate([t[:DH], ones_rows, t[DH:], ones_rows], axis=0).astype(BF16)

    ovts[0] = vt_groups(kv[:, 384:512])
    ovtw[0] = vt_groups(kv[:, 640:768])
    omisc[0] = mm(1280, 1408)
    qkv = mm(1408, 2432)
    oqkb[0] = qkv[:, 0:512]
    ovb[0] = qkv[:, 512:1024].astype(BF16)
    ovtb[0] = qkv[:, 512:1024].T.astype(BF16)
    orb[0] = mm(2432, 2944)
    omg[0] = mm(2944, 4992)


def _pack_inproj_weights(w_in):
    o = np.cumsum([0, 512, 128, 128, 128, 128, 128, 128, 24, 256, 256, 512, 16, 512, 1024, 1024])
    (q_a, k_cmp, v_cmp, k_slc, v_slc, k_win, v_win, gate_a, q_b, k_b, v_b, a_b, r_b, mg_a, mg_b) = [
        w_in[:, o[i]:o[i + 1]] for i in range(15)]
    D = w_in.shape[0]
    ga = gate_a.reshape(D, 3, NSA_GROUPS, HPG)
    z4 = jnp.zeros((D, 4), w_in.dtype)
    misc = [ga[:, :, 0, :].reshape(D, 12), z4, ga[:, :, 1, :].reshape(D, 12), z4, a_b,
            jnp.zeros((D, LANES - 48), w_in.dtype)]
    w = jnp.concatenate([q_a * (DH ** -0.5 * LOG2E), k_cmp, v_cmp, k_slc, v_slc, k_win, v_win] + misc
                        + [q_b, k_b, v_b, r_b, mg_a, mg_b], axis=1)
    return w.astype(BF16)


def _inproj(x, g_mix, w, tm=512):
    B, S, D = x.shape
    nw = w.shape[1]
    src = np.arange(NSA_GROUPS * DH)
    kplace = np.zeros((NSA_GROUPS * DH, NSA_GROUPS * LANES), np.float32)
    kplace[src, (src // DH) * LANES + src % DH] = 1.0
    kplace = jnp.asarray(kplace, BF16)
    widths = [(512, BF16), (256, BF16), (256, BF16), (256, F32), (128, F32), (512, F32), (512, BF16),
              (512, F32), (2048, F32)]
    out_shape = [jax.ShapeDtypeStruct((B, S, n), dt) for n, dt in widths]
    out_specs = [pl.BlockSpec((1, tm, n), lambda b, i: (b, i, 0)) for n, _ in widths]
    for rows in (NSA_GROUPS * VT_ROWS, NSA_GROUPS * VT_ROWS, 512):
        out_shape.append(jax.ShapeDtypeStruct((B, rows, S), BF16))
        out_specs.append(pl.BlockSpec((1, rows, tm), lambda b, i: (b, 0, i)))
    return pl.pallas_call(
        functools.partial(_inproj_kernel, tm=tm, seq=S),
        out_shape=out_shape,
        grid=(B, S // tm),
        in_specs=[
            pl.BlockSpec((1, tm, D), lambda b, i: (b, i, 0)),
            pl.BlockSpec((1, D), lambda b, i: (0, 0)),
            pl.BlockSpec((D, nw), lambda b, i: (0, 0), pipeline_mode=pl.Buffered(1)),
            pl.BlockSpec(kplace.shape, lambda b, i: (0, 0)),
        ],
        out_specs=out_specs,
        compiler_params=pltpu.CompilerParams(dimension_semantics=("parallel", "parallel"),
                                             vmem_limit_bytes=VMEM_LIMIT),
        name="inproj",
    )(x, g_mix.reshape(1, D), w, kplace)


def _gelu_tanh(x):
    return 0.5 * x * (1.0 + jnp.tanh(math.sqrt(2.0 / math.pi) * (x + 0.044715 * (x * x * x))))


def _compress_kernel(xk_ref, xv_ref, pe_ref, w1_ref, w2k_ref, w2vt_ref, ok_ref, ovt_ref, *, nsub):
    for kind, x_ref in enumerate((xk_ref, xv_ref)):
        top = jnp.zeros((nsub, 2 * CMP_HIDDEN), F32)
        bot = jnp.zeros((nsub, 2 * CMP_HIDDEN), F32)
        for r in range(CMP_STRIDE):
            xr = x_ref[0, pl.ds(r, nsub, stride=CMP_STRIDE), :]
            top = top + _nn((xr + pe_ref[kind, 0, r:r + 1, :]).astype(BF16), w1_ref[kind, 0, r])
            bot = bot + _nn((xr + pe_ref[kind, 1, r:r + 1, :]).astype(BF16), w1_ref[kind, 1, r])
        hid = _gelu_tanh(top + pltpu.roll(bot, shift=nsub - 1, axis=0)).astype(BF16)
        for g in range(NSA_GROUPS):
            hg = hid[:, g * CMP_HIDDEN:(g + 1) * CMP_HIDDEN]
            if kind == 0:
                ok_ref[0, g] = _nn(hg, w2k_ref[...]).astype(BF16)
            else:
                ovt_ref[0, g] = _nt(w2vt_ref[...], hg).astype(BF16)


def _compress(kv_cmp, pe_k, w1k, w2k, pe_v, w1v, w2v):
    B, S, _ = kv_cmp.shape
    nsub = S // CMP_STRIDE
    G = NSA_GROUPS

    def prep(pe, w1):
        pe_t = jnp.tile(pe.reshape(2, CMP_STRIDE, DH), (1, 1, G))
        a = w1.reshape(2, CMP_STRIDE, DH, CMP_HIDDEN)
        z = jnp.zeros_like(a)
        w = jnp.concatenate([jnp.concatenate([a, z], axis=3), jnp.concatenate([z, a], axis=3)], axis=2)
        return pe_t, w.astype(BF16)

    pek, w1kb = prep(pe_k, w1k)
    pev, w1vb = prep(pe_v, w1v)
    pe = jnp.stack([pek, pev])
    w1 = jnp.stack([w1kb, w1vb])
    w2kp = jnp.concatenate([w2k, jnp.zeros_like(w2k)], axis=1).astype(BF16)
    w2vt = w2v.T.astype(BF16)
    full = lambda shp: pl.BlockSpec(shp, lambda b: (0,) * len(shp))
    return pl.pallas_call(
        functools.partial(_compress_kernel, nsub=nsub),
        out_shape=[jax.ShapeDtypeStruct((B, G, nsub, LANES), BF16),
                   jax.ShapeDtypeStruct((B, G, DH, nsub), BF16)],
        grid=(B,),
        in_specs=[pl.BlockSpec((1, S, G * DH), lambda b: (b, 0, 0)), pl.BlockSpec((1, S, G * DH), lambda b: (b, 0, 1)),
                  full(pe.shape), full(w1.shape), full((CMP_HIDDEN, LANES)), full((DH, CMP_HIDDEN))],
        out_specs=[pl.BlockSpec((1, G, nsub, LANES), lambda b: (b, 0, 0, 0)),
                   pl.BlockSpec((1, G, DH, nsub), lambda b: (b, 0, 0, 0))],
        compiler_params=pltpu.CompilerParams(dimension_semantics=("parallel",), vmem_limit_bytes=VMEM_LIMIT),
        name="compress",
    )(kv_cmp, kv_cmp, pe, w1, w2kp, w2vt)


def _bias_kernel(tbl_ref, bkn_ref, bkc_ref, near_ref, cmpb_ref):
    g = pl.program_id(0)
    for h in range(HPG):
        hd = g * HPG + h

        def lookup(bk):
            acc = jnp.full(bk.shape, NEG, F32)
            for b in range(REL_BUCKETS):
                acc = jnp.where(bk == b, tbl_ref[hd, b], acc)
            return acc

        vn = lookup(bkn_ref[...])
        near_ref[0, :, h * QT:(h + 1) * QT] = jnp.where(vn > 0.5 * NEG, (vn - tbl_ref[hd, REL_BUCKETS - 1]) * LOG2E, NEG)
        vc = lookup(bkc_ref[...])
        cmpb_ref[0, :, h * QT:(h + 1) * QT] = jnp.where(vc > 0.5 * NEG, vc * LOG2E, NEG)


def _nsa_bias_tables(rel_bias, seq):
    ql = np.arange(QT)
    ncmp = seq // CMP_STRIDE

    def buckets(rel):
        return jnp.asarray(np.where(rel >= 0, _t5_bucket_np(rel), -1).astype(np.int32))

    bkn = buckets(ql[None, :] + QT - np.arange(2 * QT)[:, None])
    y = np.arange(2 * ncmp)
    bkc = buckets(ql[None, :] - CMP_STRIDE * (y[:, None] - ncmp) - (CMP_BLOCK - 1))
    nql = HPG * QT
    return pl.pallas_call(
        _bias_kernel,
        out_shape=[jax.ShapeDtypeStruct((NSA_GROUPS, 2 * QT, nql), F32),
                   jax.ShapeDtypeStruct((NSA_GROUPS, 2 * ncmp, nql), F32)],
        grid=(NSA_GROUPS,),
        in_specs=[pl.BlockSpec(memory_space=pltpu.SMEM),
                  pl.BlockSpec((2 * QT, QT), lambda g: (0, 0)),
                  pl.BlockSpec((2 * ncmp, QT), lambda g: (0, 0))],
        out_specs=[pl.BlockSpec((1, 2 * QT, nql), lambda g: (g, 0, 0)),
                   pl.BlockSpec((1, 2 * ncmp, nql), lambda g: (g, 0, 0))],
        compiler_params=pltpu.CompilerParams(dimension_semantics=("parallel",)),
        name="t5bias",
    )(rel_bias.T, bkn, bkc)


def _nsa_kernel(q_ref, kslc_ref, vtslc_ref, kwin_ref, vtwin_ref, kcb_ref, vcbt_ref, misc_ref,
                near_ref, cmpb_ref, wmask_ref, ovt_ref, eye_ref, eye4_ref,
                out_ref, qaug_s, s0_s, s1_s, *, ncmp, nslc):
    G = NSA_GROUPS
    qt = pl.program_id(1)
    nql = HPG * QT
    ck = 2 * QT

    def flash(carry, s, vt_chunk):
        m, acc = carry
        m_new = jnp.maximum(m, jnp.max(s, axis=0, keepdims=True))
        alpha = jnp.exp2(m - m_new)
        p = jnp.exp2((s - m_new).astype(BF16))
        return m_new, alpha * acc + _nn(vt_chunk, p)

    def finish(carry):
        m, acc = carry
        return acc[:DH] * (1.0 / acc[DH:DH + 1])

    init = (jnp.full((1, nql), NEG, F32), jnp.zeros((VT_ROWS, nql), F32))
    ns = pl.multiple_of(QT * qt, QT)
    nw = pl.multiple_of(QT * qt + 3 * QT, QT)
    off = pl.multiple_of(ncmp - (QT // CMP_STRIDE) * qt, SUBLANES)
    lane = lax.broadcasted_iota(jnp.int32, (nql, LANES), 1)
    win_aug = jnp.where(lane >= DH, -BIG, 0.0).astype(BF16)
    jidx = lax.broadcasted_iota(jnp.int32, (nslc, QT), 0)
    tq = qt * QT + lax.broadcasted_iota(jnp.int32, (nslc, QT), 1)
    tb = tq // SLC_BLOCK
    forced = (jidx == 0) | (jidx == tb) | (jidx == tb - 1)
    future = jidx * SLC_BLOCK > tq
    sub = lax.broadcasted_iota(jnp.int32, (SUBLANES, QT), 0)
    ones_lo = jnp.ones((DH, QT), F32)
    eye = eye_ref[...]
    ovt = ovt_ref[...]
    ngrp = nslc // SUBLANES

    def aug(sel01):
        rows = [ones_lo, sel01]
        if LANES - DH - nslc:
            rows.append(jnp.ones((LANES - DH - nslc, QT), F32))
        m01 = _nt(eye, jnp.concatenate(rows, axis=0).astype(BF16))
        return jnp.concatenate([((m01 - 1.0) * BIG).astype(BF16)] * HPG, axis=0)

    gs = range(G)
    kl = [slice(g * LANES, (g + 1) * LANES) for g in gs]
    vr = [slice(g * VT_ROWS, (g + 1) * VT_ROWS) for g in gs]
    low = lax.broadcasted_iota(jnp.int32, (QT, LANES), 1) < DH

    def head_slot(hd):
        pair = q_ref[0, :, (hd // 2) * LANES:(hd // 2 + 1) * LANES].astype(F32)
        if hd % 2:
            pair = pltpu.roll(pair, shift=DH, axis=1)
        return jnp.where(low, pair, 0.0).astype(BF16)

    q0 = [jnp.concatenate([head_slot(g * HPG + h) for h in range(HPG)], axis=0) for g in gs]
    qwin = [q0[g] + win_aug for g in gs]

    bc = [cmpb_ref[g, pl.ds(off, ncmp), :] for g in gs]
    sc = [_nt(kcb_ref[0, g], q0[g]) + bc[g] for g in gs]
    sw = [_nt(kwin_ref[0, pl.ds(ns, 3 * QT), kl[g]], qwin[g]) for g in gs]
    sw = [jnp.concatenate([sw[g][:QT] + wmask_ref[...], sw[g][QT:]], axis=0) for g in gs]

    pc = []
    for g in gs:
        mc = jnp.max(sc[g], axis=0, keepdims=True)
        ec = jnp.where(bc[g] > 0.5 * NEG, jnp.exp2(sc[g] - mc), 0.0)
        den = jnp.maximum(jnp.sum(ec, axis=0, keepdims=True), jnp.finfo(F32).tiny)
        pc.append(ec * (1.0 / den))
    wcar = [flash(init, sw[g], vtwin_ref[0, vr[g], pl.ds(ns, 3 * QT)]) for g in gs]
    o_c = [_nn(vcbt_ref[0, g], pc[g].astype(BF16)) for g in gs]

    imp = []
    for g in gs:
        psum = pc[g][:, 0:QT]
        for h in range(1, HPG):
            psum = psum + pc[g][:, h * QT:(h + 1) * QT]
        p1, p2, p3 = _split3(psum)
        v = _nn(ovt, p1) + _nn(ovt, p2) + _nn(ovt, p3)
        imp.append(jnp.where(forced, 1e30, jnp.where(future, -1e30, v)))
    sw = [_nt(kwin_ref[0, pl.ds(nw, ck), kl[g]], qwin[g]) + near_ref[g] for g in gs]

    grp = [[imp[g][SUBLANES * v:SUBLANES * (v + 1)] for v in range(ngrp)] for g in gs]
    cnt = [[jnp.zeros((SUBLANES, QT), F32) for _ in range(ngrp)] for g in gs]
    for jp in range(nslc):
        v0, r0 = divmod(jp, SUBLANES)
        for g in gs:
            row = jnp.broadcast_to(imp[g][jp:jp + 1, :], (SUBLANES, QT))
            for v in range(ngrp):
                if v < v0:
                    inc = jnp.where(row > grp[g][v], 1.0, 0.0)
                elif v > v0:
                    inc = jnp.where(row >= grp[g][v], 1.0, 0.0)
                else:
                    inc = jnp.where(sub > r0, jnp.where(row >= grp[g][v], 1.0, 0.0),
                                    jnp.where(row > grp[g][v], 1.0, 0.0))
                cnt[g][v] = cnt[g][v] + inc
    o_w = [finish(flash(wcar[g], sw[g], vtwin_ref[0, vr[g], pl.ds(nw, ck)])) for g in gs]
    for g in gs:
        sel = jnp.concatenate(cnt[g], axis=0) < float(min(SLC_TOPN, nslc))
        sel_near = jnp.where(sel, 1.0, 0.0)
        sel_far = jnp.where(jidx < 2 * (qt - 1), sel_near, 0.0)
        qaug_s[g, 0] = q0[g] + aug(sel_far)
        qaug_s[g, 1] = q0[g] + aug(sel_near)

    n_far = qt // 2
    n_chunks = n_far + 1

    def rows(i):
        return pl.multiple_of(QT * qt - ck * i, QT)

    def scores(g, i):
        i = jnp.minimum(i, n_far)
        return _nt(kslc_ref[0, pl.ds(rows(i), ck), g * LANES:(g + 1) * LANES], qaug_s[g, jnp.where(i == 0, 1, 0)])

    def vt(g, i):
        return vtslc_ref[0, g * VT_ROWS:(g + 1) * VT_ROWS, pl.ds(rows(i), ck)]

    for g in range(G):
        s0_s[g] = scores(g, 0) + near_ref[g]

    def pair_body(p, carry):
        i = 2 * p
        for g in range(G):
            s1_s[g] = scores(g, i + 1)
        carry = tuple(flash(carry[g], s0_s[g], vt(g, i)) for g in range(G))
        for g in range(G):
            s0_s[g] = scores(g, i + 2)
        return tuple(flash(carry[g], s1_s[g], vt(g, i + 1)) for g in range(G))

    carry = lax.fori_loop(0, n_chunks // 2, pair_body, (init,) * G)
    carry = lax.cond(n_chunks % 2 == 1,
                     lambda c: tuple(flash(c[g], s0_s[g], vt(g, n_far)) for g in range(G)),
                     lambda c: c, carry)

    gt = jax.nn.sigmoid(misc_ref[0]).T
    head = lax.broadcasted_iota(jnp.int32, (DH, nql), 1) // QT
    for g in range(G):
        o_s = finish(carry[g])

        def gate_row(br):
            r0 = 16 * g + br * HPG
            return jnp.concatenate([gt[r0 + h:r0 + h + 1, :] for h in range(HPG)], axis=1)

        o = gate_row(0) * o_c[g] + gate_row(1) * o_s + gate_row(2) * o_w[g]
        ob = o.astype(BF16)
        blocks = jnp.concatenate([jnp.where(head == h, ob, jnp.zeros_like(ob)) for h in range(HPG)], axis=0)
        out_ref[0, :, g * HPG * DH:(g + 1) * HPG * DH] = _nt(eye4_ref[...], blocks).astype(BF16)


def _nsa(q, kslc, vtslc, kwin, vtwin, kcb, vcbt, misc, rel_bias):
    B, S, _ = q.shape
    G = NSA_GROUPS
    nq = S // QT
    ncmp = S // CMP_STRIDE
    nslc = S // SLC_BLOCK
    nql = HPG * QT
    near, cmpb = _nsa_bias_tables(rel_bias, S)
    wmask = jnp.asarray(np.tile(np.where(np.arange(QT)[:, None] > np.arange(QT)[None, :], 0.0, NEG), (1, HPG)), F32)
    kpad_s = jnp.concatenate([jnp.zeros((DH,), BF16), jnp.ones((DH,), BF16)] * G)
    kslc_p = jnp.concatenate([jnp.broadcast_to(kpad_s, (B, SLC_PAD, G * LANES)), kslc], axis=1)
    kwin_p = jnp.concatenate([jnp.broadcast_to(kpad_s, (B, WIN_PAD, G * LANES)), kwin], axis=1)
    vtslc_p = jnp.pad(vtslc, ((0, 0), (0, 0), (SLC_PAD, 0)))
    vtwin_p = jnp.pad(vtwin, ((0, 0), (0, 0), (WIN_PAD, 0)))
    ci = np.arange(ncmp)[None, :] * CMP_STRIDE
    sj = np.arange(nslc)[:, None] * SLC_BLOCK
    ovt = jnp.asarray(((ci < sj + SLC_BLOCK) & (ci + CMP_BLOCK > sj)).astype(np.float32), BF16)
    eye = jnp.eye(QT, dtype=BF16)
    eye4 = jnp.tile(eye, (1, HPG))
    kern = functools.partial(_nsa_kernel, ncmp=ncmp, nslc=nslc)
    per_b = lambda shp: pl.BlockSpec(shp, lambda b, i: (b,) + (0,) * (len(shp) - 1))
    full = lambda shp: pl.BlockSpec(shp, lambda b, i: (0,) * len(shp))
    return pl.pallas_call(
        kern,
        out_shape=jax.ShapeDtypeStruct((B, S, NSA_HEADS * DH), BF16),
        grid=(B, nq),
        in_specs=[
            pl.BlockSpec((1, QT, NSA_HEADS * DH), lambda b, i: (b, i, 0)),
            per_b((1, SLC_PAD + S, G * LANES)),
            per_b((1, G * VT_ROWS, SLC_PAD + S)),
            per_b((1, WIN_PAD + S, G * LANES)),
            per_b((1, G * VT_ROWS, WIN_PAD + S)),
            per_b((1, G, ncmp, LANES)),
            per_b((1, G, DH, ncmp)),
            pl.BlockSpec((1, QT, LANES), lambda b, i: (b, i, 0)),
            full((G, 2 * QT, nql)),
            full((G, 2 * ncmp, nql)),
            full((QT, nql)),
            full((nslc, ncmp)),
            full((QT, QT)),
            full((QT, nql)),
        ],
        out_specs=pl.BlockSpec((1, QT, NSA_HEADS * DH), lambda b, i: (b, i, 0)),
        scratch_shapes=[pltpu.VMEM((G, 2, nql, LANES), BF16), pltpu.VMEM((G, 2 * QT, nql), F32),
                        pltpu.VMEM((G, 2 * QT, nql), F32)],
        compiler_params=pltpu.CompilerParams(dimension_semantics=("parallel", "arbitrary"),
                                             vmem_limit_bytes=VMEM_LIMIT),
        name="nsa",
    )(q, kslc_p, vtslc_p, kwin_p, vtwin_p, kcb, vcbt, misc, near, cmpb, wmask, ovt, eye, eye4)


def _gla_kernel(qk_ref, v_ref, vt_ref, misc_ref, r_ref, wal_ref, bal_ref, ng_ref, cum_ref, out_ref,
                state_s, o_s, *, ct):
    H, dk, dv, C = GLA_HEADS, GLA_DK, GLA_DV, GLA_CHUNK
    kw = H * dk

    @pl.when(pl.program_id(1) == 0)
    def _():
        state_s[...] = jnp.zeros_like(state_s)

    z = _nn(misc_ref[0].astype(BF16), wal_ref[...]) + bal_ref[...]
    log_a = (jnp.minimum(z, 0.0) - jnp.log1p(jnp.exp(-jnp.abs(z)))) * (1.0 / GLA_TAU)
    cum = cum_ref[...]
    a1, a2, a3 = _split3(log_a)
    cs = _nn(cum, a1) + _nn(cum, a2) + _nn(cum, a3)
    bc, bl = cs[:ct], cs[ct:]
    q = qk_ref[0, :, :kw]
    k = qk_ref[0, :, kw:]
    q_in = (q * (dk ** -0.5)) * jnp.exp(bc)
    k_in = (k * jnp.exp(-bc)).astype(BF16)
    k_st = k * jnp.exp(bl - bc)
    decay = jnp.exp(bl)
    lane_head = lax.broadcasted_iota(jnp.int32, (C, kw), 1) // dk
    rr = lax.broadcasted_iota(jnp.int32, (H * C, C), 0) % C
    cc = lax.broadcasted_iota(jnp.int32, (H * C, C), 1)
    causal = rr >= cc
    pair_row = lax.broadcasted_iota(jnp.int32, (2 * C, kw), 0) // C
    pair_head = lax.broadcasted_iota(jnp.int32, (2 * C, kw), 1) // dk

    for c in range(ct // C):
        r0 = c * C
        qc = q_in[r0:r0 + C]
        qcb = qc.astype(BF16)
        q_heads = jnp.concatenate([jnp.where(lane_head == h, qc, 0.0) for h in range(H)], axis=0).astype(BF16)
        attn = jnp.where(causal, _nt(q_heads, k_in[r0:r0 + C]), 0.0).astype(BF16)
        p0 = (c // 2) * 2 * C
        kst_pair = k_st[p0:p0 + 2 * C]
        dec = decay[r0:r0 + 1]
        for h in range(H):
            st = state_s[h]
            o = _nn(attn[h * C:(h + 1) * C], v_ref[0, r0:r0 + C, h * dv:(h + 1) * dv])
            o = o + _nt(qcb, st.astype(BF16))
            o_s[r0:r0 + C, h * dv:(h + 1) * dv] = o
            kst_h = jnp.where((pair_row == c % 2) & (pair_head == h), kst_pair, 0.0).astype(BF16)
            state_s[h] = st * dec + _nn(vt_ref[0, h * dv:(h + 1) * dv, p0:p0 + 2 * C], kst_h)

    for h in range(H):
        oh = o_s[:, h * dv:(h + 1) * dv]
        ms = jnp.mean(oh * oh, axis=-1, keepdims=True)
        r = r_ref[0, :, h * dv:(h + 1) * dv]
        y = oh * lax.rsqrt(ms + EPS) * ng_ref[:, h * dv:(h + 1) * dv] * (r * jax.nn.sigmoid(r))
        out_ref[0, :, h * dv:(h + 1) * dv] = y.astype(BF16)


def _gla(qkb, vb, vtb, misc, rb, w_alpha, b_alpha, norm_g, ct=512):
    B, S, _ = qkb.shape
    H, dk, dv, C = GLA_HEADS, GLA_DK, GLA_DV, GLA_CHUNK
    kw, vw = H * dk, H * dv
    wal = jnp.zeros((LANES, kw), F32).at[32:32 + GLA_RANK].set(w_alpha).astype(BF16)
    r = np.arange(ct)
    tri = (r[:, None] // C == r[None, :] // C) & (r[:, None] >= r[None, :])
    tot = r[:, None] // C == r[None, :] // C
    cum = jnp.asarray(np.concatenate([tri, tot], axis=0).astype(np.float32), BF16)
    full = lambda shp: pl.BlockSpec(shp, lambda b, i: (0,) * len(shp))
    return pl.pallas_call(
        functools.partial(_gla_kernel, ct=ct),
        out_shape=jax.ShapeDtypeStruct((B, S, vw), BF16),
        grid=(B, S // ct),
        in_specs=[
            pl.BlockSpec((1, ct, 2 * kw), lambda b, i: (b, i, 0)),
            pl.BlockSpec((1, ct, vw), lambda b, i: (b, i, 0)),
            pl.BlockSpec((1, vw, ct), lambda b, i: (b, 0, i)),
            pl.BlockSpec((1, ct, LANES), lambda b, i: (b, i, 0)),
            pl.BlockSpec((1, ct, vw), lambda b, i: (b, i, 0)),
            full((LANES, kw)), full((1, kw)), full((1, vw)), full((2 * ct, ct)),
        ],
        out_specs=pl.BlockSpec((1, ct, vw), lambda b, i: (b, i, 0)),
        scratch_shapes=[pltpu.VMEM((H, dv, kw), F32), pltpu.VMEM((ct, vw), F32)],
        compiler_params=pltpu.CompilerParams(dimension_semantics=("parallel", "arbitrary"),
                                             vmem_limit_bytes=VMEM_LIMIT),
        name="gla",
    )(qkb, vb, vtb, misc, rb, wal, b_alpha.reshape(1, kw), norm_g.reshape(1, vw), cum)


ROUTER_ROWS = 128
EXPERT_ROW0 = 8


def _local_rows(tm):
    return -(-(2 * tm + N_EXPERTS * (SUBLANES - 1)) // LANES) * LANES


def _pack_bf16_pairs(x):
    u = pltpu.bitcast(x, jnp.uint32)
    w = x.shape[1] // 2
    return u[:, :w] | (u[:, w:] >> 16)


def _unpack_bf16_pairs(w):
    xh = pltpu.bitcast(w & jnp.uint32(0xFFFF0000), F32)
    xl = pltpu.bitcast(w << 16, F32)
    return jnp.concatenate([xh, xl], axis=1).astype(BF16)


def _outproj_kernel(ya_ref, yb_ref, mg_ref, x_ref, wa_ref, wb_ref, wo_ref, gf_ref, wrh_ref, wrl_ref, br_ref,
                    tri_ref, ltri_ref, x1_ref, xsl_ref, seg_ref, rw_ref, cnt_ref, carry_s, *, tm, ls):
    D = x_ref.shape[1]

    @pl.when(pl.program_id(0) == 0)
    def _():
        carry_s[...] = jnp.zeros_like(carry_s)

    ma = _nn(ya_ref[...], wa_ref[...])
    mb = _nn(yb_ref[...], wb_ref[...])
    merged = jax.nn.sigmoid(mg_ref[:, :D]) * ma + jax.nn.sigmoid(mg_ref[:, D:]) * mb
    x1 = x_ref[...] + _nn(merged.astype(BF16), wo_ref[...])
    x1_ref[...] = x1
    ms = jnp.mean(x1 * x1, axis=-1, keepdims=True)
    h2 = x1 * lax.rsqrt(ms + EPS) * gf_ref[...]
    hi = h2.astype(BF16)
    lo = (h2 - hi.astype(F32)).astype(BF16)
    lg = _nt(wrh_ref[...], hi) + _nt(wrh_ref[...], lo) + _nt(wrl_ref[...], hi) + br_ref[...]
    row8 = lax.broadcasted_iota(jnp.int32, (SUBLANES, tm), 0)
    gl = jnp.where(row8 < N_GROUPS, lg[0:SUBLANES], NEG)
    gmax = jnp.max(gl, axis=0, keepdims=True)
    g_sel = jnp.min(jnp.where(gl == gmax, row8, SUBLANES), axis=0, keepdims=True)
    g_prob = 1.0 / jnp.sum(jnp.where(row8 < N_GROUPS, jnp.exp(gl - gmax), 0.0), axis=0, keepdims=True)
    e_sel = jnp.zeros((EPG, tm), F32)
    for gi in range(N_GROUPS):
        r0 = EXPERT_ROW0 + gi * EPG
        e_sel = e_sel + jnp.where(g_sel == gi, lg[r0:r0 + EPG], 0.0)
    v1 = jnp.max(e_sel, axis=0, keepdims=True)
    i1 = jnp.min(jnp.where(e_sel == v1, row8, EPG), axis=0, keepdims=True)
    rest = jnp.where(row8 == i1, -jnp.inf, e_sel)
    v2 = jnp.max(rest, axis=0, keepdims=True)
    i2 = jnp.min(jnp.where(rest == v2, row8, EPG), axis=0, keepdims=True)
    t = jnp.exp(v2 - v1)
    w1 = g_prob / (1.0 + t)
    w2 = g_prob * t / (1.0 + t)
    e1 = g_sel * EPG + i1
    e2 = g_sel * EPG + i2

    rowe = lax.broadcasted_iota(jnp.int32, (N_EXPERTS, tm), 0)
    oh1 = rowe == e1
    oh2 = rowe == e2
    oh = jnp.where(oh1, 1.0, 0.0) + jnp.where(oh2, 1.0, 0.0)
    pre = _nn(oh.astype(BF16), tri_ref[...])
    cnt = jnp.sum(oh, axis=1, keepdims=True)
    cnt8 = jnp.floor((cnt + (SUBLANES - 1)) * (1.0 / SUBLANES)) * SUBLANES
    c_b = jnp.broadcast_to(cnt8, (N_EXPERTS, LANES))
    c_hi = jnp.floor(c_b * (1.0 / 16.0))
    c_lo = c_b - 16.0 * c_hi
    base = 16.0 * _nn(ltri_ref[...], c_hi.astype(BF16)) + _nn(ltri_ref[...], c_lo.astype(BF16))
    loc = pre + base[:, 0:1]
    lpos1 = jnp.sum(jnp.where(oh1, loc, 0.0), axis=0, keepdims=True)
    lpos2 = jnp.sum(jnp.where(oh2, loc, 0.0), axis=0, keepdims=True)
    lane = lax.broadcasted_iota(jnp.int32, (N_EXPERTS, LANES), 1)
    seg_ref[0] = jnp.where(lane == 0, base, jnp.where(lane == 1, c_b, carry_s[...])).astype(jnp.int32)
    carry_s[...] = carry_s[...] + c_b
    cnt_ref[...] = carry_s[...].astype(jnp.int32)
    rw_ref[0] = jnp.concatenate([w1, w2, lpos1, lpos2, jnp.zeros((SUBLANES - 4, tm), F32)], axis=0)

    srow = lax.broadcasted_iota(jnp.int32, (ls, tm), 0)
    perm = jnp.where(srow == lpos1.astype(jnp.int32), 1.0, jnp.where(srow == lpos2.astype(jnp.int32), 1.0, 0.0))
    xsorted = _nn(perm.astype(BF16), hi)
    xsl_ref[0] = _pack_bf16_pairs(xsorted)


def _outproj(ya, yb, mg, x, wa, wb, wo, g_ffn, w_rg, b_rg, w_re, b_re, tm=MOE_TM):
    T, D = x.shape
    nt = T // tm
    wr = jnp.zeros((ROUTER_ROWS, D), F32).at[0:N_GROUPS].set(w_rg.T).at[EXPERT_ROW0:EXPERT_ROW0 + N_EXPERTS].set(w_re.T)
    wrh = wr.astype(BF16)
    wrl = (wr - wrh.astype(F32)).astype(BF16)
    br = jnp.zeros((ROUTER_ROWS, 1), F32).at[0:N_GROUPS, 0].set(b_rg).at[EXPERT_ROW0:EXPERT_ROW0 + N_EXPERTS, 0].set(b_re)
    r = np.arange(tm)
    tri = jnp.asarray((r[:, None] < r[None, :]).astype(np.float32), BF16)
    re = np.arange(N_EXPERTS)
    ltri = jnp.asarray((re[None, :] < re[:, None]).astype(np.float32), BF16)
    ls = _local_rows(tm)
    row = lambda n: pl.BlockSpec((tm, n), lambda i: (i, 0))
    full = lambda shp: pl.BlockSpec(shp, lambda i: (0,) * len(shp))
    return pl.pallas_call(
        functools.partial(_outproj_kernel, tm=tm, ls=ls),
        out_shape=[jax.ShapeDtypeStruct((T, D), F32), jax.ShapeDtypeStruct((nt, ls, D // 2), jnp.uint32),
                   jax.ShapeDtypeStruct((nt, N_EXPERTS, LANES), jnp.int32),
                   jax.ShapeDtypeStruct((nt, SUBLANES, tm), F32),
                   jax.ShapeDtypeStruct((N_EXPERTS, LANES), jnp.int32)],
        grid=(nt,),
        in_specs=[row(ya.shape[1]), row(yb.shape[1]), row(2 * D), row(D),
                  full(wa.shape), full(wb.shape), full(wo.shape), full((1, D)),
                  full((ROUTER_ROWS, D)), full((ROUTER_ROWS, D)), full((ROUTER_ROWS, 1)), full((tm, tm)),
                  full((N_EXPERTS, N_EXPERTS))],
        out_specs=[row(D), pl.BlockSpec((1, ls, D // 2), lambda i: (i, 0, 0)),
                   pl.BlockSpec((1, N_EXPERTS, LANES), lambda i: (i, 0, 0)),
                   pl.BlockSpec((1, SUBLANES, tm), lambda i: (i, 0, 0)),
                   full((N_EXPERTS, LANES))],
        scratch_shapes=[pltpu.VMEM((N_EXPERTS, LANES), F32)],
        compiler_params=pltpu.CompilerParams(dimension_semantics=("arbitrary",), vmem_limit_bytes=VMEM_LIMIT),
        name="outproj",
    )(ya, yb, mg, x, wa, wb, wo, g_ffn.reshape(1, D), wrh, wrl, br, tri, ltri)


SEG_FIELDS = 4
SEG_BITS = 7
TAIL_BITS = (TE // SUBLANES).bit_length() - 1


def _segment_copies(n8, bits, make_copy, wait):
    off = 0
    for bit in reversed(range(bits)):
        rows = SUBLANES << bit
        take = (n8 >> bit) & 1

        @pl.when(take == 1)
        def _(off=off, rows=rows):
            cp = make_copy(off, rows)
            if wait:
                cp.wait()
            else:
                cp.start()

        off = off + take * rows


def _dispatch_kernel(seg_ref, tail_ref, xsl_ref, xs_ref, zero_s, sem):
    i = pl.program_id(0)

    def segments(wait):
        def body(e, c):
            s0 = (i * N_EXPERTS + e) * SEG_FIELDS
            src0 = pl.multiple_of(seg_ref[s0], SUBLANES)
            dst0 = pl.multiple_of(seg_ref[s0 + 2], SUBLANES)

            def make_copy(off, rows):
                return pltpu.make_async_copy(xsl_ref.at[0, pl.ds(pl.multiple_of(src0 + off, SUBLANES), rows), :],
                                             xs_ref.at[pl.ds(pl.multiple_of(dst0 + off, SUBLANES), rows), :], sem)

            _segment_copies(seg_ref[s0 + 1], SEG_BITS, make_copy, wait)
            return c

        lax.fori_loop(0, N_EXPERTS, body, 0)

    def tails(wait):
        def body(e, c):
            dst0 = pl.multiple_of(tail_ref[2 * e], SUBLANES)

            def make_copy(off, rows):
                return pltpu.make_async_copy(zero_s.at[pl.ds(0, rows), :],
                                             xs_ref.at[pl.ds(pl.multiple_of(dst0 + off, SUBLANES), rows), :], sem)

            _segment_copies(tail_ref[2 * e + 1], TAIL_BITS, make_copy, wait)
            return c

        lax.fori_loop(0, N_EXPERTS, body, 0)

    def unused_tiles(wait):
        def body(t, c):
            cp = pltpu.make_async_copy(zero_s, xs_ref.at[pl.ds(pl.multiple_of(t * TE, TE), TE), :], sem)
            if wait:
                cp.wait()
            else:
                cp.start()
            return c

        lax.fori_loop(tail_ref[2 * N_EXPERTS], xs_ref.shape[0] // TE, body, 0)

    @pl.when(i == 0)
    def _():
        zero_s[...] = jnp.zeros_like(zero_s)
        tails(False)
        unused_tiles(False)
        tails(True)
        unused_tiles(True)

    segments(False)
    segments(True)


def _dispatch(seg, tail, xsl, n_rows):
    nt, ls, W = xsl.shape
    return pl.pallas_call(
        _dispatch_kernel,
        out_shape=jax.ShapeDtypeStruct((n_rows, W), xsl.dtype),
        grid_spec=pltpu.PrefetchScalarGridSpec(
            num_scalar_prefetch=2,
            grid=(nt,),
            in_specs=[pl.BlockSpec((1, ls, W), lambda i, sg, tl: (i, 0, 0))],
            out_specs=pl.BlockSpec(memory_space=pl.ANY),
            scratch_shapes=[pltpu.VMEM((TE, W), xsl.dtype), pltpu.SemaphoreType.DMA(())],
        ),
        compiler_params=pltpu.CompilerParams(dimension_semantics=("arbitrary",)),
        name="dispatch",
    )(seg, tail, xsl)


def _experts_kernel(te_ref, nv_ref, xs_ref, wg_ref, wu_ref, wd_ref, out_ref):
    i = pl.program_id(0)

    @pl.when(i < nv_ref[0])
    def _():
        x = _unpack_bf16_pairs(xs_ref[...])
        a = _nn(x, wg_ref[0].astype(BF16))
        u = _nn(x, wu_ref[0].astype(BF16))
        hid = (a * jax.nn.sigmoid(a)) * u
        y = _nn(hid.astype(BF16), wd_ref[0].astype(BF16))
        out_ref[...] = _pack_bf16_pairs(y.astype(BF16).astype(F32))

    @pl.when(i >= nv_ref[0])
    def _():
        out_ref[...] = jnp.zeros_like(out_ref)


def _experts(tile_expert, n_valid, xs, w_gate, w_up, w_down):
    n_rows, W = xs.shape
    D = 2 * W
    n_tiles = n_rows // TE
    last = lambda i, nv: jnp.minimum(i, nv[0] - 1)
    return pl.pallas_call(
        _experts_kernel,
        out_shape=jax.ShapeDtypeStruct((n_rows, W), jnp.uint32),
        grid_spec=pltpu.PrefetchScalarGridSpec(
            num_scalar_prefetch=2,
            grid=(n_tiles,),
            in_specs=[pl.BlockSpec((TE, W), lambda i, te, nv: (last(i, nv), 0)),
                      pl.BlockSpec((1, D, EXPERT_FF), lambda i, te, nv: (te[last(i, nv)], 0, 0)),
                      pl.BlockSpec((1, D, EXPERT_FF), lambda i, te, nv: (te[last(i, nv)], 0, 0)),
                      pl.BlockSpec((1, EXPERT_FF, D), lambda i, te, nv: (te[last(i, nv)], 0, 0))],
            out_specs=pl.BlockSpec((TE, W), lambda i, te, nv: (i, 0)),
        ),
        compiler_params=pltpu.CompilerParams(dimension_semantics=("arbitrary",), vmem_limit_bytes=VMEM_LIMIT),
        name="experts",
    )(tile_expert, n_valid, xs, w_gate, w_up, w_down)


def _combine_kernel(seg_ref, ys_ref, x1_ref, rw_ref, gfin_ref, out_ref, buf, sem, *, tm, ls, apply_norm):
    i = pl.program_id(0)
    nt = pl.num_programs(0)

    def segments(tile, wait):
        slot = tile % 2

        def body(e, c):
            s0 = (tile * N_EXPERTS + e) * SEG_FIELDS
            loc0 = pl.multiple_of(seg_ref[s0], SUBLANES)
            glob0 = pl.multiple_of(seg_ref[s0 + 2], SUBLANES)

            def make_copy(off, rows):
                return pltpu.make_async_copy(ys_ref.at[pl.ds(pl.multiple_of(glob0 + off, SUBLANES), rows), :],
                                             buf.at[slot, pl.ds(pl.multiple_of(loc0 + off, SUBLANES), rows), :],
                                             sem.at[slot])

            _segment_copies(seg_ref[s0 + 1], SEG_BITS, make_copy, wait)
            return c

        lax.fori_loop(0, N_EXPERTS, body, 0)

    @pl.when(i == 0)
    def _():
        buf[...] = jnp.zeros_like(buf)
        segments(i, False)

    @pl.when(i + 1 < nt)
    def _():
        segments(i + 1, False)

    segments(i, True)
    ysl = _unpack_bf16_pairs(buf[i % 2])
    cols = jnp.concatenate([rw_ref[0], jnp.zeros((LANES - SUBLANES, tm), F32)], axis=0).T
    srow = lax.broadcasted_iota(jnp.int32, (tm, ls), 1)
    y = x1_ref[...]
    for k in range(2):
        pick = jnp.where(srow == cols[:, 2 + k:3 + k].astype(jnp.int32), 1.0, 0.0).astype(BF16)
        y = y + cols[:, k:k + 1] * _nn(pick, ysl)
    if apply_norm:
        ms = jnp.mean(y * y, axis=-1, keepdims=True)
        y = y * lax.rsqrt(ms + EPS) * gfin_ref[...]
    out_ref[...] = y


def _combine(seg, ys, x1, rw, g_final, apply_norm, tm=MOE_TM):
    T, D = x1.shape
    nt = T // tm
    ls = _local_rows(tm)
    return pl.pallas_call(
        functools.partial(_combine_kernel, tm=tm, ls=ls, apply_norm=apply_norm),
        out_shape=jax.ShapeDtypeStruct((T, D), F32),
        grid_spec=pltpu.PrefetchScalarGridSpec(
            num_scalar_prefetch=1,
            grid=(nt,),
            in_specs=[pl.BlockSpec(memory_space=pl.ANY),
                      pl.BlockSpec((tm, D), lambda i, sg: (i, 0)),
                      pl.BlockSpec((1, SUBLANES, tm), lambda i, sg: (i, 0, 0)),
                      pl.BlockSpec((1, D), lambda i, sg: (0, 0))],
            out_specs=pl.BlockSpec((tm, D), lambda i, sg: (i, 0)),
            scratch_shapes=[pltpu.VMEM((2, ls, D // 2), jnp.uint32), pltpu.SemaphoreType.DMA((2,))],
        ),
        compiler_params=pltpu.CompilerParams(dimension_semantics=("arbitrary",), vmem_limit_bytes=VMEM_LIMIT),
        name="combine",
    )(seg, ys, x1, rw, g_final.reshape(1, D))


def _moe_plan(seg, counts, T, tm=MOE_TM):
    nt = T // tm
    n_tiles_max = (2 * T + nt * N_EXPERTS * (SUBLANES - 1)) // TE + N_EXPERTS
    total = counts[:, 0]
    tiles = (total + TE - 1) // TE
    ids = jnp.arange(N_EXPERTS)
    tile_end = jnp.sum(jnp.where(ids[None, :] <= ids[:, None], tiles[None, :], 0), axis=1)
    row0 = (tile_end - tiles) * TE
    segtab = jnp.stack([seg[:, :, 0], seg[:, :, 1] // SUBLANES, seg[:, :, 2] + row0[None, :],
                        jnp.zeros_like(seg[:, :, 0])], axis=-1).reshape(-1).astype(jnp.int32)
    tail = jnp.concatenate([jnp.stack([row0 + total, (tiles * TE - total) // SUBLANES], axis=-1).reshape(-1),
                            tile_end[-1:]]).astype(jnp.int32)
    tile_expert = jnp.minimum(jnp.sum(tile_end[None, :] <= jnp.arange(n_tiles_max)[:, None], axis=1),
                              N_EXPERTS - 1).astype(jnp.int32)
    return segtab, tail, tile_expert, tile_end[-1:].astype(jnp.int32), n_tiles_max * TE


def kernel(x, g_mix, w_in, nsa_pe_k, nsa_cmp_k_w1, nsa_cmp_k_w2, nsa_pe_v, nsa_cmp_v_w1, nsa_cmp_v_w2, rel_bias,
           gla_w_alpha, gla_b_alpha, gla_norm_g, w_branch_a, w_branch_b, w_out, g_ffn, w_router_group,
           b_router_group, w_router_expert, b_router_expert, w_exp_gate, w_exp_up, w_exp_down, g_final):
    B, S, D = x.shape
    T = B * S
    for l in range(w_in.shape[0]):
        w = _pack_inproj_weights(w_in[l])
        (q, kslc, kwin, kvc, misc, qkb, vb, rb, mg, vts, vtw, vtb) = _inproj(x, g_mix[l], w)
        kcb, vcbt = _compress(kvc, nsa_pe_k[l], nsa_cmp_k_w1[l], nsa_cmp_k_w2[l],
                              nsa_pe_v[l], nsa_cmp_v_w1[l], nsa_cmp_v_w2[l])
        ya = _nsa(q, kslc, vts, kwin, vtw, kcb, vcbt, misc, rel_bias)
        yb = _gla(qkb, vb, vtb, misc, rb, gla_w_alpha[l], gla_b_alpha[l], gla_norm_g[l])
        x1, xsl, seg, rw, counts = _outproj(
            ya.reshape(T, -1), yb.reshape(T, -1), mg.reshape(T, -1), x.reshape(T, D),
            w_branch_a[l].astype(BF16), w_branch_b[l].astype(BF16), w_out[l].astype(BF16), g_ffn[l],
            w_router_group[l], b_router_group[l], w_router_expert[l], b_router_expert[l])
        segtab, tail, tile_expert, n_valid, n_rows = _moe_plan(seg, counts, T)
        xs = _dispatch(segtab, tail, xsl, n_rows)
        ys = _experts(tile_expert, n_valid, xs, w_exp_gate[l], w_exp_up[l], w_exp_down[l])
        last_layer = l == w_in.shape[0] - 1
        x = _combine(segtab, ys, x1, rw, g_final, apply_norm=last_layer).reshape(B, S, D)
    return x
```

```python
import functools
import math

import numpy as np
import jax
import jax.numpy as jnp
from jax import lax
from jax.experimental import pallas as pl
from jax.experimental.pallas import tpu as pltpu

F32 = jnp.float32
BF16 = jnp.bfloat16

NSA_HEADS = 8
NSA_GROUPS = 2
HPG = NSA_HEADS // NSA_GROUPS
DH = 64
CMP_BLOCK = 32
CMP_STRIDE = 16
CMP_HIDDEN = 128
SLC_BLOCK = 64
SLC_TOPN = 16
WINDOW = 512
GLA_HEADS = 4
GLA_DK = 64
GLA_DV = 128
GLA_RANK = 16
GLA_TAU = 16.0
GLA_CHUNK = 64
REL_BUCKETS = 32
REL_MAX_EXACT = REL_BUCKETS // 2
REL_MAX_DIST = 128
N_GROUPS = 4
EPG = 8
N_EXPERTS = N_GROUPS * EPG
EXPERT_FF = 256
EPS = 1e-6

LANES = 128
SUBLANES = 8
VMEM_LIMIT = 56 * 1024 * 1024

LOG2E = math.log2(math.e)
NEG = -1e30
BIG = float(2.0 ** 100)
QT = 128
SLC_PAD = 128
WIN_PAD = 512
TE = 512
MOE_TM = 512
VT_ROWS = DH + 16
NSA_TILES = 2


def _nt(a, b):
    return lax.dot_general(a, b, (((1,), (1,)), ((), ())), preferred_element_type=F32)


def _nn(a, b):
    return jnp.dot(a, b, preferred_element_type=F32)


def _split3(x):
    a = x.astype(BF16)
    r = x - a.astype(F32)
    b = r.astype(BF16)
    c = (r - b.astype(F32)).astype(BF16)
    return a, b, c


def _t5_bucket_np(rel):
    n = np.maximum(rel, 0)
    nf = np.maximum(n, 1).astype(np.float32)
    large = REL_MAX_EXACT + (np.log(nf / np.float32(REL_MAX_EXACT)) / np.float32(math.log(REL_MAX_DIST / REL_MAX_EXACT))
                             * np.float32(REL_BUCKETS - REL_MAX_EXACT)).astype(np.int32)
    return np.where(n < REL_MAX_EXACT, n, np.minimum(large, REL_BUCKETS - 1)).astype(np.int32)


def _inproj_kernel(x_ref, g_ref, w_ref, kplace_ref, oq, okslc, okwin, okv, omisc, oqkb, ovb, orb, omg,
                   ovts, ovtw, ovtb, *, tm, seq):
    x = x_ref[0]
    ms = jnp.mean(x * x, axis=-1, keepdims=True)
    h = (x * lax.rsqrt(ms + EPS) * g_ref[...]).astype(BF16)

    def mm(a, b):
        return _nn(h, w_ref[:, a:b])

    oq[0] = mm(0, 512).astype(BF16)
    kv = mm(512, 1280)
    okv[0] = kv[:, 0:256]

    def spread(k):
        return _nn(k.astype(BF16), kplace_ref[...])

    row = lax.broadcasted_iota(jnp.int32, (tm, 256), 0) + pl.program_id(1) * tm
    lane = lax.broadcasted_iota(jnp.int32, (tm, 256), 1) % LANES
    onehot = jnp.where(lane - DH == row // SLC_BLOCK, 1.0, 0.0)
    okslc[0] = (spread(kv[:, 256:384]) + onehot).astype(BF16)
    okwin[0] = spread(kv[:, 512:640]).astype(BF16)

    ones_rows = jnp.where(lax.broadcasted_iota(jnp.int32, (VT_ROWS - DH, tm), 0) == 0, 1.0, 0.0)

    def vt_groups(v):
        t = v.T
        return jnp.concatenate([t[:DH], ones_rows, t[DH:], ones_rows], axis=0).astype(BF16)

    ovts[0] = vt_groups(kv[:, 384:512])
    ovtw[0] = vt_groups(kv[:, 640:768])
    omisc[0] = mm(1280, 1408)
    qkv = mm(1408, 2432)
    oqkb[0] = qkv[:, 0:512]
    ovb[0] = qkv[:, 512:1024].astype(BF16)
    ovtb[0] = qkv[:, 512:1024].T.astype(BF16)
    orb[0] = mm(2432, 2944)
    omg[0] = mm(2944, 4992)


def _pack_inproj_weights(w_in):
    o = np.cumsum([0, 512, 128, 128, 128, 128, 128, 128, 24, 256, 256, 512, 16, 512, 1024, 1024])
    (q_a, k_cmp, v_cmp, k_slc, v_slc, k_win, v_win, gate_a, q_b, k_b, v_b, a_b, r_b, mg_a, mg_b) = [
        w_in[:, o[i]:o[i + 1]] for i in range(15)]
    D = w_in.shape[0]
    ga = gate_a.reshape(D, 3, NSA_GROUPS, HPG)
    z4 = jnp.zeros((D, 4), w_in.dtype)
    misc = [ga[:, :, 0, :].reshape(D, 12), z4, ga[:, :, 1, :].reshape(D, 12), z4, a_b,
            jnp.zeros((D, LANES - 48), w_in.dtype)]
    w = jnp.concatenate([q_a * (DH ** -0.5 * LOG2E), k_cmp, v_cmp, k_slc, v_slc, k_win, v_win] + misc
                        + [q_b, k_b, v_b, r_b, mg_a, mg_b], axis=1)
    return w.astype(BF16)


def _inproj(x, g_mix, w, tm=512):
    B, S, D = x.shape
    nw = w.shape[1]
    src = np.arange(NSA_GROUPS * DH)
    kplace = np.zeros((NSA_GROUPS * DH, NSA_GROUPS * LANES), np.float32)
    kplace[src, (src // DH) * LANES + src % DH] = 1.0
    kplace = jnp.asarray(kplace, BF16)
    widths = [(512, BF16), (256, BF16), (256, BF16), (256, F32), (128, F32), (512, F32), (512, BF16),
              (512, F32), (2048, F32)]
    out_shape = [jax.ShapeDtypeStruct((B, S, n), dt) for n, dt in widths]
    out_specs = [pl.BlockSpec((1, tm, n), lambda b, i: (b, i, 0)) for n, _ in widths]
    for rows in (NSA_GROUPS * VT_ROWS, NSA_GROUPS * VT_ROWS, 512):
        out_shape.append(jax.ShapeDtypeStruct((B, rows, S), BF16))
        out_specs.append(pl.BlockSpec((1, rows, tm), lambda b, i: (b, 0, i)))
    return pl.pallas_call(
        functools.partial(_inproj_kernel, tm=tm, seq=S),
        out_shape=out_shape,
        grid=(B, S // tm),
        in_specs=[
            pl.BlockSpec((1, tm, D), lambda b, i: (b, i, 0)),
            pl.BlockSpec((1, D), lambda b, i: (0, 0)),
            pl.BlockSpec((D, nw), lambda b, i: (0, 0), pipeline_mode=pl.Buffered(1)),
            pl.BlockSpec(kplace.shape, lambda b, i: (0, 0)),
        ],
        out_specs=out_specs,
        compiler_params=pltpu.CompilerParams(dimension_semantics=("parallel", "parallel"),
                                             vmem_limit_bytes=VMEM_LIMIT),
        name="inproj",
    )(x, g_mix.reshape(1, D), w, kplace)


def _gelu_tanh(x):
    return 0.5 * x * (1.0 + jnp.tanh(math.sqrt(2.0 / math.pi) * (x + 0.044715 * (x * x * x))))


def _compress_kernel(xk_ref, xv_ref, pe_ref, w1_ref, w2k_ref, w2vt_ref, ok_ref, ovt_ref, *, nsub):
    for kind, x_ref in enumerate((xk_ref, xv_ref)):
        top = jnp.zeros((nsub, 2 * CMP_HIDDEN), F32)
        bot = jnp.zeros((nsub, 2 * CMP_HIDDEN), F32)
        for r in range(CMP_STRIDE):
            xr = x_ref[0, pl.ds(r, nsub, stride=CMP_STRIDE), :]
            top = top + _nn((xr + pe_ref[kind, 0, r:r + 1, :]).astype(BF16), w1_ref[kind, 0, r])
            bot = bot + _nn((xr + pe_ref[kind, 1, r:r + 1, :]).astype(BF16), w1_ref[kind, 1, r])
        hid = _gelu_tanh(top + pltpu.roll(bot, shift=nsub - 1, axis=0)).astype(BF16)
        for g in range(NSA_GROUPS):
            hg = hid[:, g * CMP_HIDDEN:(g + 1) * CMP_HIDDEN]
            if kind == 0:
                ok_ref[0, g] = _nn(hg, w2k_ref[...]).astype(BF16)
            else:
                ovt_ref[0, g] = _nt(w2vt_ref[...], hg).astype(BF16)


def _compress(kv_cmp, pe_k, w1k, w2k, pe_v, w1v, w2v):
    B, S, _ = kv_cmp.shape
    nsub = S // CMP_STRIDE
    G = NSA_GROUPS

    def prep(pe, w1):
        pe_t = jnp.tile(pe.reshape(2, CMP_STRIDE, DH), (1, 1, G))
        a = w1.reshape(2, CMP_STRIDE, DH, CMP_HIDDEN)
        z = jnp.zeros_like(a)
        w = jnp.concatenate([jnp.concatenate([a, z], axis=3), jnp.concatenate([z, a], axis=3)], axis=2)
        return pe_t, w.astype(BF16)

    pek, w1kb = prep(pe_k, w1k)
    pev, w1vb = prep(pe_v, w1v)
    pe = jnp.stack([pek, pev])
    w1 = jnp.stack([w1kb, w1vb])
    w2kp = jnp.concatenate([w2k, jnp.zeros_like(w2k)], axis=1).astype(BF16)
    w2vt = w2v.T.astype(BF16)
    full = lambda shp: pl.BlockSpec(shp, lambda b: (0,) * len(shp))
    return pl.pallas_call(
        functools.partial(_compress_kernel, nsub=nsub),
        out_shape=[jax.ShapeDtypeStruct((B, G, nsub, LANES), BF16),
                   jax.ShapeDtypeStruct((B, G, DH, nsub), BF16)],
        grid=(B,),
        in_specs=[pl.BlockSpec((1, S, G * DH), lambda b: (b, 0, 0)), pl.BlockSpec((1, S, G * DH), lambda b: (b, 0, 1)),
                  full(pe.shape), full(w1.shape), full((CMP_HIDDEN, LANES)), full((DH, CMP_HIDDEN))],
        out_specs=[pl.BlockSpec((1, G, nsub, LANES), lambda b: (b, 0, 0, 0)),
                   pl.BlockSpec((1, G, DH, nsub), lambda b: (b, 0, 0, 0))],
        compiler_params=pltpu.CompilerParams(dimension_semantics=("parallel",), vmem_limit_bytes=VMEM_LIMIT),
        name="compress",
    )(kv_cmp, kv_cmp, pe, w1, w2kp, w2vt)


def _bias_kernel(tbl_ref, bkn_ref, bkc_ref, near_ref, cmpb_ref):
    g = pl.program_id(0)
    for h in range(HPG):
        hd = g * HPG + h

        def lookup(bk):
            acc = jnp.full(bk.shape, NEG, F32)
            for b in range(REL_BUCKETS):
                acc = jnp.where(bk == b, tbl_ref[hd, b], acc)
            return acc

        vn = lookup(bkn_ref[...])
        near_ref[0, :, h * QT:(h + 1) * QT] = jnp.where(vn > 0.5 * NEG, (vn - tbl_ref[hd, REL_BUCKETS - 1]) * LOG2E, NEG)
        vc = lookup(bkc_ref[...])
        cmpb_ref[0, :, h * QT:(h + 1) * QT] = jnp.where(vc > 0.5 * NEG, vc * LOG2E, NEG)


def _nsa_bias_tables(rel_bias, seq):
    ql = np.arange(QT)
    ncmp = seq // CMP_STRIDE

    def buckets(rel):
        return jnp.asarray(np.where(rel >= 0, _t5_bucket_np(rel), -1).astype(np.int32))

    bkn = buckets(ql[None, :] + QT - np.arange(2 * QT)[:, None])
    y = np.arange(2 * ncmp)
    bkc = buckets(ql[None, :] - CMP_STRIDE * (y[:, None] - ncmp) - (CMP_BLOCK - 1))
    nql = HPG * QT
    return pl.pallas_call(
        _bias_kernel,
        out_shape=[jax.ShapeDtypeStruct((NSA_GROUPS, 2 * QT, nql), F32),
                   jax.ShapeDtypeStruct((NSA_GROUPS, 2 * ncmp, nql), F32)],
        grid=(NSA_GROUPS,),
        in_specs=[pl.BlockSpec(memory_space=pltpu.SMEM),
                  pl.BlockSpec((2 * QT, QT), lambda g: (0, 0)),
                  pl.BlockSpec((2 * ncmp, QT), lambda g: (0, 0))],
        out_specs=[pl.BlockSpec((1, 2 * QT, nql), lambda g: (g, 0, 0)),
                   pl.BlockSpec((1, 2 * ncmp, nql), lambda g: (g, 0, 0))],
        compiler_params=pltpu.CompilerParams(dimension_semantics=("parallel",)),
        name="t5bias",
    )(rel_bias.T, bkn, bkc)


def _nsa_kernel(q_ref, kslc_ref, vtslc_ref, kwin_ref, vtwin_ref, kcb_ref, vcbt_ref, misc_ref,
                near_ref, cmpb_ref, wmask_ref, ovt_ref, eye_ref, eye4_ref,
                out_ref, qaug_s, s0_s, s1_s, *, ncmp, nslc):
    G = NSA_GROUPS
    qts = [NSA_TILES * pl.program_id(1) + t for t in range(NSA_TILES)]
    chains = [(t, g) for t in range(NSA_TILES) for g in range(G)]
    nql = HPG * QT
    ck = 2 * QT

    def flash(carry, s, vt_chunk):
        m, acc = carry
        m_new = jnp.maximum(m, jnp.max(s, axis=0, keepdims=True))
        alpha = jnp.exp2(m - m_new)
        p = jnp.exp2((s - m_new).astype(BF16))
        return m_new, alpha * acc + _nn(vt_chunk, p)

    def finish(carry):
        m, acc = carry
        return acc[:DH] * (1.0 / acc[DH:DH + 1])

    init = (jnp.full((1, nql), NEG, F32), jnp.zeros((VT_ROWS, nql), F32))
    ns = [pl.multiple_of(QT * qt, QT) for qt in qts]
    nw = [pl.multiple_of(QT * qt + 3 * QT, QT) for qt in qts]
    off = [pl.multiple_of(ncmp - (QT // CMP_STRIDE) * qt, SUBLANES) for qt in qts]
    lane = lax.broadcasted_iota(jnp.int32, (nql, LANES), 1)
    win_aug = jnp.where(lane >= DH, -BIG, 0.0).astype(BF16)
    jidx = lax.broadcasted_iota(jnp.int32, (nslc, QT), 0)
    tq = [qt * QT + lax.broadcasted_iota(jnp.int32, (nslc, QT), 1) for qt in qts]
    forced = [(jidx == 0) | (jidx == tq[t] // SLC_BLOCK) | (jidx == tq[t] // SLC_BLOCK - 1) for t in range(NSA_TILES)]
    future = [jidx * SLC_BLOCK > tq[t] for t in range(NSA_TILES)]
    sub = lax.broadcasted_iota(jnp.int32, (SUBLANES, QT), 0)
    ones_lo = jnp.ones((DH, QT), F32)
    eye = eye_ref[...]
    ovt = ovt_ref[...]
    ngrp = nslc // SUBLANES

    def aug(sel01):
        rows = [ones_lo, sel01]
        if LANES - DH - nslc:
            rows.append(jnp.ones((LANES - DH - nslc, QT), F32))
        m01 = _nt(eye, jnp.concatenate(rows, axis=0).astype(BF16))
        return jnp.concatenate([((m01 - 1.0) * BIG).astype(BF16)] * HPG, axis=0)

    cs = range(len(chains))
    kl = [slice(g * LANES, (g + 1) * LANES) for _, g in chains]
    vr = [slice(g * VT_ROWS, (g + 1) * VT_ROWS) for _, g in chains]
    low = lax.broadcasted_iota(jnp.int32, (QT, LANES), 1) < DH

    def head_slot(t, hd):
        pair = q_ref[0, t * QT:(t + 1) * QT, (hd // 2) * LANES:(hd // 2 + 1) * LANES].astype(F32)
        if hd % 2:
            pair = pltpu.roll(pair, shift=DH, axis=1)
        return jnp.where(low, pair, 0.0).astype(BF16)

    q0 = [jnp.concatenate([head_slot(t, g * HPG + h) for h in range(HPG)], axis=0) for t, g in chains]
    qwin = [q0[c] + win_aug for c in cs]

    bc = [cmpb_ref[g, pl.ds(off[t], ncmp), :] for t, g in chains]
    sc = [_nt(kcb_ref[0, chains[c][1]], q0[c]) + bc[c] for c in cs]
    sw = [_nt(kwin_ref[0, pl.ds(ns[chains[c][0]], 3 * QT), kl[c]], qwin[c]) for c in cs]
    sw = [jnp.concatenate([sw[c][:QT] + wmask_ref[...], sw[c][QT:]], axis=0) for c in cs]

    pc = []
    for c in cs:
        mc = jnp.max(sc[c], axis=0, keepdims=True)
        ec = jnp.where(bc[c] > 0.5 * NEG, jnp.exp2(sc[c] - mc), 0.0)
        den = jnp.maximum(jnp.sum(ec, axis=0, keepdims=True), jnp.finfo(F32).tiny)
        pc.append(ec * (1.0 / den))
    wcar = [flash(init, sw[c], vtwin_ref[0, vr[c], pl.ds(ns[chains[c][0]], 3 * QT)]) for c in cs]
    o_c = [_nn(vcbt_ref[0, chains[c][1]], pc[c].astype(BF16)) for c in cs]

    imp = []
    for c in cs:
        t = chains[c][0]
        psum = pc[c][:, 0:QT]
        for h in range(1, HPG):
            psum = psum + pc[c][:, h * QT:(h + 1) * QT]
        p1, p2, p3 = _split3(psum)
        v = _nn(ovt, p1) + _nn(ovt, p2) + _nn(ovt, p3)
        imp.append(jnp.where(forced[t], 1e30, jnp.where(future[t], -1e30, v)))
    sw = [_nt(kwin_ref[0, pl.ds(nw[chains[c][0]], ck), kl[c]], qwin[c]) + near_ref[chains[c][1]]
          for c in cs]

    grp = [[imp[c][SUBLANES * v:SUBLANES * (v + 1)] for v in range(ngrp)] for c in cs]
    cnt = [[jnp.zeros((SUBLANES, QT), F32) for _ in range(ngrp)] for c in cs]
    for jp in range(nslc):
        v0, r0 = divmod(jp, SUBLANES)
        for c in cs:
            row = jnp.broadcast_to(imp[c][jp:jp + 1, :], (SUBLANES, QT))
            for v in range(ngrp):
                if v < v0:
                    inc = jnp.where(row > grp[c][v], 1.0, 0.0)
                elif v > v0:
                    inc = jnp.where(row >= grp[c][v], 1.0, 0.0)
                else:
                    inc = jnp.where(sub > r0, jnp.where(row >= grp[c][v], 1.0, 0.0),
                                    jnp.where(row > grp[c][v], 1.0, 0.0))
                cnt[c][v] = cnt[c][v] + inc
    o_w = [finish(flash(wcar[c], sw[c], vtwin_ref[0, vr[c], pl.ds(nw[chains[c][0]], ck)])) for c in cs]
    for c in cs:
        sel = jnp.concatenate(cnt[c], axis=0) < float(min(SLC_TOPN, nslc))
        sel_near = jnp.where(sel, 1.0, 0.0)
        sel_far = jnp.where(jidx < 2 * (qts[chains[c][0]] - 1), sel_near, 0.0)
        qaug_s[c, 0] = q0[c] + aug(sel_far)
        qaug_s[c, 1] = q0[c] + aug(sel_near)

    n_far = pl.program_id(1)
    n_chunks = n_far + 1

    def rows(c, i):
        return pl.multiple_of(QT * qts[chains[c][0]] - ck * i, QT)

    def scores(c, i):
        i = jnp.minimum(i, n_far)
        return _nt(kslc_ref[0, pl.ds(rows(c, i), ck), kl[c]], qaug_s[c, jnp.where(i == 0, 1, 0)])

    def vt(c, i):
        return vtslc_ref[0, vr[c], pl.ds(rows(c, i), ck)]

    for c in cs:
        s0_s[c] = scores(c, 0) + near_ref[chains[c][1]]

    def pair_body(p, carry):
        i = 2 * p
        carry = list(carry)
        for t in range(NSA_TILES):
            tc = [c for c in cs if chains[c][0] == t]
            for c in tc:
                s1_s[c] = scores(c, i + 1)
            for c in tc:
                carry[c] = flash(carry[c], s0_s[c], vt(c, i))
            for c in tc:
                s0_s[c] = scores(c, i + 2)
            for c in tc:
                carry[c] = flash(carry[c], s1_s[c], vt(c, i + 1))
        return tuple(carry)

    carry = lax.fori_loop(0, n_chunks // 2, pair_body, (init,) * len(chains))
    carry = lax.cond(n_chunks % 2 == 1,
                     lambda cr: tuple(flash(cr[c], s0_s[c], vt(c, n_far)) for c in cs),
                     lambda cr: cr, carry)

    gts = [jax.nn.sigmoid(misc_ref[0, t * QT:(t + 1) * QT, :]).T for t in range(NSA_TILES)]
    head = lax.broadcasted_iota(jnp.int32, (DH, nql), 1) // QT
    for c, (t, g) in enumerate(chains):
        o_s = finish(carry[c])

        def gate_row(br):
            r0 = 16 * g + br * HPG
            return jnp.concatenate([gts[t][r0 + h:r0 + h + 1, :] for h in range(HPG)], axis=1)

        o = gate_row(0) * o_c[c] + gate_row(1) * o_s + gate_row(2) * o_w[c]
        ob = o.astype(BF16)
        blocks = jnp.concatenate([jnp.where(head == h, ob, jnp.zeros_like(ob)) for h in range(HPG)], axis=0)
        out_ref[0, t * QT:(t + 1) * QT, g * HPG * DH:(g + 1) * HPG * DH] = _nt(eye4_ref[...], blocks).astype(BF16)


def _nsa(q, kslc, vtslc, kwin, vtwin, kcb, vcbt, misc, rel_bias):
    B, S, _ = q.shape
    G = NSA_GROUPS
    nq = S // QT
    ncmp = S // CMP_STRIDE
    nslc = S // SLC_BLOCK
    nql = HPG * QT
    near, cmpb = _nsa_bias_tables(rel_bias, S)
    wmask = jnp.asarray(np.tile(np.where(np.arange(QT)[:, None] > np.arange(QT)[None, :], 0.0, NEG), (1, HPG)), F32)
    kpad_s = jnp.concatenate([jnp.zeros((DH,), BF16), jnp.ones((DH,), BF16)] * G)
    kslc_p = jnp.concatenate([jnp.broadcast_to(kpad_s, (B, SLC_PAD, G * LANES)), kslc], axis=1)
    kwin_p = jnp.concatenate([jnp.broadcast_to(kpad_s, (B, WIN_PAD, G * LANES)), kwin], axis=1)
    vtslc_p = jnp.pad(vtslc, ((0, 0), (0, 0), (SLC_PAD, 0)))
    vtwin_p = jnp.pad(vtwin, ((0, 0), (0, 0), (WIN_PAD, 0)))
    ci = np.arange(ncmp)[None, :] * CMP_STRIDE
    sj = np.arange(nslc)[:, None] * SLC_BLOCK
    ovt = jnp.asarray(((ci < sj + SLC_BLOCK) & (ci + CMP_BLOCK > sj)).astype(np.float32), BF16)
    eye = jnp.eye(QT, dtype=BF16)
    eye4 = jnp.tile(eye, (1, HPG))
    kern = functools.partial(_nsa_kernel, ncmp=ncmp, nslc=nslc)
    nch = NSA_TILES * G
    per_b = lambda shp: pl.BlockSpec(shp, lambda b, i: (b,) + (0,) * (len(shp) - 1))
    full = lambda shp: pl.BlockSpec(shp, lambda b, i: (0,) * len(shp))
    return pl.pallas_call(
        kern,
        out_shape=jax.ShapeDtypeStruct((B, S, NSA_HEADS * DH), BF16),
        grid=(B, nq // NSA_TILES),
        in_specs=[
            pl.BlockSpec((1, NSA_TILES * QT, NSA_HEADS * DH), lambda b, i: (b, i, 0)),
            per_b((1, SLC_PAD + S, G * LANES)),
            per_b((1, G * VT_ROWS, SLC_PAD + S)),
            per_b((1, WIN_PAD + S, G * LANES)),
            per_b((1, G * VT_ROWS, WIN_PAD + S)),
            per_b((1, G, ncmp, LANES)),
            per_b((1, G, DH, ncmp)),
            pl.BlockSpec((1, NSA_TILES * QT, LANES), lambda b, i: (b, i, 0)),
            full((G, 2 * QT, nql)),
            full((G, 2 * ncmp, nql)),
            full((QT, nql)),
            full((nslc, ncmp)),
            full((QT, QT)),
            full((QT, nql)),
        ],
        out_specs=pl.BlockSpec((1, NSA_TILES * QT, NSA_HEADS * DH), lambda b, i: (b, i, 0)),
        scratch_shapes=[pltpu.VMEM((nch, 2, nql, LANES), BF16), pltpu.VMEM((nch, 2 * QT, nql), F32),
                        pltpu.VMEM((nch, 2 * QT, nql), F32)],
        compiler_params=pltpu.CompilerParams(dimension_semantics=("parallel", "arbitrary"),
                                             vmem_limit_bytes=VMEM_LIMIT),
        name="nsa",
    )(q, kslc_p, vtslc_p, kwin_p, vtwin_p, kcb, vcbt, misc, near, cmpb, wmask, ovt, eye, eye4)


def _gla_kernel(qk_ref, v_ref, vt_ref, misc_ref, r_ref, wal_ref, bal_ref, ng_ref, cum_ref, out_ref,
                state_s, o_s, *, ct):
    H, dk, dv, C = GLA_HEADS, GLA_DK, GLA_DV, GLA_CHUNK
    kw = H * dk

    @pl.when(pl.program_id(1) == 0)
    def _():
        state_s[...] = jnp.zeros_like(state_s)

    z = _nn(misc_ref[0].astype(BF16), wal_ref[...]) + bal_ref[...]
    log_a = (jnp.minimum(z, 0.0) - jnp.log1p(jnp.exp(-jnp.abs(z)))) * (1.0 / GLA_TAU)
    cum = cum_ref[...]
    a1, a2, a3 = _split3(log_a)
    cs = _nn(cum, a1) + _nn(cum, a2) + _nn(cum, a3)
    bc, bl = cs[:ct], cs[ct:]
    q = qk_ref[0, :, :kw]
    k = qk_ref[0, :, kw:]
    q_in = (q * (dk ** -0.5)) * jnp.exp(bc)
    k_in = (k * jnp.exp(-bc)).astype(BF16)
    k_st = k * jnp.exp(bl - bc)
    decay = jnp.exp(bl)
    lane_head = lax.broadcasted_iota(jnp.int32, (C, kw), 1) // dk
    rr = lax.broadcasted_iota(jnp.int32, (H * C, C), 0) % C
    cc = lax.broadcasted_iota(jnp.int32, (H * C, C), 1)
    causal = rr >= cc
    pair_row = lax.broadcasted_iota(jnp.int32, (2 * C, kw), 0) // C
    pair_head = lax.broadcasted_iota(jnp.int32, (2 * C, kw), 1) // dk

    for c in range(ct // C):
        r0 = c * C
        qc = q_in[r0:r0 + C]
        qcb = qc.astype(BF16)
        q_heads = jnp.concatenate([jnp.where(lane_head == h, qc, 0.0) for h in range(H)], axis=0).astype(BF16)
        attn = jnp.where(causal, _nt(q_heads, k_in[r0:r0 + C]), 0.0).astype(BF16)
        p0 = (c // 2) * 2 * C
        kst_pair = k_st[p0:p0 + 2 * C]
        dec = decay[r0:r0 + 1]
        for h in range(H):
            st = state_s[h]
            o = _nn(attn[h * C:(h + 1) * C], v_ref[0, r0:r0 + C, h * dv:(h + 1) * dv])
            o = o + _nt(qcb, st.astype(BF16))
            o_s[r0:r0 + C, h * dv:(h + 1) * dv] = o
            kst_h = jnp.where((pair_row == c % 2) & (pair_head == h), kst_pair, 0.0).astype(BF16)
            state_s[h] = st * dec + _nn(vt_ref[0, h * dv:(h + 1) * dv, p0:p0 + 2 * C], kst_h)

    for h in range(H):
        oh = o_s[:, h * dv:(h + 1) * dv]
        ms = jnp.mean(oh * oh, axis=-1, keepdims=True)
        r = r_ref[0, :, h * dv:(h + 1) * dv]
        y = oh * lax.rsqrt(ms + EPS) * ng_ref[:, h * dv:(h + 1) * dv] * (r * jax.nn.sigmoid(r))
        out_ref[0, :, h * dv:(h + 1) * dv] = y.astype(BF16)


def _gla(qkb, vb, vtb, misc, rb, w_alpha, b_alpha, norm_g, ct=512):
    B, S, _ = qkb.shape
    H, dk, dv, C = GLA_HEADS, GLA_DK, GLA_DV, GLA_CHUNK
    kw, vw = H * dk, H * dv
    wal = jnp.zeros((LANES, kw), F32).at[32:32 + GLA_RANK].set(w_alpha).astype(BF16)
    r = np.arange(ct)
    tri = (r[:, None] // C == r[None, :] // C) & (r[:, None] >= r[None, :])
    tot = r[:, None] // C == r[None, :] // C
    cum = jnp.asarray(np.concatenate([tri, tot], axis=0).astype(np.float32), BF16)
    full = lambda shp: pl.BlockSpec(shp, lambda b, i: (0,) * len(shp))
    return pl.pallas_call(
        functools.partial(_gla_kernel, ct=ct),
        out_shape=jax.ShapeDtypeStruct((B, S, vw), BF16),
        grid=(B, S // ct),
        in_specs=[
            pl.BlockSpec((1, ct, 2 * kw), lambda b, i: (b, i, 0)),
            pl.BlockSpec((1, ct, vw), lambda b, i: (b, i, 0)),
            pl.BlockSpec((1, vw, ct), lambda b, i: (b, 0, i)),
            pl.BlockSpec((1, ct, LANES), lambda b, i: (b, i, 0)),
            pl.BlockSpec((1, ct, vw), lambda b, i: (b, i, 0)),
            full((LANES, kw)), full((1, kw)), full((1, vw)), full((2 * ct, ct)),
        ],
        out_specs=pl.BlockSpec((1, ct, vw), lambda b, i: (b, i, 0)),
        scratch_shapes=[pltpu.VMEM((H, dv, kw), F32), pltpu.VMEM((ct, vw), F32)],
        compiler_params=pltpu.CompilerParams(dimension_semantics=("parallel", "arbitrary"),
                                             vmem_limit_bytes=VMEM_LIMIT),
        name="gla",
    )(qkb, vb, vtb, misc, rb, wal, b_alpha.reshape(1, kw), norm_g.reshape(1, vw), cum)


ROUTER_ROWS = 128
EXPERT_ROW0 = 8


def _local_rows(tm):
    return -(-(2 * tm + N_EXPERTS * (SUBLANES - 1)) // LANES) * LANES


def _pack_bf16_pairs(x):
    u = pltpu.bitcast(x, jnp.uint32)
    w = x.shape[1] // 2
    return u[:, :w] | (u[:, w:] >> 16)


def _unpack_bf16_pairs(w):
    xh = pltpu.bitcast(w & jnp.uint32(0xFFFF0000), F32)
    xl = pltpu.bitcast(w << 16, F32)
    return jnp.concatenate([xh, xl], axis=1).astype(BF16)


def _outproj_kernel(ya_ref, yb_ref, mg_ref, x_ref, wa_ref, wb_ref, wo_ref, gf_ref, wrh_ref, wrl_ref, br_ref,
                    tri_ref, ltri_ref, x1_ref, xsl_ref, seg_ref, rw_ref, cnt_ref, carry_s, *, tm, ls):
    D = x_ref.shape[1]

    @pl.when(pl.program_id(0) == 0)
    def _():
        carry_s[...] = jnp.zeros_like(carry_s)

    ma = _nn(ya_ref[...], wa_ref[...])
    mb = _nn(yb_ref[...], wb_ref[...])
    merged = jax.nn.sigmoid(mg_ref[:, :D]) * ma + jax.nn.sigmoid(mg_ref[:, D:]) * mb
    x1 = x_ref[...] + _nn(merged.astype(BF16), wo_ref[...])
    x1_ref[...] = x1
    ms = jnp.mean(x1 * x1, axis=-1, keepdims=True)
    h2 = x1 * lax.rsqrt(ms + EPS) * gf_ref[...]
    hi = h2.astype(BF16)
    lo = (h2 - hi.astype(F32)).astype(BF16)
    lg = _nt(wrh_ref[...], hi) + _nt(wrh_ref[...], lo) + _nt(wrl_ref[...], hi) + br_ref[...]
    row8 = lax.broadcasted_iota(jnp.int32, (SUBLANES, tm), 0)
    gl = jnp.where(row8 < N_GROUPS, lg[0:SUBLANES], NEG)
    gmax = jnp.max(gl, axis=0, keepdims=True)
    g_sel = jnp.min(jnp.where(gl == gmax, row8, SUBLANES), axis=0, keepdims=True)
    g_prob = 1.0 / jnp.sum(jnp.where(row8 < N_GROUPS, jnp.exp(gl - gmax), 0.0), axis=0, keepdims=True)
    e_sel = jnp.zeros((EPG, tm), F32)
    for gi in range(N_GROUPS):
        r0 = EXPERT_ROW0 + gi * EPG
        e_sel = e_sel + jnp.where(g_sel == gi, lg[r0:r0 + EPG], 0.0)
    v1 = jnp.max(e_sel, axis=0, keepdims=True)
    i1 = jnp.min(jnp.where(e_sel == v1, row8, EPG), axis=0, keepdims=True)
    rest = jnp.where(row8 == i1, -jnp.inf, e_sel)
    v2 = jnp.max(rest, axis=0, keepdims=True)
    i2 = jnp.min(jnp.where(rest == v2, row8, EPG), axis=0, keepdims=True)
    t = jnp.exp(v2 - v1)
    w1 = g_prob / (1.0 + t)
    w2 = g_prob * t / (1.0 + t)
    e1 = g_sel * EPG + i1
    e2 = g_sel * EPG + i2

    rowe = lax.broadcasted_iota(jnp.int32, (N_EXPERTS, tm), 0)
    oh1 = rowe == e1
    oh2 = rowe == e2
    oh = jnp.where(oh1, 1.0, 0.0) + jnp.where(oh2, 1.0, 0.0)
    pre = _nn(oh.astype(BF16), tri_ref[...])
    cnt = jnp.sum(oh, axis=1, keepdims=True)
    cnt8 = jnp.floor((cnt + (SUBLANES - 1)) * (1.0 / SUBLANES)) * SUBLANES
    c_b = jnp.broadcast_to(cnt8, (N_EXPERTS, LANES))
    c_hi = jnp.floor(c_b * (1.0 / 16.0))
    c_lo = c_b - 16.0 * c_hi
    base = 16.0 * _nn(ltri_ref[...], c_hi.astype(BF16)) + _nn(ltri_ref[...], c_lo.astype(BF16))
    loc = pre + base[:, 0:1]
    lpos1 = jnp.sum(jnp.where(oh1, loc, 0.0), axis=0, keepdims=True)
    lpos2 = jnp.sum(jnp.where(oh2, loc, 0.0), axis=0, keepdims=True)
    lane = lax.broadcasted_iota(jnp.int32, (N_EXPERTS, LANES), 1)
    seg_ref[0] = jnp.where(lane == 0, base, jnp.where(lane == 1, c_b, carry_s[...])).astype(jnp.int32)
    carry_s[...] = carry_s[...] + c_b
    cnt_ref[...] = carry_s[...].astype(jnp.int32)
    rw_ref[0] = jnp.concatenate([w1, w2, lpos1, lpos2, jnp.zeros((SUBLANES - 4, tm), F32)], axis=0)

    srow = lax.broadcasted_iota(jnp.int32, (ls, tm), 0)
    perm = jnp.where(srow == lpos1.astype(jnp.int32), 1.0, jnp.where(srow == lpos2.astype(jnp.int32), 1.0, 0.0))
    xsorted = _nn(perm.astype(BF16), hi)
    xsl_ref[0] = _pack_bf16_pairs(xsorted)


def _outproj(ya, yb, mg, x, wa, wb, wo, g_ffn, w_rg, b_rg, w_re, b_re, tm=MOE_TM):
    T, D = x.shape
    nt = T // tm
    wr = jnp.zeros((ROUTER_ROWS, D), F32).at[0:N_GROUPS].set(w_rg.T).at[EXPERT_ROW0:EXPERT_ROW0 + N_EXPERTS].set(w_re.T)
    wrh = wr.astype(BF16)
    wrl = (wr - wrh.astype(F32)).astype(BF16)
    br = jnp.zeros((ROUTER_ROWS, 1), F32).at[0:N_GROUPS, 0].set(b_rg).at[EXPERT_ROW0:EXPERT_ROW0 + N_EXPERTS, 0].set(b_re)
    r = np.arange(tm)
    tri = jnp.asarray((r[:, None] < r[None, :]).astype(np.float32), BF16)
    re = np.arange(N_EXPERTS)
    ltri = jnp.asarray((re[None, :] < re[:, None]).astype(np.float32), BF16)
    ls = _local_rows(tm)
    row = lambda n: pl.BlockSpec((tm, n), lambda i: (i, 0))
    full = lambda shp: pl.BlockSpec(shp, lambda i: (0,) * len(shp))
    return pl.pallas_call(
        functools.partial(_outproj_kernel, tm=tm, ls=ls),
        out_shape=[jax.ShapeDtypeStruct((T, D), F32), jax.ShapeDtypeStruct((nt, ls, D // 2), jnp.uint32),
                   jax.ShapeDtypeStruct((nt, N_EXPERTS, LANES), jnp.int32),
                   jax.ShapeDtypeStruct((nt, SUBLANES, tm), F32),
                   jax.ShapeDtypeStruct((N_EXPERTS, LANES), jnp.int32)],
        grid=(nt,),
        in_specs=[row(ya.shape[1]), row(yb.shape[1]), row(2 * D), row(D),
                  full(wa.shape), full(wb.shape), full(wo.shape), full((1, D)),
                  full((ROUTER_ROWS, D)), full((ROUTER_ROWS, D)), full((ROUTER_ROWS, 1)), full((tm, tm)),
                  full((N_EXPERTS, N_EXPERTS))],
        out_specs=[row(D), pl.BlockSpec((1, ls, D // 2), lambda i: (i, 0, 0)),
                   pl.BlockSpec((1, N_EXPERTS, LANES), lambda i: (i, 0, 0)),
                   pl.BlockSpec((1, SUBLANES, tm), lambda i: (i, 0, 0)),
                   full((N_EXPERTS, LANES))],
        scratch_shapes=[pltpu.VMEM((N_EXPERTS, LANES), F32)],
        compiler_params=pltpu.CompilerParams(dimension_semantics=("arbitrary",), vmem_limit_bytes=VMEM_LIMIT),
        name="outproj",
    )(ya, yb, mg, x, wa, wb, wo, g_ffn.reshape(1, D), wrh, wrl, br, tri, ltri)


SEG_FIELDS = 4
SEG_BITS = 7
TAIL_BITS = (TE // SUBLANES).bit_length() - 1


def _segment_copies(n8, bits, make_copy, wait):
    off = 0
    for bit in reversed(range(bits)):
        rows = SUBLANES << bit
        take = (n8 >> bit) & 1

        @pl.when(take == 1)
        def _(off=off, rows=rows):
            cp = make_copy(off, rows)
            if wait:
                cp.wait()
            else:
                cp.start()

        off = off + take * rows


def _dispatch_kernel(seg_ref, tail_ref, xsl_ref, xs_ref, zero_s, sem):
    i = pl.program_id(0)

    def segments(wait):
        def body(e, c):
            s0 = (i * N_EXPERTS + e) * SEG_FIELDS
            src0 = pl.multiple_of(seg_ref[s0], SUBLANES)
            dst0 = pl.multiple_of(seg_ref[s0 + 2], SUBLANES)

            def make_copy(off, rows):
                return pltpu.make_async_copy(xsl_ref.at[0, pl.ds(pl.multiple_of(src0 + off, SUBLANES), rows), :],
                                             xs_ref.at[pl.ds(pl.multiple_of(dst0 + off, SUBLANES), rows), :], sem)

            _segment_copies(seg_ref[s0 + 1], SEG_BITS, make_copy, wait)
            return c

        lax.fori_loop(0, N_EXPERTS, body, 0)

    def tails(wait):
        def body(e, c):
            dst0 = pl.multiple_of(tail_ref[2 * e], SUBLANES)

            def make_copy(off, rows):
                return pltpu.make_async_copy(zero_s.at[pl.ds(0, rows), :],
                                             xs_ref.at[pl.ds(pl.multiple_of(dst0 + off, SUBLANES), rows), :], sem)

            _segment_copies(tail_ref[2 * e + 1], TAIL_BITS, make_copy, wait)
            return c

        lax.fori_loop(0, N_EXPERTS, body, 0)

    def unused_tiles(wait):
        def body(t, c):
            cp = pltpu.make_async_copy(zero_s, xs_ref.at[pl.ds(pl.multiple_of(t * TE, TE), TE), :], sem)
            if wait:
                cp.wait()
            else:
                cp.start()
            return c

        lax.fori_loop(tail_ref[2 * N_EXPERTS], xs_ref.shape[0] // TE, body, 0)

    @pl.when(i == 0)
    def _():
        zero_s[...] = jnp.zeros_like(zero_s)
        tails(False)
        unused_tiles(False)
        tails(True)
        unused_tiles(True)

    segments(False)
    segments(True)


def _dispatch(seg, tail, xsl, n_rows):
    nt, ls, W = xsl.shape
    return pl.pallas_call(
        _dispatch_kernel,
        out_shape=jax.ShapeDtypeStruct((n_rows, W), xsl.dtype),
        grid_spec=pltpu.PrefetchScalarGridSpec(
            num_scalar_prefetch=2,
            grid=(nt,),
            in_specs=[pl.BlockSpec((1, ls, W), lambda i, sg, tl: (i, 0, 0))],
            out_specs=pl.BlockSpec(memory_space=pl.ANY),
            scratch_shapes=[pltpu.VMEM((TE, W), xsl.dtype), pltpu.SemaphoreType.DMA(())],
        ),
        compiler_params=pltpu.CompilerParams(dimension_semantics=("arbitrary",)),
        name="dispatch",
    )(seg, tail, xsl)


def _experts_kernel(te_ref, nv_ref, xs_ref, wg_ref, wu_ref, wd_ref, out_ref):
    i = pl.program_id(0)

    @pl.when(i < nv_ref[0])
    def _():
        x = _unpack_bf16_pairs(xs_ref[...])
        a = _nn(x, wg_ref[0].astype(BF16))
        u = _nn(x, wu_ref[0].astype(BF16))
        hid = (a * jax.nn.sigmoid(a)) * u
        y = _nn(hid.astype(BF16), wd_ref[0].astype(BF16))
        out_ref[...] = _pack_bf16_pairs(y.astype(BF16).astype(F32))

    @pl.when(i >= nv_ref[0])
    def _():
        out_ref[...] = jnp.zeros_like(out_ref)


def _experts(tile_expert, n_valid, xs, w_gate, w_up, w_down):
    n_rows, W = xs.shape
    D = 2 * W
    n_tiles = n_rows // TE
    last = lambda i, nv: jnp.minimum(i, nv[0] - 1)
    return pl.pallas_call(
        _experts_kernel,
        out_shape=jax.ShapeDtypeStruct((n_rows, W), jnp.uint32),
        grid_spec=pltpu.PrefetchScalarGridSpec(
            num_scalar_prefetch=2,
            grid=(n_tiles,),
            in_specs=[pl.BlockSpec((TE, W), lambda i, te, nv: (last(i, nv), 0)),
                      pl.BlockSpec((1, D, EXPERT_FF), lambda i, te, nv: (te[last(i, nv)], 0, 0)),
                      pl.BlockSpec((1, D, EXPERT_FF), lambda i, te, nv: (te[last(i, nv)], 0, 0)),
                      pl.BlockSpec((1, EXPERT_FF, D), lambda i, te, nv: (te[last(i, nv)], 0, 0))],
            out_specs=pl.BlockSpec((TE, W), lambda i, te, nv: (i, 0)),
        ),
        compiler_params=pltpu.CompilerParams(dimension_semantics=("arbitrary",), vmem_limit_bytes=VMEM_LIMIT),
        name="experts",
    )(tile_expert, n_valid, xs, w_gate, w_up, w_down)


def _combine_kernel(seg_ref, ys_ref, x1_ref, rw_ref, gfin_ref, out_ref, buf, sem, *, tm, ls, apply_norm):
    i = pl.program_id(0)
    nt = pl.num_programs(0)

    def segments(tile, wait):
        slot = tile % 2

        def body(e, c):
            s0 = (tile * N_EXPERTS + e) * SEG_FIELDS
            loc0 = pl.multiple_of(seg_ref[s0], SUBLANES)
            glob0 = pl.multiple_of(seg_ref[s0 + 2], SUBLANES)

            def make_copy(off, rows):
                return pltpu.make_async_copy(ys_ref.at[pl.ds(pl.multiple_of(glob0 + off, SUBLANES), rows), :],
                                             buf.at[slot, pl.ds(pl.multiple_of(loc0 + off, SUBLANES), rows), :],
                                             sem.at[slot])

            _segment_copies(seg_ref[s0 + 1], SEG_BITS, make_copy, wait)
            return c

        lax.fori_loop(0, N_EXPERTS, body, 0)

    @pl.when(i == 0)
    def _():
        buf[...] = jnp.zeros_like(buf)
        segments(i, False)

    @pl.when(i + 1 < nt)
    def _():
        segments(i + 1, False)

    segments(i, True)
    ysl = _unpack_bf16_pairs(buf[i % 2])
    cols = jnp.concatenate([rw_ref[0], jnp.zeros((LANES - SUBLANES, tm), F32)], axis=0).T
    srow = lax.broadcasted_iota(jnp.int32, (tm, ls), 1)
    y = x1_ref[...]
    for k in range(2):
        pick = jnp.where(srow == cols[:, 2 + k:3 + k].astype(jnp.int32), 1.0, 0.0).astype(BF16)
        y = y + cols[:, k:k + 1] * _nn(pick, ysl)
    if apply_norm:
        ms = jnp.mean(y * y, axis=-1, keepdims=True)
        y = y * lax.rsqrt(ms + EPS) * gfin_ref[...]
    out_ref[...] = y


def _combine(seg, ys, x1, rw, g_final, apply_norm, tm=MOE_TM):
    T, D = x1.shape
    nt = T // tm
    ls = _local_rows(tm)
    return pl.pallas_call(
        functools.partial(_combine_kernel, tm=tm, ls=ls, apply_norm=apply_norm),
        out_shape=jax.ShapeDtypeStruct((T, D), F32),
        grid_spec=pltpu.PrefetchScalarGridSpec(
            num_scalar_prefetch=1,
            grid=(nt,),
            in_specs=[pl.BlockSpec(memory_space=pl.ANY),
                      pl.BlockSpec((tm, D), lambda i, sg: (i, 0)),
                      pl.BlockSpec((1, SUBLANES, tm), lambda i, sg: (i, 0, 0)),
                      pl.BlockSpec((1, D), lambda i, sg: (0, 0))],
            out_specs=pl.BlockSpec((tm, D), lambda i, sg: (i, 0)),
            scratch_shapes=[pltpu.VMEM((2, ls, D // 2), jnp.uint32), pltpu.SemaphoreType.DMA((2,))],
        ),
        compiler_params=pltpu.CompilerParams(dimension_semantics=("arbitrary",), vmem_limit_bytes=VMEM_LIMIT),
        name="combine",
    )(seg, ys, x1, rw, g_final.reshape(1, D))


def _moe_plan(seg, counts, T, tm=MOE_TM):
    nt = T // tm
    n_tiles_max = (2 * T + nt * N_EXPERTS * (SUBLANES - 1)) // TE + N_EXPERTS
    total = counts[:, 0]
    tiles = (total + TE - 1) // TE
    ids = jnp.arange(N_EXPERTS)
    tile_end = jnp.sum(jnp.where(ids[None, :] <= ids[:, None], tiles[None, :], 0), axis=1)
    row0 = (tile_end - tiles) * TE
    segtab = jnp.stack([seg[:, :, 0], seg[:, :, 1] // SUBLANES, seg[:, :, 2] + row0[None, :],
                        jnp.zeros_like(seg[:, :, 0])], axis=-1).reshape(-1).astype(jnp.int32)
    tail = jnp.concatenate([jnp.stack([row0 + total, (tiles * TE - total) // SUBLANES], axis=-1).reshape(-1),
                            tile_end[-1:]]).astype(jnp.int32)
    tile_expert = jnp.minimum(jnp.sum(tile_end[None, :] <= jnp.arange(n_tiles_max)[:, None], axis=1),
                              N_EXPERTS - 1).astype(jnp.int32)
    return segtab, tail, tile_expert, tile_end[-1:].astype(jnp.int32), n_tiles_max * TE


def kernel(x, g_mix, w_in, nsa_pe_k, nsa_cmp_k_w1, nsa_cmp_k_w2, nsa_pe_v, nsa_cmp_v_w1, nsa_cmp_v_w2, rel_bias,
           gla_w_alpha, gla_b_alpha, gla_norm_g, w_branch_a, w_branch_b, w_out, g_ffn, w_router_group,
           b_router_group, w_router_expert, b_router_expert, w_exp_gate, w_exp_up, w_exp_down, g_final):
    B, S, D = x.shape
    T = B * S
    for l in range(w_in.shape[0]):
        w = _pack_inproj_weights(w_in[l])
        (q, kslc, kwin, kvc, misc, qkb, vb, rb, mg, vts, vtw, vtb) = _inproj(x, g_mix[l], w)
        kcb, vcbt = _compress(kvc, nsa_pe_k[l], nsa_cmp_k_w1[l], nsa_cmp_k_w2[l],
                              nsa_pe_v[l], nsa_cmp_v_w1[l], nsa_cmp_v_w2[l])
        ya = _nsa(q, kslc, vts, kwin, vtw, kcb, vcbt, misc, rel_bias)
        yb = _gla(qkb, vb, vtb, misc, rb, gla_w_alpha[l], gla_b_alpha[l], gla_norm_g[l])
        x1, xsl, seg, rw, counts = _outproj(
            ya.reshape(T, -1), yb.reshape(T, -1), mg.reshape(T, -1), x.reshape(T, D),
            w_branch_a[l].astype(BF16), w_branch_b[l].astype(BF16), w_out[l].astype(BF16), g_ffn[l],
            w_router_group[l], b_router_group[l], w_router_expert[l], b_router_expert[l])
        segtab, tail, tile_expert, n_valid, n_rows = _moe_plan(seg, counts, T)
        xs = _dispatch(segtab, tail, xsl, n_rows)
        ys = _experts(tile_expert, n_valid, xs, w_exp_gate[l], w_exp_up[l], w_exp_down[l])
        last_layer = l == w_in.shape[0] - 1
        x = _combine(segtab, ys, x1, rw, g_final, apply_norm=last_layer).reshape(B, S, D)
    return x
```

```python
import functools
import math

import numpy as np
import jax
import jax.numpy as jnp
from jax import lax
from jax.experimental import pallas as pl
from jax.experimental.pallas import tpu as pltpu

F32 = jnp.float32
BF16 = jnp.bfloat16

NSA_HEADS = 8
NSA_GROUPS = 2
HPG = NSA_HEADS // NSA_GROUPS
DH = 64
CMP_BLOCK = 32
CMP_STRIDE = 16
CMP_HIDDEN = 128
SLC_BLOCK = 64
SLC_TOPN = 16
WINDOW = 512
GLA_HEADS = 4
GLA_DK = 64
GLA_DV = 128
GLA_RANK = 16
GLA_TAU = 16.0
GLA_CHUNK = 64
REL_BUCKETS = 32
REL_MAX_EXACT = REL_BUCKETS // 2
REL_MAX_DIST = 128
N_GROUPS = 4
EPG = 8
N_EXPERTS = N_GROUPS * EPG
EXPERT_FF = 256
EPS = 1e-6

LANES = 128
SUBLANES = 8
VMEM_LIMIT = 56 * 1024 * 1024

LOG2E = math.log2(math.e)
NEG = -1e30
BIG = float(2.0 ** 100)
QT = 128
SLC_PAD = 128
WIN_PAD = 512
TE = 512
MOE_TM = 512
VT_ROWS = DH + 16
NSA_TILES = 2


def _nt(a, b):
    return lax.dot_general(a, b, (((1,), (1,)), ((), ())), preferred_element_type=F32)


def _nn(a, b):
    return jnp.dot(a, b, preferred_element_type=F32)


def _split3(x):
    a = x.astype(BF16)
    r = x - a.astype(F32)
    b = r.astype(BF16)
    c = (r - b.astype(F32)).astype(BF16)
    return a, b, c


def _t5_bucket_np(rel):
    n = np.maximum(rel, 0)
    nf = np.maximum(n, 1).astype(np.float32)
    large = REL_MAX_EXACT + (np.log(nf / np.float32(REL_MAX_EXACT)) / np.float32(math.log(REL_MAX_DIST / REL_MAX_EXACT))
                             * np.float32(REL_BUCKETS - REL_MAX_EXACT)).astype(np.int32)
    return np.where(n < REL_MAX_EXACT, n, np.minimum(large, REL_BUCKETS - 1)).astype(np.int32)


def _inproj_kernel(x_ref, g_ref, w_ref, kplace_ref, oq, okslc, okwin, okv, omisc, oqkb, ovb, orb, omg,
                   ovts, ovtw, ovtb, *, tm, seq):
    x = x_ref[0]
    ms = jnp.mean(x * x, axis=-1, keepdims=True)
    h = (x * lax.rsqrt(ms + EPS) * g_ref[...]).astype(BF16)

    def mm(a, b):
        return _nn(h, w_ref[:, a:b])

    oq[0] = mm(0, 512).astype(BF16)
    kv = mm(512, 1280)
    okv[0] = kv[:, 0:256]

    def spread(k):
        return _nn(k.astype(BF16), kplace_ref[...])

    row = lax.broadcasted_iota(jnp.int32, (tm, 256), 0) + pl.program_id(1) * tm
    lane = lax.broadcasted_iota(jnp.int32, (tm, 256), 1) % LANES
    onehot = jnp.where(lane - DH == row // SLC_BLOCK, 1.0, 0.0)
    okslc[0] = (spread(kv[:, 256:384]) + onehot).astype(BF16)
    okwin[0] = spread(kv[:, 512:640]).astype(BF16)

    ones_rows = jnp.where(lax.broadcasted_iota(jnp.int32, (VT_ROWS - DH, tm), 0) == 0, 1.0, 0.0)

    def vt_groups(v):
        t = v.T
        return jnp.concatenate([t[:DH], ones_rows, t[DH:], ones_rows], axis=0).astype(BF16)

    ovts[0] = vt_groups(kv[:, 384:512])
    ovtw[0] = vt_groups(kv[:, 640:768])
    omisc[0] = mm(1280, 1408)
    qkv = mm(1408, 2432)
    oqkb[0] = qkv[:, 0:512]
    ovb[0] = qkv[:, 512:1024].astype(BF16)
    ovtb[0] = qkv[:, 512:1024].T.astype(BF16)
    orb[0] = mm(2432, 2944)
    omg[0] = mm(2944, 4992)


def _pack_inproj_weights(w_in):
    o = np.cumsum([0, 512, 128, 128, 128, 128, 128, 128, 24, 256, 256, 512, 16, 512, 1024, 1024])
    (q_a, k_cmp, v_cmp, k_slc, v_slc, k_win, v_win, gate_a, q_b, k_b, v_b, a_b, r_b, mg_a, mg_b) = [
        w_in[:, o[i]:o[i + 1]] for i in range(15)]
    D = w_in.shape[0]
    ga = gate_a.reshape(D, 3, NSA_GROUPS, HPG)
    z4 = jnp.zeros((D, 4), w_in.dtype)
    misc = [ga[:, :, 0, :].reshape(D, 12), z4, ga[:, :, 1, :].reshape(D, 12), z4, a_b,
            jnp.zeros((D, LANES - 48), w_in.dtype)]
    w = jnp.concatenate([q_a * (DH ** -0.5 * LOG2E), k_cmp, v_cmp, k_slc, v_slc, k_win, v_win] + misc
                        + [q_b, k_b, v_b, r_b, mg_a, mg_b], axis=1)
    return w.astype(BF16)


def _inproj(x, g_mix, w, tm=512):
    B, S, D = x.shape
    nw = w.shape[1]
    src = np.arange(NSA_GROUPS * DH)
    kplace = np.zeros((NSA_GROUPS * DH, NSA_GROUPS * LANES), np.float32)
    kplace[src, (src // DH) * LANES + src % DH] = 1.0
    kplace = jnp.asarray(kplace, BF16)
    widths = [(512, BF16), (256, BF16), (256, BF16), (256, F32), (128, F32), (512, F32), (512, BF16),
              (512, F32), (2048, F32)]
    out_shape = [jax.ShapeDtypeStruct((B, S, n), dt) for n, dt in widths]
    out_specs = [pl.BlockSpec((1, tm, n), lambda b, i: (b, i, 0)) for n, _ in widths]
    for rows in (NSA_GROUPS * VT_ROWS, NSA_GROUPS * VT_ROWS, 512):
        out_shape.append(jax.ShapeDtypeStruct((B, rows, S), BF16))
        out_specs.append(pl.BlockSpec((1, rows, tm), lambda b, i: (b, 0, i)))
    return pl.pallas_call(
        functools.partial(_inproj_kernel, tm=tm, seq=S),
        out_shape=out_shape,
        grid=(B, S // tm),
        in_specs=[
            pl.BlockSpec((1, tm, D), lambda b, i: (b, i, 0)),
            pl.BlockSpec((1, D), lambda b, i: (0, 0)),
            pl.BlockSpec((D, nw), lambda b, i: (0, 0), pipeline_mode=pl.Buffered(1)),
            pl.BlockSpec(kplace.shape, lambda b, i: (0, 0)),
        ],
        out_specs=out_specs,
        compiler_params=pltpu.CompilerParams(dimension_semantics=("parallel", "parallel"),
                                             vmem_limit_bytes=VMEM_LIMIT),
        name="inproj",
    )(x, g_mix.reshape(1, D), w, kplace)


def _gelu_tanh(x):
    return 0.5 * x * (1.0 + jnp.tanh(math.sqrt(2.0 / math.pi) * (x + 0.044715 * (x * x * x))))


def _compress_kernel(xk_ref, xv_ref, pe_ref, w1_ref, w2k_ref, w2vt_ref, ok_ref, ovt_ref, *, nsub):
    for kind, x_ref in enumerate((xk_ref, xv_ref)):
        top = jnp.zeros((nsub, 2 * CMP_HIDDEN), F32)
        bot = jnp.zeros((nsub, 2 * CMP_HIDDEN), F32)
        for r in range(CMP_STRIDE):
            xr = x_ref[0, pl.ds(r, nsub, stride=CMP_STRIDE), :]
            top = top + _nn((xr + pe_ref[kind, 0, r:r + 1, :]).astype(BF16), w1_ref[kind, 0, r])
            bot = bot + _nn((xr + pe_ref[kind, 1, r:r + 1, :]).astype(BF16), w1_ref[kind, 1, r])
        hid = _gelu_tanh(top + pltpu.roll(bot, shift=nsub - 1, axis=0)).astype(BF16)
        for g in range(NSA_GROUPS):
            hg = hid[:, g * CMP_HIDDEN:(g + 1) * CMP_HIDDEN]
            if kind == 0:
                ok_ref[0, g] = _nn(hg, w2k_ref[...]).astype(BF16)
            else:
                ovt_ref[0, g] = _nt(w2vt_ref[...], hg).astype(BF16)


def _compress(kv_cmp, pe_k, w1k, w2k, pe_v, w1v, w2v):
    B, S, _ = kv_cmp.shape
    nsub = S // CMP_STRIDE
    G = NSA_GROUPS

    def prep(pe, w1):
        pe_t = jnp.tile(pe.reshape(2, CMP_STRIDE, DH), (1, 1, G))
        a = w1.reshape(2, CMP_STRIDE, DH, CMP_HIDDEN)
        z = jnp.zeros_like(a)
        w = jnp.concatenate([jnp.concatenate([a, z], axis=3), jnp.concatenate([z, a], axis=3)], axis=2)
        return pe_t, w.astype(BF16)

    pek, w1kb = prep(pe_k, w1k)
    pev, w1vb = prep(pe_v, w1v)
    pe = jnp.stack([pek, pev])
    w1 = jnp.stack([w1kb, w1vb])
    w2kp = jnp.concatenate([w2k, jnp.zeros_like(w2k)], axis=1).astype(BF16)
    w2vt = w2v.T.astype(BF16)
    full = lambda shp: pl.BlockSpec(shp, lambda b: (0,) * len(shp))
    return pl.pallas_call(
        functools.partial(_compress_kernel, nsub=nsub),
        out_shape=[jax.ShapeDtypeStruct((B, G, nsub, LANES), BF16),
                   jax.ShapeDtypeStruct((B, G, DH, nsub), BF16)],
        grid=(B,),
        in_specs=[pl.BlockSpec((1, S, G * DH), lambda b: (b, 0, 0)), pl.BlockSpec((1, S, G * DH), lambda b: (b, 0, 1)),
                  full(pe.shape), full(w1.shape), full((CMP_HIDDEN, LANES)), full((DH, CMP_HIDDEN))],
        out_specs=[pl.BlockSpec((1, G, nsub, LANES), lambda b: (b, 0, 0, 0)),
                   pl.BlockSpec((1, G, DH, nsub), lambda b: (b, 0, 0, 0))],
        compiler_params=pltpu.CompilerParams(dimension_semantics=("parallel",), vmem_limit_bytes=VMEM_LIMIT),
        name="compress",
    )(kv_cmp, kv_cmp, pe, w1, w2kp, w2vt)


def _bias_kernel(tbl_ref, bkn_ref, bkc_ref, near_ref, cmpb_ref):
    g = pl.program_id(0)
    for h in range(HPG):
        hd = g * HPG + h

        def lookup(bk):
            acc = jnp.full(bk.shape, NEG, F32)
            for b in range(REL_BUCKETS):
                acc = jnp.where(bk == b, tbl_ref[hd, b], acc)
            return acc

        vn = lookup(bkn_ref[...])
        near_ref[0, :, h * QT:(h + 1) * QT] = jnp.where(vn > 0.5 * NEG, (vn - tbl_ref[hd, REL_BUCKETS - 1]) * LOG2E, NEG)
        vc = lookup(bkc_ref[...])
        cmpb_ref[0, :, h * QT:(h + 1) * QT] = jnp.where(vc > 0.5 * NEG, vc * LOG2E, NEG)


def _nsa_bias_tables(rel_bias, seq):
    ql = np.arange(QT)
    ncmp = seq // CMP_STRIDE

    def buckets(rel):
        return jnp.asarray(np.where(rel >= 0, _t5_bucket_np(rel), -1).astype(np.int32))

    bkn = buckets(ql[None, :] + QT - np.arange(2 * QT)[:, None])
    y = np.arange(2 * ncmp)
    bkc = buckets(ql[None, :] - CMP_STRIDE * (y[:, None] - ncmp) - (CMP_BLOCK - 1))
    nql = HPG * QT
    return pl.pallas_call(
        _bias_kernel,
        out_shape=[jax.ShapeDtypeStruct((NSA_GROUPS, 2 * QT, nql), F32),
                   jax.ShapeDtypeStruct((NSA_GROUPS, 2 * ncmp, nql), F32)],
        grid=(NSA_GROUPS,),
        in_specs=[pl.BlockSpec(memory_space=pltpu.SMEM),
                  pl.BlockSpec((2 * QT, QT), lambda g: (0, 0)),
                  pl.BlockSpec((2 * ncmp, QT), lambda g: (0, 0))],
        out_specs=[pl.BlockSpec((1, 2 * QT, nql), lambda g: (g, 0, 0)),
                   pl.BlockSpec((1, 2 * ncmp, nql), lambda g: (g, 0, 0))],
        compiler_params=pltpu.CompilerParams(dimension_semantics=("parallel",)),
        name="t5bias",
    )(rel_bias.T, bkn, bkc)


def _nsa_kernel(q_ref, kslc_ref, vtslc_ref, kwin_ref, vtwin_ref, kcb_ref, vcbt_ref, misc_ref,
                near_ref, cmpb_ref, wmask_ref, ovt_ref, eye_ref, eye4_ref,
                out_ref, qaug_s, s0_s, s1_s, *, ncmp, nslc):
    G = NSA_GROUPS
    qts = [NSA_TILES * pl.program_id(1) + t for t in range(NSA_TILES)]
    chains = [(t, g) for t in range(NSA_TILES) for g in range(G)]
    nql = HPG * QT
    ck = 2 * QT

    def flash(carry, s, vt_chunk):
        m, acc = carry
        m_new = jnp.maximum(m, jnp.max(s, axis=0, keepdims=True))
        alpha = jnp.exp2(m - m_new)
        p = jnp.exp2((s - m_new).astype(BF16))
        return m_new, alpha * acc + _nn(vt_chunk, p)

    def finish(carry):
        m, acc = carry
        return acc[:DH] * (1.0 / acc[DH:DH + 1])

    init = (jnp.full((1, nql), NEG, F32), jnp.zeros((VT_ROWS, nql), F32))
    ns = [pl.multiple_of(QT * qt, QT) for qt in qts]
    nw = [pl.multiple_of(QT * qt + 3 * QT, QT) for qt in qts]
    off = [pl.multiple_of(ncmp - (QT // CMP_STRIDE) * qt, SUBLANES) for qt in qts]
    lane = lax.broadcasted_iota(jnp.int32, (nql, LANES), 1)
    win_aug = jnp.where(lane >= DH, -BIG, 0.0).astype(BF16)
    jidx = lax.broadcasted_iota(jnp.int32, (nslc, QT), 0)
    tq = [qt * QT + lax.broadcasted_iota(jnp.int32, (nslc, QT), 1) for qt in qts]
    forced = [(jidx == 0) | (jidx == tq[t] // SLC_BLOCK) | (jidx == tq[t] // SLC_BLOCK - 1) for t in range(NSA_TILES)]
    future = [jidx * SLC_BLOCK > tq[t] for t in range(NSA_TILES)]
    sub = lax.broadcasted_iota(jnp.int32, (SUBLANES, QT), 0)
    ones_lo = jnp.ones((DH, QT), F32)
    eye = eye_ref[...]
    ovt = ovt_ref[...]
    ngrp = nslc // SUBLANES

    def aug(sel01):
        rows = [ones_lo, sel01]
        if LANES - DH - nslc:
            rows.append(jnp.ones((LANES - DH - nslc, QT), F32))
        m01 = _nt(eye, jnp.concatenate(rows, axis=0).astype(BF16))
        return jnp.concatenate([((m01 - 1.0) * BIG).astype(BF16)] * HPG, axis=0)

    cs = range(len(chains))
    kl = [slice(g * LANES, (g + 1) * LANES) for _, g in chains]
    vr = [slice(g * VT_ROWS, (g + 1) * VT_ROWS) for _, g in chains]
    low = lax.broadcasted_iota(jnp.int32, (QT, LANES), 1) < DH

    def head_slot(t, hd):
        pair = q_ref[0, t * QT:(t + 1) * QT, (hd // 2) * LANES:(hd // 2 + 1) * LANES].astype(F32)
        if hd % 2:
            pair = pltpu.roll(pair, shift=DH, axis=1)
        return jnp.where(low, pair, 0.0).astype(BF16)

    q0 = [jnp.concatenate([head_slot(t, g * HPG + h) for h in range(HPG)], axis=0) for t, g in chains]
    qwin = [q0[c] + win_aug for c in cs]

    bc = [cmpb_ref[g, pl.ds(off[t], ncmp), :] for t, g in chains]
    sc = [_nt(kcb_ref[0, chains[c][1]], q0[c]) + bc[c] for c in cs]
    sw = [_nt(kwin_ref[0, pl.ds(ns[chains[c][0]], 3 * QT), kl[c]], qwin[c]) for c in cs]
    sw = [jnp.concatenate([sw[c][:QT] + wmask_ref[...], sw[c][QT:]], axis=0) for c in cs]

    pc = []
    for c in cs:
        mc = jnp.max(sc[c], axis=0, keepdims=True)
        ec = jnp.where(bc[c] > 0.5 * NEG, jnp.exp2(sc[c] - mc), 0.0)
        den = jnp.maximum(jnp.sum(ec, axis=0, keepdims=True), jnp.finfo(F32).tiny)
        pc.append(ec * (1.0 / den))
    wcar = [flash(init, sw[c], vtwin_ref[0, vr[c], pl.ds(ns[chains[c][0]], 3 * QT)]) for c in cs]
    o_c = [_nn(vcbt_ref[0, chains[c][1]], pc[c].astype(BF16)) for c in cs]

    imp = []
    for c in cs:
        t = chains[c][0]
        psum = pc[c][:, 0:QT]
        for h in range(1, HPG):
            psum = psum + pc[c][:, h * QT:(h + 1) * QT]
        p1, p2, p3 = _split3(psum)
        v = _nn(ovt, p1) + _nn(ovt, p2) + _nn(ovt, p3)
        imp.append(jnp.where(forced[t], 1e30, jnp.where(future[t], -1e30, v)))
    sw = [_nt(kwin_ref[0, pl.ds(nw[chains[c][0]], ck), kl[c]], qwin[c]) + near_ref[chains[c][1]]
          for c in cs]

    grp = [[imp[c][SUBLANES * v:SUBLANES * (v + 1)] for v in range(ngrp)] for c in cs]
    cnt = [[jnp.zeros((SUBLANES, QT), F32) for _ in range(ngrp)] for c in cs]
    for jp in range(nslc):
        v0, r0 = divmod(jp, SUBLANES)
        for c in cs:
            row = jnp.broadcast_to(imp[c][jp:jp + 1, :], (SUBLANES, QT))
            for v in range(ngrp):
                if v < v0:
                    inc = jnp.where(row > grp[c][v], 1.0, 0.0)
                elif v > v0:
                    inc = jnp.where(row >= grp[c][v], 1.0, 0.0)
                else:
                    inc = jnp.where(sub > r0, jnp.where(row >= grp[c][v], 1.0, 0.0),
                                    jnp.where(row > grp[c][v], 1.0, 0.0))
                cnt[c][v] = cnt[c][v] + inc
    o_w = [finish(flash(wcar[c], sw[c], vtwin_ref[0, vr[c], pl.ds(nw[chains[c][0]], ck)])) for c in cs]
    for c in cs:
        sel = jnp.concatenate(cnt[c], axis=0) < float(min(SLC_TOPN, nslc))
        sel_near = jnp.where(sel, 1.0, 0.0)
        sel_far = jnp.where(jidx < 2 * (qts[chains[c][0]] - 1), sel_near, 0.0)
        qaug_s[c, 0] = q0[c] + aug(sel_far)
        qaug_s[c, 1] = q0[c] + aug(sel_near)

    n_far = pl.program_id(1)
    n_chunks = n_far + 1

    def rows(c, i):
        return pl.multiple_of(QT * qts[chains[c][0]] - ck * i, QT)

    def scores(c, i):
        i = jnp.minimum(i, n_far)
        return _nt(kslc_ref[0, pl.ds(rows(c, i), ck), kl[c]], qaug_s[c, jnp.where(i == 0, 1, 0)])

    def vt(c, i):
        return vtslc_ref[0, vr[c], pl.ds(rows(c, i), ck)]

    for c in cs:
        s0_s[c] = scores(c, 0) + near_ref[chains[c][1]]

    def pair_body(p, carry):
        i = 2 * p
        carry = list(carry)
        for t in range(NSA_TILES):
            tc = [c for c in cs if chains[c][0] == t]
            for c in tc:
                s1_s[c] = scores(c, i + 1)
            for c in tc:
                carry[c] = flash(carry[c], s0_s[c], vt(c, i))
            for c in tc:
                s0_s[c] = scores(c, i + 2)
            for c in tc:
                carry[c] = flash(carry[c], s1_s[c], vt(c, i + 1))
        return tuple(carry)

    carry = lax.fori_loop(0, n_chunks // 2, pair_body, (init,) * len(chains))
    carry = lax.cond(n_chunks % 2 == 1,
                     lambda cr: tuple(flash(cr[c], s0_s[c], vt(c, n_far)) for c in cs),
                     lambda cr: cr, carry)

    gts = [jax.nn.sigmoid(misc_ref[0, t * QT:(t + 1) * QT, :]).T for t in range(NSA_TILES)]
    head = lax.broadcasted_iota(jnp.int32, (DH, nql), 1) // QT
    for c, (t, g) in enumerate(chains):
        o_s = finish(carry[c])

        def gate_row(br):
            r0 = 16 * g + br * HPG
            return jnp.concatenate([gts[t][r0 + h:r0 + h + 1, :] for h in range(HPG)], axis=1)

        o = gate_row(0) * o_c[c] + gate_row(1) * o_s + gate_row(2) * o_w[c]
        ob = o.astype(BF16)
        blocks = jnp.concatenate([jnp.where(head == h, ob, jnp.zeros_like(ob)) for h in range(HPG)], axis=0)
        out_ref[0, t * QT:(t + 1) * QT, g * HPG * DH:(g + 1) * HPG * DH] = _nt(eye4_ref[...], blocks).astype(BF16)


def _nsa(q, kslc, vtslc, kwin, vtwin, kcb, vcbt, misc, rel_bias):
    B, S, _ = q.shape
    G = NSA_GROUPS
    nq = S // QT
    ncmp = S // CMP_STRIDE
    nslc = S // SLC_BLOCK
    nql = HPG * QT
    near, cmpb = _nsa_bias_tables(rel_bias, S)
    wmask = jnp.asarray(np.tile(np.where(np.arange(QT)[:, None] > np.arange(QT)[None, :], 0.0, NEG), (1, HPG)), F32)
    kpad_s = jnp.concatenate([jnp.zeros((DH,), BF16), jnp.ones((DH,), BF16)] * G)
    kslc_p = jnp.concatenate([jnp.broadcast_to(kpad_s, (B, SLC_PAD, G * LANES)), kslc], axis=1)
    kwin_p = jnp.concatenate([jnp.broadcast_to(kpad_s, (B, WIN_PAD, G * LANES)), kwin], axis=1)
    vtslc_p = jnp.pad(vtslc, ((0, 0), (0, 0), (SLC_PAD, 0)))
    vtwin_p = jnp.pad(vtwin, ((0, 0), (0, 0), (WIN_PAD, 0)))
    ci = np.arange(ncmp)[None, :] * CMP_STRIDE
    sj = np.arange(nslc)[:, None] * SLC_BLOCK
    ovt = jnp.asarray(((ci < sj + SLC_BLOCK) & (ci + CMP_BLOCK > sj)).astype(np.float32), BF16)
    eye = jnp.eye(QT, dtype=BF16)
    eye4 = jnp.tile(eye, (1, HPG))
    kern = functools.partial(_nsa_kernel, ncmp=ncmp, nslc=nslc)
    nch = NSA_TILES * G
    per_b = lambda shp: pl.BlockSpec(shp, lambda b, i: (b,) + (0,) * (len(shp) - 1))
    full = lambda shp: pl.BlockSpec(shp, lambda b, i: (0,) * len(shp))
    return pl.pallas_call(
        kern,
        out_shape=jax.ShapeDtypeStruct((B, S, NSA_HEADS * DH), BF16),
        grid=(B, nq // NSA_TILES),
        in_specs=[
            pl.BlockSpec((1, NSA_TILES * QT, NSA_HEADS * DH), lambda b, i: (b, i, 0)),
            per_b((1, SLC_PAD + S, G * LANES)),
            per_b((1, G * VT_ROWS, SLC_PAD + S)),
            per_b((1, WIN_PAD + S, G * LANES)),
            per_b((1, G * VT_ROWS, WIN_PAD + S)),
            per_b((1, G, ncmp, LANES)),
            per_b((1, G, DH, ncmp)),
            pl.BlockSpec((1, NSA_TILES * QT, LANES), lambda b, i: (b, i, 0)),
            full((G, 2 * QT, nql)),
            full((G, 2 * ncmp, nql)),
            full((QT, nql)),
            full((nslc, ncmp)),
            full((QT, QT)),
            full((QT, nql)),
        ],
        out_specs=pl.BlockSpec((1, NSA_TILES * QT, NSA_HEADS * DH), lambda b, i: (b, i, 0)),
        scratch_shapes=[pltpu.VMEM((nch, 2, nql, LANES), BF16), pltpu.VMEM((nch, 2 * QT, nql), F32),
                        pltpu.VMEM((nch, 2 * QT, nql), F32)],
        compiler_params=pltpu.CompilerParams(dimension_semantics=("parallel", "arbitrary"),
                                             vmem_limit_bytes=VMEM_LIMIT),
        name="nsa",
    )(q, kslc_p, vtslc_p, kwin_p, vtwin_p, kcb, vcbt, misc, near, cmpb, wmask, ovt, eye, eye4)


GLA_NB = 2


def _gla_kernel(qk_ref, v_ref, vt_ref, misc_ref, r_ref, wal_ref, bal_ref, ng_ref, cum_ref, out_ref,
                state_s, o_s, *, ct, nbatch):
    H, dk, dv, C = GLA_HEADS, GLA_DK, GLA_DV, GLA_CHUNK
    kw = H * dk
    nb = range(nbatch)

    @pl.when(pl.program_id(1) == 0)
    def _():
        state_s[...] = jnp.zeros_like(state_s)

    cum = cum_ref[...]
    q_in, k_in, k_st, decay = [], [], [], []
    for bb in nb:
        z = _nn(misc_ref[bb].astype(BF16), wal_ref[...]) + bal_ref[...]
        log_a = (jnp.minimum(z, 0.0) - jnp.log1p(jnp.exp(-jnp.abs(z)))) * (1.0 / GLA_TAU)
        a1, a2, a3 = _split3(log_a)
        cs = _nn(cum, a1) + _nn(cum, a2) + _nn(cum, a3)
        bc, bl = cs[:ct], cs[ct:]
        q = qk_ref[bb, :, :kw]
        k = qk_ref[bb, :, kw:]
        q_in.append((q * (dk ** -0.5)) * jnp.exp(bc))
        k_in.append((k * jnp.exp(-bc)).astype(BF16))
        k_st.append(k * jnp.exp(bl - bc))
        decay.append(jnp.exp(bl))
    lane_head = lax.broadcasted_iota(jnp.int32, (C, kw), 1) // dk
    rr = lax.broadcasted_iota(jnp.int32, (H * C, C), 0) % C
    cc = lax.broadcasted_iota(jnp.int32, (H * C, C), 1)
    causal = rr >= cc
    pair_row = lax.broadcasted_iota(jnp.int32, (2 * C, kw), 0) // C
    pair_head = lax.broadcasted_iota(jnp.int32, (2 * C, kw), 1) // dk

    for c in range(ct // C):
        r0 = c * C
        p0 = (c // 2) * 2 * C
        for bb in nb:
            qc = q_in[bb][r0:r0 + C]
            qcb = qc.astype(BF16)
            q_heads = jnp.concatenate([jnp.where(lane_head == h, qc, 0.0) for h in range(H)], axis=0).astype(BF16)
            attn = jnp.where(causal, _nt(q_heads, k_in[bb][r0:r0 + C]), 0.0).astype(BF16)
            kst_pair = k_st[bb][p0:p0 + 2 * C]
            dec = decay[bb][r0:r0 + 1]
            for h in range(H):
                st = state_s[bb, h]
                o = _nn(attn[h * C:(h + 1) * C], v_ref[bb, r0:r0 + C, h * dv:(h + 1) * dv])
                o = o + _nt(qcb, st.astype(BF16))
                o_s[bb, r0:r0 + C, h * dv:(h + 1) * dv] = o
                kst_h = jnp.where((pair_row == c % 2) & (pair_head == h), kst_pair, 0.0).astype(BF16)
                state_s[bb, h] = st * dec + _nn(vt_ref[bb, h * dv:(h + 1) * dv, p0:p0 + 2 * C], kst_h)

    for bb in nb:
        for h in range(H):
            oh = o_s[bb, :, h * dv:(h + 1) * dv]
            ms = jnp.mean(oh * oh, axis=-1, keepdims=True)
            r = r_ref[bb, :, h * dv:(h + 1) * dv]
            y = oh * lax.rsqrt(ms + EPS) * ng_ref[:, h * dv:(h + 1) * dv] * (r * jax.nn.sigmoid(r))
            out_ref[bb, :, h * dv:(h + 1) * dv] = y.astype(BF16)


def _gla(qkb, vb, vtb, misc, rb, w_alpha, b_alpha, norm_g, ct=512):
    B, S, _ = qkb.shape
    H, dk, dv, C = GLA_HEADS, GLA_DK, GLA_DV, GLA_CHUNK
    kw, vw = H * dk, H * dv
    nb = GLA_NB if B % GLA_NB == 0 else 1
    wal = jnp.zeros((LANES, kw), F32).at[32:32 + GLA_RANK].set(w_alpha).astype(BF16)
    r = np.arange(ct)
    tri = (r[:, None] // C == r[None, :] // C) & (r[:, None] >= r[None, :])
    tot = r[:, None] // C == r[None, :] // C
    cum = jnp.asarray(np.concatenate([tri, tot], axis=0).astype(np.float32), BF16)
    full = lambda shp: pl.BlockSpec(shp, lambda b, i: (0,) * len(shp))
    return pl.pallas_call(
        functools.partial(_gla_kernel, ct=ct, nbatch=nb),
        out_shape=jax.ShapeDtypeStruct((B, S, vw), BF16),
        grid=(B // nb, S // ct),
        in_specs=[
            pl.BlockSpec((nb, ct, 2 * kw), lambda b, i: (b, i, 0)),
            pl.BlockSpec((nb, ct, vw), lambda b, i: (b, i, 0)),
            pl.BlockSpec((nb, vw, ct), lambda b, i: (b, 0, i)),
            pl.BlockSpec((nb, ct, LANES), lambda b, i: (b, i, 0)),
            pl.BlockSpec((nb, ct, vw), lambda b, i: (b, i, 0)),
            full((LANES, kw)), full((1, kw)), full((1, vw)), full((2 * ct, ct)),
        ],
        out_specs=pl.BlockSpec((nb, ct, vw), lambda b, i: (b, i, 0)),
        scratch_shapes=[pltpu.VMEM((nb, H, dv, kw), F32), pltpu.VMEM((nb, ct, vw), F32)],
        compiler_params=pltpu.CompilerParams(dimension_semantics=("parallel", "arbitrary"),
                                             vmem_limit_bytes=VMEM_LIMIT),
        name="gla",
    )(qkb, vb, vtb, misc, rb, wal, b_alpha.reshape(1, kw), norm_g.reshape(1, vw), cum)


ROUTER_ROWS = 128
EXPERT_ROW0 = 8


def _local_rows(tm):
    return -(-(2 * tm + N_EXPERTS * (SUBLANES - 1)) // LANES) * LANES


def _pack_bf16_pairs(x):
    u = pltpu.bitcast(x, jnp.uint32)
    w = x.shape[1] // 2
    return u[:, :w] | (u[:, w:] >> 16)


def _unpack_bf16_pairs(w):
    xh = pltpu.bitcast(w & jnp.uint32(0xFFFF0000), F32)
    xl = pltpu.bitcast(w << 16, F32)
    return jnp.concatenate([xh, xl], axis=1).astype(BF16)


def _outproj_kernel(ya_ref, yb_ref, mg_ref, x_ref, wa_ref, wb_ref, wo_ref, gf_ref, wrh_ref, wrl_ref, br_ref,
                    tri_ref, ltri_ref, x1_ref, xsl_ref, seg_ref, rw_ref, cnt_ref, carry_s, *, tm, ls):
    D = x_ref.shape[1]

    @pl.when(pl.program_id(0) == 0)
    def _():
        carry_s[...] = jnp.zeros_like(carry_s)

    ma = _nn(ya_ref[...], wa_ref[...])
    mb = _nn(yb_ref[...], wb_ref[...])
    merged = jax.nn.sigmoid(mg_ref[:, :D]) * ma + jax.nn.sigmoid(mg_ref[:, D:]) * mb
    x1 = x_ref[...] + _nn(merged.astype(BF16), wo_ref[...])
    x1_ref[...] = x1
    ms = jnp.mean(x1 * x1, axis=-1, keepdims=True)
    h2 = x1 * lax.rsqrt(ms + EPS) * gf_ref[...]
    hi = h2.astype(BF16)
    lo = (h2 - hi.astype(F32)).astype(BF16)
    lg = _nt(wrh_ref[...], hi) + _nt(wrh_ref[...], lo) + _nt(wrl_ref[...], hi) + br_ref[...]
    row8 = lax.broadcasted_iota(jnp.int32, (SUBLANES, tm), 0)
    gl = jnp.where(row8 < N_GROUPS, lg[0:SUBLANES], NEG)
    gmax = jnp.max(gl, axis=0, keepdims=True)
    g_sel = jnp.min(jnp.where(gl == gmax, row8, SUBLANES), axis=0, keepdims=True)
    g_prob = 1.0 / jnp.sum(jnp.where(row8 < N_GROUPS, jnp.exp(gl - gmax), 0.0), axis=0, keepdims=True)
    e_sel = jnp.zeros((EPG, tm), F32)
    for gi in range(N_GROUPS):
        r0 = EXPERT_ROW0 + gi * EPG
        e_sel = e_sel + jnp.where(g_sel == gi, lg[r0:r0 + EPG], 0.0)
    v1 = jnp.max(e_sel, axis=0, keepdims=True)
    i1 = jnp.min(jnp.where(e_sel == v1, row8, EPG), axis=0, keepdims=True)
    rest = jnp.where(row8 == i1, -jnp.inf, e_sel)
    v2 = jnp.max(rest, axis=0, keepdims=True)
    i2 = jnp.min(jnp.where(rest == v2, row8, EPG), axis=0, keepdims=True)
    t = jnp.exp(v2 - v1)
    w1 = g_prob / (1.0 + t)
    w2 = g_prob * t / (1.0 + t)
    e1 = g_sel * EPG + i1
    e2 = g_sel * EPG + i2

    rowe = lax.broadcasted_iota(jnp.int32, (N_EXPERTS, tm), 0)
    oh1 = rowe == e1
    oh2 = rowe == e2
    oh = jnp.where(oh1, 1.0, 0.0) + jnp.where(oh2, 1.0, 0.0)
    pre = _nn(oh.astype(BF16), tri_ref[...])
    cnt = jnp.sum(oh, axis=1, keepdims=True)
    cnt8 = jnp.floor((cnt + (SUBLANES - 1)) * (1.0 / SUBLANES)) * SUBLANES
    c_b = jnp.broadcast_to(cnt8, (N_EXPERTS, LANES))
    c_hi = jnp.floor(c_b * (1.0 / 16.0))
    c_lo = c_b - 16.0 * c_hi
    base = 16.0 * _nn(ltri_ref[...], c_hi.astype(BF16)) + _nn(ltri_ref[...], c_lo.astype(BF16))
    loc = pre + base[:, 0:1]
    lpos1 = jnp.sum(jnp.where(oh1, loc, 0.0), axis=0, keepdims=True)
    lpos2 = jnp.sum(jnp.where(oh2, loc, 0.0), axis=0, keepdims=True)
    lane = lax.broadcasted_iota(jnp.int32, (N_EXPERTS, LANES), 1)
    seg_ref[0] = jnp.where(lane == 0, base, jnp.where(lane == 1, c_b, carry_s[...])).astype(jnp.int32)
    carry_s[...] = carry_s[...] + c_b
    cnt_ref[...] = carry_s[...].astype(jnp.int32)
    rw_ref[0] = jnp.concatenate([w1, w2, lpos1, lpos2, jnp.zeros((SUBLANES - 4, tm), F32)], axis=0)

    srow = lax.broadcasted_iota(jnp.int32, (ls, tm), 0)
    perm = jnp.where(srow == lpos1.astype(jnp.int32), 1.0, jnp.where(srow == lpos2.astype(jnp.int32), 1.0, 0.0))
    xsorted = _nn(perm.astype(BF16), hi)
    xsl_ref[0] = _pack_bf16_pairs(xsorted)


def _outproj(ya, yb, mg, x, wa, wb, wo, g_ffn, w_rg, b_rg, w_re, b_re, tm=MOE_TM):
    T, D = x.shape
    nt = T // tm
    wr = jnp.zeros((ROUTER_ROWS, D), F32).at[0:N_GROUPS].set(w_rg.T).at[EXPERT_ROW0:EXPERT_ROW0 + N_EXPERTS].set(w_re.T)
    wrh = wr.astype(BF16)
    wrl = (wr - wrh.astype(F32)).astype(BF16)
    br = jnp.zeros((ROUTER_ROWS, 1), F32).at[0:N_GROUPS, 0].set(b_rg).at[EXPERT_ROW0:EXPERT_ROW0 + N_EXPERTS, 0].set(b_re)
    r = np.arange(tm)
    tri = jnp.asarray((r[:, None] < r[None, :]).astype(np.float32), BF16)
    re = np.arange(N_EXPERTS)
    ltri = jnp.asarray((re[None, :] < re[:, None]).astype(np.float32), BF16)
    ls = _local_rows(tm)
    row = lambda n: pl.BlockSpec((tm, n), lambda i: (i, 0))
    full = lambda shp: pl.BlockSpec(shp, lambda i: (0,) * len(shp))
    return pl.pallas_call(
        functools.partial(_outproj_kernel, tm=tm, ls=ls),
        out_shape=[jax.ShapeDtypeStruct((T, D), F32), jax.ShapeDtypeStruct((nt, ls, D // 2), jnp.uint32),
                   jax.ShapeDtypeStruct((nt, N_EXPERTS, LANES), jnp.int32),
                   jax.ShapeDtypeStruct((nt, SUBLANES, tm), F32),
                   jax.ShapeDtypeStruct((N_EXPERTS, LANES), jnp.int32)],
        grid=(nt,),
        in_specs=[row(ya.shape[1]), row(yb.shape[1]), row(2 * D), row(D),
                  full(wa.shape), full(wb.shape), full(wo.shape), full((1, D)),
                  full((ROUTER_ROWS, D)), full((ROUTER_ROWS, D)), full((ROUTER_ROWS, 1)), full((tm, tm)),
                  full((N_EXPERTS, N_EXPERTS))],
        out_specs=[row(D), pl.BlockSpec((1, ls, D // 2), lambda i: (i, 0, 0)),
                   pl.BlockSpec((1, N_EXPERTS, LANES), lambda i: (i, 0, 0)),
                   pl.BlockSpec((1, SUBLANES, tm), lambda i: (i, 0, 0)),
                   full((N_EXPERTS, LANES))],
        scratch_shapes=[pltpu.VMEM((N_EXPERTS, LANES), F32)],
        compiler_params=pltpu.CompilerParams(dimension_semantics=("arbitrary",), vmem_limit_bytes=VMEM_LIMIT),
        name="outproj",
    )(ya, yb, mg, x, wa, wb, wo, g_ffn.reshape(1, D), wrh, wrl, br, tri, ltri)


SEG_FIELDS = 4
SEG_BITS = 7
TAIL_BITS = (TE // SUBLANES).bit_length() - 1


def _segment_copies(n8, bits, make_copy, wait):
    off = 0
    for bit in reversed(range(bits)):
        rows = SUBLANES << bit
        take = (n8 >> bit) & 1

        @pl.when(take == 1)
        def _(off=off, rows=rows):
            cp = make_copy(off, rows)
            if wait:
                cp.wait()
            else:
                cp.start()

        off = off + take * rows


def _dispatch_kernel(seg_ref, tail_ref, xsl_ref, xs_ref, zero_s, sem):
    i = pl.program_id(0)

    def segments(wait):
        def body(e, c):
            s0 = (i * N_EXPERTS + e) * SEG_FIELDS
            src0 = pl.multiple_of(seg_ref[s0], SUBLANES)
            dst0 = pl.multiple_of(seg_ref[s0 + 2], SUBLANES)

            def make_copy(off, rows):
                return pltpu.make_async_copy(xsl_ref.at[0, pl.ds(pl.multiple_of(src0 + off, SUBLANES), rows), :],
                                             xs_ref.at[pl.ds(pl.multiple_of(dst0 + off, SUBLANES), rows), :], sem)

            _segment_copies(seg_ref[s0 + 1], SEG_BITS, make_copy, wait)
            return c

        lax.fori_loop(0, N_EXPERTS, body, 0)

    def tails(wait):
        def body(e, c):
            dst0 = pl.multiple_of(tail_ref[2 * e], SUBLANES)

            def make_copy(off, rows):
                return pltpu.make_async_copy(zero_s.at[pl.ds(0, rows), :],
                                             xs_ref.at[pl.ds(pl.multiple_of(dst0 + off, SUBLANES), rows), :], sem)

            _segment_copies(tail_ref[2 * e + 1], TAIL_BITS, make_copy, wait)
            return c

        lax.fori_loop(0, N_EXPERTS, body, 0)

    def unused_tiles(wait):
        def body(t, c):
            cp = pltpu.make_async_copy(zero_s, xs_ref.at[pl.ds(pl.multiple_of(t * TE, TE), TE), :], sem)
            if wait:
                cp.wait()
            else:
                cp.start()
            return c

        lax.fori_loop(tail_ref[2 * N_EXPERTS], xs_ref.shape[0] // TE, body, 0)

    @pl.when(i == 0)
    def _():
        zero_s[...] = jnp.zeros_like(zero_s)
        tails(False)
        unused_tiles(False)
        tails(True)
        unused_tiles(True)

    segments(False)
    segments(True)


def _dispatch(seg, tail, xsl, n_rows):
    nt, ls, W = xsl.shape
    return pl.pallas_call(
        _dispatch_kernel,
        out_shape=jax.ShapeDtypeStruct((n_rows, W), xsl.dtype),
        grid_spec=pltpu.PrefetchScalarGridSpec(
            num_scalar_prefetch=2,
            grid=(nt,),
            in_specs=[pl.BlockSpec((1, ls, W), lambda i, sg, tl: (i, 0, 0))],
            out_specs=pl.BlockSpec(memory_space=pl.ANY),
            scratch_shapes=[pltpu.VMEM((TE, W), xsl.dtype), pltpu.SemaphoreType.DMA(())],
        ),
        compiler_params=pltpu.CompilerParams(dimension_semantics=("arbitrary",)),
        name="dispatch",
    )(seg, tail, xsl)


def _experts_kernel(te_ref, nv_ref, xs_ref, wg_ref, wu_ref, wd_ref, out_ref):
    i = pl.program_id(0)

    @pl.when(i < nv_ref[0])
    def _():
        x = _unpack_bf16_pairs(xs_ref[...])
        a = _nn(x, wg_ref[0].astype(BF16))
        u = _nn(x, wu_ref[0].astype(BF16))
        hid = (a * jax.nn.sigmoid(a)) * u
        y = _nn(hid.astype(BF16), wd_ref[0].astype(BF16))
        out_ref[...] = _pack_bf16_pairs(y.astype(BF16).astype(F32))

    @pl.when(i >= nv_ref[0])
    def _():
        out_ref[...] = jnp.zeros_like(out_ref)


def _experts(tile_expert, n_valid, xs, w_gate, w_up, w_down):
    n_rows, W = xs.shape
    D = 2 * W
    n_tiles = n_rows // TE
    last = lambda i, nv: jnp.minimum(i, nv[0] - 1)
    return pl.pallas_call(
        _experts_kernel,
        out_shape=jax.ShapeDtypeStruct((n_rows, W), jnp.uint32),
        grid_spec=pltpu.PrefetchScalarGridSpec(
            num_scalar_prefetch=2,
            grid=(n_tiles,),
            in_specs=[pl.BlockSpec((TE, W), lambda i, te, nv: (last(i, nv), 0)),
                      pl.BlockSpec((1, D, EXPERT_FF), lambda i, te, nv: (te[last(i, nv)], 0, 0)),
                      pl.BlockSpec((1, D, EXPERT_FF), lambda i, te, nv: (te[last(i, nv)], 0, 0)),
                      pl.BlockSpec((1, EXPERT_FF, D), lambda i, te, nv: (te[last(i, nv)], 0, 0))],
            out_specs=pl.BlockSpec((TE, W), lambda i, te, nv: (i, 0)),
        ),
        compiler_params=pltpu.CompilerParams(dimension_semantics=("arbitrary",), vmem_limit_bytes=VMEM_LIMIT),
        name="experts",
    )(tile_expert, n_valid, xs, w_gate, w_up, w_down)


def _combine_kernel(seg_ref, ys_ref, x1_ref, rw_ref, gfin_ref, out_ref, buf, sem, *, tm, ls, apply_norm):
    i = pl.program_id(0)
    nt = pl.num_programs(0)

    def segments(tile, wait):
        slot = tile % 2

        def body(e, c):
            s0 = (tile * N_EXPERTS + e) * SEG_FIELDS
            loc0 = pl.multiple_of(seg_ref[s0], SUBLANES)
            glob0 = pl.multiple_of(seg_ref[s0 + 2], SUBLANES)

            def make_copy(off, rows):
                return pltpu.make_async_copy(ys_ref.at[pl.ds(pl.multiple_of(glob0 + off, SUBLANES), rows), :],
                                             buf.at[slot, pl.ds(pl.multiple_of(loc0 + off, SUBLANES), rows), :],
                                             sem.at[slot])

            _segment_copies(seg_ref[s0 + 1], SEG_BITS, make_copy, wait)
            return c

        lax.fori_loop(0, N_EXPERTS, body, 0)

    @pl.when(i == 0)
    def _():
        buf[...] = jnp.zeros_like(buf)
        segments(i, False)

    @pl.when(i + 1 < nt)
    def _():
        segments(i + 1, False)

    segments(i, True)
    ysl = _unpack_bf16_pairs(buf[i % 2])
    cols = jnp.concatenate([rw_ref[0], jnp.zeros((LANES - SUBLANES, tm), F32)], axis=0).T
    srow = lax.broadcasted_iota(jnp.int32, (tm, ls), 1)
    y = x1_ref[...]
    for k in range(2):
        pick = jnp.where(srow == cols[:, 2 + k:3 + k].astype(jnp.int32), 1.0, 0.0).astype(BF16)
        y = y + cols[:, k:k + 1] * _nn(pick, ysl)
    if apply_norm:
        ms = jnp.mean(y * y, axis=-1, keepdims=True)
        y = y * lax.rsqrt(ms + EPS) * gfin_ref[...]
    out_ref[...] = y


def _combine(seg, ys, x1, rw, g_final, apply_norm, tm=MOE_TM):
    T, D = x1.shape
    nt = T // tm
    ls = _local_rows(tm)
    return pl.pallas_call(
        functools.partial(_combine_kernel, tm=tm, ls=ls, apply_norm=apply_norm),
        out_shape=jax.ShapeDtypeStruct((T, D), F32),
        grid_spec=pltpu.PrefetchScalarGridSpec(
            num_scalar_prefetch=1,
            grid=(nt,),
            in_specs=[pl.BlockSpec(memory_space=pl.ANY),
                      pl.BlockSpec((tm, D), lambda i, sg: (i, 0)),
                      pl.BlockSpec((1, SUBLANES, tm), lambda i, sg: (i, 0, 0)),
                      pl.BlockSpec((1, D), lambda i, sg: (0, 0))],
            out_specs=pl.BlockSpec((tm, D), lambda i, sg: (i, 0)),
            scratch_shapes=[pltpu.VMEM((2, ls, D // 2), jnp.uint32), pltpu.SemaphoreType.DMA((2,))],
        ),
        compiler_params=pltpu.CompilerParams(dimension_semantics=("arbitrary",), vmem_limit_bytes=VMEM_LIMIT),
        name="combine",
    )(seg, ys, x1, rw, g_final.reshape(1, D))


def _moe_plan(seg, counts, T, tm=MOE_TM):
    nt = T // tm
    n_tiles_max = (2 * T + nt * N_EXPERTS * (SUBLANES - 1)) // TE + N_EXPERTS
    total = counts[:, 0]
    tiles = (total + TE - 1) // TE
    ids = jnp.arange(N_EXPERTS)
    tile_end = jnp.sum(jnp.where(ids[None, :] <= ids[:, None], tiles[None, :], 0), axis=1)
    row0 = (tile_end - tiles) * TE
    segtab = jnp.stack([seg[:, :, 0], seg[:, :, 1] // SUBLANES, seg[:, :, 2] + row0[None, :],
                        jnp.zeros_like(seg[:, :, 0])], axis=-1).reshape(-1).astype(jnp.int32)
    tail = jnp.concatenate([jnp.stack([row0 + total, (tiles * TE - total) // SUBLANES], axis=-1).reshape(-1),
                            tile_end[-1:]]).astype(jnp.int32)
    tile_expert = jnp.minimum(jnp.sum(tile_end[None, :] <= jnp.arange(n_tiles_max)[:, None], axis=1),
                              N_EXPERTS - 1).astype(jnp.int32)
    return segtab, tail, tile_expert, tile_end[-1:].astype(jnp.int32), n_tiles_max * TE


def kernel(x, g_mix, w_in, nsa_pe_k, nsa_cmp_k_w1, nsa_cmp_k_w2, nsa_pe_v, nsa_cmp_v_w1, nsa_cmp_v_w2, rel_bias,
           gla_w_alpha, gla_b_alpha, gla_norm_g, w_branch_a, w_branch_b, w_out, g_ffn, w_router_group,
           b_router_group, w_router_expert, b_router_expert, w_exp_gate, w_exp_up, w_exp_down, g_final):
    B, S, D = x.shape
    T = B * S
    for l in range(w_in.shape[0]):
        w = _pack_inproj_weights(w_in[l])
        (q, kslc, kwin, kvc, misc, qkb, vb, rb, mg, vts, vtw, vtb) = _inproj(x, g_mix[l], w)
        kcb, vcbt = _compress(kvc, nsa_pe_k[l], nsa_cmp_k_w1[l], nsa_cmp_k_w2[l],
                              nsa_pe_v[l], nsa_cmp_v_w1[l], nsa_cmp_v_w2[l])
        ya = _nsa(q, kslc, vts, kwin, vtw, kcb, vcbt, misc, rel_bias)
        yb = _gla(qkb, vb, vtb, misc, rb, gla_w_alpha[l], gla_b_alpha[l], gla_norm_g[l])
        x1, xsl, seg, rw, counts = _outproj(
            ya.reshape(T, -1), yb.reshape(T, -1), mg.reshape(T, -1), x.reshape(T, D),
            w_branch_a[l].astype(BF16), w_branch_b[l].astype(BF16), w_out[l].astype(BF16), g_ffn[l],
            w_router_group[l], b_router_group[l], w_router_expert[l], b_router_expert[l])
        segtab, tail, tile_expert, n_valid, n_rows = _moe_plan(seg, counts, T)
        xs = _dispatch(segtab, tail, xsl, n_rows)
        ys = _experts(tile_expert, n_valid, xs, w_exp_gate[l], w_exp_up[l], w_exp_down[l])
        last_layer = l == w_in.shape[0] - 1
        x = _combine(segtab, ys, x1, rw, g_final, apply_norm=last_layer).reshape(B, S, D)
    return x
```

```python
import functools
import math

import numpy as np
import jax
import jax.numpy as jnp
from jax import lax
from jax.experimental import pallas as pl
from jax.experimental.pallas import tpu as pltpu

F32 = jnp.float32
BF16 = jnp.bfloat16

NSA_HEADS = 8
NSA_GROUPS = 2
HPG = NSA_HEADS // NSA_GROUPS
DH = 64
CMP_BLOCK = 32
CMP_STRIDE = 16
CMP_HIDDEN = 128
SLC_BLOCK = 64
SLC_TOPN = 16
WINDOW = 512
GLA_HEADS = 4
GLA_DK = 64
GLA_DV = 128
GLA_RANK = 16
GLA_TAU = 16.0
GLA_CHUNK = 64
REL_BUCKETS = 32
REL_MAX_EXACT = REL_BUCKETS // 2
REL_MAX_DIST = 128
N_GROUPS = 4
EPG = 8
N_EXPERTS = N_GROUPS * EPG
EXPERT_FF = 256
EPS = 1e-6

LANES = 128
SUBLANES = 8
VMEM_LIMIT = 56 * 1024 * 1024

LOG2E = math.log2(math.e)
NEG = -1e30
BIG = float(2.0 ** 100)
QT = 128
SLC_PAD = 128
WIN_PAD = 512
TE = 512
MOE_TM = 512
VT_ROWS = DH + 16
NSA_TILES = 2


def _nt(a, b):
    return lax.dot_general(a, b, (((1,), (1,)), ((), ())), preferred_element_type=F32)


def _nn(a, b):
    return jnp.dot(a, b, preferred_element_type=F32)


def _split3(x):
    a = x.astype(BF16)
    r = x - a.astype(F32)
    b = r.astype(BF16)
    c = (r - b.astype(F32)).astype(BF16)
    return a, b, c


def _t5_bucket_np(rel):
    n = np.maximum(rel, 0)
    nf = np.maximum(n, 1).astype(np.float32)
    large = REL_MAX_EXACT + (np.log(nf / np.float32(REL_MAX_EXACT)) / np.float32(math.log(REL_MAX_DIST / REL_MAX_EXACT))
                             * np.float32(REL_BUCKETS - REL_MAX_EXACT)).astype(np.int32)
    return np.where(n < REL_MAX_EXACT, n, np.minimum(large, REL_BUCKETS - 1)).astype(np.int32)


def _inproj_kernel(x_ref, g_ref, w_ref, kplace_ref, *refs, tm, seq, ncast):
    cast_in, refs = refs[:ncast], refs[ncast:]
    (oq, okslc, okwin, okv, omisc, oqkb, ovb, orb, omg, ovts, ovtw, ovtb), cast_out = refs[:12], refs[12:]
    for src, dst in zip(cast_in, cast_out):
        dst[...] = src[...].astype(BF16)
    x = x_ref[0]
    ms = jnp.mean(x * x, axis=-1, keepdims=True)
    h = (x * lax.rsqrt(ms + EPS) * g_ref[...]).astype(BF16)

    def mm(a, b):
        return _nn(h, w_ref[:, a:b])

    oq[0] = mm(0, 512).astype(BF16)
    kv = mm(512, 1280)
    okv[0] = kv[:, 0:256]

    def spread(k):
        return _nn(k.astype(BF16), kplace_ref[...])

    row = lax.broadcasted_iota(jnp.int32, (tm, 256), 0) + pl.program_id(1) * tm
    lane = lax.broadcasted_iota(jnp.int32, (tm, 256), 1) % LANES
    onehot = jnp.where(lane - DH == row // SLC_BLOCK, 1.0, 0.0)
    okslc[0] = (spread(kv[:, 256:384]) + onehot).astype(BF16)
    okwin[0] = spread(kv[:, 512:640]).astype(BF16)

    ones_rows = jnp.where(lax.broadcasted_iota(jnp.int32, (VT_ROWS - DH, tm), 0) == 0, 1.0, 0.0)

    def vt_groups(v):
        t = v.T
        return jnp.concatenate([t[:DH], ones_rows, t[DH:], ones_rows], axis=0).astype(BF16)

    ovts[0] = vt_groups(kv[:, 384:512])
    ovtw[0] = vt_groups(kv[:, 640:768])
    omisc[0] = mm(1280, 1408)
    qkv = mm(1408, 2432)
    oqkb[0] = qkv[:, 0:512]
    ovb[0] = qkv[:, 512:1024].astype(BF16)
    ovtb[0] = qkv[:, 512:1024].T.astype(BF16)
    orb[0] = mm(2432, 2944)
    omg[0] = mm(2944, 4992)


def _pack_inproj_weights(w_in):
    o = np.cumsum([0, 512, 128, 128, 128, 128, 128, 128, 24, 256, 256, 512, 16, 512, 1024, 1024])
    (q_a, k_cmp, v_cmp, k_slc, v_slc, k_win, v_win, gate_a, q_b, k_b, v_b, a_b, r_b, mg_a, mg_b) = [
        w_in[:, o[i]:o[i + 1]] for i in range(15)]
    D = w_in.shape[0]
    ga = gate_a.reshape(D, 3, NSA_GROUPS, HPG)
    z4 = jnp.zeros((D, 4), w_in.dtype)
    misc = [ga[:, :, 0, :].reshape(D, 12), z4, ga[:, :, 1, :].reshape(D, 12), z4, a_b,
            jnp.zeros((D, LANES - 48), w_in.dtype)]
    w = jnp.concatenate([q_a * (DH ** -0.5 * LOG2E), k_cmp, v_cmp, k_slc, v_slc, k_win, v_win] + misc
                        + [q_b, k_b, v_b, r_b, mg_a, mg_b], axis=1)
    return w.astype(BF16)


def _inproj(x, g_mix, w, to_bf16, tm=512):
    B, S, D = x.shape
    nw = w.shape[1]
    nsteps = B * (S // tm)
    cast_specs = [pl.BlockSpec((a.shape[0] // nsteps,) + a.shape[1:],
                               lambda b, i, nd=a.ndim: (b * (S // tm) + i,) + (0,) * (nd - 1)) for a in to_bf16]
    src = np.arange(NSA_GROUPS * DH)
    kplace = np.zeros((NSA_GROUPS * DH, NSA_GROUPS * LANES), np.float32)
    kplace[src, (src // DH) * LANES + src % DH] = 1.0
    kplace = jnp.asarray(kplace, BF16)
    widths = [(512, BF16), (256, BF16), (256, BF16), (256, F32), (128, F32), (512, F32), (512, BF16),
              (512, F32), (2048, F32)]
    out_shape = [jax.ShapeDtypeStruct((B, S, n), dt) for n, dt in widths]
    out_specs = [pl.BlockSpec((1, tm, n), lambda b, i: (b, i, 0)) for n, _ in widths]
    for rows in (NSA_GROUPS * VT_ROWS, NSA_GROUPS * VT_ROWS, 512):
        out_shape.append(jax.ShapeDtypeStruct((B, rows, S), BF16))
        out_specs.append(pl.BlockSpec((1, rows, tm), lambda b, i: (b, 0, i)))
    return pl.pallas_call(
        functools.partial(_inproj_kernel, tm=tm, seq=S, ncast=len(to_bf16)),
        out_shape=out_shape + [jax.ShapeDtypeStruct(a.shape, BF16) for a in to_bf16],
        grid=(B, S // tm),
        in_specs=[
            pl.BlockSpec((1, tm, D), lambda b, i: (b, i, 0)),
            pl.BlockSpec((1, D), lambda b, i: (0, 0)),
            pl.BlockSpec((D, nw), lambda b, i: (0, 0), pipeline_mode=pl.Buffered(1)),
            pl.BlockSpec(kplace.shape, lambda b, i: (0, 0)),
        ] + cast_specs,
        out_specs=out_specs + cast_specs,
        compiler_params=pltpu.CompilerParams(dimension_semantics=("parallel", "parallel"),
                                             vmem_limit_bytes=VMEM_LIMIT),
        name="inproj",
    )(x, g_mix.reshape(1, D), w, kplace, *to_bf16)


def _gelu_tanh(x):
    return 0.5 * x * (1.0 + jnp.tanh(math.sqrt(2.0 / math.pi) * (x + 0.044715 * (x * x * x))))


def _compress_kernel(xk_ref, xv_ref, pe_ref, w1_ref, w2k_ref, w2vt_ref, ok_ref, ovt_ref, *, nsub):
    for kind, x_ref in enumerate((xk_ref, xv_ref)):
        top = jnp.zeros((nsub, 2 * CMP_HIDDEN), F32)
        bot = jnp.zeros((nsub, 2 * CMP_HIDDEN), F32)
        for r in range(CMP_STRIDE):
            xr = x_ref[0, pl.ds(r, nsub, stride=CMP_STRIDE), :]
            top = top + _nn((xr + pe_ref[kind, 0, r:r + 1, :]).astype(BF16), w1_ref[kind, 0, r])
            bot = bot + _nn((xr + pe_ref[kind, 1, r:r + 1, :]).astype(BF16), w1_ref[kind, 1, r])
        hid = _gelu_tanh(top + pltpu.roll(bot, shift=nsub - 1, axis=0)).astype(BF16)
        for g in range(NSA_GROUPS):
            hg = hid[:, g * CMP_HIDDEN:(g + 1) * CMP_HIDDEN]
            if kind == 0:
                ok_ref[0, g] = _nn(hg, w2k_ref[...]).astype(BF16)
            else:
                ovt_ref[0, g] = _nt(w2vt_ref[...], hg).astype(BF16)


def _compress(kv_cmp, pe_k, w1k, w2k, pe_v, w1v, w2v):
    B, S, _ = kv_cmp.shape
    nsub = S // CMP_STRIDE
    G = NSA_GROUPS

    def prep(pe, w1):
        pe_t = jnp.tile(pe.reshape(2, CMP_STRIDE, DH), (1, 1, G))
        a = w1.reshape(2, CMP_STRIDE, DH, CMP_HIDDEN)
        z = jnp.zeros_like(a)
        w = jnp.concatenate([jnp.concatenate([a, z], axis=3), jnp.concatenate([z, a], axis=3)], axis=2)
        return pe_t, w.astype(BF16)

    pek, w1kb = prep(pe_k, w1k)
    pev, w1vb = prep(pe_v, w1v)
    pe = jnp.stack([pek, pev])
    w1 = jnp.stack([w1kb, w1vb])
    w2kp = jnp.concatenate([w2k, jnp.zeros_like(w2k)], axis=1).astype(BF16)
    w2vt = w2v.T.astype(BF16)
    full = lambda shp: pl.BlockSpec(shp, lambda b: (0,) * len(shp))
    return pl.pallas_call(
        functools.partial(_compress_kernel, nsub=nsub),
        out_shape=[jax.ShapeDtypeStruct((B, G, nsub, LANES), BF16),
                   jax.ShapeDtypeStruct((B, G, DH, nsub), BF16)],
        grid=(B,),
        in_specs=[pl.BlockSpec((1, S, G * DH), lambda b: (b, 0, 0)), pl.BlockSpec((1, S, G * DH), lambda b: (b, 0, 1)),
                  full(pe.shape), full(w1.shape), full((CMP_HIDDEN, LANES)), full((DH, CMP_HIDDEN))],
        out_specs=[pl.BlockSpec((1, G, nsub, LANES), lambda b: (b, 0, 0, 0)),
                   pl.BlockSpec((1, G, DH, nsub), lambda b: (b, 0, 0, 0))],
        compiler_params=pltpu.CompilerParams(dimension_semantics=("parallel",), vmem_limit_bytes=VMEM_LIMIT),
        name="compress",
    )(kv_cmp, kv_cmp, pe, w1, w2kp, w2vt)


def _bias_kernel(tbl_ref, bkn_ref, bkc_ref, near_ref, cmpb_ref):
    g = pl.program_id(0)
    for h in range(HPG):
        hd = g * HPG + h

        def lookup(bk):
            acc = jnp.full(bk.shape, NEG, F32)
            for b in range(REL_BUCKETS):
                acc = jnp.where(bk == b, tbl_ref[hd, b], acc)
            return acc

        vn = lookup(bkn_ref[...])
        near_ref[0, :, h * QT:(h + 1) * QT] = jnp.where(vn > 0.5 * NEG, (vn - tbl_ref[hd, REL_BUCKETS - 1]) * LOG2E, NEG)
        vc = lookup(bkc_ref[...])
        cmpb_ref[0, :, h * QT:(h + 1) * QT] = jnp.where(vc > 0.5 * NEG, vc * LOG2E, NEG)


def _nsa_bias_tables(rel_bias, seq):
    ql = np.arange(QT)
    ncmp = seq // CMP_STRIDE

    def buckets(rel):
        return jnp.asarray(np.where(rel >= 0, _t5_bucket_np(rel), -1).astype(np.int32))

    bkn = buckets(ql[None, :] + QT - np.arange(2 * QT)[:, None])
    y = np.arange(2 * ncmp)
    bkc = buckets(ql[None, :] - CMP_STRIDE * (y[:, None] - ncmp) - (CMP_BLOCK - 1))
    nql = HPG * QT
    return pl.pallas_call(
        _bias_kernel,
        out_shape=[jax.ShapeDtypeStruct((NSA_GROUPS, 2 * QT, nql), F32),
                   jax.ShapeDtypeStruct((NSA_GROUPS, 2 * ncmp, nql), F32)],
        grid=(NSA_GROUPS,),
        in_specs=[pl.BlockSpec(memory_space=pltpu.SMEM),
                  pl.BlockSpec((2 * QT, QT), lambda g: (0, 0)),
                  pl.BlockSpec((2 * ncmp, QT), lambda g: (0, 0))],
        out_specs=[pl.BlockSpec((1, 2 * QT, nql), lambda g: (g, 0, 0)),
                   pl.BlockSpec((1, 2 * ncmp, nql), lambda g: (g, 0, 0))],
        compiler_params=pltpu.CompilerParams(dimension_semantics=("parallel",)),
        name="t5bias",
    )(rel_bias.T, bkn, bkc)


def _nsa_kernel(q_ref, kslc_ref, vtslc_ref, kwin_ref, vtwin_ref, kcb_ref, vcbt_ref, misc_ref,
                near_ref, cmpb_ref, wmask_ref, ovt_ref, eye_ref, eye4_ref,
                out_ref, qaug_s, s0_s, s1_s, *, ncmp, nslc):
    G = NSA_GROUPS
    qts = [NSA_TILES * pl.program_id(1) + t for t in range(NSA_TILES)]
    chains = [(t, g) for t in range(NSA_TILES) for g in range(G)]
    nql = HPG * QT
    ck = 2 * QT

    def flash(carry, s, vt_chunk):
        m, acc = carry
        m_new = jnp.maximum(m, jnp.max(s, axis=0, keepdims=True))
        alpha = jnp.exp2(m - m_new)
        p = jnp.exp2((s - m_new).astype(BF16))
        return m_new, alpha * acc + _nn(vt_chunk, p)

    def finish(carry):
        m, acc = carry
        return acc[:DH] * (1.0 / acc[DH:DH + 1])

    init = (jnp.full((1, nql), NEG, F32), jnp.zeros((VT_ROWS, nql), F32))
    ns = [pl.multiple_of(QT * qt, QT) for qt in qts]
    nw = [pl.multiple_of(QT * qt + 3 * QT, QT) for qt in qts]
    off = [pl.multiple_of(ncmp - (QT // CMP_STRIDE) * qt, SUBLANES) for qt in qts]
    lane = lax.broadcasted_iota(jnp.int32, (nql, LANES), 1)
    win_aug = jnp.where(lane >= DH, -BIG, 0.0).astype(BF16)
    jidx = lax.broadcasted_iota(jnp.int32, (nslc, QT), 0)
    tq = [qt * QT + lax.broadcasted_iota(jnp.int32, (nslc, QT), 1) for qt in qts]
    forced = [(jidx == 0) | (jidx == tq[t] // SLC_BLOCK) | (jidx == tq[t] // SLC_BLOCK - 1) for t in range(NSA_TILES)]
    future = [jidx * SLC_BLOCK > tq[t] for t in range(NSA_TILES)]
    sub = lax.broadcasted_iota(jnp.int32, (SUBLANES, QT), 0)
    ones_lo = jnp.ones((DH, QT), F32)
    eye = eye_ref[...]
    ovt = ovt_ref[...]
    ngrp = nslc // SUBLANES

    def aug(sel01):
        rows = [ones_lo, sel01]
        if LANES - DH - nslc:
            rows.append(jnp.ones((LANES - DH - nslc, QT), F32))
        m01 = _nt(eye, jnp.concatenate(rows, axis=0).astype(BF16))
        return jnp.concatenate([((m01 - 1.0) * BIG).astype(BF16)] * HPG, axis=0)

    cs = range(len(chains))
    kl = [slice(g * LANES, (g + 1) * LANES) for _, g in chains]
    vr = [slice(g * VT_ROWS, (g + 1) * VT_ROWS) for _, g in chains]
    low = lax.broadcasted_iota(jnp.int32, (QT, LANES), 1) < DH

    def head_slot(t, hd):
        pair = q_ref[0, t * QT:(t + 1) * QT, (hd // 2) * LANES:(hd // 2 + 1) * LANES].astype(F32)
        if hd % 2:
            pair = pltpu.roll(pair, shift=DH, axis=1)
        return jnp.where(low, pair, 0.0).astype(BF16)

    q0 = [jnp.concatenate([head_slot(t, g * HPG + h) for h in range(HPG)], axis=0) for t, g in chains]
    qwin = [q0[c] + win_aug for c in cs]

    bc = [cmpb_ref[g, pl.ds(off[t], ncmp), :] for t, g in chains]
    sc = [_nt(kcb_ref[0, chains[c][1]], q0[c]) + bc[c] for c in cs]
    sw = [_nt(kwin_ref[0, pl.ds(ns[chains[c][0]], 3 * QT), kl[c]], qwin[c]) for c in cs]
    sw = [jnp.concatenate([sw[c][:QT] + wmask_ref[...], sw[c][QT:]], axis=0) for c in cs]

    ecb, rden = [], []
    for c in cs:
        mc = jnp.maximum(jnp.max(sc[c], axis=0, keepdims=True), 0.5 * NEG)
        ec = jnp.exp2(sc[c] - mc)
        rden.append(1.0 / jnp.maximum(jnp.sum(ec, axis=0, keepdims=True), jnp.finfo(F32).tiny))
        ecb.append(ec.astype(BF16))
    wcar = [flash(init, sw[c], vtwin_ref[0, vr[c], pl.ds(ns[chains[c][0]], 3 * QT)]) for c in cs]
    o_c = [_nn(vcbt_ref[0, chains[c][1]], ecb[c]) * rden[c] for c in cs]

    imp = []
    for c in cs:
        t = chains[c][0]
        v = _nn(ovt, ecb[c][:, 0:QT]) * rden[c][:, 0:QT]
        for h in range(1, HPG):
            v = v + _nn(ovt, ecb[c][:, h * QT:(h + 1) * QT]) * rden[c][:, h * QT:(h + 1) * QT]
        imp.append(jnp.where(forced[t], 1e30, jnp.where(future[t], -1e30, v)))
    sw = [_nt(kwin_ref[0, pl.ds(nw[chains[c][0]], ck), kl[c]], qwin[c]) + near_ref[chains[c][1]]
          for c in cs]

    grp = [[imp[c][SUBLANES * v:SUBLANES * (v + 1)] for v in range(ngrp)] for c in cs]
    cnt = [[jnp.zeros((SUBLANES, QT), F32) for _ in range(ngrp)] for c in cs]
    for jp in range(nslc):
        v0, r0 = divmod(jp, SUBLANES)
        for c in cs:
            row = jnp.broadcast_to(imp[c][jp:jp + 1, :], (SUBLANES, QT))
            for v in range(ngrp):
                if v < v0:
                    inc = jnp.where(row > grp[c][v], 1.0, 0.0)
                elif v > v0:
                    inc = jnp.where(row >= grp[c][v], 1.0, 0.0)
                else:
                    inc = jnp.where(sub > r0, jnp.where(row >= grp[c][v], 1.0, 0.0),
                                    jnp.where(row > grp[c][v], 1.0, 0.0))
                cnt[c][v] = cnt[c][v] + inc
    o_w = [finish(flash(wcar[c], sw[c], vtwin_ref[0, vr[c], pl.ds(nw[chains[c][0]], ck)])) for c in cs]
    for c in cs:
        sel = jnp.concatenate(cnt[c], axis=0) < float(min(SLC_TOPN, nslc))
        sel_near = jnp.where(sel, 1.0, 0.0)
        sel_far = jnp.where(jidx < 2 * (qts[chains[c][0]] - 1), sel_near, 0.0)
        qaug_s[c, 0] = q0[c] + aug(sel_far)
        qaug_s[c, 1] = q0[c] + aug(sel_near)

    n_far = pl.program_id(1)
    n_chunks = n_far + 1

    def rows(c, i):
        return pl.multiple_of(QT * qts[chains[c][0]] - ck * i, QT)

    def scores(c, i):
        i = jnp.minimum(i, n_far)
        return _nt(kslc_ref[0, pl.ds(rows(c, i), ck), kl[c]], qaug_s[c, jnp.where(i == 0, 1, 0)])

    def vt(c, i):
        return vtslc_ref[0, vr[c], pl.ds(rows(c, i), ck)]

    for c in cs:
        s0_s[c] = scores(c, 0) + near_ref[chains[c][1]]

    def pair_body(p, carry):
        i = 2 * p
        carry = list(carry)
        for t in range(NSA_TILES):
            tc = [c for c in cs if chains[c][0] == t]
            for c in tc:
                s1_s[c] = scores(c, i + 1)
            for c in tc:
                carry[c] = flash(carry[c], s0_s[c], vt(c, i))
            for c in tc:
                s0_s[c] = scores(c, i + 2)
            for c in tc:
                carry[c] = flash(carry[c], s1_s[c], vt(c, i + 1))
        return tuple(carry)

    carry = lax.fori_loop(0, n_chunks // 2, pair_body, (init,) * len(chains))
    carry = lax.cond(n_chunks % 2 == 1,
                     lambda cr: tuple(flash(cr[c], s0_s[c], vt(c, n_far)) for c in cs),
                     lambda cr: cr, carry)

    gts = [jax.nn.sigmoid(misc_ref[0, t * QT:(t + 1) * QT, :]).T for t in range(NSA_TILES)]
    head = lax.broadcasted_iota(jnp.int32, (DH, nql), 1) // QT
    for c, (t, g) in enumerate(chains):
        o_s = finish(carry[c])

        def gate_row(br):
            r0 = 16 * g + br * HPG
            return jnp.concatenate([gts[t][r0 + h:r0 + h + 1, :] for h in range(HPG)], axis=1)

        o = gate_row(0) * o_c[c] + gate_row(1) * o_s + gate_row(2) * o_w[c]
        ob = o.astype(BF16)
        blocks = jnp.concatenate([jnp.where(head == h, ob, jnp.zeros_like(ob)) for h in range(HPG)], axis=0)
        out_ref[0, t * QT:(t + 1) * QT, g * HPG * DH:(g + 1) * HPG * DH] = _nt(eye4_ref[...], blocks).astype(BF16)


def _nsa(q, kslc, vtslc, kwin, vtwin, kcb, vcbt, misc, rel_bias):
    B, S, _ = q.shape
    G = NSA_GROUPS
    nq = S // QT
    ncmp = S // CMP_STRIDE
    nslc = S // SLC_BLOCK
    nql = HPG * QT
    near, cmpb = _nsa_bias_tables(rel_bias, S)
    wmask = jnp.asarray(np.tile(np.where(np.arange(QT)[:, None] > np.arange(QT)[None, :], 0.0, NEG), (1, HPG)), F32)
    kpad_s = jnp.concatenate([jnp.zeros((DH,), BF16), jnp.ones((DH,), BF16)] * G)
    kslc_p = jnp.concatenate([jnp.broadcast_to(kpad_s, (B, SLC_PAD, G * LANES)), kslc], axis=1)
    kwin_p = jnp.concatenate([jnp.broadcast_to(kpad_s, (B, WIN_PAD, G * LANES)), kwin], axis=1)
    vtslc_p = jnp.pad(vtslc, ((0, 0), (0, 0), (SLC_PAD, 0)))
    vtwin_p = jnp.pad(vtwin, ((0, 0), (0, 0), (WIN_PAD, 0)))
    ci = np.arange(ncmp)[None, :] * CMP_STRIDE
    sj = np.arange(nslc)[:, None] * SLC_BLOCK
    ovt = jnp.asarray(((ci < sj + SLC_BLOCK) & (ci + CMP_BLOCK > sj)).astype(np.float32), BF16)
    eye = jnp.eye(QT, dtype=BF16)
    eye4 = jnp.tile(eye, (1, HPG))
    kern = functools.partial(_nsa_kernel, ncmp=ncmp, nslc=nslc)
    nch = NSA_TILES * G
    per_b = lambda shp: pl.BlockSpec(shp, lambda b, i: (b,) + (0,) * (len(shp) - 1))
    full = lambda shp: pl.BlockSpec(shp, lambda b, i: (0,) * len(shp))
    return pl.pallas_call(
        kern,
        out_shape=jax.ShapeDtypeStruct((B, S, NSA_HEADS * DH), BF16),
        grid=(B, nq // NSA_TILES),
        in_specs=[
            pl.BlockSpec((1, NSA_TILES * QT, NSA_HEADS * DH), lambda b, i: (b, i, 0)),
            per_b((1, SLC_PAD + S, G * LANES)),
            per_b((1, G * VT_ROWS, SLC_PAD + S)),
            per_b((1, WIN_PAD + S, G * LANES)),
            per_b((1, G * VT_ROWS, WIN_PAD + S)),
            per_b((1, G, ncmp, LANES)),
            per_b((1, G, DH, ncmp)),
            pl.BlockSpec((1, NSA_TILES * QT, LANES), lambda b, i: (b, i, 0)),
            full((G, 2 * QT, nql)),
            full((G, 2 * ncmp, nql)),
            full((QT, nql)),
            full((nslc, ncmp)),
            full((QT, QT)),
            full((QT, nql)),
        ],
        out_specs=pl.BlockSpec((1, NSA_TILES * QT, NSA_HEADS * DH), lambda b, i: (b, i, 0)),
        scratch_shapes=[pltpu.VMEM((nch, 2, nql, LANES), BF16), pltpu.VMEM((nch, 2 * QT, nql), F32),
                        pltpu.VMEM((nch, 2 * QT, nql), F32)],
        compiler_params=pltpu.CompilerParams(dimension_semantics=("parallel", "arbitrary"),
                                             vmem_limit_bytes=VMEM_LIMIT),
        name="nsa",
    )(q, kslc_p, vtslc_p, kwin_p, vtwin_p, kcb, vcbt, misc, near, cmpb, wmask, ovt, eye, eye4)


GLA_NB = 2


def _gla_kernel(qk_ref, v_ref, vt_ref, misc_ref, r_ref, wal_ref, bal_ref, ng_ref, cum_ref, out_ref,
                state_s, o_s, *, ct, nbatch):
    H, dk, dv, C = GLA_HEADS, GLA_DK, GLA_DV, GLA_CHUNK
    kw = H * dk
    nb = range(nbatch)

    @pl.when(pl.program_id(1) == 0)
    def _():
        state_s[...] = jnp.zeros_like(state_s)

    cum = cum_ref[...]
    q_in, k_in, k_st, decay = [], [], [], []
    for bb in nb:
        z = _nn(misc_ref[bb].astype(BF16), wal_ref[...]) + bal_ref[...]
        log_a = (jnp.minimum(z, 0.0) - jnp.log1p(jnp.exp(-jnp.abs(z)))) * (1.0 / GLA_TAU)
        a1, a2, a3 = _split3(log_a)
        cs = _nn(cum, a1) + _nn(cum, a2) + _nn(cum, a3)
        bc, bl = cs[:ct], cs[ct:]
        q = qk_ref[bb, :, :kw]
        k = qk_ref[bb, :, kw:]
        q_in.append((q * (dk ** -0.5)) * jnp.exp(bc))
        k_in.append((k * jnp.exp(-bc)).astype(BF16))
        k_st.append(k * jnp.exp(bl - bc))
        decay.append(jnp.exp(bl))
    lane_head = lax.broadcasted_iota(jnp.int32, (C, kw), 1) // dk
    rr = lax.broadcasted_iota(jnp.int32, (H * C, C), 0) % C
    cc = lax.broadcasted_iota(jnp.int32, (H * C, C), 1)
    causal = rr >= cc
    pair_row = lax.broadcasted_iota(jnp.int32, (2 * C, kw), 0) // C
    pair_head = lax.broadcasted_iota(jnp.int32, (2 * C, kw), 1) // dk

    for c in range(ct // C):
        r0 = c * C
        p0 = (c // 2) * 2 * C
        for bb in nb:
            qc = q_in[bb][r0:r0 + C]
            qcb = qc.astype(BF16)
            q_heads = jnp.concatenate([jnp.where(lane_head == h, qc, 0.0) for h in range(H)], axis=0).astype(BF16)
            attn = jnp.where(causal, _nt(q_heads, k_in[bb][r0:r0 + C]), 0.0).astype(BF16)
            kst_pair = k_st[bb][p0:p0 + 2 * C]
            dec = decay[bb][r0:r0 + 1]
            for h in range(H):
                st = state_s[bb, h]
                o = _nn(attn[h * C:(h + 1) * C], v_ref[bb, r0:r0 + C, h * dv:(h + 1) * dv])
                o = o + _nt(qcb, st.astype(BF16))
                o_s[bb, r0:r0 + C, h * dv:(h + 1) * dv] = o
                kst_h = jnp.where((pair_row == c % 2) & (pair_head == h), kst_pair, 0.0).astype(BF16)
                state_s[bb, h] = st * dec + _nn(vt_ref[bb, h * dv:(h + 1) * dv, p0:p0 + 2 * C], kst_h)

    for bb in nb:
        for h in range(H):
            oh = o_s[bb, :, h * dv:(h + 1) * dv]
            ms = jnp.mean(oh * oh, axis=-1, keepdims=True)
            r = r_ref[bb, :, h * dv:(h + 1) * dv]
            y = oh * lax.rsqrt(ms + EPS) * ng_ref[:, h * dv:(h + 1) * dv] * (r * jax.nn.sigmoid(r))
            out_ref[bb, :, h * dv:(h + 1) * dv] = y.astype(BF16)


def _gla(qkb, vb, vtb, misc, rb, w_alpha, b_alpha, norm_g, ct=512):
    B, S, _ = qkb.shape
    H, dk, dv, C = GLA_HEADS, GLA_DK, GLA_DV, GLA_CHUNK
    kw, vw = H * dk, H * dv
    nb = GLA_NB if B % GLA_NB == 0 else 1
    wal = jnp.zeros((LANES, kw), F32).at[32:32 + GLA_RANK].set(w_alpha).astype(BF16)
    r = np.arange(ct)
    tri = (r[:, None] // C == r[None, :] // C) & (r[:, None] >= r[None, :])
    tot = r[:, None] // C == r[None, :] // C
    cum = jnp.asarray(np.concatenate([tri, tot], axis=0).astype(np.float32), BF16)
    full = lambda shp: pl.BlockSpec(shp, lambda b, i: (0,) * len(shp))
    return pl.pallas_call(
        functools.partial(_gla_kernel, ct=ct, nbatch=nb),
        out_shape=jax.ShapeDtypeStruct((B, S, vw), BF16),
        grid=(B // nb, S // ct),
        in_specs=[
            pl.BlockSpec((nb, ct, 2 * kw), lambda b, i: (b, i, 0)),
            pl.BlockSpec((nb, ct, vw), lambda b, i: (b, i, 0)),
            pl.BlockSpec((nb, vw, ct), lambda b, i: (b, 0, i)),
            pl.BlockSpec((nb, ct, LANES), lambda b, i: (b, i, 0)),
            pl.BlockSpec((nb, ct, vw), lambda b, i: (b, i, 0)),
            full((LANES, kw)), full((1, kw)), full((1, vw)), full((2 * ct, ct)),
        ],
        out_specs=pl.BlockSpec((nb, ct, vw), lambda b, i: (b, i, 0)),
        scratch_shapes=[pltpu.VMEM((nb, H, dv, kw), F32), pltpu.VMEM((nb, ct, vw), F32)],
        compiler_params=pltpu.CompilerParams(dimension_semantics=("parallel", "arbitrary"),
                                             vmem_limit_bytes=VMEM_LIMIT),
        name="gla",
    )(qkb, vb, vtb, misc, rb, wal, b_alpha.reshape(1, kw), norm_g.reshape(1, vw), cum)


ROUTER_ROWS = 128
EXPERT_ROW0 = 8


def _local_rows(tm):
    return -(-(2 * tm + N_EXPERTS * (SUBLANES - 1)) // LANES) * LANES


def _pack_bf16_pairs(x):
    u = pltpu.bitcast(x, jnp.uint32)
    w = x.shape[1] // 2
    return u[:, :w] | (u[:, w:] >> 16)


def _unpack_bf16_pairs(w):
    xh = pltpu.bitcast(w & jnp.uint32(0xFFFF0000), F32)
    xl = pltpu.bitcast(w << 16, F32)
    return jnp.concatenate([xh, xl], axis=1).astype(BF16)


def _outproj_kernel(ya_ref, yb_ref, mg_ref, x_ref, wa_ref, wb_ref, wo_ref, gf_ref, wrh_ref, wrl_ref, br_ref,
                    tri_ref, ltri_ref, x1_ref, xsl_ref, seg_ref, rw_ref, cnt_ref, carry_s, *, tm, ls):
    D = x_ref.shape[1]

    @pl.when(pl.program_id(0) == 0)
    def _():
        carry_s[...] = jnp.zeros_like(carry_s)

    ma = _nn(ya_ref[...], wa_ref[...])
    mb = _nn(yb_ref[...], wb_ref[...])
    merged = jax.nn.sigmoid(mg_ref[:, :D]) * ma + jax.nn.sigmoid(mg_ref[:, D:]) * mb
    x1 = x_ref[...] + _nn(merged.astype(BF16), wo_ref[...])
    x1_ref[...] = x1
    ms = jnp.mean(x1 * x1, axis=-1, keepdims=True)
    h2 = x1 * lax.rsqrt(ms + EPS) * gf_ref[...]
    hi = h2.astype(BF16)
    lo = (h2 - hi.astype(F32)).astype(BF16)
    lg = _nt(wrh_ref[...], hi) + _nt(wrh_ref[...], lo) + _nt(wrl_ref[...], hi) + br_ref[...]
    row8 = lax.broadcasted_iota(jnp.int32, (SUBLANES, tm), 0)
    gl = jnp.where(row8 < N_GROUPS, lg[0:SUBLANES], NEG)
    gmax = jnp.max(gl, axis=0, keepdims=True)
    g_sel = jnp.min(jnp.where(gl == gmax, row8, SUBLANES), axis=0, keepdims=True)
    g_prob = 1.0 / jnp.sum(jnp.where(row8 < N_GROUPS, jnp.exp(gl - gmax), 0.0), axis=0, keepdims=True)
    e_sel = jnp.zeros((EPG, tm), F32)
    for gi in range(N_GROUPS):
        r0 = EXPERT_ROW0 + gi * EPG
        e_sel = e_sel + jnp.where(g_sel == gi, lg[r0:r0 + EPG], 0.0)
    v1 = jnp.max(e_sel, axis=0, keepdims=True)
    i1 = jnp.min(jnp.where(e_sel == v1, row8, EPG), axis=0, keepdims=True)
    rest = jnp.where(row8 == i1, -jnp.inf, e_sel)
    v2 = jnp.max(rest, axis=0, keepdims=True)
    i2 = jnp.min(jnp.where(rest == v2, row8, EPG), axis=0, keepdims=True)
    t = jnp.exp(v2 - v1)
    w1 = g_prob / (1.0 + t)
    w2 = g_prob * t / (1.0 + t)
    e1 = g_sel * EPG + i1
    e2 = g_sel * EPG + i2

    rowe = lax.broadcasted_iota(jnp.int32, (N_EXPERTS, tm), 0)
    oh1 = rowe == e1
    oh2 = rowe == e2
    oh = jnp.where(oh1, 1.0, 0.0) + jnp.where(oh2, 1.0, 0.0)
    pre = _nn(oh.astype(BF16), tri_ref[...])
    cnt = jnp.sum(oh, axis=1, keepdims=True)
    cnt8 = jnp.floor((cnt + (SUBLANES - 1)) * (1.0 / SUBLANES)) * SUBLANES
    c_b = jnp.broadcast_to(cnt8, (N_EXPERTS, LANES))
    c_hi = jnp.floor(c_b * (1.0 / 16.0))
    c_lo = c_b - 16.0 * c_hi
    base = 16.0 * _nn(ltri_ref[...], c_hi.astype(BF16)) + _nn(ltri_ref[...], c_lo.astype(BF16))
    loc = pre + base[:, 0:1]
    lpos1 = jnp.sum(jnp.where(oh1, loc, 0.0), axis=0, keepdims=True)
    lpos2 = jnp.sum(jnp.where(oh2, loc, 0.0), axis=0, keepdims=True)
    lane = lax.broadcasted_iota(jnp.int32, (N_EXPERTS, LANES), 1)
    seg_ref[0] = jnp.where(lane == 0, base, jnp.where(lane == 1, c_b, carry_s[...])).astype(jnp.int32)
    carry_s[...] = carry_s[...] + c_b
    cnt_ref[...] = carry_s[...].astype(jnp.int32)
    rw_ref[0] = jnp.concatenate([w1, w2, lpos1, lpos2, jnp.zeros((SUBLANES - 4, tm), F32)], axis=0)

    srow = lax.broadcasted_iota(jnp.int32, (ls, tm), 0)
    perm = jnp.where(srow == lpos1.astype(jnp.int32), 1.0, jnp.where(srow == lpos2.astype(jnp.int32), 1.0, 0.0))
    xsorted = _nn(perm.astype(BF16), hi)
    xsl_ref[0] = _pack_bf16_pairs(xsorted)


def _outproj(ya, yb, mg, x, wa, wb, wo, g_ffn, w_rg, b_rg, w_re, b_re, tm=MOE_TM):
    T, D = x.shape
    nt = T // tm
    wr = jnp.zeros((ROUTER_ROWS, D), F32).at[0:N_GROUPS].set(w_rg.T).at[EXPERT_ROW0:EXPERT_ROW0 + N_EXPERTS].set(w_re.T)
    wrh = wr.astype(BF16)
    wrl = (wr - wrh.astype(F32)).astype(BF16)
    br = jnp.zeros((ROUTER_ROWS, 1), F32).at[0:N_GROUPS, 0].set(b_rg).at[EXPERT_ROW0:EXPERT_ROW0 + N_EXPERTS, 0].set(b_re)
    r = np.arange(tm)
    tri = jnp.asarray((r[:, None] < r[None, :]).astype(np.float32), BF16)
    re = np.arange(N_EXPERTS)
    ltri = jnp.asarray((re[None, :] < re[:, None]).astype(np.float32), BF16)
    ls = _local_rows(tm)
    row = lambda n: pl.BlockSpec((tm, n), lambda i: (i, 0))
    full = lambda shp: pl.BlockSpec(shp, lambda i: (0,) * len(shp))
    return pl.pallas_call(
        functools.partial(_outproj_kernel, tm=tm, ls=ls),
        out_shape=[jax.ShapeDtypeStruct((T, D), F32), jax.ShapeDtypeStruct((nt, ls, D // 2), jnp.uint32),
                   jax.ShapeDtypeStruct((nt, N_EXPERTS, LANES), jnp.int32),
                   jax.ShapeDtypeStruct((nt, SUBLANES, tm), F32),
                   jax.ShapeDtypeStruct((N_EXPERTS, LANES), jnp.int32)],
        grid=(nt,),
        in_specs=[row(ya.shape[1]), row(yb.shape[1]), row(2 * D), row(D),
                  full(wa.shape), full(wb.shape), full(wo.shape), full((1, D)),
                  full((ROUTER_ROWS, D)), full((ROUTER_ROWS, D)), full((ROUTER_ROWS, 1)), full((tm, tm)),
                  full((N_EXPERTS, N_EXPERTS))],
        out_specs=[row(D), pl.BlockSpec((1, ls, D // 2), lambda i: (i, 0, 0)),
                   pl.BlockSpec((1, N_EXPERTS, LANES), lambda i: (i, 0, 0)),
                   pl.BlockSpec((1, SUBLANES, tm), lambda i: (i, 0, 0)),
                   full((N_EXPERTS, LANES))],
        scratch_shapes=[pltpu.VMEM((N_EXPERTS, LANES), F32)],
        compiler_params=pltpu.CompilerParams(dimension_semantics=("arbitrary",), vmem_limit_bytes=VMEM_LIMIT),
        name="outproj",
    )(ya, yb, mg, x, wa, wb, wo, g_ffn.reshape(1, D), wrh, wrl, br, tri, ltri)


SEG_FIELDS = 4
SEG_BITS = 7
TAIL_BITS = (TE // SUBLANES).bit_length() - 1


def _segment_copies(n8, bits, make_copy, wait):
    off = 0
    for bit in reversed(range(bits)):
        rows = SUBLANES << bit
        take = (n8 >> bit) & 1

        @pl.when(take == 1)
        def _(off=off, rows=rows):
            cp = make_copy(off, rows)
            if wait:
                cp.wait()
            else:
                cp.start()

        off = off + take * rows


def _dispatch_kernel(seg_ref, tail_ref, xsl_ref, xs_ref, zero_s, sem):
    i = pl.program_id(0)

    def segments(wait):
        def body(e, c):
            s0 = (i * N_EXPERTS + e) * SEG_FIELDS
            src0 = pl.multiple_of(seg_ref[s0], SUBLANES)
            dst0 = pl.multiple_of(seg_ref[s0 + 2], SUBLANES)

            def make_copy(off, rows):
                return pltpu.make_async_copy(xsl_ref.at[0, pl.ds(pl.multiple_of(src0 + off, SUBLANES), rows), :],
                                             xs_ref.at[pl.ds(pl.multiple_of(dst0 + off, SUBLANES), rows), :], sem)

            _segment_copies(seg_ref[s0 + 1], SEG_BITS, make_copy, wait)
            return c

        lax.fori_loop(0, N_EXPERTS, body, 0)

    def tails(wait):
        def body(e, c):
            dst0 = pl.multiple_of(tail_ref[2 * e], SUBLANES)

            def make_copy(off, rows):
                return pltpu.make_async_copy(zero_s.at[pl.ds(0, rows), :],
                                             xs_ref.at[pl.ds(pl.multiple_of(dst0 + off, SUBLANES), rows), :], sem)

            _segment_copies(tail_ref[2 * e + 1], TAIL_BITS, make_copy, wait)
            return c

        lax.fori_loop(0, N_EXPERTS, body, 0)

    def unused_tiles(wait):
        def body(t, c):
            cp = pltpu.make_async_copy(zero_s, xs_ref.at[pl.ds(pl.multiple_of(t * TE, TE), TE), :], sem)
            if wait:
                cp.wait()
            else:
                cp.start()
            return c

        lax.fori_loop(tail_ref[2 * N_EXPERTS], xs_ref.shape[0] // TE, body, 0)

    @pl.when(i == 0)
    def _():
        zero_s[...] = jnp.zeros_like(zero_s)
        tails(False)
        unused_tiles(False)
        tails(True)
        unused_tiles(True)

    segments(False)
    segments(True)


def _dispatch(seg, tail, xsl, n_rows):
    nt, ls, W = xsl.shape
    return pl.pallas_call(
        _dispatch_kernel,
        out_shape=jax.ShapeDtypeStruct((n_rows, W), xsl.dtype),
        grid_spec=pltpu.PrefetchScalarGridSpec(
            num_scalar_prefetch=2,
            grid=(nt,),
            in_specs=[pl.BlockSpec((1, ls, W), lambda i, sg, tl: (i, 0, 0))],
            out_specs=pl.BlockSpec(memory_space=pl.ANY),
            scratch_shapes=[pltpu.VMEM((TE, W), xsl.dtype), pltpu.SemaphoreType.DMA(())],
        ),
        compiler_params=pltpu.CompilerParams(dimension_semantics=("arbitrary",)),
        name="dispatch",
    )(seg, tail, xsl)


def _experts_kernel(te_ref, nv_ref, xs_ref, wg_ref, wu_ref, wd_ref, out_ref):
    i = pl.program_id(0)

    @pl.when(i < nv_ref[0])
    def _():
        x = _unpack_bf16_pairs(xs_ref[...])
        a = _nn(x, wg_ref[0])
        u = _nn(x, wu_ref[0])
        hid = (a * jax.nn.sigmoid(a)) * u
        y = _nn(hid.astype(BF16), wd_ref[0])
        out_ref[...] = _pack_bf16_pairs(y.astype(BF16).astype(F32))

    @pl.when(i >= nv_ref[0])
    def _():
        out_ref[...] = jnp.zeros_like(out_ref)


def _experts(tile_expert, n_valid, xs, w_gate, w_up, w_down):
    n_rows, W = xs.shape
    D = 2 * W
    n_tiles = n_rows // TE
    last = lambda i, nv: jnp.minimum(i, nv[0] - 1)
    return pl.pallas_call(
        _experts_kernel,
        out_shape=jax.ShapeDtypeStruct((n_rows, W), jnp.uint32),
        grid_spec=pltpu.PrefetchScalarGridSpec(
            num_scalar_prefetch=2,
            grid=(n_tiles,),
            in_specs=[pl.BlockSpec((TE, W), lambda i, te, nv: (last(i, nv), 0)),
                      pl.BlockSpec((1, D, EXPERT_FF), lambda i, te, nv: (te[last(i, nv)], 0, 0)),
                      pl.BlockSpec((1, D, EXPERT_FF), lambda i, te, nv: (te[last(i, nv)], 0, 0)),
                      pl.BlockSpec((1, EXPERT_FF, D), lambda i, te, nv: (te[last(i, nv)], 0, 0))],
            out_specs=pl.BlockSpec((TE, W), lambda i, te, nv: (i, 0)),
        ),
        compiler_params=pltpu.CompilerParams(dimension_semantics=("arbitrary",), vmem_limit_bytes=VMEM_LIMIT),
        name="experts",
    )(tile_expert, n_valid, xs, w_gate, w_up, w_down)


def _combine_kernel(seg_ref, ys_ref, x1_ref, rw_ref, gfin_ref, out_ref, buf, sem, *, tm, ls, apply_norm):
    i = pl.program_id(0)
    nt = pl.num_programs(0)

    def segments(tile, wait):
        slot = tile % 2

        def body(e, c):
            s0 = (tile * N_EXPERTS + e) * SEG_FIELDS
            loc0 = pl.multiple_of(seg_ref[s0], SUBLANES)
            glob0 = pl.multiple_of(seg_ref[s0 + 2], SUBLANES)

            def make_copy(off, rows):
                return pltpu.make_async_copy(ys_ref.at[pl.ds(pl.multiple_of(glob0 + off, SUBLANES), rows), :],
                                             buf.at[slot, pl.ds(pl.multiple_of(loc0 + off, SUBLANES), rows), :],
                                             sem.at[slot])

            _segment_copies(seg_ref[s0 + 1], SEG_BITS, make_copy, wait)
            return c

        lax.fori_loop(0, N_EXPERTS, body, 0)

    @pl.when(i == 0)
    def _():
        buf[...] = jnp.zeros_like(buf)
        segments(i, False)

    @pl.when(i + 1 < nt)
    def _():
        segments(i + 1, False)

    segments(i, True)
    ysl = _unpack_bf16_pairs(buf[i % 2])
    cols = jnp.concatenate([rw_ref[0], jnp.zeros((LANES - SUBLANES, tm), F32)], axis=0).T
    srow = lax.broadcasted_iota(jnp.int32, (tm, ls), 1)
    y = x1_ref[...]
    for k in range(2):
        pick = jnp.where(srow == cols[:, 2 + k:3 + k].astype(jnp.int32), 1.0, 0.0).astype(BF16)
        y = y + cols[:, k:k + 1] * _nn(pick, ysl)
    if apply_norm:
        ms = jnp.mean(y * y, axis=-1, keepdims=True)
        y = y * lax.rsqrt(ms + EPS) * gfin_ref[...]
    out_ref[...] = y


def _combine(seg, ys, x1, rw, g_final, apply_norm, tm=MOE_TM):
    T, D = x1.shape
    nt = T // tm
    ls = _local_rows(tm)
    return pl.pallas_call(
        functools.partial(_combine_kernel, tm=tm, ls=ls, apply_norm=apply_norm),
        out_shape=jax.ShapeDtypeStruct((T, D), F32),
        grid_spec=pltpu.PrefetchScalarGridSpec(
            num_scalar_prefetch=1,
            grid=(nt,),
            in_specs=[pl.BlockSpec(memory_space=pl.ANY),
                      pl.BlockSpec((tm, D), lambda i, sg: (i, 0)),
                      pl.BlockSpec((1, SUBLANES, tm), lambda i, sg: (i, 0, 0)),
                      pl.BlockSpec((1, D), lambda i, sg: (0, 0))],
            out_specs=pl.BlockSpec((tm, D), lambda i, sg: (i, 0)),
            scratch_shapes=[pltpu.VMEM((2, ls, D // 2), jnp.uint32), pltpu.SemaphoreType.DMA((2,))],
        ),
        compiler_params=pltpu.CompilerParams(dimension_semantics=("arbitrary",), vmem_limit_bytes=VMEM_LIMIT),
        name="combine",
    )(seg, ys, x1, rw, g_final.reshape(1, D))


def _moe_plan(seg, counts, T, tm=MOE_TM):
    nt = T // tm
    n_tiles_max = (2 * T + nt * N_EXPERTS * (SUBLANES - 1)) // TE + N_EXPERTS
    total = counts[:, 0]
    tiles = (total + TE - 1) // TE
    ids = jnp.arange(N_EXPERTS)
    tile_end = jnp.sum(jnp.where(ids[None, :] <= ids[:, None], tiles[None, :], 0), axis=1)
    row0 = (tile_end - tiles) * TE
    segtab = jnp.stack([seg[:, :, 0], seg[:, :, 1] // SUBLANES, seg[:, :, 2] + row0[None, :],
                        jnp.zeros_like(seg[:, :, 0])], axis=-1).reshape(-1).astype(jnp.int32)
    tail = jnp.concatenate([jnp.stack([row0 + total, (tiles * TE - total) // SUBLANES], axis=-1).reshape(-1),
                            tile_end[-1:]]).astype(jnp.int32)
    tile_expert = jnp.minimum(jnp.sum(tile_end[None, :] <= jnp.arange(n_tiles_max)[:, None], axis=1),
                              N_EXPERTS - 1).astype(jnp.int32)
    return segtab, tail, tile_expert, tile_end[-1:].astype(jnp.int32), n_tiles_max * TE


def kernel(x, g_mix, w_in, nsa_pe_k, nsa_cmp_k_w1, nsa_cmp_k_w2, nsa_pe_v, nsa_cmp_v_w1, nsa_cmp_v_w2, rel_bias,
           gla_w_alpha, gla_b_alpha, gla_norm_g, w_branch_a, w_branch_b, w_out, g_ffn, w_router_group,
           b_router_group, w_router_expert, b_router_expert, w_exp_gate, w_exp_up, w_exp_down, g_final):
    B, S, D = x.shape
    T = B * S
    for l in range(w_in.shape[0]):
        w = _pack_inproj_weights(w_in[l])
        (q, kslc, kwin, kvc, misc, qkb, vb, rb, mg, vts, vtw, vtb, wg16, wu16, wd16) = _inproj(
            x, g_mix[l], w, [w_exp_gate[l], w_exp_up[l], w_exp_down[l]])
        kcb, vcbt = _compress(kvc, nsa_pe_k[l], nsa_cmp_k_w1[l], nsa_cmp_k_w2[l],
                              nsa_pe_v[l], nsa_cmp_v_w1[l], nsa_cmp_v_w2[l])
        ya = _nsa(q, kslc, vts, kwin, vtw, kcb, vcbt, misc, rel_bias)
        yb = _gla(qkb, vb, vtb, misc, rb, gla_w_alpha[l], gla_b_alpha[l], gla_norm_g[l])
        x1, xsl, seg, rw, counts = _outproj(
            ya.reshape(T, -1), yb.reshape(T, -1), mg.reshape(T, -1), x.reshape(T, D),
            w_branch_a[l].astype(BF16), w_branch_b[l].astype(BF16), w_out[l].astype(BF16), g_ffn[l],
            w_router_group[l], b_router_group[l], w_router_expert[l], b_router_expert[l])
        segtab, tail, tile_expert, n_valid, n_rows = _moe_plan(seg, counts, T)
        xs = _dispatch(segtab, tail, xsl, n_rows)
        ys = _experts(tile_expert, n_valid, xs, wg16, wu16, wd16)
        last_layer = l == w_in.shape[0] - 1
        x = _combine(segtab, ys, x1, rw, g_final, apply_norm=last_layer).reshape(B, S, D)
    return x
```

```python
import functools
import math

import numpy as np
import jax
import jax.numpy as jnp
from jax import lax
from jax.experimental import pallas as pl
from jax.experimental.pallas import tpu as pltpu

F32 = jnp.float32
BF16 = jnp.bfloat16

NSA_HEADS = 8
NSA_GROUPS = 2
HPG = NSA_HEADS // NSA_GROUPS
DH = 64
CMP_BLOCK = 32
CMP_STRIDE = 16
CMP_HIDDEN = 128
SLC_BLOCK = 64
SLC_TOPN = 16
WINDOW = 512
GLA_HEADS = 4
GLA_DK = 64
GLA_DV = 128
GLA_RANK = 16
GLA_TAU = 16.0
GLA_CHUNK = 64
REL_BUCKETS = 32
REL_MAX_EXACT = REL_BUCKETS // 2
REL_MAX_DIST = 128
N_GROUPS = 4
EPG = 8
N_EXPERTS = N_GROUPS * EPG
EXPERT_FF = 256
EPS = 1e-6

LANES = 128
SUBLANES = 8
VMEM_LIMIT = 56 * 1024 * 1024

LOG2E = math.log2(math.e)
NEG = -1e30
BIG = float(2.0 ** 100)
QT = 128
SLC_CHUNK_TILES = 2
SLC_PAD = (SLC_CHUNK_TILES - 1) * QT
WIN_PAD = 512
TE = 512
MOE_TM = 512
VT_ROWS = DH + 16
NSA_TILES = 2


def _nt(a, b):
    return lax.dot_general(a, b, (((1,), (1,)), ((), ())), preferred_element_type=F32)


def _nn(a, b):
    return jnp.dot(a, b, preferred_element_type=F32)


def _split3(x):
    a = x.astype(BF16)
    r = x - a.astype(F32)
    b = r.astype(BF16)
    c = (r - b.astype(F32)).astype(BF16)
    return a, b, c


def _t5_bucket_np(rel):
    n = np.maximum(rel, 0)
    nf = np.maximum(n, 1).astype(np.float32)
    large = REL_MAX_EXACT + (np.log(nf / np.float32(REL_MAX_EXACT)) / np.float32(math.log(REL_MAX_DIST / REL_MAX_EXACT))
                             * np.float32(REL_BUCKETS - REL_MAX_EXACT)).astype(np.int32)
    return np.where(n < REL_MAX_EXACT, n, np.minimum(large, REL_BUCKETS - 1)).astype(np.int32)


def _inproj_kernel(x_ref, g_ref, w_ref, kplace_ref, *refs, tm, seq, ncast):
    cast_in, refs = refs[:ncast], refs[ncast:]
    (oq, okslc, okwin, okv, omisc, oqkb, ovb, orb, omg, ovts, ovtw, ovtb), cast_out = refs[:12], refs[12:]
    for src, dst in zip(cast_in, cast_out):
        dst[...] = src[...].astype(BF16)
    x = x_ref[0]
    ms = jnp.mean(x * x, axis=-1, keepdims=True)
    h = (x * lax.rsqrt(ms + EPS) * g_ref[...]).astype(BF16)

    def mm(a, b):
        return _nn(h, w_ref[:, a:b])

    oq[0] = mm(0, 512).astype(BF16)
    kv = mm(512, 1280)
    okv[0] = kv[:, 0:256]

    def spread(k):
        return _nn(k.astype(BF16), kplace_ref[...])

    row = lax.broadcasted_iota(jnp.int32, (tm, 256), 0) + pl.program_id(1) * tm
    lane = lax.broadcasted_iota(jnp.int32, (tm, 256), 1) % LANES
    onehot = jnp.where(lane - DH == row // SLC_BLOCK, 1.0, 0.0)
    okslc[0] = (spread(kv[:, 256:384]) + onehot).astype(BF16)
    okwin[0] = spread(kv[:, 512:640]).astype(BF16)

    ones_rows = jnp.where(lax.broadcasted_iota(jnp.int32, (VT_ROWS - DH, tm), 0) == 0, 1.0, 0.0)

    def vt_groups(v):
        t = v.T
        return jnp.concatenate([t[:DH], ones_rows, t[DH:], ones_rows], axis=0).astype(BF16)

    ovts[0] = vt_groups(kv[:, 384:512])
    ovtw[0] = vt_groups(kv[:, 640:768])
    omisc[0] = mm(1280, 1408)
    qkv = mm(1408, 2432)
    oqkb[0] = qkv[:, 0:512]
    ovb[0] = qkv[:, 512:1024].astype(BF16)
    ovtb[0] = qkv[:, 512:1024].T.astype(BF16)
    orb[0] = mm(2432, 2944)
    omg[0] = mm(2944, 4992)


def _pack_inproj_weights(w_in):
    o = np.cumsum([0, 512, 128, 128, 128, 128, 128, 128, 24, 256, 256, 512, 16, 512, 1024, 1024])
    (q_a, k_cmp, v_cmp, k_slc, v_slc, k_win, v_win, gate_a, q_b, k_b, v_b, a_b, r_b, mg_a, mg_b) = [
        w_in[:, o[i]:o[i + 1]] for i in range(15)]
    D = w_in.shape[0]
    ga = gate_a.reshape(D, 3, NSA_GROUPS, HPG)
    z4 = jnp.zeros((D, 4), w_in.dtype)
    misc = [ga[:, :, 0, :].reshape(D, 12), z4, ga[:, :, 1, :].reshape(D, 12), z4, a_b,
            jnp.zeros((D, LANES - 48), w_in.dtype)]
    w = jnp.concatenate([q_a * (DH ** -0.5 * LOG2E), k_cmp, v_cmp, k_slc, v_slc, k_win, v_win] + misc
                        + [q_b, k_b, v_b, r_b, mg_a, mg_b], axis=1)
    return w.astype(BF16)


def _inproj(x, g_mix, w, to_bf16, tm=512):
    B, S, D = x.shape
    nw = w.shape[1]
    nsteps = B * (S // tm)
    cast_specs = [pl.BlockSpec((a.shape[0] // nsteps,) + a.shape[1:],
                               lambda b, i, nd=a.ndim: (b * (S // tm) + i,) + (0,) * (nd - 1)) for a in to_bf16]
    src = np.arange(NSA_GROUPS * DH)
    kplace = np.zeros((NSA_GROUPS * DH, NSA_GROUPS * LANES), np.float32)
    kplace[src, (src // DH) * LANES + src % DH] = 1.0
    kplace = jnp.asarray(kplace, BF16)
    widths = [(512, BF16), (256, BF16), (256, BF16), (256, F32), (128, F32), (512, F32), (512, BF16),
              (512, F32), (2048, F32)]
    out_shape = [jax.ShapeDtypeStruct((B, S, n), dt) for n, dt in widths]
    out_specs = [pl.BlockSpec((1, tm, n), lambda b, i: (b, i, 0)) for n, _ in widths]
    for rows in (NSA_GROUPS * VT_ROWS, NSA_GROUPS * VT_ROWS, 512):
        out_shape.append(jax.ShapeDtypeStruct((B, rows, S), BF16))
        out_specs.append(pl.BlockSpec((1, rows, tm), lambda b, i: (b, 0, i)))
    return pl.pallas_call(
        functools.partial(_inproj_kernel, tm=tm, seq=S, ncast=len(to_bf16)),
        out_shape=out_shape + [jax.ShapeDtypeStruct(a.shape, BF16) for a in to_bf16],
        grid=(B, S // tm),
        in_specs=[
            pl.BlockSpec((1, tm, D), lambda b, i: (b, i, 0)),
            pl.BlockSpec((1, D), lambda b, i: (0, 0)),
            pl.BlockSpec((D, nw), lambda b, i: (0, 0), pipeline_mode=pl.Buffered(1)),
            pl.BlockSpec(kplace.shape, lambda b, i: (0, 0)),
        ] + cast_specs,
        out_specs=out_specs + cast_specs,
        compiler_params=pltpu.CompilerParams(dimension_semantics=("parallel", "parallel"),
                                             vmem_limit_bytes=VMEM_LIMIT),
        name="inproj",
    )(x, g_mix.reshape(1, D), w, kplace, *to_bf16)


def _gelu_tanh(x):
    return 0.5 * x * (1.0 + jnp.tanh(math.sqrt(2.0 / math.pi) * (x + 0.044715 * (x * x * x))))


def _compress_kernel(xk_ref, xv_ref, pe_ref, w1_ref, w2k_ref, w2vt_ref, ok_ref, ovt_ref, *, nsub):
    for kind, x_ref in enumerate((xk_ref, xv_ref)):
        top = jnp.zeros((nsub, 2 * CMP_HIDDEN), F32)
        bot = jnp.zeros((nsub, 2 * CMP_HIDDEN), F32)
        for r in range(CMP_STRIDE):
            xr = x_ref[0, pl.ds(r, nsub, stride=CMP_STRIDE), :]
            top = top + _nn((xr + pe_ref[kind, 0, r:r + 1, :]).astype(BF16), w1_ref[kind, 0, r])
            bot = bot + _nn((xr + pe_ref[kind, 1, r:r + 1, :]).astype(BF16), w1_ref[kind, 1, r])
        hid = _gelu_tanh(top + pltpu.roll(bot, shift=nsub - 1, axis=0)).astype(BF16)
        for g in range(NSA_GROUPS):
            hg = hid[:, g * CMP_HIDDEN:(g + 1) * CMP_HIDDEN]
            if kind == 0:
                ok_ref[0, g] = _nn(hg, w2k_ref[...]).astype(BF16)
            else:
                ovt_ref[0, g] = _nt(w2vt_ref[...], hg).astype(BF16)


def _compress(kv_cmp, pe_k, w1k, w2k, pe_v, w1v, w2v):
    B, S, _ = kv_cmp.shape
    nsub = S // CMP_STRIDE
    G = NSA_GROUPS

    def prep(pe, w1):
        pe_t = jnp.tile(pe.reshape(2, CMP_STRIDE, DH), (1, 1, G))
        a = w1.reshape(2, CMP_STRIDE, DH, CMP_HIDDEN)
        z = jnp.zeros_like(a)
        w = jnp.concatenate([jnp.concatenate([a, z], axis=3), jnp.concatenate([z, a], axis=3)], axis=2)
        return pe_t, w.astype(BF16)

    pek, w1kb = prep(pe_k, w1k)
    pev, w1vb = prep(pe_v, w1v)
    pe = jnp.stack([pek, pev])
    w1 = jnp.stack([w1kb, w1vb])
    w2kp = jnp.concatenate([w2k, jnp.zeros_like(w2k)], axis=1).astype(BF16)
    w2vt = w2v.T.astype(BF16)
    full = lambda shp: pl.BlockSpec(shp, lambda b: (0,) * len(shp))
    return pl.pallas_call(
        functools.partial(_compress_kernel, nsub=nsub),
        out_shape=[jax.ShapeDtypeStruct((B, G, nsub, LANES), BF16),
                   jax.ShapeDtypeStruct((B, G, DH, nsub), BF16)],
        grid=(B,),
        in_specs=[pl.BlockSpec((1, S, G * DH), lambda b: (b, 0, 0)), pl.BlockSpec((1, S, G * DH), lambda b: (b, 0, 1)),
                  full(pe.shape), full(w1.shape), full((CMP_HIDDEN, LANES)), full((DH, CMP_HIDDEN))],
        out_specs=[pl.BlockSpec((1, G, nsub, LANES), lambda b: (b, 0, 0, 0)),
                   pl.BlockSpec((1, G, DH, nsub), lambda b: (b, 0, 0, 0))],
        compiler_params=pltpu.CompilerParams(dimension_semantics=("parallel",), vmem_limit_bytes=VMEM_LIMIT),
        name="compress",
    )(kv_cmp, kv_cmp, pe, w1, w2kp, w2vt)


def _bias_kernel(tbl_ref, bkn_ref, bkc_ref, near_ref, cmpb_ref):
    g = pl.program_id(0)
    for h in range(HPG):
        hd = g * HPG + h

        def lookup(bk):
            acc = jnp.full(bk.shape, NEG, F32)
            for b in range(REL_BUCKETS):
                acc = jnp.where(bk == b, tbl_ref[hd, b], acc)
            return acc

        vn = lookup(bkn_ref[...])
        near_ref[0, :, h * QT:(h + 1) * QT] = jnp.where(vn > 0.5 * NEG, (vn - tbl_ref[hd, REL_BUCKETS - 1]) * LOG2E, NEG)
        vc = lookup(bkc_ref[...])
        cmpb_ref[0, :, h * QT:(h + 1) * QT] = jnp.where(vc > 0.5 * NEG, vc * LOG2E, NEG)


def _nsa_bias_tables(rel_bias, seq):
    ql = np.arange(QT)
    ncmp = seq // CMP_STRIDE

    def buckets(rel):
        return jnp.asarray(np.where(rel >= 0, _t5_bucket_np(rel), -1).astype(np.int32))

    nk = SLC_CHUNK_TILES * QT
    bkn = buckets(ql[None, :] + nk - QT - np.arange(nk)[:, None])
    y = np.arange(2 * ncmp)
    bkc = buckets(ql[None, :] - CMP_STRIDE * (y[:, None] - ncmp) - (CMP_BLOCK - 1))
    nql = HPG * QT
    return pl.pallas_call(
        _bias_kernel,
        out_shape=[jax.ShapeDtypeStruct((NSA_GROUPS, nk, nql), F32),
                   jax.ShapeDtypeStruct((NSA_GROUPS, 2 * ncmp, nql), F32)],
        grid=(NSA_GROUPS,),
        in_specs=[pl.BlockSpec(memory_space=pltpu.SMEM),
                  pl.BlockSpec((nk, QT), lambda g: (0, 0)),
                  pl.BlockSpec((2 * ncmp, QT), lambda g: (0, 0))],
        out_specs=[pl.BlockSpec((1, nk, nql), lambda g: (g, 0, 0)),
                   pl.BlockSpec((1, 2 * ncmp, nql), lambda g: (g, 0, 0))],
        compiler_params=pltpu.CompilerParams(dimension_semantics=("parallel",)),
        name="t5bias",
    )(rel_bias.T, bkn, bkc)


def _nsa_kernel(q_ref, kslc_ref, vtslc_ref, kwin_ref, vtwin_ref, kcb_ref, vcbt_ref, misc_ref,
                near_ref, cmpb_ref, wmask_ref, ovt_ref, eye_ref, eye4_ref,
                out_ref, qaug_s, s0_s, s1_s, *, ncmp, nslc):
    G = NSA_GROUPS
    qts = [NSA_TILES * pl.program_id(1) + t for t in range(NSA_TILES)]
    chains = [(t, g) for t in range(NSA_TILES) for g in range(G)]
    nql = HPG * QT
    ck = SLC_CHUNK_TILES * QT
    wk = 2 * QT

    def flash(carry, s, vt_chunk):
        m, acc = carry
        m_new = jnp.maximum(m, jnp.max(s, axis=0, keepdims=True))
        alpha = jnp.exp2(m - m_new)
        p = jnp.exp2((s - m_new).astype(BF16))
        return m_new, alpha * acc + _nn(vt_chunk, p)

    def finish(carry):
        m, acc = carry
        return acc[:DH] * (1.0 / acc[DH:DH + 1])

    init = (jnp.full((1, nql), NEG, F32), jnp.zeros((VT_ROWS, nql), F32))
    ns = [pl.multiple_of(QT * qt, QT) for qt in qts]
    nw = [pl.multiple_of(QT * qt + 3 * QT, QT) for qt in qts]
    off = [pl.multiple_of(ncmp - (QT // CMP_STRIDE) * qt, SUBLANES) for qt in qts]
    lane = lax.broadcasted_iota(jnp.int32, (nql, LANES), 1)
    win_aug = jnp.where(lane >= DH, -BIG, 0.0).astype(BF16)
    jidx = lax.broadcasted_iota(jnp.int32, (nslc, QT), 0)
    tq = [qt * QT + lax.broadcasted_iota(jnp.int32, (nslc, QT), 1) for qt in qts]
    forced = [(jidx == 0) | (jidx == tq[t] // SLC_BLOCK) | (jidx == tq[t] // SLC_BLOCK - 1) for t in range(NSA_TILES)]
    future = [jidx * SLC_BLOCK > tq[t] for t in range(NSA_TILES)]
    sub = lax.broadcasted_iota(jnp.int32, (SUBLANES, QT), 0)
    ones_lo = jnp.ones((DH, QT), F32)
    eye = eye_ref[...]
    ovt = ovt_ref[...]
    ngrp = nslc // SUBLANES

    def aug(sel01):
        rows = [ones_lo, sel01]
        if LANES - DH - nslc:
            rows.append(jnp.ones((LANES - DH - nslc, QT), F32))
        m01 = _nt(eye, jnp.concatenate(rows, axis=0).astype(BF16))
        return jnp.concatenate([((m01 - 1.0) * BIG).astype(BF16)] * HPG, axis=0)

    cs = range(len(chains))
    kl = [slice(g * LANES, (g + 1) * LANES) for _, g in chains]
    vr = [slice(g * VT_ROWS, (g + 1) * VT_ROWS) for _, g in chains]
    low = lax.broadcasted_iota(jnp.int32, (QT, LANES), 1) < DH

    def head_slot(t, hd):
        pair = q_ref[0, t * QT:(t + 1) * QT, (hd // 2) * LANES:(hd // 2 + 1) * LANES].astype(F32)
        if hd % 2:
            pair = pltpu.roll(pair, shift=DH, axis=1)
        return jnp.where(low, pair, 0.0).astype(BF16)

    q0 = [jnp.concatenate([head_slot(t, g * HPG + h) for h in range(HPG)], axis=0) for t, g in chains]
    qwin = [q0[c] + win_aug for c in cs]

    bc = [cmpb_ref[g, pl.ds(off[t], ncmp), :] for t, g in chains]
    sc = [_nt(kcb_ref[0, chains[c][1]], q0[c]) + bc[c] for c in cs]
    sw = [_nt(kwin_ref[0, pl.ds(ns[chains[c][0]], 3 * QT), kl[c]], qwin[c]) for c in cs]
    sw = [jnp.concatenate([sw[c][:QT] + wmask_ref[...], sw[c][QT:]], axis=0) for c in cs]

    ecb, rden = [], []
    for c in cs:
        mc = jnp.maximum(jnp.max(sc[c], axis=0, keepdims=True), 0.5 * NEG)
        ec = jnp.exp2(sc[c] - mc)
        rden.append(1.0 / jnp.maximum(jnp.sum(ec, axis=0, keepdims=True), jnp.finfo(F32).tiny))
        ecb.append(ec.astype(BF16))
    wcar = [flash(init, sw[c], vtwin_ref[0, vr[c], pl.ds(ns[chains[c][0]], 3 * QT)]) for c in cs]
    o_c = [_nn(vcbt_ref[0, chains[c][1]], ecb[c]) * rden[c] for c in cs]

    imp = []
    for c in cs:
        t = chains[c][0]
        v = _nn(ovt, ecb[c][:, 0:QT]) * rden[c][:, 0:QT]
        for h in range(1, HPG):
            v = v + _nn(ovt, ecb[c][:, h * QT:(h + 1) * QT]) * rden[c][:, h * QT:(h + 1) * QT]
        imp.append(jnp.where(forced[t], 1e30, jnp.where(future[t], -1e30, v)))
    sw = [_nt(kwin_ref[0, pl.ds(nw[chains[c][0]], wk), kl[c]], qwin[c]) + near_ref[chains[c][1], ck - wk:, :]
          for c in cs]

    grp = [[imp[c][SUBLANES * v:SUBLANES * (v + 1)] for v in range(ngrp)] for c in cs]
    cnt = [[jnp.zeros((SUBLANES, QT), F32) for _ in range(ngrp)] for c in cs]
    for jp in range(nslc):
        v0, r0 = divmod(jp, SUBLANES)
        for c in cs:
            row = jnp.broadcast_to(imp[c][jp:jp + 1, :], (SUBLANES, QT))
            for v in range(ngrp):
                if v < v0:
                    inc = jnp.where(row > grp[c][v], 1.0, 0.0)
                elif v > v0:
                    inc = jnp.where(row >= grp[c][v], 1.0, 0.0)
                else:
                    inc = jnp.where(sub > r0, jnp.where(row >= grp[c][v], 1.0, 0.0),
                                    jnp.where(row > grp[c][v], 1.0, 0.0))
                cnt[c][v] = cnt[c][v] + inc
    o_w = [finish(flash(wcar[c], sw[c], vtwin_ref[0, vr[c], pl.ds(nw[chains[c][0]], wk)])) for c in cs]
    for c in cs:
        sel = jnp.concatenate(cnt[c], axis=0) < float(min(SLC_TOPN, nslc))
        sel_near = jnp.where(sel, 1.0, 0.0)
        first_near = (QT // SLC_BLOCK) * (qts[chains[c][0]] - (SLC_CHUNK_TILES - 1))
        sel_far = jnp.where(jidx < first_near, sel_near, 0.0)
        qaug_s[c, 0] = q0[c] + aug(sel_far)
        qaug_s[c, 1] = q0[c] + aug(sel_near)

    gts = [jax.nn.sigmoid(misc_ref[0, t * QT:(t + 1) * QT, :]).T for t in range(NSA_TILES)]
    gates = [[jnp.concatenate([gts[t][16 * g + br * HPG + h:16 * g + br * HPG + h + 1, :] for h in range(HPG)], axis=1)
              for br in range(3)] for t, g in chains]

    n_far = (NSA_TILES * pl.program_id(1)) // SLC_CHUNK_TILES
    n_chunks = n_far + 1

    def rows(c, i):
        return pl.multiple_of(QT * qts[chains[c][0]] - ck * i, QT)

    def scores(c, i):
        i = jnp.minimum(i, n_far)
        return _nt(kslc_ref[0, pl.ds(rows(c, i), ck), kl[c]], qaug_s[c, jnp.where(i == 0, 1, 0)])

    def vt(c, i):
        return vtslc_ref[0, vr[c], pl.ds(rows(c, i), ck)]

    for c in cs:
        s0_s[c] = scores(c, 0) + near_ref[chains[c][1]]

    def pair_body(p, carry):
        i = 2 * p
        carry = list(carry)
        for t in range(NSA_TILES):
            tc = [c for c in cs if chains[c][0] == t]
            for c in tc:
                s1_s[c] = scores(c, i + 1)
            for c in tc:
                carry[c] = flash(carry[c], s0_s[c], vt(c, i))
            for c in tc:
                s0_s[c] = scores(c, i + 2)
            for c in tc:
                carry[c] = flash(carry[c], s1_s[c], vt(c, i + 1))
        return tuple(carry)

    carry = lax.fori_loop(0, n_chunks // 2, pair_body, (init,) * len(chains))
    carry = lax.cond(n_chunks % 2 == 1,
                     lambda cr: tuple(flash(cr[c], s0_s[c], vt(c, n_far)) for c in cs),
                     lambda cr: cr, carry)

    head = lax.broadcasted_iota(jnp.int32, (DH, nql), 1) // QT
    o_s = [finish(carry[c]) for c in cs]
    o = [gates[c][0] * o_c[c] + gates[c][1] * o_s[c] + gates[c][2] * o_w[c] for c in cs]
    ob = [o[c].astype(BF16) for c in cs]
    blocks = [jnp.concatenate([jnp.where(head == h, ob[c], jnp.zeros_like(ob[c])) for h in range(HPG)], axis=0)
              for c in cs]
    y = [_nt(eye4_ref[...], blocks[c]).astype(BF16) for c in cs]
    for c, (t, g) in enumerate(chains):
        out_ref[0, t * QT:(t + 1) * QT, g * HPG * DH:(g + 1) * HPG * DH] = y[c]


def _nsa(q, kslc, vtslc, kwin, vtwin, kcb, vcbt, misc, rel_bias):
    B, S, _ = q.shape
    G = NSA_GROUPS
    nq = S // QT
    ncmp = S // CMP_STRIDE
    nslc = S // SLC_BLOCK
    nql = HPG * QT
    near, cmpb = _nsa_bias_tables(rel_bias, S)
    wmask = jnp.asarray(np.tile(np.where(np.arange(QT)[:, None] > np.arange(QT)[None, :], 0.0, NEG), (1, HPG)), F32)
    kpad_s = jnp.concatenate([jnp.zeros((DH,), BF16), jnp.ones((DH,), BF16)] * G)
    kslc_p = jnp.concatenate([jnp.broadcast_to(kpad_s, (B, SLC_PAD, G * LANES)), kslc], axis=1)
    kwin_p = jnp.concatenate([jnp.broadcast_to(kpad_s, (B, WIN_PAD, G * LANES)), kwin], axis=1)
    vtslc_p = jnp.pad(vtslc, ((0, 0), (0, 0), (SLC_PAD, 0)))
    vtwin_p = jnp.pad(vtwin, ((0, 0), (0, 0), (WIN_PAD, 0)))
    ci = np.arange(ncmp)[None, :] * CMP_STRIDE
    sj = np.arange(nslc)[:, None] * SLC_BLOCK
    ovt = jnp.asarray(((ci < sj + SLC_BLOCK) & (ci + CMP_BLOCK > sj)).astype(np.float32), BF16)
    eye = jnp.eye(QT, dtype=BF16)
    eye4 = jnp.tile(eye, (1, HPG))
    kern = functools.partial(_nsa_kernel, ncmp=ncmp, nslc=nslc)
    nch = NSA_TILES * G
    per_b = lambda shp: pl.BlockSpec(shp, lambda b, i: (b,) + (0,) * (len(shp) - 1))
    full = lambda shp: pl.BlockSpec(shp, lambda b, i: (0,) * len(shp))
    return pl.pallas_call(
        kern,
        out_shape=jax.ShapeDtypeStruct((B, S, NSA_HEADS * DH), BF16),
        grid=(B, nq // NSA_TILES),
        in_specs=[
            pl.BlockSpec((1, NSA_TILES * QT, NSA_HEADS * DH), lambda b, i: (b, i, 0)),
            per_b((1, SLC_PAD + S, G * LANES)),
            per_b((1, G * VT_ROWS, SLC_PAD + S)),
            per_b((1, WIN_PAD + S, G * LANES)),
            per_b((1, G * VT_ROWS, WIN_PAD + S)),
            per_b((1, G, ncmp, LANES)),
            per_b((1, G, DH, ncmp)),
            pl.BlockSpec((1, NSA_TILES * QT, LANES), lambda b, i: (b, i, 0)),
            full((G, SLC_CHUNK_TILES * QT, nql)),
            full((G, 2 * ncmp, nql)),
            full((QT, nql)),
            full((nslc, ncmp)),
            full((QT, QT)),
            full((QT, nql)),
        ],
        out_specs=pl.BlockSpec((1, NSA_TILES * QT, NSA_HEADS * DH), lambda b, i: (b, i, 0)),
        scratch_shapes=[pltpu.VMEM((nch, 2, nql, LANES), BF16), pltpu.VMEM((nch, SLC_CHUNK_TILES * QT, nql), F32),
                        pltpu.VMEM((nch, SLC_CHUNK_TILES * QT, nql), F32)],
        compiler_params=pltpu.CompilerParams(dimension_semantics=("parallel", "arbitrary"),
                                             vmem_limit_bytes=VMEM_LIMIT),
        name="nsa",
    )(q, kslc_p, vtslc_p, kwin_p, vtwin_p, kcb, vcbt, misc, near, cmpb, wmask, ovt, eye, eye4)


GLA_NB = 2


def _gla_kernel(qk_ref, v_ref, vt_ref, misc_ref, r_ref, wal_ref, bal_ref, ng_ref, cum_ref, out_ref,
                state_s, o_s, *, ct, nbatch):
    H, dk, dv, C = GLA_HEADS, GLA_DK, GLA_DV, GLA_CHUNK
    kw = H * dk
    nb = range(nbatch)

    @pl.when(pl.program_id(1) == 0)
    def _():
        state_s[...] = jnp.zeros_like(state_s)

    cum = cum_ref[...]
    q_in, k_in, k_st, decay = [], [], [], []
    for bb in nb:
        z = _nn(misc_ref[bb].astype(BF16), wal_ref[...]) + bal_ref[...]
        log_a = (jnp.minimum(z, 0.0) - jnp.log1p(jnp.exp(-jnp.abs(z)))) * (1.0 / GLA_TAU)
        a1, a2, a3 = _split3(log_a)
        cs = _nn(cum, a1) + _nn(cum, a2) + _nn(cum, a3)
        bc, bl = cs[:ct], cs[ct:]
        q = qk_ref[bb, :, :kw]
        k = qk_ref[bb, :, kw:]
        q_in.append((q * (dk ** -0.5)) * jnp.exp(bc))
        k_in.append((k * jnp.exp(-bc)).astype(BF16))
        k_st.append(k * jnp.exp(bl - bc))
        decay.append(jnp.exp(bl))
    lane_head = lax.broadcasted_iota(jnp.int32, (C, kw), 1) // dk
    rr = lax.broadcasted_iota(jnp.int32, (H * C, C), 0) % C
    cc = lax.broadcasted_iota(jnp.int32, (H * C, C), 1)
    causal = rr >= cc
    pair_row = lax.broadcasted_iota(jnp.int32, (2 * C, kw), 0) // C
    pair_head = lax.broadcasted_iota(jnp.int32, (2 * C, kw), 1) // dk

    for c in range(ct // C):
        r0 = c * C
        p0 = (c // 2) * 2 * C
        for bb in nb:
            qc = q_in[bb][r0:r0 + C]
            qcb = qc.astype(BF16)
            q_heads = jnp.concatenate([jnp.where(lane_head == h, qc, 0.0) for h in range(H)], axis=0).astype(BF16)
            attn = jnp.where(causal, _nt(q_heads, k_in[bb][r0:r0 + C]), 0.0).astype(BF16)
            kst_pair = k_st[bb][p0:p0 + 2 * C]
            dec = decay[bb][r0:r0 + 1]
            for h in range(H):
                st = state_s[bb, h]
                o = _nn(attn[h * C:(h + 1) * C], v_ref[bb, r0:r0 + C, h * dv:(h + 1) * dv])
                o = o + _nt(qcb, st.astype(BF16))
                o_s[bb, r0:r0 + C, h * dv:(h + 1) * dv] = o
                kst_h = jnp.where((pair_row == c % 2) & (pair_head == h), kst_pair, 0.0).astype(BF16)
                state_s[bb, h] = st * dec + _nn(vt_ref[bb, h * dv:(h + 1) * dv, p0:p0 + 2 * C], kst_h)

    for bb in nb:
        for h in range(H):
            oh = o_s[bb, :, h * dv:(h + 1) * dv]
            ms = jnp.mean(oh * oh, axis=-1, keepdims=True)
            r = r_ref[bb, :, h * dv:(h + 1) * dv]
            y = oh * lax.rsqrt(ms + EPS) * ng_ref[:, h * dv:(h + 1) * dv] * (r * jax.nn.sigmoid(r))
            out_ref[bb, :, h * dv:(h + 1) * dv] = y.astype(BF16)


def _gla(qkb, vb, vtb, misc, rb, w_alpha, b_alpha, norm_g, ct=512):
    B, S, _ = qkb.shape
    H, dk, dv, C = GLA_HEADS, GLA_DK, GLA_DV, GLA_CHUNK
    kw, vw = H * dk, H * dv
    nb = GLA_NB if B % GLA_NB == 0 else 1
    wal = jnp.zeros((LANES, kw), F32).at[32:32 + GLA_RANK].set(w_alpha).astype(BF16)
    r = np.arange(ct)
    tri = (r[:, None] // C == r[None, :] // C) & (r[:, None] >= r[None, :])
    tot = r[:, None] // C == r[None, :] // C
    cum = jnp.asarray(np.concatenate([tri, tot], axis=0).astype(np.float32), BF16)
    full = lambda shp: pl.BlockSpec(shp, lambda b, i: (0,) * len(shp))
    return pl.pallas_call(
        functools.partial(_gla_kernel, ct=ct, nbatch=nb),
        out_shape=jax.ShapeDtypeStruct((B, S, vw), BF16),
        grid=(B // nb, S // ct),
        in_specs=[
            pl.BlockSpec((nb, ct, 2 * kw), lambda b, i: (b, i, 0)),
            pl.BlockSpec((nb, ct, vw), lambda b, i: (b, i, 0)),
            pl.BlockSpec((nb, vw, ct), lambda b, i: (b, 0, i)),
            pl.BlockSpec((nb, ct, LANES), lambda b, i: (b, i, 0)),
            pl.BlockSpec((nb, ct, vw), lambda b, i: (b, i, 0)),
            full((LANES, kw)), full((1, kw)), full((1, vw)), full((2 * ct, ct)),
        ],
        out_specs=pl.BlockSpec((nb, ct, vw), lambda b, i: (b, i, 0)),
        scratch_shapes=[pltpu.VMEM((nb, H, dv, kw), F32), pltpu.VMEM((nb, ct, vw), F32)],
        compiler_params=pltpu.CompilerParams(dimension_semantics=("parallel", "arbitrary"),
                                             vmem_limit_bytes=VMEM_LIMIT),
        name="gla",
    )(qkb, vb, vtb, misc, rb, wal, b_alpha.reshape(1, kw), norm_g.reshape(1, vw), cum)


ROUTER_ROWS = 128
EXPERT_ROW0 = 8


def _local_rows(tm):
    return -(-(2 * tm + N_EXPERTS * (SUBLANES - 1)) // LANES) * LANES


def _pack_bf16_pairs(x):
    u = pltpu.bitcast(x, jnp.uint32)
    w = x.shape[1] // 2
    return u[:, :w] | (u[:, w:] >> 16)


def _unpack_bf16_pairs(w):
    xh = pltpu.bitcast(w & jnp.uint32(0xFFFF0000), F32)
    xl = pltpu.bitcast(w << 16, F32)
    return jnp.concatenate([xh, xl], axis=1).astype(BF16)


def _outproj_kernel(ya_ref, yb_ref, mg_ref, x_ref, wa_ref, wb_ref, wo_ref, gf_ref, wrh_ref, wrl_ref, br_ref,
                    tri_ref, ltri_ref, x1_ref, xsl_ref, seg_ref, rw_ref, cnt_ref, carry_s, *, tm, ls):
    D = x_ref.shape[1]

    @pl.when(pl.program_id(0) == 0)
    def _():
        carry_s[...] = jnp.zeros_like(carry_s)

    ma = _nn(ya_ref[...], wa_ref[...])
    mb = _nn(yb_ref[...], wb_ref[...])
    merged = jax.nn.sigmoid(mg_ref[:, :D]) * ma + jax.nn.sigmoid(mg_ref[:, D:]) * mb
    x1 = x_ref[...] + _nn(merged.astype(BF16), wo_ref[...])
    x1_ref[...] = x1
    ms = jnp.mean(x1 * x1, axis=-1, keepdims=True)
    h2 = x1 * lax.rsqrt(ms + EPS) * gf_ref[...]
    hi = h2.astype(BF16)
    lo = (h2 - hi.astype(F32)).astype(BF16)
    lg = _nt(wrh_ref[...], hi) + _nt(wrh_ref[...], lo) + _nt(wrl_ref[...], hi) + br_ref[...]
    row8 = lax.broadcasted_iota(jnp.int32, (SUBLANES, tm), 0)
    gl = jnp.where(row8 < N_GROUPS, lg[0:SUBLANES], NEG)
    gmax = jnp.max(gl, axis=0, keepdims=True)
    g_sel = jnp.min(jnp.where(gl == gmax, row8, SUBLANES), axis=0, keepdims=True)
    g_prob = 1.0 / jnp.sum(jnp.where(row8 < N_GROUPS, jnp.exp(gl - gmax), 0.0), axis=0, keepdims=True)
    e_sel = jnp.zeros((EPG, tm), F32)
    for gi in range(N_GROUPS):
        r0 = EXPERT_ROW0 + gi * EPG
        e_sel = e_sel + jnp.where(g_sel == gi, lg[r0:r0 + EPG], 0.0)
    v1 = jnp.max(e_sel, axis=0, keepdims=True)
    i1 = jnp.min(jnp.where(e_sel == v1, row8, EPG), axis=0, keepdims=True)
    rest = jnp.where(row8 == i1, -jnp.inf, e_sel)
    v2 = jnp.max(rest, axis=0, keepdims=True)
    i2 = jnp.min(jnp.where(rest == v2, row8, EPG), axis=0, keepdims=True)
    t = jnp.exp(v2 - v1)
    w1 = g_prob / (1.0 + t)
    w2 = g_prob * t / (1.0 + t)
    e1 = g_sel * EPG + i1
    e2 = g_sel * EPG + i2

    rowe = lax.broadcasted_iota(jnp.int32, (N_EXPERTS, tm), 0)
    oh1 = rowe == e1
    oh2 = rowe == e2
    oh = jnp.where(oh1, 1.0, 0.0) + jnp.where(oh2, 1.0, 0.0)
    pre = _nn(oh.astype(BF16), tri_ref[...])
    cnt = jnp.sum(oh, axis=1, keepdims=True)
    cnt8 = jnp.floor((cnt + (SUBLANES - 1)) * (1.0 / SUBLANES)) * SUBLANES
    c_b = jnp.broadcast_to(cnt8, (N_EXPERTS, LANES))
    c_hi = jnp.floor(c_b * (1.0 / 16.0))
    c_lo = c_b - 16.0 * c_hi
    base = 16.0 * _nn(ltri_ref[...], c_hi.astype(BF16)) + _nn(ltri_ref[...], c_lo.astype(BF16))
    loc = pre + base[:, 0:1]
    lpos1 = jnp.sum(jnp.where(oh1, loc, 0.0), axis=0, keepdims=True)
    lpos2 = jnp.sum(jnp.where(oh2, loc, 0.0), axis=0, keepdims=True)
    lane = lax.broadcasted_iota(jnp.int32, (N_EXPERTS, LANES), 1)
    seg_ref[0] = jnp.where(lane == 0, base, jnp.where(lane == 1, c_b, carry_s[...])).astype(jnp.int32)
    carry_s[...] = carry_s[...] + c_b
    cnt_ref[...] = carry_s[...].astype(jnp.int32)
    rw_ref[0] = jnp.concatenate([w1, w2, lpos1, lpos2, jnp.zeros((SUBLANES - 4, tm), F32)], axis=0)

    srow = lax.broadcasted_iota(jnp.int32, (ls, tm), 0)
    perm = jnp.where(srow == lpos1.astype(jnp.int32), 1.0, jnp.where(srow == lpos2.astype(jnp.int32), 1.0, 0.0))
    xsorted = _nn(perm.astype(BF16), hi)
    xsl_ref[0] = _pack_bf16_pairs(xsorted)


def _outproj(ya, yb, mg, x, wa, wb, wo, g_ffn, w_rg, b_rg, w_re, b_re, tm=MOE_TM):
    T, D = x.shape
    nt = T // tm
    wr = jnp.zeros((ROUTER_ROWS, D), F32).at[0:N_GROUPS].set(w_rg.T).at[EXPERT_ROW0:EXPERT_ROW0 + N_EXPERTS].set(w_re.T)
    wrh = wr.astype(BF16)
    wrl = (wr - wrh.astype(F32)).astype(BF16)
    br = jnp.zeros((ROUTER_ROWS, 1), F32).at[0:N_GROUPS, 0].set(b_rg).at[EXPERT_ROW0:EXPERT_ROW0 + N_EXPERTS, 0].set(b_re)
    r = np.arange(tm)
    tri = jnp.asarray((r[:, None] < r[None, :]).astype(np.float32), BF16)
    re = np.arange(N_EXPERTS)
    ltri = jnp.asarray((re[None, :] < re[:, None]).astype(np.float32), BF16)
    ls = _local_rows(tm)
    row = lambda n: pl.BlockSpec((tm, n), lambda i: (i, 0))
    full = lambda shp: pl.BlockSpec(shp, lambda i: (0,) * len(shp))
    return pl.pallas_call(
        functools.partial(_outproj_kernel, tm=tm, ls=ls),
        out_shape=[jax.ShapeDtypeStruct((T, D), F32), jax.ShapeDtypeStruct((nt, ls, D // 2), jnp.uint32),
                   jax.ShapeDtypeStruct((nt, N_EXPERTS, LANES), jnp.int32),
                   jax.ShapeDtypeStruct((nt, SUBLANES, tm), F32),
                   jax.ShapeDtypeStruct((N_EXPERTS, LANES), jnp.int32)],
        grid=(nt,),
        in_specs=[row(ya.shape[1]), row(yb.shape[1]), row(2 * D), row(D),
                  full(wa.shape), full(wb.shape), full(wo.shape), full((1, D)),
                  full((ROUTER_ROWS, D)), full((ROUTER_ROWS, D)), full((ROUTER_ROWS, 1)), full((tm, tm)),
                  full((N_EXPERTS, N_EXPERTS))],
        out_specs=[row(D), pl.BlockSpec((1, ls, D // 2), lambda i: (i, 0, 0)),
                   pl.BlockSpec((1, N_EXPERTS, LANES), lambda i: (i, 0, 0)),
                   pl.BlockSpec((1, SUBLANES, tm), lambda i: (i, 0, 0)),
                   full((N_EXPERTS, LANES))],
        scratch_shapes=[pltpu.VMEM((N_EXPERTS, LANES), F32)],
        compiler_params=pltpu.CompilerParams(dimension_semantics=("arbitrary",), vmem_limit_bytes=VMEM_LIMIT),
        name="outproj",
    )(ya, yb, mg, x, wa, wb, wo, g_ffn.reshape(1, D), wrh, wrl, br, tri, ltri)


SEG_FIELDS = 4
SEG_BITS = 7
TILE_BITS = 8
TAIL_BITS = (TE // SUBLANES).bit_length() - 1


def _segment_copies(n8, bits, make_copy, wait):
    def arms(lo, hi, off):
        for bit in reversed(range(lo, hi)):
            rows = SUBLANES << bit
            take = (n8 >> bit) & 1

            @pl.when(take == 1)
            def _(off=off, rows=rows):
                cp = make_copy(off, rows)
                if wait:
                    cp.wait()
                else:
                    cp.start()

            off = off + take * rows

    low = min(4, bits)
    if bits > low:
        @pl.when((n8 >> low) != 0)
        def _():
            arms(low, bits, 0)

    arms(0, low, ((n8 >> low) << low) * SUBLANES)


def _wait_rows(n8, sem, like_src, like_dst):
    def make_copy(off, rows):
        return pltpu.make_async_copy(like_src.at[pl.ds(0, rows), :], like_dst.at[pl.ds(0, rows), :], sem)

    _segment_copies(n8, TILE_BITS, make_copy, True)


def _dispatch_kernel(seg_ref, tail_ref, xsl_ref, xs_ref, zero_s, sem):
    i = pl.program_id(0)

    def segments(wait):
        def body(e, c):
            s0 = (i * N_EXPERTS + e) * SEG_FIELDS
            src0 = pl.multiple_of(seg_ref[s0], SUBLANES)
            dst0 = pl.multiple_of(seg_ref[s0 + 2], SUBLANES)

            def make_copy(off, rows):
                return pltpu.make_async_copy(xsl_ref.at[0, pl.ds(pl.multiple_of(src0 + off, SUBLANES), rows), :],
                                             xs_ref.at[pl.ds(pl.multiple_of(dst0 + off, SUBLANES), rows), :], sem)

            _segment_copies(seg_ref[s0 + 1], SEG_BITS, make_copy, wait)
            return c

        lax.fori_loop(0, N_EXPERTS, body, 0)

    def tails(wait):
        def body(e, c):
            dst0 = pl.multiple_of(tail_ref[2 * e], SUBLANES)

            def make_copy(off, rows):
                return pltpu.make_async_copy(zero_s.at[pl.ds(0, rows), :],
                                             xs_ref.at[pl.ds(pl.multiple_of(dst0 + off, SUBLANES), rows), :], sem)

            _segment_copies(tail_ref[2 * e + 1], TAIL_BITS, make_copy, wait)
            return c

        lax.fori_loop(0, N_EXPERTS, body, 0)

    def unused_tiles(wait):
        def body(t, c):
            cp = pltpu.make_async_copy(zero_s, xs_ref.at[pl.ds(pl.multiple_of(t * TE, TE), TE), :], sem)
            if wait:
                cp.wait()
            else:
                cp.start()
            return c

        lax.fori_loop(tail_ref[2 * N_EXPERTS], xs_ref.shape[0] // TE, body, 0)

    @pl.when(i == 0)
    def _():
        zero_s[...] = jnp.zeros_like(zero_s)
        tails(False)
        unused_tiles(False)
        tails(True)
        unused_tiles(True)

    segments(False)
    _wait_rows(seg_ref[i * N_EXPERTS * SEG_FIELDS + 3], sem, xsl_ref.at[0], xs_ref)


def _dispatch(seg, tail, xsl, n_rows):
    nt, ls, W = xsl.shape
    return pl.pallas_call(
        _dispatch_kernel,
        out_shape=jax.ShapeDtypeStruct((n_rows, W), xsl.dtype),
        grid_spec=pltpu.PrefetchScalarGridSpec(
            num_scalar_prefetch=2,
            grid=(nt,),
            in_specs=[pl.BlockSpec((1, ls, W), lambda i, sg, tl: (i, 0, 0))],
            out_specs=pl.BlockSpec(memory_space=pl.ANY),
            scratch_shapes=[pltpu.VMEM((TE, W), xsl.dtype), pltpu.SemaphoreType.DMA(())],
        ),
        compiler_params=pltpu.CompilerParams(dimension_semantics=("arbitrary",)),
        name="dispatch",
    )(seg, tail, xsl)


def _experts_kernel(te_ref, nv_ref, xs_ref, wg_ref, wu_ref, wd_ref, out_ref):
    i = pl.program_id(0)

    @pl.when(i < nv_ref[0])
    def _():
        x = _unpack_bf16_pairs(xs_ref[...])
        a = _nn(x, wg_ref[0])
        u = _nn(x, wu_ref[0])
        hid = (a * jax.nn.sigmoid(a)) * u
        y = _nn(hid.astype(BF16), wd_ref[0])
        out_ref[...] = _pack_bf16_pairs(y.astype(BF16).astype(F32))

    @pl.when(i >= nv_ref[0])
    def _():
        out_ref[...] = jnp.zeros_like(out_ref)


def _experts(tile_expert, n_valid, xs, w_gate, w_up, w_down):
    n_rows, W = xs.shape
    D = 2 * W
    n_tiles = n_rows // TE
    last = lambda i, nv: jnp.minimum(i, nv[0] - 1)
    return pl.pallas_call(
        _experts_kernel,
        out_shape=jax.ShapeDtypeStruct((n_rows, W), jnp.uint32),
        grid_spec=pltpu.PrefetchScalarGridSpec(
            num_scalar_prefetch=2,
            grid=(n_tiles,),
            in_specs=[pl.BlockSpec((TE, W), lambda i, te, nv: (last(i, nv), 0)),
                      pl.BlockSpec((1, D, EXPERT_FF), lambda i, te, nv: (te[last(i, nv)], 0, 0)),
                      pl.BlockSpec((1, D, EXPERT_FF), lambda i, te, nv: (te[last(i, nv)], 0, 0)),
                      pl.BlockSpec((1, EXPERT_FF, D), lambda i, te, nv: (te[last(i, nv)], 0, 0))],
            out_specs=pl.BlockSpec((TE, W), lambda i, te, nv: (i, 0)),
        ),
        compiler_params=pltpu.CompilerParams(dimension_semantics=("arbitrary",), vmem_limit_bytes=VMEM_LIMIT),
        name="experts",
    )(tile_expert, n_valid, xs, w_gate, w_up, w_down)


def _combine_kernel(seg_ref, ys_ref, x1_ref, rw_ref, gfin_ref, out_ref, buf, sem, *, tm, ls, apply_norm):
    i = pl.program_id(0)
    nt = pl.num_programs(0)

    def segments(tile, wait):
        slot = tile % 2

        def body(e, c):
            s0 = (tile * N_EXPERTS + e) * SEG_FIELDS
            loc0 = pl.multiple_of(seg_ref[s0], SUBLANES)
            glob0 = pl.multiple_of(seg_ref[s0 + 2], SUBLANES)

            def make_copy(off, rows):
                return pltpu.make_async_copy(ys_ref.at[pl.ds(pl.multiple_of(glob0 + off, SUBLANES), rows), :],
                                             buf.at[slot, pl.ds(pl.multiple_of(loc0 + off, SUBLANES), rows), :],
                                             sem.at[slot])

            _segment_copies(seg_ref[s0 + 1], SEG_BITS, make_copy, wait)
            return c

        lax.fori_loop(0, N_EXPERTS, body, 0)

    @pl.when(i == 0)
    def _():
        buf[...] = jnp.zeros_like(buf)
        segments(i, False)

    @pl.when(i + 1 < nt)
    def _():
        segments(i + 1, False)

    _wait_rows(seg_ref[i * N_EXPERTS * SEG_FIELDS + 3], sem.at[i % 2], ys_ref, buf.at[i % 2])
    ysl = _unpack_bf16_pairs(buf[i % 2])
    cols = jnp.concatenate([rw_ref[0], jnp.zeros((LANES - SUBLANES, tm), F32)], axis=0).T
    srow = lax.broadcasted_iota(jnp.int32, (tm, ls), 1)
    y = x1_ref[...]
    for k in range(2):
        pick = jnp.where(srow == cols[:, 2 + k:3 + k].astype(jnp.int32), 1.0, 0.0).astype(BF16)
        y = y + cols[:, k:k + 1] * _nn(pick, ysl)
    if apply_norm:
        ms = jnp.mean(y * y, axis=-1, keepdims=True)
        y = y * lax.rsqrt(ms + EPS) * gfin_ref[...]
    out_ref[...] = y


def _combine(seg, ys, x1, rw, g_final, apply_norm, tm=MOE_TM):
    T, D = x1.shape
    nt = T // tm
    ls = _local_rows(tm)
    return pl.pallas_call(
        functools.partial(_combine_kernel, tm=tm, ls=ls, apply_norm=apply_norm),
        out_shape=jax.ShapeDtypeStruct((T, D), F32),
        grid_spec=pltpu.PrefetchScalarGridSpec(
            num_scalar_prefetch=1,
            grid=(nt,),
            in_specs=[pl.BlockSpec(memory_space=pl.ANY),
                      pl.BlockSpec((tm, D), lambda i, sg: (i, 0)),
                      pl.BlockSpec((1, SUBLANES, tm), lambda i, sg: (i, 0, 0)),
                      pl.BlockSpec((1, D), lambda i, sg: (0, 0))],
            out_specs=pl.BlockSpec((tm, D), lambda i, sg: (i, 0)),
            scratch_shapes=[pltpu.VMEM((2, ls, D // 2), jnp.uint32), pltpu.SemaphoreType.DMA((2,))],
        ),
        compiler_params=pltpu.CompilerParams(dimension_semantics=("arbitrary",), vmem_limit_bytes=VMEM_LIMIT),
        name="combine",
    )(seg, ys, x1, rw, g_final.reshape(1, D))


def _moe_plan(seg, counts, T, tm=MOE_TM):
    nt = T // tm
    n_tiles_max = (2 * T + nt * N_EXPERTS * (SUBLANES - 1)) // TE + N_EXPERTS
    total = counts[:, 0]
    tiles = (total + TE - 1) // TE
    ids = jnp.arange(N_EXPERTS)
    tile_end = jnp.sum(jnp.where(ids[None, :] <= ids[:, None], tiles[None, :], 0), axis=1)
    row0 = (tile_end - tiles) * TE
    n8 = seg[:, :, 1] // SUBLANES
    segtab = jnp.stack([seg[:, :, 0], n8, seg[:, :, 2] + row0[None, :],
                        jnp.broadcast_to(jnp.sum(n8, axis=1, keepdims=True), n8.shape)],
                       axis=-1).reshape(-1).astype(jnp.int32)
    tail = jnp.concatenate([jnp.stack([row0 + total, (tiles * TE - total) // SUBLANES], axis=-1).reshape(-1),
                            tile_end[-1:]]).astype(jnp.int32)
    tile_expert = jnp.minimum(jnp.sum(tile_end[None, :] <= jnp.arange(n_tiles_max)[:, None], axis=1),
                              N_EXPERTS - 1).astype(jnp.int32)
    return segtab, tail, tile_expert, tile_end[-1:].astype(jnp.int32), n_tiles_max * TE


def kernel(x, g_mix, w_in, nsa_pe_k, nsa_cmp_k_w1, nsa_cmp_k_w2, nsa_pe_v, nsa_cmp_v_w1, nsa_cmp_v_w2, rel_bias,
           gla_w_alpha, gla_b_alpha, gla_norm_g, w_branch_a, w_branch_b, w_out, g_ffn, w_router_group,
           b_router_group, w_router_expert, b_router_expert, w_exp_gate, w_exp_up, w_exp_down, g_final):
    B, S, D = x.shape
    T = B * S
    for l in range(w_in.shape[0]):
        w = _pack_inproj_weights(w_in[l])
        (q, kslc, kwin, kvc, misc, qkb, vb, rb, mg, vts, vtw, vtb, wg16, wu16, wd16) = _inproj(
            x, g_mix[l], w, [w_exp_gate[l], w_exp_up[l], w_exp_down[l]])
        kcb, vcbt = _compress(kvc, nsa_pe_k[l], nsa_cmp_k_w1[l], nsa_cmp_k_w2[l],
                              nsa_pe_v[l], nsa_cmp_v_w1[l], nsa_cmp_v_w2[l])
        ya = _nsa(q, kslc, vts, kwin, vtw, kcb, vcbt, misc, rel_bias)
        yb = _gla(qkb, vb, vtb, misc, rb, gla_w_alpha[l], gla_b_alpha[l], gla_norm_g[l])
        x1, xsl, seg, rw, counts = _outproj(
            ya.reshape(T, -1), yb.reshape(T, -1), mg.reshape(T, -1), x.reshape(T, D),
            w_branch_a[l].astype(BF16), w_branch_b[l].astype(BF16), w_out[l].astype(BF16), g_ffn[l],
            w_router_group[l], b_router_group[l], w_router_expert[l], b_router_expert[l])
        segtab, tail, tile_expert, n_valid, n_rows = _moe_plan(seg, counts, T)
        xs = _dispatch(segtab, tail, xsl, n_rows)
        ys = _experts(tile_expert, n_valid, xs, wg16, wu16, wd16)
        last_layer = l == w_in.shape[0] - 1
        x = _combine(segtab, ys, x1, rw, g_final, apply_norm=last_layer).reshape(B, S, D)
    return x
```

```python
import functools
import math

import numpy as np
import jax
import jax.numpy as jnp
from jax import lax
from jax.experimental import pallas as pl
from jax.experimental.pallas import tpu as pltpu

F32 = jnp.float32
BF16 = jnp.bfloat16

NSA_HEADS = 8
NSA_GROUPS = 2
HPG = NSA_HEADS // NSA_GROUPS
DH = 64
CMP_BLOCK = 32
CMP_STRIDE = 16
CMP_HIDDEN = 128
SLC_BLOCK = 64
SLC_TOPN = 16
WINDOW = 512
GLA_HEADS = 4
GLA_DK = 64
GLA_DV = 128
GLA_RANK = 16
GLA_TAU = 16.0
GLA_CHUNK = 64
REL_BUCKETS = 32
REL_MAX_EXACT = REL_BUCKETS // 2
REL_MAX_DIST = 128
N_GROUPS = 4
EPG = 8
N_EXPERTS = N_GROUPS * EPG
EXPERT_FF = 256
EPS = 1e-6

LANES = 128
SUBLANES = 8
VMEM_LIMIT = 56 * 1024 * 1024

LOG2E = math.log2(math.e)
NEG = -1e30
BIG = float(2.0 ** 100)
QT = 128
SLC_CHUNK_TILES = 2
SLC_PAD = (SLC_CHUNK_TILES - 1) * QT
WIN_PAD = 512
TE = 512
MOE_TM = 512
VT_ROWS = DH + 16
NSA_TILES = 2


def _nt(a, b):
    return lax.dot_general(a, b, (((1,), (1,)), ((), ())), preferred_element_type=F32)


def _nn(a, b):
    return jnp.dot(a, b, preferred_element_type=F32)


def _split3(x):
    a = x.astype(BF16)
    r = x - a.astype(F32)
    b = r.astype(BF16)
    c = (r - b.astype(F32)).astype(BF16)
    return a, b, c


def _t5_bucket_np(rel):
    n = np.maximum(rel, 0)
    nf = np.maximum(n, 1).astype(np.float32)
    large = REL_MAX_EXACT + (np.log(nf / np.float32(REL_MAX_EXACT)) / np.float32(math.log(REL_MAX_DIST / REL_MAX_EXACT))
                             * np.float32(REL_BUCKETS - REL_MAX_EXACT)).astype(np.int32)
    return np.where(n < REL_MAX_EXACT, n, np.minimum(large, REL_BUCKETS - 1)).astype(np.int32)


def _inproj_kernel(x_ref, g_ref, wq_ref, wkv_ref, wmisc_ref, wb_ref, wrest_ref, kplace_ref, *refs, tm, seq, ncast):
    cast_in, refs = refs[:ncast], refs[ncast:]
    (oq, okslc, okwin, okv, omisc, oqkb, ovb, orb, omg, ovts, ovtw, ovtb), cast_out = refs[:12], refs[12:]
    for src, dst in zip(cast_in, cast_out):
        dst[...] = src[...].astype(BF16)
    x = x_ref[0]
    ms = jnp.mean(x * x, axis=-1, keepdims=True)
    h = (x * lax.rsqrt(ms + EPS) * g_ref[...]).astype(BF16)

    oq[0] = _nn(h, wq_ref[...]).astype(BF16)
    kv = _nn(h, wkv_ref[...])
    okv[0] = kv[:, 0:256]

    def spread(k):
        return _nn(k.astype(BF16), kplace_ref[...])

    row = lax.broadcasted_iota(jnp.int32, (tm, 256), 0) + pl.program_id(1) * tm
    lane = lax.broadcasted_iota(jnp.int32, (tm, 256), 1) % LANES
    onehot = jnp.where(lane - DH == row // SLC_BLOCK, 1.0, 0.0)
    okslc[0] = (spread(kv[:, 256:384]) + onehot).astype(BF16)
    okwin[0] = spread(kv[:, 512:640]).astype(BF16)

    ones_rows = jnp.where(lax.broadcasted_iota(jnp.int32, (VT_ROWS - DH, tm), 0) == 0, 1.0, 0.0)

    def vt_groups(v):
        t = v.T
        return jnp.concatenate([t[:DH], ones_rows, t[DH:], ones_rows], axis=0).astype(BF16)

    ovts[0] = vt_groups(kv[:, 384:512])
    ovtw[0] = vt_groups(kv[:, 640:768])
    omisc[0] = _nn(h, wmisc_ref[...])
    qkv = _nn(h, wb_ref[...])
    oqkb[0] = qkv[:, 0:512]
    ovb[0] = qkv[:, 512:1024].astype(BF16)
    ovtb[0] = qkv[:, 512:1024].T.astype(BF16)
    orb[0] = _nn(h, wrest_ref[:, 0:512])
    omg[0] = _nn(h, wrest_ref[:, 512:])


def _pack_inproj_weights(w_in):
    o = np.cumsum([0, 512, 128, 128, 128, 128, 128, 128, 24, 256, 256, 512, 16, 512, 1024, 1024])
    (q_a, k_cmp, v_cmp, k_slc, v_slc, k_win, v_win, gate_a, q_b, k_b, v_b, a_b, r_b, mg_a, mg_b) = [
        w_in[:, o[i]:o[i + 1]] for i in range(15)]
    D = w_in.shape[0]
    ga = gate_a.reshape(D, 3, NSA_GROUPS, HPG)
    z4 = jnp.zeros((D, 4), w_in.dtype)
    misc = [ga[:, :, 0, :].reshape(D, 12), z4, ga[:, :, 1, :].reshape(D, 12), z4, a_b,
            jnp.zeros((D, LANES - 48), w_in.dtype)]
    del k_cmp, v_cmp, k_slc, v_slc, k_win, v_win, q_b, k_b, v_b, r_b, mg_a, mg_b
    return [(q_a * (DH ** -0.5 * LOG2E)).astype(BF16), w_in[:, o[1]:o[7]].astype(BF16),
            jnp.concatenate(misc, axis=1).astype(BF16), w_in[:, o[8]:o[11]].astype(BF16),
            w_in[:, o[12]:o[15]].astype(BF16)]


def _inproj(x, g_mix, ws, to_bf16, tm=512):
    B, S, D = x.shape
    nsteps = B * (S // tm)
    cast_specs = [pl.BlockSpec((a.shape[0] // nsteps,) + a.shape[1:],
                               lambda b, i, nd=a.ndim: (b * (S // tm) + i,) + (0,) * (nd - 1)) for a in to_bf16]
    src = np.arange(NSA_GROUPS * DH)
    kplace = np.zeros((NSA_GROUPS * DH, NSA_GROUPS * LANES), np.float32)
    kplace[src, (src // DH) * LANES + src % DH] = 1.0
    kplace = jnp.asarray(kplace, BF16)
    widths = [(512, BF16), (256, BF16), (256, BF16), (256, F32), (128, F32), (512, F32), (512, BF16),
              (512, F32), (2048, F32)]
    out_shape = [jax.ShapeDtypeStruct((B, S, n), dt) for n, dt in widths]
    out_specs = [pl.BlockSpec((1, tm, n), lambda b, i: (b, i, 0)) for n, _ in widths]
    for rows in (NSA_GROUPS * VT_ROWS, NSA_GROUPS * VT_ROWS, 512):
        out_shape.append(jax.ShapeDtypeStruct((B, rows, S), BF16))
        out_specs.append(pl.BlockSpec((1, rows, tm), lambda b, i: (b, 0, i)))
    return pl.pallas_call(
        functools.partial(_inproj_kernel, tm=tm, seq=S, ncast=len(to_bf16)),
        out_shape=out_shape + [jax.ShapeDtypeStruct(a.shape, BF16) for a in to_bf16],
        grid=(B, S // tm),
        in_specs=[
            pl.BlockSpec((1, tm, D), lambda b, i: (b, i, 0)),
            pl.BlockSpec((1, D), lambda b, i: (0, 0)),
        ] + [pl.BlockSpec(w.shape, lambda b, i: (0, 0), pipeline_mode=pl.Buffered(1)) for w in ws] + [
            pl.BlockSpec(kplace.shape, lambda b, i: (0, 0)),
        ] + cast_specs,
        out_specs=out_specs + cast_specs,
        compiler_params=pltpu.CompilerParams(dimension_semantics=("parallel", "parallel"),
                                             vmem_limit_bytes=VMEM_LIMIT),
        name="inproj",
    )(x, g_mix.reshape(1, D), *ws, kplace, *to_bf16)


def _gelu_tanh(x):
    return 0.5 * x * (1.0 + jnp.tanh(math.sqrt(2.0 / math.pi) * (x + 0.044715 * (x * x * x))))


def _compress_kernel(xk_ref, xv_ref, pe_ref, w1_ref, w2k_ref, w2vt_ref, ok_ref, ovt_ref, *, nsub):
    for kind, x_ref in enumerate((xk_ref, xv_ref)):
        top = jnp.zeros((nsub, 2 * CMP_HIDDEN), F32)
        bot = jnp.zeros((nsub, 2 * CMP_HIDDEN), F32)
        for r in range(CMP_STRIDE):
            xr = x_ref[0, pl.ds(r, nsub, stride=CMP_STRIDE), :]
            top = top + _nn((xr + pe_ref[kind, 0, r:r + 1, :]).astype(BF16), w1_ref[kind, 0, r])
            bot = bot + _nn((xr + pe_ref[kind, 1, r:r + 1, :]).astype(BF16), w1_ref[kind, 1, r])
        hid = _gelu_tanh(top + pltpu.roll(bot, shift=nsub - 1, axis=0)).astype(BF16)
        for g in range(NSA_GROUPS):
            hg = hid[:, g * CMP_HIDDEN:(g + 1) * CMP_HIDDEN]
            if kind == 0:
                ok_ref[0, g] = _nn(hg, w2k_ref[...]).astype(BF16)
            else:
                ovt_ref[0, g] = _nt(w2vt_ref[...], hg).astype(BF16)


def _compress(kv_cmp, pe_k, w1k, w2k, pe_v, w1v, w2v):
    B, S, _ = kv_cmp.shape
    nsub = S // CMP_STRIDE
    G = NSA_GROUPS

    def prep(pe, w1):
        pe_t = jnp.tile(pe.reshape(2, CMP_STRIDE, DH), (1, 1, G))
        a = w1.reshape(2, CMP_STRIDE, DH, CMP_HIDDEN)
        z = jnp.zeros_like(a)
        w = jnp.concatenate([jnp.concatenate([a, z], axis=3), jnp.concatenate([z, a], axis=3)], axis=2)
        return pe_t, w.astype(BF16)

    pek, w1kb = prep(pe_k, w1k)
    pev, w1vb = prep(pe_v, w1v)
    pe = jnp.stack([pek, pev])
    w1 = jnp.stack([w1kb, w1vb])
    w2kp = jnp.concatenate([w2k, jnp.zeros_like(w2k)], axis=1).astype(BF16)
    w2vt = w2v.T.astype(BF16)
    full = lambda shp: pl.BlockSpec(shp, lambda b: (0,) * len(shp))
    return pl.pallas_call(
        functools.partial(_compress_kernel, nsub=nsub),
        out_shape=[jax.ShapeDtypeStruct((B, G, nsub, LANES), BF16),
                   jax.ShapeDtypeStruct((B, G, DH, nsub), BF16)],
        grid=(B,),
        in_specs=[pl.BlockSpec((1, S, G * DH), lambda b: (b, 0, 0)), pl.BlockSpec((1, S, G * DH), lambda b: (b, 0, 1)),
                  full(pe.shape), full(w1.shape), full((CMP_HIDDEN, LANES)), full((DH, CMP_HIDDEN))],
        out_specs=[pl.BlockSpec((1, G, nsub, LANES), lambda b: (b, 0, 0, 0)),
                   pl.BlockSpec((1, G, DH, nsub), lambda b: (b, 0, 0, 0))],
        compiler_params=pltpu.CompilerParams(dimension_semantics=("parallel",), vmem_limit_bytes=VMEM_LIMIT),
        name="compress",
    )(kv_cmp, kv_cmp, pe, w1, w2kp, w2vt)


def _bias_kernel(tbl_ref, bkn_ref, bkc_ref, near_ref, cmpb_ref):
    g = pl.program_id(0)
    for h in range(HPG):
        hd = g * HPG + h

        def lookup(bk):
            acc = jnp.full(bk.shape, NEG, F32)
            for b in range(REL_BUCKETS):
                acc = jnp.where(bk == b, tbl_ref[hd, b], acc)
            return acc

        vn = lookup(bkn_ref[...])
        near_ref[0, :, h * QT:(h + 1) * QT] = jnp.where(vn > 0.5 * NEG, (vn - tbl_ref[hd, REL_BUCKETS - 1]) * LOG2E, NEG)
        vc = lookup(bkc_ref[...])
        cmpb_ref[0, :, h * QT:(h + 1) * QT] = jnp.where(vc > 0.5 * NEG, vc * LOG2E, NEG)


def _nsa_bias_tables(rel_bias, seq):
    ql = np.arange(QT)
    ncmp = seq // CMP_STRIDE

    def buckets(rel):
        return jnp.asarray(np.where(rel >= 0, _t5_bucket_np(rel), -1).astype(np.int32))

    nk = SLC_CHUNK_TILES * QT
    bkn = buckets(ql[None, :] + nk - QT - np.arange(nk)[:, None])
    y = np.arange(2 * ncmp)
    bkc = buckets(ql[None, :] - CMP_STRIDE * (y[:, None] - ncmp) - (CMP_BLOCK - 1))
    nql = HPG * QT
    return pl.pallas_call(
        _bias_kernel,
        out_shape=[jax.ShapeDtypeStruct((NSA_GROUPS, nk, nql), F32),
                   jax.ShapeDtypeStruct((NSA_GROUPS, 2 * ncmp, nql), F32)],
        grid=(NSA_GROUPS,),
        in_specs=[pl.BlockSpec(memory_space=pltpu.SMEM),
                  pl.BlockSpec((nk, QT), lambda g: (0, 0)),
                  pl.BlockSpec((2 * ncmp, QT), lambda g: (0, 0))],
        out_specs=[pl.BlockSpec((1, nk, nql), lambda g: (g, 0, 0)),
                   pl.BlockSpec((1, 2 * ncmp, nql), lambda g: (g, 0, 0))],
        compiler_params=pltpu.CompilerParams(dimension_semantics=("parallel",)),
        name="t5bias",
    )(rel_bias.T, bkn, bkc)


def _nsa_kernel(q_ref, kslc_ref, vtslc_ref, kwin_ref, vtwin_ref, kcb_ref, vcbt_ref, misc_ref,
                near_ref, cmpb_ref, wmask_ref, ovt_ref, eye_ref, eye4_ref,
                out_ref, qaug_s, s0_s, s1_s, *, ncmp, nslc):
    G = NSA_GROUPS
    qts = [NSA_TILES * pl.program_id(1) + t for t in range(NSA_TILES)]
    chains = [(t, g) for t in range(NSA_TILES) for g in range(G)]
    nql = HPG * QT
    ck = SLC_CHUNK_TILES * QT
    wk = 2 * QT

    def flash(carry, s, vt_chunk):
        m, acc = carry
        m_new = jnp.maximum(m, jnp.max(s, axis=0, keepdims=True))
        alpha = jnp.exp2(m - m_new)
        p = jnp.exp2((s - m_new).astype(BF16))
        return m_new, alpha * acc + _nn(vt_chunk, p)

    def finish(carry):
        m, acc = carry
        return acc[:DH] * (1.0 / acc[DH:DH + 1])

    init = (jnp.full((1, nql), NEG, F32), jnp.zeros((VT_ROWS, nql), F32))
    ns = [pl.multiple_of(QT * qt, QT) for qt in qts]
    nw = [pl.multiple_of(QT * qt + 3 * QT, QT) for qt in qts]
    off = [pl.multiple_of(ncmp - (QT // CMP_STRIDE) * qt, SUBLANES) for qt in qts]
    lane = lax.broadcasted_iota(jnp.int32, (nql, LANES), 1)
    win_aug = jnp.where(lane >= DH, -BIG, 0.0).astype(BF16)
    jidx = lax.broadcasted_iota(jnp.int32, (nslc, QT), 0)
    tq = [qt * QT + lax.broadcasted_iota(jnp.int32, (nslc, QT), 1) for qt in qts]
    forced = [(jidx == 0) | (jidx == tq[t] // SLC_BLOCK) | (jidx == tq[t] // SLC_BLOCK - 1) for t in range(NSA_TILES)]
    future = [jidx * SLC_BLOCK > tq[t] for t in range(NSA_TILES)]
    sub = lax.broadcasted_iota(jnp.int32, (SUBLANES, QT), 0)
    ones_lo = jnp.ones((DH, QT), F32)
    eye = eye_ref[...]
    ovt = ovt_ref[...]
    ngrp = nslc // SUBLANES

    def aug(sel01):
        rows = [ones_lo, sel01]
        if LANES - DH - nslc:
            rows.append(jnp.ones((LANES - DH - nslc, QT), F32))
        m01 = _nt(eye, jnp.concatenate(rows, axis=0).astype(BF16))
        return jnp.concatenate([((m01 - 1.0) * BIG).astype(BF16)] * HPG, axis=0)

    cs = range(len(chains))
    kl = [slice(g * LANES, (g + 1) * LANES) for _, g in chains]
    vr = [slice(g * VT_ROWS, (g + 1) * VT_ROWS) for _, g in chains]
    low = lax.broadcasted_iota(jnp.int32, (QT, LANES), 1) < DH

    def head_slot(t, hd):
        pair = q_ref[0, t * QT:(t + 1) * QT, (hd // 2) * LANES:(hd // 2 + 1) * LANES].astype(F32)
        if hd % 2:
            pair = pltpu.roll(pair, shift=DH, axis=1)
        return jnp.where(low, pair, 0.0).astype(BF16)

    q0 = [jnp.concatenate([head_slot(t, g * HPG + h) for h in range(HPG)], axis=0) for t, g in chains]
    qwin = [q0[c] + win_aug for c in cs]

    bc = [cmpb_ref[g, pl.ds(off[t], ncmp), :] for t, g in chains]
    sc = [_nt(kcb_ref[0, chains[c][1]], q0[c]) + bc[c] for c in cs]
    sw = [_nt(kwin_ref[0, pl.ds(ns[chains[c][0]], 3 * QT), kl[c]], qwin[c]) for c in cs]
    sw = [jnp.concatenate([sw[c][:QT] + wmask_ref[...], sw[c][QT:]], axis=0) for c in cs]

    ecb, rden = [], []
    for c in cs:
        mc = jnp.maximum(jnp.max(sc[c], axis=0, keepdims=True), 0.5 * NEG)
        ec = jnp.exp2(sc[c] - mc)
        rden.append(1.0 / jnp.maximum(jnp.sum(ec, axis=0, keepdims=True), jnp.finfo(F32).tiny))
        ecb.append(ec.astype(BF16))
    wcar = [flash(init, sw[c], vtwin_ref[0, vr[c], pl.ds(ns[chains[c][0]], 3 * QT)]) for c in cs]
    o_c = [_nn(vcbt_ref[0, chains[c][1]], ecb[c]) * rden[c] for c in cs]

    imp = []
    for c in cs:
        t = chains[c][0]
        v = _nn(ovt, ecb[c][:, 0:QT]) * rden[c][:, 0:QT]
        for h in range(1, HPG):
            v = v + _nn(ovt, ecb[c][:, h * QT:(h + 1) * QT]) * rden[c][:, h * QT:(h + 1) * QT]
        imp.append(jnp.where(forced[t], 1e30, jnp.where(future[t], -1e30, v)))
    sw = [_nt(kwin_ref[0, pl.ds(nw[chains[c][0]], wk), kl[c]], qwin[c]) + near_ref[chains[c][1], ck - wk:, :]
          for c in cs]

    grp = [[imp[c][SUBLANES * v:SUBLANES * (v + 1)] for v in range(ngrp)] for c in cs]
    cnt = [[jnp.zeros((SUBLANES, QT), F32) for _ in range(ngrp)] for c in cs]
    for jp in range(nslc):
        v0, r0 = divmod(jp, SUBLANES)
        for c in cs:
            row = jnp.broadcast_to(imp[c][jp:jp + 1, :], (SUBLANES, QT))
            for v in range(ngrp):
                if v < v0:
                    inc = jnp.where(row > grp[c][v], 1.0, 0.0)
                elif v > v0:
                    inc = jnp.where(row >= grp[c][v], 1.0, 0.0)
                else:
                    inc = jnp.where(sub > r0, jnp.where(row >= grp[c][v], 1.0, 0.0),
                                    jnp.where(row > grp[c][v], 1.0, 0.0))
                cnt[c][v] = cnt[c][v] + inc
    o_w = [finish(flash(wcar[c], sw[c], vtwin_ref[0, vr[c], pl.ds(nw[chains[c][0]], wk)])) for c in cs]
    for c in cs:
        sel = jnp.concatenate(cnt[c], axis=0) < float(min(SLC_TOPN, nslc))
        sel_near = jnp.where(sel, 1.0, 0.0)
        first_near = (QT // SLC_BLOCK) * (qts[chains[c][0]] - (SLC_CHUNK_TILES - 1))
        sel_far = jnp.where(jidx < first_near, sel_near, 0.0)
        qaug_s[c, 0] = q0[c] + aug(sel_far)
        qaug_s[c, 1] = q0[c] + aug(sel_near)

    gts = [jax.nn.sigmoid(misc_ref[0, t * QT:(t + 1) * QT, :]).T for t in range(NSA_TILES)]
    gates = [[jnp.concatenate([gts[t][16 * g + br * HPG + h:16 * g + br * HPG + h + 1, :] for h in range(HPG)], axis=1)
              for br in range(3)] for t, g in chains]

    n_far = (NSA_TILES * pl.program_id(1)) // SLC_CHUNK_TILES
    n_chunks = n_far + 1

    def rows(c, i):
        return pl.multiple_of(QT * qts[chains[c][0]] - ck * i, QT)

    def scores(c, i):
        i = jnp.minimum(i, n_far)
        return _nt(kslc_ref[0, pl.ds(rows(c, i), ck), kl[c]], qaug_s[c, jnp.where(i == 0, 1, 0)])

    def vt(c, i):
        return vtslc_ref[0, vr[c], pl.ds(rows(c, i), ck)]

    for c in cs:
        s0_s[c] = scores(c, 0) + near_ref[chains[c][1]]

    def pair_body(p, carry):
        i = 2 * p
        carry = list(carry)
        for t in range(NSA_TILES):
            tc = [c for c in cs if chains[c][0] == t]
            for c in tc:
                s1_s[c] = scores(c, i + 1)
            for c in tc:
                carry[c] = flash(carry[c], s0_s[c], vt(c, i))
            for c in tc:
                s0_s[c] = scores(c, i + 2)
            for c in tc:
                carry[c] = flash(carry[c], s1_s[c], vt(c, i + 1))
        return tuple(carry)

    carry = lax.fori_loop(0, n_chunks // 2, pair_body, (init,) * len(chains))
    carry = lax.cond(n_chunks % 2 == 1,
                     lambda cr: tuple(flash(cr[c], s0_s[c], vt(c, n_far)) for c in cs),
                     lambda cr: cr, carry)

    head = lax.broadcasted_iota(jnp.int32, (DH, nql), 1) // QT
    o_s = [finish(carry[c]) for c in cs]
    o = [gates[c][0] * o_c[c] + gates[c][1] * o_s[c] + gates[c][2] * o_w[c] for c in cs]
    ob = [o[c].astype(BF16) for c in cs]
    blocks = [jnp.concatenate([jnp.where(head == h, ob[c], jnp.zeros_like(ob[c])) for h in range(HPG)], axis=0)
              for c in cs]
    y = [_nt(eye4_ref[...], blocks[c]).astype(BF16) for c in cs]
    for c, (t, g) in enumerate(chains):
        out_ref[0, t * QT:(t + 1) * QT, g * HPG * DH:(g + 1) * HPG * DH] = y[c]


def _nsa(q, kslc, vtslc, kwin, vtwin, kcb, vcbt, misc, rel_bias):
    B, S, _ = q.shape
    G = NSA_GROUPS
    nq = S // QT
    ncmp = S // CMP_STRIDE
    nslc = S // SLC_BLOCK
    nql = HPG * QT
    near, cmpb = _nsa_bias_tables(rel_bias, S)
    wmask = jnp.asarray(np.tile(np.where(np.arange(QT)[:, None] > np.arange(QT)[None, :], 0.0, NEG), (1, HPG)), F32)
    kpad_s = jnp.concatenate([jnp.zeros((DH,), BF16), jnp.ones((DH,), BF16)] * G)
    kslc_p = jnp.concatenate([jnp.broadcast_to(kpad_s, (B, SLC_PAD, G * LANES)), kslc], axis=1)
    kwin_p = jnp.concatenate([jnp.broadcast_to(kpad_s, (B, WIN_PAD, G * LANES)), kwin], axis=1)
    vtslc_p = jnp.pad(vtslc, ((0, 0), (0, 0), (SLC_PAD, 0)))
    vtwin_p = jnp.pad(vtwin, ((0, 0), (0, 0), (WIN_PAD, 0)))
    ci = np.arange(ncmp)[None, :] * CMP_STRIDE
    sj = np.arange(nslc)[:, None] * SLC_BLOCK
    ovt = jnp.asarray(((ci < sj + SLC_BLOCK) & (ci + CMP_BLOCK > sj)).astype(np.float32), BF16)
    eye = jnp.eye(QT, dtype=BF16)
    eye4 = jnp.tile(eye, (1, HPG))
    kern = functools.partial(_nsa_kernel, ncmp=ncmp, nslc=nslc)
    nch = NSA_TILES * G
    per_b = lambda shp: pl.BlockSpec(shp, lambda b, i: (b,) + (0,) * (len(shp) - 1))
    full = lambda shp: pl.BlockSpec(shp, lambda b, i: (0,) * len(shp))
    return pl.pallas_call(
        kern,
        out_shape=jax.ShapeDtypeStruct((B, S, NSA_HEADS * DH), BF16),
        grid=(B, nq // NSA_TILES),
        in_specs=[
            pl.BlockSpec((1, NSA_TILES * QT, NSA_HEADS * DH), lambda b, i: (b, i, 0)),
            per_b((1, SLC_PAD + S, G * LANES)),
            per_b((1, G * VT_ROWS, SLC_PAD + S)),
            per_b((1, WIN_PAD + S, G * LANES)),
            per_b((1, G * VT_ROWS, WIN_PAD + S)),
            per_b((1, G, ncmp, LANES)),
            per_b((1, G, DH, ncmp)),
            pl.BlockSpec((1, NSA_TILES * QT, LANES), lambda b, i: (b, i, 0)),
            full((G, SLC_CHUNK_TILES * QT, nql)),
            full((G, 2 * ncmp, nql)),
            full((QT, nql)),
            full((nslc, ncmp)),
            full((QT, QT)),
            full((QT, nql)),
        ],
        out_specs=pl.BlockSpec((1, NSA_TILES * QT, NSA_HEADS * DH), lambda b, i: (b, i, 0)),
        scratch_shapes=[pltpu.VMEM((nch, 2, nql, LANES), BF16), pltpu.VMEM((nch, SLC_CHUNK_TILES * QT, nql), F32),
                        pltpu.VMEM((nch, SLC_CHUNK_TILES * QT, nql), F32)],
        compiler_params=pltpu.CompilerParams(dimension_semantics=("parallel", "arbitrary"),
                                             vmem_limit_bytes=VMEM_LIMIT),
        name="nsa",
    )(q, kslc_p, vtslc_p, kwin_p, vtwin_p, kcb, vcbt, misc, near, cmpb, wmask, ovt, eye, eye4)


GLA_NB = 2


def _gla_kernel(qk_ref, v_ref, vt_ref, misc_ref, r_ref, wal_ref, bal_ref, ng_ref, cum_ref, out_ref,
                state_s, o_s, *, ct, nbatch):
    H, dk, dv, C = GLA_HEADS, GLA_DK, GLA_DV, GLA_CHUNK
    kw = H * dk
    nb = range(nbatch)

    @pl.when(pl.program_id(1) == 0)
    def _():
        state_s[...] = jnp.zeros_like(state_s)

    cum = cum_ref[...]
    q_in, k_in, k_st, decay = [], [], [], []
    for bb in nb:
        z = _nn(misc_ref[bb].astype(BF16), wal_ref[...]) + bal_ref[...]
        log_a = (jnp.minimum(z, 0.0) - jnp.log(1.0 + jnp.exp(-jnp.abs(z)))) * (1.0 / GLA_TAU)
        a1, a2, a3 = _split3(log_a)
        cs = _nn(cum, a1) + _nn(cum, a2) + _nn(cum, a3)
        bc, bl = cs[:ct], cs[ct:]
        q = qk_ref[bb, :, :kw]
        k = qk_ref[bb, :, kw:]
        q_in.append((q * (dk ** -0.5)) * jnp.exp(bc))
        k_in.append((k * jnp.exp(-bc)).astype(BF16))
        k_st.append(k * jnp.exp(bl - bc))
        decay.append(jnp.exp(bl))
    lane_head = lax.broadcasted_iota(jnp.int32, (C, kw), 1) // dk
    rr = lax.broadcasted_iota(jnp.int32, (H * C, C), 0) % C
    cc = lax.broadcasted_iota(jnp.int32, (H * C, C), 1)
    causal = rr >= cc
    pair_row = lax.broadcasted_iota(jnp.int32, (2 * C, kw), 0) // C
    pair_head = lax.broadcasted_iota(jnp.int32, (2 * C, kw), 1) // dk

    for c in range(ct // C):
        r0 = c * C
        p0 = (c // 2) * 2 * C
        for bb in nb:
            qc = q_in[bb][r0:r0 + C]
            qcb = qc.astype(BF16)
            q_heads = jnp.concatenate([jnp.where(lane_head == h, qc, 0.0) for h in range(H)], axis=0).astype(BF16)
            attn = jnp.where(causal, _nt(q_heads, k_in[bb][r0:r0 + C]), 0.0).astype(BF16)
            kst_pair = k_st[bb][p0:p0 + 2 * C]
            dec = decay[bb][r0:r0 + 1]
            for h in range(H):
                st = state_s[bb, h]
                o = _nn(attn[h * C:(h + 1) * C], v_ref[bb, r0:r0 + C, h * dv:(h + 1) * dv])
                o = o + _nt(qcb, st.astype(BF16))
                o_s[bb, r0:r0 + C, h * dv:(h + 1) * dv] = o
                kst_h = jnp.where((pair_row == c % 2) & (pair_head == h), kst_pair, 0.0).astype(BF16)
                state_s[bb, h] = st * dec + _nn(vt_ref[bb, h * dv:(h + 1) * dv, p0:p0 + 2 * C], kst_h)

    for bb in nb:
        for h in range(H):
            oh = o_s[bb, :, h * dv:(h + 1) * dv]
            ms = jnp.mean(oh * oh, axis=-1, keepdims=True)
            r = r_ref[bb, :, h * dv:(h + 1) * dv]
            y = oh * lax.rsqrt(ms + EPS) * ng_ref[:, h * dv:(h + 1) * dv] * (r * jax.nn.sigmoid(r))
            out_ref[bb, :, h * dv:(h + 1) * dv] = y.astype(BF16)


def _gla(qkb, vb, vtb, misc, rb, w_alpha, b_alpha, norm_g, ct=512):
    B, S, _ = qkb.shape
    H, dk, dv, C = GLA_HEADS, GLA_DK, GLA_DV, GLA_CHUNK
    kw, vw = H * dk, H * dv
    nb = GLA_NB if B % GLA_NB == 0 else 1
    wal = jnp.zeros((LANES, kw), F32).at[32:32 + GLA_RANK].set(w_alpha).astype(BF16)
    r = np.arange(ct)
    tri = (r[:, None] // C == r[None, :] // C) & (r[:, None] >= r[None, :])
    tot = r[:, None] // C == r[None, :] // C
    cum = jnp.asarray(np.concatenate([tri, tot], axis=0).astype(np.float32), BF16)
    full = lambda shp: pl.BlockSpec(shp, lambda b, i: (0,) * len(shp))
    return pl.pallas_call(
        functools.partial(_gla_kernel, ct=ct, nbatch=nb),
        out_shape=jax.ShapeDtypeStruct((B, S, vw), BF16),
        grid=(B // nb, S // ct),
        in_specs=[
            pl.BlockSpec((nb, ct, 2 * kw), lambda b, i: (b, i, 0)),
            pl.BlockSpec((nb, ct, vw), lambda b, i: (b, i, 0)),
            pl.BlockSpec((nb, vw, ct), lambda b, i: (b, 0, i)),
            pl.BlockSpec((nb, ct, LANES), lambda b, i: (b, i, 0)),
            pl.BlockSpec((nb, ct, vw), lambda b, i: (b, i, 0)),
            full((LANES, kw)), full((1, kw)), full((1, vw)), full((2 * ct, ct)),
        ],
        out_specs=pl.BlockSpec((nb, ct, vw), lambda b, i: (b, i, 0)),
        scratch_shapes=[pltpu.VMEM((nb, H, dv, kw), F32), pltpu.VMEM((nb, ct, vw), F32)],
        compiler_params=pltpu.CompilerParams(dimension_semantics=("parallel", "arbitrary"),
                                             vmem_limit_bytes=VMEM_LIMIT),
        name="gla",
    )(qkb, vb, vtb, misc, rb, wal, b_alpha.reshape(1, kw), norm_g.reshape(1, vw), cum)


ROUTER_ROWS = 128
EXPERT_ROW0 = 8


def _local_rows(tm):
    return -(-(2 * tm + N_EXPERTS * (SUBLANES - 1)) // LANES) * LANES


def _pack_bf16_pairs(x):
    u = pltpu.bitcast(x, jnp.uint32)
    w = x.shape[1] // 2
    return u[:, :w] | (u[:, w:] >> 16)


def _unpack_bf16_pairs(w):
    xh = pltpu.bitcast(w & jnp.uint32(0xFFFF0000), F32)
    xl = pltpu.bitcast(w << 16, F32)
    return jnp.concatenate([xh, xl], axis=1).astype(BF16)


def _outproj_kernel(ya_ref, yb_ref, mg_ref, x_ref, wa_ref, wb_ref, wo_ref, gf_ref, wrh_ref, wrl_ref, br_ref,
                    tri_ref, ltri_ref, x1_ref, xsl_ref, seg_ref, rw_ref, cnt_ref, carry_s, *, tm, ls):
    D = x_ref.shape[1]

    @pl.when(pl.program_id(0) == 0)
    def _():
        carry_s[...] = jnp.zeros_like(carry_s)

    ma = _nn(ya_ref[...], wa_ref[...])
    mb = _nn(yb_ref[...], wb_ref[...])
    merged = jax.nn.sigmoid(mg_ref[:, :D]) * ma + jax.nn.sigmoid(mg_ref[:, D:]) * mb
    x1 = x_ref[...] + _nn(merged.astype(BF16), wo_ref[...])
    x1_ref[...] = x1
    ms = jnp.mean(x1 * x1, axis=-1, keepdims=True)
    h2 = x1 * lax.rsqrt(ms + EPS) * gf_ref[...]
    hi = h2.astype(BF16)
    lo = (h2 - hi.astype(F32)).astype(BF16)
    lg = _nt(wrh_ref[...], hi) + _nt(wrh_ref[...], lo) + _nt(wrl_ref[...], hi) + br_ref[...]
    row8 = lax.broadcasted_iota(jnp.int32, (SUBLANES, tm), 0)
    gl = jnp.where(row8 < N_GROUPS, lg[0:SUBLANES], NEG)
    gmax = jnp.max(gl, axis=0, keepdims=True)
    g_sel = jnp.min(jnp.where(gl == gmax, row8, SUBLANES), axis=0, keepdims=True)
    g_prob = 1.0 / jnp.sum(jnp.where(row8 < N_GROUPS, jnp.exp(gl - gmax), 0.0), axis=0, keepdims=True)
    e_sel = jnp.zeros((EPG, tm), F32)
    for gi in range(N_GROUPS):
        r0 = EXPERT_ROW0 + gi * EPG
        e_sel = e_sel + jnp.where(g_sel == gi, lg[r0:r0 + EPG], 0.0)
    v1 = jnp.max(e_sel, axis=0, keepdims=True)
    i1 = jnp.min(jnp.where(e_sel == v1, row8, EPG), axis=0, keepdims=True)
    rest = jnp.where(row8 == i1, -jnp.inf, e_sel)
    v2 = jnp.max(rest, axis=0, keepdims=True)
    i2 = jnp.min(jnp.where(rest == v2, row8, EPG), axis=0, keepdims=True)
    t = jnp.exp(v2 - v1)
    w1 = g_prob / (1.0 + t)
    w2 = g_prob * t / (1.0 + t)
    e1 = g_sel * EPG + i1
    e2 = g_sel * EPG + i2

    rowe = lax.broadcasted_iota(jnp.int32, (N_EXPERTS, tm), 0)
    oh1 = rowe == e1
    oh2 = rowe == e2
    oh = jnp.where(oh1, 1.0, 0.0) + jnp.where(oh2, 1.0, 0.0)
    pre = _nn(oh.astype(BF16), tri_ref[...])
    cnt = jnp.sum(oh, axis=1, keepdims=True)
    cnt8 = jnp.floor((cnt + (SUBLANES - 1)) * (1.0 / SUBLANES)) * SUBLANES
    c_b = jnp.broadcast_to(cnt8, (N_EXPERTS, LANES))
    c_hi = jnp.floor(c_b * (1.0 / 16.0))
    c_lo = c_b - 16.0 * c_hi
    base = 16.0 * _nn(ltri_ref[...], c_hi.astype(BF16)) + _nn(ltri_ref[...], c_lo.astype(BF16))
    loc = pre + base[:, 0:1]
    lpos1 = jnp.sum(jnp.where(oh1, loc, 0.0), axis=0, keepdims=True)
    lpos2 = jnp.sum(jnp.where(oh2, loc, 0.0), axis=0, keepdims=True)
    lane = lax.broadcasted_iota(jnp.int32, (N_EXPERTS, LANES), 1)
    seg_ref[0] = jnp.where(lane == 0, base, jnp.where(lane == 1, c_b, carry_s[...])).astype(jnp.int32)
    carry_s[...] = carry_s[...] + c_b
    cnt_ref[...] = carry_s[...].astype(jnp.int32)
    rw_ref[0] = jnp.concatenate([w1, w2, lpos1, lpos2, jnp.zeros((SUBLANES - 4, tm), F32)], axis=0)

    srow = lax.broadcasted_iota(jnp.int32, (ls, tm), 0)
    perm = jnp.where(srow == lpos1.astype(jnp.int32), 1.0, jnp.where(srow == lpos2.astype(jnp.int32), 1.0, 0.0))
    xsorted = _nn(perm.astype(BF16), hi)
    xsl_ref[0] = _pack_bf16_pairs(xsorted)


def _outproj(ya, yb, mg, x, wa, wb, wo, g_ffn, w_rg, b_rg, w_re, b_re, tm=MOE_TM):
    T, D = x.shape
    nt = T // tm
    wr = jnp.zeros((ROUTER_ROWS, D), F32).at[0:N_GROUPS].set(w_rg.T).at[EXPERT_ROW0:EXPERT_ROW0 + N_EXPERTS].set(w_re.T)
    wrh = wr.astype(BF16)
    wrl = (wr - wrh.astype(F32)).astype(BF16)
    br = jnp.zeros((ROUTER_ROWS, 1), F32).at[0:N_GROUPS, 0].set(b_rg).at[EXPERT_ROW0:EXPERT_ROW0 + N_EXPERTS, 0].set(b_re)
    r = np.arange(tm)
    tri = jnp.asarray((r[:, None] < r[None, :]).astype(np.float32), BF16)
    re = np.arange(N_EXPERTS)
    ltri = jnp.asarray((re[None, :] < re[:, None]).astype(np.float32), BF16)
    ls = _local_rows(tm)
    row = lambda n: pl.BlockSpec((tm, n), lambda i: (i, 0))
    full = lambda shp: pl.BlockSpec(shp, lambda i: (0,) * len(shp))
    return pl.pallas_call(
        functools.partial(_outproj_kernel, tm=tm, ls=ls),
        out_shape=[jax.ShapeDtypeStruct((T, D), F32), jax.ShapeDtypeStruct((nt, ls, D // 2), jnp.uint32),
                   jax.ShapeDtypeStruct((nt, N_EXPERTS, LANES), jnp.int32),
                   jax.ShapeDtypeStruct((nt, SUBLANES, tm), F32),
                   jax.ShapeDtypeStruct((N_EXPERTS, LANES), jnp.int32)],
        grid=(nt,),
        in_specs=[row(ya.shape[1]), row(yb.shape[1]), row(2 * D), row(D),
                  full(wa.shape), full(wb.shape), full(wo.shape), full((1, D)),
                  full((ROUTER_ROWS, D)), full((ROUTER_ROWS, D)), full((ROUTER_ROWS, 1)), full((tm, tm)),
                  full((N_EXPERTS, N_EXPERTS))],
        out_specs=[row(D), pl.BlockSpec((1, ls, D // 2), lambda i: (i, 0, 0)),
                   pl.BlockSpec((1, N_EXPERTS, LANES), lambda i: (i, 0, 0)),
                   pl.BlockSpec((1, SUBLANES, tm), lambda i: (i, 0, 0)),
                   full((N_EXPERTS, LANES))],
        scratch_shapes=[pltpu.VMEM((N_EXPERTS, LANES), F32)],
        compiler_params=pltpu.CompilerParams(dimension_semantics=("arbitrary",), vmem_limit_bytes=VMEM_LIMIT),
        name="outproj",
    )(ya, yb, mg, x, wa, wb, wo, g_ffn.reshape(1, D), wrh, wrl, br, tri, ltri)


SEG_FIELDS = 4
SEG_BITS = 7
TILE_BITS = 8
TAIL_BITS = (TE // SUBLANES).bit_length() - 1


def _segment_copies(n8, bits, make_copy, wait):
    def arms(lo, hi, off):
        for bit in reversed(range(lo, hi)):
            rows = SUBLANES << bit
            take = (n8 >> bit) & 1

            @pl.when(take == 1)
            def _(off=off, rows=rows):
                cp = make_copy(off, rows)
                if wait:
                    cp.wait()
                else:
                    cp.start()

            off = off + take * rows

    low = min(4, bits)
    if bits > low:
        @pl.when((n8 >> low) != 0)
        def _():
            arms(low, bits, 0)

    arms(0, low, ((n8 >> low) << low) * SUBLANES)


def _wait_rows(n8, sem, like_src, like_dst):
    def make_copy(off, rows):
        return pltpu.make_async_copy(like_src.at[pl.ds(0, rows), :], like_dst.at[pl.ds(0, rows), :], sem)

    _segment_copies(n8, TILE_BITS, make_copy, True)


def _dispatch_kernel(seg_ref, tail_ref, xsl_ref, xs_ref, zero_s, sem):
    i = pl.program_id(0)

    def segments(wait):
        def body(e, c):
            s0 = (i * N_EXPERTS + e) * SEG_FIELDS
            src0 = pl.multiple_of(seg_ref[s0], SUBLANES)
            dst0 = pl.multiple_of(seg_ref[s0 + 2], SUBLANES)

            def make_copy(off, rows):
                return pltpu.make_async_copy(xsl_ref.at[0, pl.ds(pl.multiple_of(src0 + off, SUBLANES), rows), :],
                                             xs_ref.at[pl.ds(pl.multiple_of(dst0 + off, SUBLANES), rows), :], sem)

            _segment_copies(seg_ref[s0 + 1], SEG_BITS, make_copy, wait)
            return c

        lax.fori_loop(0, N_EXPERTS, body, 0)

    def tails(wait):
        def body(e, c):
            dst0 = pl.multiple_of(tail_ref[2 * e], SUBLANES)

            def make_copy(off, rows):
                return pltpu.make_async_copy(zero_s.at[pl.ds(0, rows), :],
                                             xs_ref.at[pl.ds(pl.multiple_of(dst0 + off, SUBLANES), rows), :], sem)

            _segment_copies(tail_ref[2 * e + 1], TAIL_BITS, make_copy, wait)
            return c

        lax.fori_loop(0, N_EXPERTS, body, 0)

    def unused_tiles(wait):
        def body(t, c):
            cp = pltpu.make_async_copy(zero_s, xs_ref.at[pl.ds(pl.multiple_of(t * TE, TE), TE), :], sem)
            if wait:
                cp.wait()
            else:
                cp.start()
            return c

        lax.fori_loop(tail_ref[2 * N_EXPERTS], xs_ref.shape[0] // TE, body, 0)

    @pl.when(i == 0)
    def _():
        zero_s[...] = jnp.zeros_like(zero_s)
        tails(False)
        unused_tiles(False)
        tails(True)
        unused_tiles(True)

    segments(False)
    _wait_rows(seg_ref[i * N_EXPERTS * SEG_FIELDS + 3], sem, xsl_ref.at[0], xs_ref)


def _dispatch(seg, tail, xsl, n_rows):
    nt, ls, W = xsl.shape
    return pl.pallas_call(
        _dispatch_kernel,
        out_shape=jax.ShapeDtypeStruct((n_rows, W), xsl.dtype),
        grid_spec=pltpu.PrefetchScalarGridSpec(
            num_scalar_prefetch=2,
            grid=(nt,),
            in_specs=[pl.BlockSpec((1, ls, W), lambda i, sg, tl: (i, 0, 0))],
            out_specs=pl.BlockSpec(memory_space=pl.ANY),
            scratch_shapes=[pltpu.VMEM((TE, W), xsl.dtype), pltpu.SemaphoreType.DMA(())],
        ),
        compiler_params=pltpu.CompilerParams(dimension_semantics=("arbitrary",)),
        name="dispatch",
    )(seg, tail, xsl)


def _experts_kernel(te_ref, nv_ref, xs_ref, wg_ref, wu_ref, wd_ref, out_ref):
    i = pl.program_id(0)

    @pl.when(i < nv_ref[0])
    def _():
        x = _unpack_bf16_pairs(xs_ref[...])
        a = _nn(x, wg_ref[0])
        u = _nn(x, wu_ref[0])
        hid = (a * jax.nn.sigmoid(a)) * u
        y = _nn(hid.astype(BF16), wd_ref[0])
        out_ref[...] = _pack_bf16_pairs(y.astype(BF16).astype(F32))

    @pl.when(i >= nv_ref[0])
    def _():
        out_ref[...] = jnp.zeros_like(out_ref)


def _experts(tile_expert, n_valid, xs, w_gate, w_up, w_down):
    n_rows, W = xs.shape
    D = 2 * W
    n_tiles = n_rows // TE
    last = lambda i, nv: jnp.minimum(i, nv[0] - 1)
    return pl.pallas_call(
        _experts_kernel,
        out_shape=jax.ShapeDtypeStruct((n_rows, W), jnp.uint32),
        grid_spec=pltpu.PrefetchScalarGridSpec(
            num_scalar_prefetch=2,
            grid=(n_tiles,),
            in_specs=[pl.BlockSpec((TE, W), lambda i, te, nv: (last(i, nv), 0)),
                      pl.BlockSpec((1, D, EXPERT_FF), lambda i, te, nv: (te[last(i, nv)], 0, 0)),
                      pl.BlockSpec((1, D, EXPERT_FF), lambda i, te, nv: (te[last(i, nv)], 0, 0)),
                      pl.BlockSpec((1, EXPERT_FF, D), lambda i, te, nv: (te[last(i, nv)], 0, 0))],
            out_specs=pl.BlockSpec((TE, W), lambda i, te, nv: (i, 0)),
        ),
        compiler_params=pltpu.CompilerParams(dimension_semantics=("arbitrary",), vmem_limit_bytes=VMEM_LIMIT),
        name="experts",
    )(tile_expert, n_valid, xs, w_gate, w_up, w_down)


def _combine_kernel(seg_ref, ys_ref, x1_ref, rw_ref, gfin_ref, out_ref, buf, sem, *, tm, ls, apply_norm):
    i = pl.program_id(0)
    nt = pl.num_programs(0)

    def segments(tile, wait):
        slot = tile % 2

        def body(e, c):
            s0 = (tile * N_EXPERTS + e) * SEG_FIELDS
            loc0 = pl.multiple_of(seg_ref[s0], SUBLANES)
            glob0 = pl.multiple_of(seg_ref[s0 + 2], SUBLANES)

            def make_copy(off, rows):
                return pltpu.make_async_copy(ys_ref.at[pl.ds(pl.multiple_of(glob0 + off, SUBLANES), rows), :],
                                             buf.at[slot, pl.ds(pl.multiple_of(loc0 + off, SUBLANES), rows), :],
                                             sem.at[slot])

            _segment_copies(seg_ref[s0 + 1], SEG_BITS, make_copy, wait)
            return c

        lax.fori_loop(0, N_EXPERTS, body, 0)

    @pl.when(i == 0)
    def _():
        buf[...] = jnp.zeros_like(buf)
        segments(i, False)

    @pl.when(i + 1 < nt)
    def _():
        segments(i + 1, False)

    _wait_rows(seg_ref[i * N_EXPERTS * SEG_FIELDS + 3], sem.at[i % 2], ys_ref, buf.at[i % 2])
    ysl = _unpack_bf16_pairs(buf[i % 2])
    cols = jnp.concatenate([rw_ref[0], jnp.zeros((LANES - SUBLANES, tm), F32)], axis=0).T
    srow = lax.broadcasted_iota(jnp.int32, (tm, ls), 1)
    y = x1_ref[...]
    for k in range(2):
        pick = jnp.where(srow == cols[:, 2 + k:3 + k].astype(jnp.int32), 1.0, 0.0).astype(BF16)
        y = y + cols[:, k:k + 1] * _nn(pick, ysl)
    if apply_norm:
        ms = jnp.mean(y * y, axis=-1, keepdims=True)
        y = y * lax.rsqrt(ms + EPS) * gfin_ref[...]
    out_ref[...] = y


def _combine(seg, ys, x1, rw, g_final, apply_norm, tm=MOE_TM):
    T, D = x1.shape
    nt = T // tm
    ls = _local_rows(tm)
    return pl.pallas_call(
        functools.partial(_combine_kernel, tm=tm, ls=ls, apply_norm=apply_norm),
        out_shape=jax.ShapeDtypeStruct((T, D), F32),
        grid_spec=pltpu.PrefetchScalarGridSpec(
            num_scalar_prefetch=1,
            grid=(nt,),
            in_specs=[pl.BlockSpec(memory_space=pl.ANY),
                      pl.BlockSpec((tm, D), lambda i, sg: (i, 0)),
                      pl.BlockSpec((1, SUBLANES, tm), lambda i, sg: (i, 0, 0)),
                      pl.BlockSpec((1, D), lambda i, sg: (0, 0))],
            out_specs=pl.BlockSpec((tm, D), lambda i, sg: (i, 0)),
            scratch_shapes=[pltpu.VMEM((2, ls, D // 2), jnp.uint32), pltpu.SemaphoreType.DMA((2,))],
        ),
        compiler_params=pltpu.CompilerParams(dimension_semantics=("arbitrary",), vmem_limit_bytes=VMEM_LIMIT),
        name="combine",
    )(seg, ys, x1, rw, g_final.reshape(1, D))


def _moe_plan(seg, counts, T, tm=MOE_TM):
    nt = T // tm
    n_tiles_max = (2 * T + nt * N_EXPERTS * (SUBLANES - 1)) // TE + N_EXPERTS
    total = counts[:, 0]
    tiles = (total + TE - 1) // TE
    ids = jnp.arange(N_EXPERTS)
    tile_end = jnp.sum(jnp.where(ids[None, :] <= ids[:, None], tiles[None, :], 0), axis=1)
    row0 = (tile_end - tiles) * TE
    n8 = seg[:, :, 1] // SUBLANES
    segtab = jnp.stack([seg[:, :, 0], n8, seg[:, :, 2] + row0[None, :],
                        jnp.broadcast_to(jnp.sum(n8, axis=1, keepdims=True), n8.shape)],
                       axis=-1).reshape(-1).astype(jnp.int32)
    tail = jnp.concatenate([jnp.stack([row0 + total, (tiles * TE - total) // SUBLANES], axis=-1).reshape(-1),
                            tile_end[-1:]]).astype(jnp.int32)
    tile_expert = jnp.minimum(jnp.sum(tile_end[None, :] <= jnp.arange(n_tiles_max)[:, None], axis=1),
                              N_EXPERTS - 1).astype(jnp.int32)
    return segtab, tail, tile_expert, tile_end[-1:].astype(jnp.int32), n_tiles_max * TE


def kernel(x, g_mix, w_in, nsa_pe_k, nsa_cmp_k_w1, nsa_cmp_k_w2, nsa_pe_v, nsa_cmp_v_w1, nsa_cmp_v_w2, rel_bias,
           gla_w_alpha, gla_b_alpha, gla_norm_g, w_branch_a, w_branch_b, w_out, g_ffn, w_router_group,
           b_router_group, w_router_expert, b_router_expert, w_exp_gate, w_exp_up, w_exp_down, g_final):
    B, S, D = x.shape
    T = B * S
    for l in range(w_in.shape[0]):
        w = _pack_inproj_weights(w_in[l])
        (q, kslc, kwin, kvc, misc, qkb, vb, rb, mg, vts, vtw, vtb, wg16, wu16, wd16) = _inproj(
            x, g_mix[l], w, [w_exp_gate[l], w_exp_up[l], w_exp_down[l]])
        kcb, vcbt = _compress(kvc, nsa_pe_k[l], nsa_cmp_k_w1[l], nsa_cmp_k_w2[l],
                              nsa_pe_v[l], nsa_cmp_v_w1[l], nsa_cmp_v_w2[l])
        ya = _nsa(q, kslc, vts, kwin, vtw, kcb, vcbt, misc, rel_bias)
        yb = _gla(qkb, vb, vtb, misc, rb, gla_w_alpha[l], gla_b_alpha[l], gla_norm_g[l])
        x1, xsl, seg, rw, counts = _outproj(
            ya.reshape(T, -1), yb.reshape(T, -1), mg.reshape(T, -1), x.reshape(T, D),
            w_branch_a[l].astype(BF16), w_branch_b[l].astype(BF16), w_out[l].astype(BF16), g_ffn[l],
            w_router_group[l], b_router_group[l], w_router_expert[l], b_router_expert[l])
        segtab, tail, tile_expert, n_valid, n_rows = _moe_plan(seg, counts, T)
        xs = _dispatch(segtab, tail, xsl, n_rows)
        ys = _experts(tile_expert, n_valid, xs, wg16, wu16, wd16)
        last_layer = l == w_in.shape[0] - 1
        x = _combine(segtab, ys, x1, rw, g_final, apply_norm=last_layer).reshape(B, S, D)
    return x
```

```python
import functools
import math

import numpy as np
import jax
import jax.numpy as jnp
from jax import lax
from jax.experimental import pallas as pl
from jax.experimental.pallas import tpu as pltpu

F32 = jnp.float32
BF16 = jnp.bfloat16

NSA_HEADS = 8
NSA_GROUPS = 2
HPG = NSA_HEADS // NSA_GROUPS
DH = 64
CMP_BLOCK = 32
CMP_STRIDE = 16
CMP_HIDDEN = 128
SLC_BLOCK = 64
SLC_TOPN = 16
WINDOW = 512
GLA_HEADS = 4
GLA_DK = 64
GLA_DV = 128
GLA_RANK = 16
GLA_TAU = 16.0
GLA_CHUNK = 64
REL_BUCKETS = 32
REL_MAX_EXACT = REL_BUCKETS // 2
REL_MAX_DIST = 128
N_GROUPS = 4
EPG = 8
N_EXPERTS = N_GROUPS * EPG
EXPERT_FF = 256
EPS = 1e-6

LANES = 128
SUBLANES = 8
VMEM_LIMIT = 56 * 1024 * 1024

LOG2E = math.log2(math.e)
NEG = -1e30
BIG = float(2.0 ** 100)
QT = 128
SLC_CHUNK_TILES = 2
KV_PAD = WINDOW
SLC_OFF = KV_PAD - (SLC_CHUNK_TILES - 1) * QT
WIN_FAR = WINDOW - QT
TE = 512
MOE_TM = 512
VT_ROWS = DH + 16
NSA_TILES = 2


def _nt(a, b):
    return lax.dot_general(a, b, (((1,), (1,)), ((), ())), preferred_element_type=F32)


def _nn(a, b):
    return jnp.dot(a, b, preferred_element_type=F32)


def _split3(x):
    a = x.astype(BF16)
    r = x - a.astype(F32)
    b = r.astype(BF16)
    c = (r - b.astype(F32)).astype(BF16)
    return a, b, c


def _t5_bucket_np(rel):
    n = np.maximum(rel, 0)
    nf = np.maximum(n, 1).astype(np.float32)
    large = REL_MAX_EXACT + (np.log(nf / np.float32(REL_MAX_EXACT)) / np.float32(math.log(REL_MAX_DIST / REL_MAX_EXACT))
                             * np.float32(REL_BUCKETS - REL_MAX_EXACT)).astype(np.int32)
    return np.where(n < REL_MAX_EXACT, n, np.minimum(large, REL_BUCKETS - 1)).astype(np.int32)


def _inproj_tile(x_ref, g_ref, wq_ref, wkv_ref, wmisc_ref, wb_ref, wrest_ref, kplace_ref, *refs, tm, cast_groups,
                 seq_tile):
    ncast = sum(cast_groups)
    cast_in, refs = refs[:ncast], refs[ncast:]
    (oq, okslc, okwin, okv, omisc, oqkb, ovb, orb, omg, ovts, ovtw, ovtb), cast_out = refs[:12], refs[12:]
    first = 0
    for dst, n in zip(cast_out, cast_groups):
        col = 0
        for src in cast_in[first:first + n]:
            dst[:, :, col:col + src.shape[-1]] = src[...].astype(BF16)
            col += src.shape[-1]
        first += n
    x = x_ref[0]
    ms = jnp.mean(x * x, axis=-1, keepdims=True)
    h = (x * lax.rsqrt(ms + EPS) * g_ref[...]).astype(BF16)

    oq[0] = _nt(h, wq_ref[...]).astype(BF16)
    kv = _nt(h, wkv_ref[...])
    okv[0] = kv[:, 0:256]

    def spread(k):
        return _nn(k.astype(BF16), kplace_ref[...])

    row = lax.broadcasted_iota(jnp.int32, (tm, 256), 0) + seq_tile * tm
    lane = lax.broadcasted_iota(jnp.int32, (tm, 256), 1) % LANES
    onehot = jnp.where(lane - DH == row // SLC_BLOCK, 1.0, 0.0)
    okslc[0] = (spread(kv[:, 256:384]) + onehot).astype(BF16)
    okwin[0] = spread(kv[:, 512:640]).astype(BF16)

    ones_rows = jnp.where(lax.broadcasted_iota(jnp.int32, (VT_ROWS - DH, tm), 0) == 0, 1.0, 0.0)

    def vt_groups(v):
        t = v.T
        return jnp.concatenate([t[:DH], ones_rows, t[DH:], ones_rows], axis=0).astype(BF16)

    ovts[0] = vt_groups(kv[:, 384:512])
    ovtw[0] = vt_groups(kv[:, 640:768])
    omisc[0] = _nt(h, wmisc_ref[...])
    qkv = _nt(h, wb_ref[...])
    oqkb[0] = qkv[:, 0:512]
    ovb[0] = qkv[:, 512:1024].astype(BF16)
    ovtb[0] = qkv[:, 512:1024].T.astype(BF16)
    orb[0] = _nt(h, wrest_ref[0:512, :])
    omg[0] = _nt(h, wrest_ref[512:, :])


W_RUNS = (512, 6 * NSA_GROUPS * DH, 3 * NSA_HEADS, 1024, 16, 2560)
W_RUN0 = tuple(int(v) for v in np.cumsum((0,) + W_RUNS))
GATE_ROWS = W_RUNS[2] + W_RUNS[4]


def _load_inproj_weights(wt_hbm, wq, wkv, wmisc, wb, wrest, stage, gates, sem):
    rows = stage.shape[1]
    chunks = []
    for run, dst, dst0, scale in ((0, wq, 0, DH ** -0.5 * LOG2E), (1, wkv, 0, None), (2, gates, 0, None),
                                  (3, wb, 0, None), (4, gates, W_RUNS[2], None), (5, wrest, 0, None)):
        for r in range(0, W_RUNS[run], rows):
            chunks.append((W_RUN0[run] + r, min(rows, W_RUNS[run] - r), dst, dst0 + r, scale))

    def copy(k):
        src0, n = chunks[k][:2]
        return pltpu.make_async_copy(wt_hbm.at[pl.ds(src0, n)], stage.at[k % 2, pl.ds(0, n)], sem.at[k % 2])

    copy(0).start()
    for k, (_, n, dst, dst0, scale) in enumerate(chunks):
        if k + 1 < len(chunks):
            copy(k + 1).start()
        copy(k).wait()
        w = stage[k % 2, 0:n, :]
        if scale is not None:
            w = w * scale
        dst[dst0:dst0 + n, :] = w.astype(dst.dtype)
    g = gates[0:W_RUNS[2], :]
    z4 = jnp.zeros((HPG, g.shape[1]), F32)
    per_group = [g[(br * NSA_GROUPS + grp) * HPG:(br * NSA_GROUPS + grp + 1) * HPG]
                 for grp in range(NSA_GROUPS) for br in range(3)]
    misc = (per_group[0:3] + [z4] + per_group[3:6] + [z4, gates[W_RUNS[2]:GATE_ROWS, :],
            jnp.zeros((LANES - 48, g.shape[1]), F32)])
    wmisc[...] = jnp.concatenate(misc, axis=0).astype(BF16)


def _inproj_kernel(*refs, tm, cast_groups, batch, tiles):
    s = pl.program_id(0)
    ncast = sum(cast_groups)
    x_ref, g_ref, wt_hbm, kplace_ref = refs[:4]
    cast_in = refs[4:4 + ncast]
    outs = refs[4 + ncast:4 + ncast + 12 + len(cast_groups)]
    weights = refs[-8:-3]
    stage, gates, sem = refs[-3:]
    okslc, okwin, ovts, ovtw = outs[1], outs[2], outs[9], outs[10]

    @pl.when(s == 0)
    def _():
        _load_inproj_weights(wt_hbm, *weights, stage, gates, sem)

    @pl.when(s < batch)
    def _():
        lane = lax.broadcasted_iota(jnp.int32, okslc.shape[1:], 1) % LANES
        pad_keys = jnp.where(lane >= DH, 1.0, 0.0).astype(BF16)
        okslc[0] = pad_keys
        okwin[0] = pad_keys
        ovts[0] = jnp.zeros_like(ovts[0])
        ovtw[0] = jnp.zeros_like(ovtw[0])

    @pl.when(s >= batch)
    def _():
        _inproj_tile(x_ref, g_ref, *weights, kplace_ref, *cast_in, *outs, tm=tm, cast_groups=cast_groups,
                     seq_tile=(s - batch) % tiles)


def _inproj(x, g_mix, wt, to_bf16, tm=KV_PAD):
    B, S, D = x.shape
    assert wt.shape == (W_RUN0[-1], D)
    nsb = S // tm
    nsteps = B * nsb
    t_of = lambda s: jnp.maximum(s - B, 0)
    b_of = lambda s: jnp.where(s < B, s, (s - B) // nsb)
    step_block = lambda shp: pl.BlockSpec((shp[0] // nsteps,) + shp[1:], lambda s: (t_of(s), 0, 0))
    cast_in = [a for grp in to_bf16 for a in grp]
    cast_shapes = [grp[0].shape[:2] + (sum(a.shape[2] for a in grp),) for grp in to_bf16]
    src = np.arange(NSA_GROUPS * DH)
    kplace = np.zeros((NSA_GROUPS * DH, NSA_GROUPS * LANES), np.float32)
    kplace[src, (src // DH) * LANES + src % DH] = 1.0
    kplace = jnp.asarray(kplace, BF16)
    widths = [(512, BF16), (256, BF16), (256, BF16), (256, F32), (128, F32), (512, F32), (512, BF16),
              (512, F32), (2048, F32)]
    out_shape = [jax.ShapeDtypeStruct((B, S, n), dt) for n, dt in widths]
    out_specs = [pl.BlockSpec((1, tm, n), lambda s: (t_of(s) // nsb, t_of(s) % nsb, 0)) for n, _ in widths]
    for rows in (NSA_GROUPS * VT_ROWS, NSA_GROUPS * VT_ROWS, 512):
        out_shape.append(jax.ShapeDtypeStruct((B, rows, S), BF16))
        out_specs.append(pl.BlockSpec((1, rows, tm), lambda s: (t_of(s) // nsb, 0, t_of(s) % nsb)))
    pad_or_tile = lambda s: jnp.where(s < B, 0, (s - B) % nsb + 1)
    for k in (1, 2):
        out_shape[k] = jax.ShapeDtypeStruct((B, KV_PAD + S, widths[k][0]), BF16)
        out_specs[k] = pl.BlockSpec((1, tm, widths[k][0]), lambda s: (b_of(s), pad_or_tile(s), 0))
    for k in (9, 10):
        out_shape[k] = jax.ShapeDtypeStruct((B, NSA_GROUPS * VT_ROWS, KV_PAD + S), BF16)
        out_specs[k] = pl.BlockSpec((1, NSA_GROUPS * VT_ROWS, tm), lambda s: (b_of(s), 0, pad_or_tile(s)))
    return pl.pallas_call(
        functools.partial(_inproj_kernel, tm=tm, cast_groups=tuple(len(grp) for grp in to_bf16), batch=B, tiles=nsb),
        out_shape=out_shape + [jax.ShapeDtypeStruct(shp, BF16) for shp in cast_shapes],
        grid=(B + nsteps,),
        in_specs=[
            pl.BlockSpec((1, tm, D), lambda s: (t_of(s) // nsb, t_of(s) % nsb, 0)),
            pl.BlockSpec((1, D), lambda s: (0, 0)),
            pl.BlockSpec(memory_space=pl.ANY),
            pl.BlockSpec(kplace.shape, lambda s: (0, 0)),
        ] + [step_block(a.shape) for a in cast_in],
        out_specs=out_specs + [step_block(shp) for shp in cast_shapes],
        scratch_shapes=[pltpu.VMEM((W_RUNS[0], D), BF16), pltpu.VMEM((W_RUNS[1], D), BF16),
                        pltpu.VMEM((LANES, D), BF16), pltpu.VMEM((W_RUNS[3], D), BF16),
                        pltpu.VMEM((W_RUNS[5], D), BF16), pltpu.VMEM((2, tm, D), F32),
                        pltpu.VMEM((GATE_ROWS, D), F32), pltpu.SemaphoreType.DMA((2,))],
        compiler_params=pltpu.CompilerParams(dimension_semantics=("arbitrary",),
                                             vmem_limit_bytes=VMEM_LIMIT),
        name="inproj",
    )(x, g_mix.reshape(1, D), wt, kplace, *cast_in)


def _gelu_tanh(x):
    return 0.5 * x * (1.0 + jnp.tanh(math.sqrt(2.0 / math.pi) * (x + 0.044715 * (x * x * x))))


def _compress_kernel(xk_ref, xv_ref, pe_ref, w1_ref, w2k_ref, w2vt_ref, ok_ref, ovt_ref, *, nsub):
    for kind, x_ref in enumerate((xk_ref, xv_ref)):
        top = jnp.zeros((nsub, 2 * CMP_HIDDEN), F32)
        bot = jnp.zeros((nsub, 2 * CMP_HIDDEN), F32)
        for r in range(CMP_STRIDE):
            xr = x_ref[0, pl.ds(r, nsub, stride=CMP_STRIDE), :]
            top = top + _nn((xr + pe_ref[kind, 0, r:r + 1, :]).astype(BF16), w1_ref[kind, 0, r])
            bot = bot + _nn((xr + pe_ref[kind, 1, r:r + 1, :]).astype(BF16), w1_ref[kind, 1, r])
        hid = _gelu_tanh(top + pltpu.roll(bot, shift=nsub - 1, axis=0)).astype(BF16)
        for g in range(NSA_GROUPS):
            hg = hid[:, g * CMP_HIDDEN:(g + 1) * CMP_HIDDEN]
            if kind == 0:
                ok_ref[0, g] = _nn(hg, w2k_ref[...]).astype(BF16)
            else:
                ovt_ref[0, g] = _nt(w2vt_ref[...], hg).astype(BF16)


def _compress(kv_cmp, pe_k, w1k, w2k, pe_v, w1v, w2v):
    B, S, _ = kv_cmp.shape
    nsub = S // CMP_STRIDE
    G = NSA_GROUPS

    def prep(pe, w1):
        pe_t = jnp.tile(pe.reshape(2, CMP_STRIDE, DH), (1, 1, G))
        a = w1.reshape(2, CMP_STRIDE, DH, CMP_HIDDEN)
        z = jnp.zeros_like(a)
        w = jnp.concatenate([jnp.concatenate([a, z], axis=3), jnp.concatenate([z, a], axis=3)], axis=2)
        return pe_t, w.astype(BF16)

    pek, w1kb = prep(pe_k, w1k)
    pev, w1vb = prep(pe_v, w1v)
    pe = jnp.stack([pek, pev])
    w1 = jnp.stack([w1kb, w1vb])
    w2kp = jnp.concatenate([w2k, jnp.zeros_like(w2k)], axis=1).astype(BF16)
    w2vt = w2v.T.astype(BF16)
    full = lambda shp: pl.BlockSpec(shp, lambda b: (0,) * len(shp))
    return pl.pallas_call(
        functools.partial(_compress_kernel, nsub=nsub),
        out_shape=[jax.ShapeDtypeStruct((B, G, nsub, LANES), BF16),
                   jax.ShapeDtypeStruct((B, G, DH, nsub), BF16)],
        grid=(B,),
        in_specs=[pl.BlockSpec((1, S, G * DH), lambda b: (b, 0, 0)), pl.BlockSpec((1, S, G * DH), lambda b: (b, 0, 1)),
                  full(pe.shape), full(w1.shape), full((CMP_HIDDEN, LANES)), full((DH, CMP_HIDDEN))],
        out_specs=[pl.BlockSpec((1, G, nsub, LANES), lambda b: (b, 0, 0, 0)),
                   pl.BlockSpec((1, G, DH, nsub), lambda b: (b, 0, 0, 0))],
        compiler_params=pltpu.CompilerParams(dimension_semantics=("parallel",), vmem_limit_bytes=VMEM_LIMIT),
        name="compress",
    )(kv_cmp, kv_cmp, pe, w1, w2kp, w2vt)


def _bias_kernel(tbl_ref, bkn_ref, bkc_ref, near_ref, cmpb_ref):
    g = pl.program_id(0)
    for h in range(HPG):
        hd = g * HPG + h

        def lookup(bk):
            acc = jnp.full(bk.shape, NEG, F32)
            for b in range(REL_BUCKETS):
                acc = jnp.where(bk == b, tbl_ref[hd, b], acc)
            return acc

        vn = lookup(bkn_ref[...])
        near_ref[0, :, h * QT:(h + 1) * QT] = jnp.where(vn > 0.5 * NEG, (vn - tbl_ref[hd, REL_BUCKETS - 1]) * LOG2E, NEG)
        vc = lookup(bkc_ref[...])
        cmpb_ref[0, :, h * QT:(h + 1) * QT] = jnp.where(vc > 0.5 * NEG, vc * LOG2E, NEG)


def _nsa_bias_tables(rel_bias, seq):
    ql = np.arange(QT)
    ncmp = seq // CMP_STRIDE

    def buckets(rel):
        return jnp.asarray(np.where(rel >= 0, _t5_bucket_np(rel), -1).astype(np.int32))

    nk = SLC_CHUNK_TILES * QT
    bkn = buckets(ql[None, :] + nk - QT - np.arange(nk)[:, None])
    y = np.arange(2 * ncmp)
    bkc = buckets(ql[None, :] - CMP_STRIDE * (y[:, None] - ncmp) - (CMP_BLOCK - 1))
    nql = HPG * QT
    return pl.pallas_call(
        _bias_kernel,
        out_shape=[jax.ShapeDtypeStruct((NSA_GROUPS, nk, nql), F32),
                   jax.ShapeDtypeStruct((NSA_GROUPS, 2 * ncmp, nql), F32)],
        grid=(NSA_GROUPS,),
        in_specs=[pl.BlockSpec(memory_space=pltpu.SMEM),
                  pl.BlockSpec((nk, QT), lambda g: (0, 0)),
                  pl.BlockSpec((2 * ncmp, QT), lambda g: (0, 0))],
        out_specs=[pl.BlockSpec((1, nk, nql), lambda g: (g, 0, 0)),
                   pl.BlockSpec((1, 2 * ncmp, nql), lambda g: (g, 0, 0))],
        compiler_params=pltpu.CompilerParams(dimension_semantics=("parallel",)),
        name="t5bias",
    )(rel_bias.T, bkn, bkc)


def _nsa_kernel(q_ref, kslc_ref, vtslc_ref, kwin_ref, vtwin_ref, kcb_ref, vcbt_ref, misc_ref,
                near_ref, cmpb_ref, wmask_ref, ovt_ref, eye_ref, eye4_ref,
                out_ref, qaug_s, s0_s, s1_s, *, ncmp, nslc):
    G = NSA_GROUPS
    qts = [NSA_TILES * pl.program_id(1) + t for t in range(NSA_TILES)]
    chains = [(t, g) for t in range(NSA_TILES) for g in range(G)]
    nql = HPG * QT
    ck = SLC_CHUNK_TILES * QT
    wk = 2 * QT

    def flash(carry, s, vt_chunk):
        m, acc = carry
        m_new = jnp.maximum(m, jnp.max(s, axis=0, keepdims=True))
        alpha = jnp.exp2(m - m_new)
        p = jnp.exp2((s - m_new).astype(BF16))
        return m_new, alpha * acc + _nn(vt_chunk, p)

    def finish(carry):
        m, acc = carry
        return acc[:DH] * (1.0 / acc[DH:DH + 1])

    init = (jnp.full((1, nql), NEG, F32), jnp.zeros((VT_ROWS, nql), F32))
    halves = [slice(k * nql // 2, (k + 1) * nql // 2) for k in range(2)]
    init_halves = tuple((init[0][:, hs], init[1][:, hs]) for hs in halves)
    ns = [pl.multiple_of(QT * qt, QT) for qt in qts]
    nw = [pl.multiple_of(QT * qt + WIN_FAR, QT) for qt in qts]
    off = [pl.multiple_of(ncmp - (QT // CMP_STRIDE) * qt, SUBLANES) for qt in qts]
    lane = lax.broadcasted_iota(jnp.int32, (nql, LANES), 1)
    win_aug = jnp.where(lane >= DH, -BIG, 0.0).astype(BF16)
    jidx = lax.broadcasted_iota(jnp.int32, (nslc, QT), 0)
    tq = [qt * QT + lax.broadcasted_iota(jnp.int32, (nslc, QT), 1) for qt in qts]
    forced = [(jidx == 0) | (jidx == tq[t] // SLC_BLOCK) | (jidx == tq[t] // SLC_BLOCK - 1) for t in range(NSA_TILES)]
    future = [jidx * SLC_BLOCK > tq[t] for t in range(NSA_TILES)]
    sub = lax.broadcasted_iota(jnp.int32, (SUBLANES, QT), 0)
    ones_lo = jnp.ones((DH, QT), F32)
    eye = eye_ref[...]
    ovt = ovt_ref[...]
    ngrp = nslc // SUBLANES

    def aug(sel01):
        rows = [ones_lo, sel01]
        if LANES - DH - nslc:
            rows.append(jnp.ones((LANES - DH - nslc, QT), F32))
        m01 = _nt(eye, jnp.concatenate(rows, axis=0).astype(BF16))
        return jnp.concatenate([((m01 - 1.0) * BIG).astype(BF16)] * HPG, axis=0)

    cs = range(len(chains))
    kl = [slice(g * LANES, (g + 1) * LANES) for _, g in chains]
    vr = [slice(g * VT_ROWS, (g + 1) * VT_ROWS) for _, g in chains]
    low = lax.broadcasted_iota(jnp.int32, (QT, LANES), 1) < DH

    def head_slot(t, hd):
        pair = q_ref[0, t * QT:(t + 1) * QT, (hd // 2) * LANES:(hd // 2 + 1) * LANES].astype(F32)
        if hd % 2:
            pair = pltpu.roll(pair, shift=DH, axis=1)
        return jnp.where(low, pair, 0.0).astype(BF16)

    q0 = [jnp.concatenate([head_slot(t, g * HPG + h) for h in range(HPG)], axis=0) for t, g in chains]
    qwin = [q0[c] + win_aug for c in cs]

    bc = [cmpb_ref[g, pl.ds(off[t], ncmp), :] for t, g in chains]
    sc = [_nt(kcb_ref[0, chains[c][1]], q0[c]) + bc[c] for c in cs]
    sw = [_nt(kwin_ref[0, pl.ds(ns[chains[c][0]], WIN_FAR), kl[c]], qwin[c]) for c in cs]
    sw = [jnp.concatenate([sw[c][:QT] + wmask_ref[...], sw[c][QT:]], axis=0) for c in cs]

    ecb, rden = [], []
    for c in cs:
        mc = jnp.maximum(jnp.max(sc[c], axis=0, keepdims=True), 0.5 * NEG)
        ec = jnp.exp2(sc[c] - mc)
        rden.append(1.0 / jnp.maximum(jnp.sum(ec, axis=0, keepdims=True), jnp.finfo(F32).tiny))
        ecb.append(ec.astype(BF16))
    wcar = [[flash(init_halves[k], sw[c][:, halves[k]], vtwin_ref[0, vr[c], pl.ds(ns[chains[c][0]], WIN_FAR)])
             for k in range(2)] for c in cs]
    o_c = [_nn(vcbt_ref[0, chains[c][1]], ecb[c]) * rden[c] for c in cs]

    imp = []
    for c in cs:
        t = chains[c][0]
        v = _nn(ovt, ecb[c][:, 0:QT]) * rden[c][:, 0:QT]
        for h in range(1, HPG):
            v = v + _nn(ovt, ecb[c][:, h * QT:(h + 1) * QT]) * rden[c][:, h * QT:(h + 1) * QT]
        imp.append(jnp.where(forced[t], 1e30, jnp.where(future[t], -1e30, v)))
    sw = [_nt(kwin_ref[0, pl.ds(nw[chains[c][0]], wk), kl[c]], qwin[c]) + near_ref[chains[c][1], ck - wk:, :]
          for c in cs]

    grp = [[imp[c][SUBLANES * v:SUBLANES * (v + 1)] for v in range(ngrp)] for c in cs]
    cnt = [[jnp.zeros((SUBLANES, QT), F32) for _ in range(ngrp)] for c in cs]
    for jp in range(nslc):
        v0, r0 = divmod(jp, SUBLANES)
        for c in cs:
            row = jnp.broadcast_to(imp[c][jp:jp + 1, :], (SUBLANES, QT))
            for v in range(ngrp):
                if v < v0:
                    inc = jnp.where(row > grp[c][v], 1.0, 0.0)
                elif v > v0:
                    inc = jnp.where(row >= grp[c][v], 1.0, 0.0)
                else:
                    inc = jnp.where(sub > r0, jnp.where(row >= grp[c][v], 1.0, 0.0),
                                    jnp.where(row > grp[c][v], 1.0, 0.0))
                cnt[c][v] = cnt[c][v] + inc
    o_w = [jnp.concatenate([finish(flash(wcar[c][k], sw[c][:, halves[k]], vtwin_ref[0, vr[c], pl.ds(nw[chains[c][0]], wk)]))
                            for k in range(2)], axis=1) for c in cs]
    for c in cs:
        sel = jnp.concatenate(cnt[c], axis=0) < float(min(SLC_TOPN, nslc))
        sel_near = jnp.where(sel, 1.0, 0.0)
        first_near = (QT // SLC_BLOCK) * (qts[chains[c][0]] - (SLC_CHUNK_TILES - 1))
        sel_far = jnp.where(jidx < first_near, sel_near, 0.0)
        qaug_s[c, 0] = q0[c] + aug(sel_far)
        qaug_s[c, 1] = q0[c] + aug(sel_near)

    gts = [jax.nn.sigmoid(misc_ref[0, t * QT:(t + 1) * QT, :]).T for t in range(NSA_TILES)]
    gates = [[jnp.concatenate([gts[t][16 * g + br * HPG + h:16 * g + br * HPG + h + 1, :] for h in range(HPG)], axis=1)
              for br in range(3)] for t, g in chains]

    n_far = (NSA_TILES * pl.program_id(1)) // SLC_CHUNK_TILES
    n_chunks = n_far + 1

    def rows(c, i):
        return pl.multiple_of(QT * qts[chains[c][0]] - ck * i + SLC_OFF, QT)

    def scores(c, i):
        i = jnp.minimum(i, n_far)
        return _nt(kslc_ref[0, pl.ds(rows(c, i), ck), kl[c]], qaug_s[c, jnp.where(i == 0, 1, 0)])

    def vt(c, i):
        return vtslc_ref[0, vr[c], pl.ds(rows(c, i), ck)]

    for c in cs:
        s0_s[c] = scores(c, 0) + near_ref[chains[c][1]]

    def flash_halves(carry_c, s_ref, c, vt_chunk):
        return tuple(flash(carry_c[k], s_ref[c, :, halves[k]], vt_chunk) for k in range(2))

    def pair_body(p, carry):
        i = 2 * p
        carry = list(carry)
        for t in range(NSA_TILES):
            tc = [c for c in cs if chains[c][0] == t]
            for c in tc:
                s1_s[c] = scores(c, i + 1)
            for c in tc:
                carry[c] = flash_halves(carry[c], s0_s, c, vt(c, i))
            for c in tc:
                s0_s[c] = scores(c, i + 2)
            for c in tc:
                carry[c] = flash_halves(carry[c], s1_s, c, vt(c, i + 1))
        return tuple(carry)

    carry = lax.fori_loop(0, n_chunks // 2, pair_body, (init_halves,) * len(chains))
    carry = lax.cond(n_chunks % 2 == 1,
                     lambda cr: tuple(flash_halves(cr[c], s0_s, c, vt(c, n_far)) for c in cs),
                     lambda cr: cr, carry)
    carry = [tuple(jnp.concatenate([h[k] for h in carry[c]], axis=1) for k in range(2)) for c in cs]

    head = lax.broadcasted_iota(jnp.int32, (DH, nql), 1) // QT
    o_s = [finish(carry[c]) for c in cs]
    o = [gates[c][0] * o_c[c] + gates[c][1] * o_s[c] + gates[c][2] * o_w[c] for c in cs]
    ob = [o[c].astype(BF16) for c in cs]
    blocks = [jnp.concatenate([jnp.where(head == h, ob[c], jnp.zeros_like(ob[c])) for h in range(HPG)], axis=0)
              for c in cs]
    y = [_nt(eye4_ref[...], blocks[c]).astype(BF16) for c in cs]
    for c, (t, g) in enumerate(chains):
        out_ref[0, t * QT:(t + 1) * QT, g * HPG * DH:(g + 1) * HPG * DH] = y[c]


def _nsa(q, kslc, vtslc, kwin, vtwin, kcb, vcbt, misc, rel_bias):
    B, S, _ = q.shape
    G = NSA_GROUPS
    nq = S // QT
    ncmp = S // CMP_STRIDE
    nslc = S // SLC_BLOCK
    nql = HPG * QT
    near, cmpb = _nsa_bias_tables(rel_bias, S)
    wmask = jnp.asarray(np.tile(np.where(np.arange(QT)[:, None] > np.arange(QT)[None, :], 0.0, NEG), (1, HPG)), F32)
    ci = np.arange(ncmp)[None, :] * CMP_STRIDE
    sj = np.arange(nslc)[:, None] * SLC_BLOCK
    ovt = jnp.asarray(((ci < sj + SLC_BLOCK) & (ci + CMP_BLOCK > sj)).astype(np.float32), BF16)
    eye = jnp.eye(QT, dtype=BF16)
    eye4 = jnp.tile(eye, (1, HPG))
    kern = functools.partial(_nsa_kernel, ncmp=ncmp, nslc=nslc)
    nch = NSA_TILES * G
    per_b = lambda shp: pl.BlockSpec(shp, lambda b, i: (b,) + (0,) * (len(shp) - 1))
    full = lambda shp: pl.BlockSpec(shp, lambda b, i: (0,) * len(shp))
    return pl.pallas_call(
        kern,
        out_shape=jax.ShapeDtypeStruct((B, S, NSA_HEADS * DH), BF16),
        grid=(B, nq // NSA_TILES),
        in_specs=[
            pl.BlockSpec((1, NSA_TILES * QT, NSA_HEADS * DH), lambda b, i: (b, i, 0)),
            per_b((1, KV_PAD + S, G * LANES)),
            per_b((1, G * VT_ROWS, KV_PAD + S)),
            per_b((1, KV_PAD + S, G * LANES)),
            per_b((1, G * VT_ROWS, KV_PAD + S)),
            per_b((1, G, ncmp, LANES)),
            per_b((1, G, DH, ncmp)),
            pl.BlockSpec((1, NSA_TILES * QT, LANES), lambda b, i: (b, i, 0)),
            full((G, SLC_CHUNK_TILES * QT, nql)),
            full((G, 2 * ncmp, nql)),
            full((QT, nql)),
            full((nslc, ncmp)),
            full((QT, QT)),
            full((QT, nql)),
        ],
        out_specs=pl.BlockSpec((1, NSA_TILES * QT, NSA_HEADS * DH), lambda b, i: (b, i, 0)),
        scratch_shapes=[pltpu.VMEM((nch, 2, nql, LANES), BF16), pltpu.VMEM((nch, SLC_CHUNK_TILES * QT, nql), F32),
                        pltpu.VMEM((nch, SLC_CHUNK_TILES * QT, nql), F32)],
        compiler_params=pltpu.CompilerParams(dimension_semantics=("parallel", "arbitrary"),
                                             vmem_limit_bytes=VMEM_LIMIT),
        name="nsa",
    )(q, kslc, vtslc, kwin, vtwin, kcb, vcbt, misc, near, cmpb, wmask, ovt, eye, eye4)


GLA_NB = 2


def _gla_kernel(qk_ref, v_ref, vt_ref, misc_ref, r_ref, wal_ref, bal_ref, ng_ref, cum_ref, out_ref,
                state_s, o_s, *, ct, nbatch):
    H, dk, dv, C = GLA_HEADS, GLA_DK, GLA_DV, GLA_CHUNK
    kw = H * dk
    nb = range(nbatch)

    @pl.when(pl.program_id(1) == 0)
    def _():
        state_s[...] = jnp.zeros_like(state_s)

    cum = cum_ref[...]
    q_in, k_in, k_st, decay = [], [], [], []
    for bb in nb:
        z = _nn(misc_ref[bb].astype(BF16), wal_ref[...]) + bal_ref[...]
        log_a = (jnp.minimum(z, 0.0) - jnp.log(1.0 + jnp.exp(-jnp.abs(z)))) * (1.0 / GLA_TAU)
        a1, a2, a3 = _split3(log_a)
        cs = _nn(cum, a1) + _nn(cum, a2) + _nn(cum, a3)
        bc, bl = cs[:ct], cs[ct:]
        q = qk_ref[bb, :, :kw]
        k = qk_ref[bb, :, kw:]
        q_in.append((q * (dk ** -0.5)) * jnp.exp(bc))
        k_in.append((k * jnp.exp(-bc)).astype(BF16))
        k_st.append(k * jnp.exp(bl - bc))
        decay.append(jnp.exp(bl))
    lane_head = lax.broadcasted_iota(jnp.int32, (C, kw), 1) // dk
    rr = lax.broadcasted_iota(jnp.int32, (H * C, C), 0) % C
    cc = lax.broadcasted_iota(jnp.int32, (H * C, C), 1)
    causal = rr >= cc
    pair_row = lax.broadcasted_iota(jnp.int32, (2 * C, kw), 0) // C
    pair_head = lax.broadcasted_iota(jnp.int32, (2 * C, kw), 1) // dk

    for c in range(ct // C):
        r0 = c * C
        p0 = (c // 2) * 2 * C
        for bb in nb:
            qc = q_in[bb][r0:r0 + C]
            qcb = qc.astype(BF16)
            q_heads = jnp.concatenate([jnp.where(lane_head == h, qc, 0.0) for h in range(H)], axis=0).astype(BF16)
            attn = jnp.where(causal, _nt(q_heads, k_in[bb][r0:r0 + C]), 0.0).astype(BF16)
            kst_pair = k_st[bb][p0:p0 + 2 * C]
            dec = decay[bb][r0:r0 + 1]
            for h in range(H):
                st = state_s[bb, h]
                o = _nn(attn[h * C:(h + 1) * C], v_ref[bb, r0:r0 + C, h * dv:(h + 1) * dv])
                o = o + _nt(qcb, st.astype(BF16))
                o_s[bb, r0:r0 + C, h * dv:(h + 1) * dv] = o
                kst_h = jnp.where((pair_row == c % 2) & (pair_head == h), kst_pair, 0.0).astype(BF16)
                state_s[bb, h] = st * dec + _nn(vt_ref[bb, h * dv:(h + 1) * dv, p0:p0 + 2 * C], kst_h)

    for bb in nb:
        for h in range(H):
            oh = o_s[bb, :, h * dv:(h + 1) * dv]
            ms = jnp.mean(oh * oh, axis=-1, keepdims=True)
            r = r_ref[bb, :, h * dv:(h + 1) * dv]
            y = oh * lax.rsqrt(ms + EPS) * ng_ref[:, h * dv:(h + 1) * dv] * (r * jax.nn.sigmoid(r))
            out_ref[bb, :, h * dv:(h + 1) * dv] = y.astype(BF16)


def _gla(qkb, vb, vtb, misc, rb, w_alpha, b_alpha, norm_g, ct=512):
    B, S, _ = qkb.shape
    H, dk, dv, C = GLA_HEADS, GLA_DK, GLA_DV, GLA_CHUNK
    kw, vw = H * dk, H * dv
    nb = GLA_NB if B % GLA_NB == 0 else 1
    wal = jnp.zeros((LANES, kw), F32).at[32:32 + GLA_RANK].set(w_alpha).astype(BF16)
    r = np.arange(ct)
    tri = (r[:, None] // C == r[None, :] // C) & (r[:, None] >= r[None, :])
    tot = r[:, None] // C == r[None, :] // C
    cum = jnp.asarray(np.concatenate([tri, tot], axis=0).astype(np.float32), BF16)
    full = lambda shp: pl.BlockSpec(shp, lambda b, i: (0,) * len(shp))
    return pl.pallas_call(
        functools.partial(_gla_kernel, ct=ct, nbatch=nb),
        out_shape=jax.ShapeDtypeStruct((B, S, vw), BF16),
        grid=(B // nb, S // ct),
        in_specs=[
            pl.BlockSpec((nb, ct, 2 * kw), lambda b, i: (b, i, 0)),
            pl.BlockSpec((nb, ct, vw), lambda b, i: (b, i, 0)),
            pl.BlockSpec((nb, vw, ct), lambda b, i: (b, 0, i)),
            pl.BlockSpec((nb, ct, LANES), lambda b, i: (b, i, 0)),
            pl.BlockSpec((nb, ct, vw), lambda b, i: (b, i, 0)),
            full((LANES, kw)), full((1, kw)), full((1, vw)), full((2 * ct, ct)),
        ],
        out_specs=pl.BlockSpec((nb, ct, vw), lambda b, i: (b, i, 0)),
        scratch_shapes=[pltpu.VMEM((nb, H, dv, kw), F32), pltpu.VMEM((nb, ct, vw), F32)],
        compiler_params=pltpu.CompilerParams(dimension_semantics=("parallel", "arbitrary"),
                                             vmem_limit_bytes=VMEM_LIMIT),
        name="gla",
    )(qkb, vb, vtb, misc, rb, wal, b_alpha.reshape(1, kw), norm_g.reshape(1, vw), cum)


ROUTER_ROWS = 48
EXPERT_ROW0 = 8


SEG_ROWS = 16


def _local_rows(tm):
    return -(-(2 * tm + N_EXPERTS * (SEG_ROWS - 1)) // LANES) * LANES


def _outproj_kernel(ya_ref, yb_ref, mg_ref, x_ref, wa_ref, wb_ref, wo_ref, gf_ref, wr_ref, br_ref,
                    tri_ref, ltri_ref, x1_ref, xsl_ref, seg_ref, rw_ref, cnt_ref, carry_s, *, tm, ls):
    D = x_ref.shape[1]

    @pl.when(pl.program_id(0) == 0)
    def _():
        carry_s[...] = jnp.zeros_like(carry_s)

    ma = _nn(ya_ref[...], wa_ref[...])
    mb = _nn(yb_ref[...], wb_ref[...])
    merged = jax.nn.sigmoid(mg_ref[:, :D]) * ma + jax.nn.sigmoid(mg_ref[:, D:]) * mb
    x1 = x_ref[...] + _nn(merged.astype(BF16), wo_ref[...])
    x1_ref[...] = x1
    ms = jnp.mean(x1 * x1, axis=-1, keepdims=True)
    h2 = x1 * lax.rsqrt(ms + EPS) * gf_ref[...]
    hi = h2.astype(BF16)
    lg = _nt(wr_ref[...], hi) + br_ref[...]
    row8 = lax.broadcasted_iota(jnp.int32, (SUBLANES, tm), 0)
    gl = jnp.where(row8 < N_GROUPS, lg[0:SUBLANES], NEG)
    gmax = jnp.max(gl, axis=0, keepdims=True)
    g_sel = jnp.min(jnp.where(gl == gmax, row8, SUBLANES), axis=0, keepdims=True)
    g_prob = 1.0 / jnp.sum(jnp.where(row8 < N_GROUPS, jnp.exp(gl - gmax), 0.0), axis=0, keepdims=True)
    e_sel = jnp.zeros((EPG, tm), F32)
    for gi in range(N_GROUPS):
        r0 = EXPERT_ROW0 + gi * EPG
        e_sel = e_sel + jnp.where(g_sel == gi, lg[r0:r0 + EPG], 0.0)
    v1 = jnp.max(e_sel, axis=0, keepdims=True)
    i1 = jnp.min(jnp.where(e_sel == v1, row8, EPG), axis=0, keepdims=True)
    rest = jnp.where(row8 == i1, -jnp.inf, e_sel)
    v2 = jnp.max(rest, axis=0, keepdims=True)
    i2 = jnp.min(jnp.where(rest == v2, row8, EPG), axis=0, keepdims=True)
    t = jnp.exp(v2 - v1)
    w1 = g_prob / (1.0 + t)
    w2 = g_prob * t / (1.0 + t)
    e1 = g_sel * EPG + i1
    e2 = g_sel * EPG + i2

    rowe = lax.broadcasted_iota(jnp.int32, (N_EXPERTS, tm), 0)
    oh1 = rowe == e1
    oh2 = rowe == e2
    oh = jnp.where(oh1, 1.0, 0.0) + jnp.where(oh2, 1.0, 0.0)
    pre = _nn(oh.astype(BF16), tri_ref[...])
    cnt = jnp.sum(oh, axis=1, keepdims=True)
    padded = jnp.floor((cnt + (SEG_ROWS - 1)) * (1.0 / SEG_ROWS)) * SEG_ROWS
    c_b = jnp.broadcast_to(padded, (N_EXPERTS, LANES))
    c_hi = jnp.floor(c_b * (1.0 / 16.0))
    c_lo = c_b - 16.0 * c_hi
    base = 16.0 * _nn(ltri_ref[...], c_hi.astype(BF16)) + _nn(ltri_ref[...], c_lo.astype(BF16))
    loc = pre + base[:, 0:1]
    lpos1 = jnp.sum(jnp.where(oh1, loc, 0.0), axis=0, keepdims=True)
    lpos2 = jnp.sum(jnp.where(oh2, loc, 0.0), axis=0, keepdims=True)
    lane = lax.broadcasted_iota(jnp.int32, (N_EXPERTS, LANES), 1)
    seg_ref[0] = jnp.where(lane == 0, base, jnp.where(lane == 1, c_b, carry_s[...])).astype(jnp.int32)
    carry_s[...] = carry_s[...] + c_b
    cnt_ref[...] = carry_s[...].astype(jnp.int32)
    rw_ref[0] = jnp.concatenate([w1, w2, lpos1, lpos2, jnp.zeros((SUBLANES - 4, tm), F32)], axis=0)

    srow = lax.broadcasted_iota(jnp.int32, (ls, tm), 0)
    perm = jnp.where(srow == lpos1.astype(jnp.int32), 1.0, jnp.where(srow == lpos2.astype(jnp.int32), 1.0, 0.0))
    xsorted = _nn(perm.astype(BF16), hi)
    xsl_ref[0] = xsorted.astype(BF16)


def _outproj(ya, yb, mg, x, wa, wb, wo, g_ffn, w_rg, b_rg, w_re, b_re, tm=MOE_TM):
    T, D = x.shape
    nt = T // tm
    wr = jnp.zeros((ROUTER_ROWS, D), F32).at[0:N_GROUPS].set(w_rg.T).at[EXPERT_ROW0:EXPERT_ROW0 + N_EXPERTS].set(w_re.T)
    wr = wr.astype(BF16)
    br = jnp.zeros((ROUTER_ROWS, 1), F32).at[0:N_GROUPS, 0].set(b_rg).at[EXPERT_ROW0:EXPERT_ROW0 + N_EXPERTS, 0].set(b_re)
    r = np.arange(tm)
    tri = jnp.asarray((r[:, None] < r[None, :]).astype(np.float32), BF16)
    re = np.arange(N_EXPERTS)
    ltri = jnp.asarray((re[None, :] < re[:, None]).astype(np.float32), BF16)
    ls = _local_rows(tm)
    row = lambda n: pl.BlockSpec((tm, n), lambda i: (i, 0))
    full = lambda shp: pl.BlockSpec(shp, lambda i: (0,) * len(shp))
    return pl.pallas_call(
        functools.partial(_outproj_kernel, tm=tm, ls=ls),
        out_shape=[jax.ShapeDtypeStruct((T, D), F32), jax.ShapeDtypeStruct((nt, ls, D), BF16),
                   jax.ShapeDtypeStruct((nt, N_EXPERTS, LANES), jnp.int32),
                   jax.ShapeDtypeStruct((nt, SUBLANES, tm), F32),
                   jax.ShapeDtypeStruct((N_EXPERTS, LANES), jnp.int32)],
        grid=(nt,),
        in_specs=[row(ya.shape[1]), row(yb.shape[1]), row(2 * D), row(D),
                  full(wa.shape), full(wb.shape), full(wo.shape), full((1, D)),
                  full((ROUTER_ROWS, D)), full((ROUTER_ROWS, 1)), full((tm, tm)),
                  full((N_EXPERTS, N_EXPERTS))],
        out_specs=[row(D), pl.BlockSpec((1, ls, D), lambda i: (i, 0, 0)),
                   pl.BlockSpec((1, N_EXPERTS, LANES), lambda i: (i, 0, 0)),
                   pl.BlockSpec((1, SUBLANES, tm), lambda i: (i, 0, 0)),
                   full((N_EXPERTS, LANES))],
        scratch_shapes=[pltpu.VMEM((N_EXPERTS, LANES), F32)],
        compiler_params=pltpu.CompilerParams(dimension_semantics=("arbitrary",), vmem_limit_bytes=VMEM_LIMIT),
        name="outproj",
    )(ya, yb, mg, x, wa, wb, wo, g_ffn.reshape(1, D), wr, br, tri, ltri)


SEG_FIELDS = 4
SEG_BITS = (MOE_TM // SEG_ROWS).bit_length()
TILE_BITS = (_local_rows(MOE_TM) // SEG_ROWS).bit_length()
TAIL_BITS = (TE // SEG_ROWS).bit_length() - 1


def _segment_copies(n8, bits, make_copy, wait):
    def arms(lo, hi, off):
        for bit in reversed(range(lo, hi)):
            rows = SEG_ROWS << bit
            take = (n8 >> bit) & 1

            @pl.when(take == 1)
            def _(off=off, rows=rows):
                cp = make_copy(off, rows)
                if wait:
                    cp.wait()
                else:
                    cp.start()

            off = off + take * rows

    low = min(4, bits)
    if bits > low:
        @pl.when((n8 >> low) != 0)
        def _():
            arms(low, bits, 0)

    arms(0, low, ((n8 >> low) << low) * SEG_ROWS)


def _wait_rows(n8, sem, like_src, like_dst):
    def make_copy(off, rows):
        return pltpu.make_async_copy(like_src.at[pl.ds(0, rows), :], like_dst.at[pl.ds(0, rows), :], sem)

    _segment_copies(n8, TILE_BITS, make_copy, True)


def _dispatch_kernel(seg_ref, tail_ref, xsl_ref, xs_ref, zero_s, sem):
    i = pl.program_id(0)

    def segments(wait):
        def body(e, c):
            s0 = (i * N_EXPERTS + e) * SEG_FIELDS
            src0 = pl.multiple_of(seg_ref[s0], SEG_ROWS)
            dst0 = pl.multiple_of(seg_ref[s0 + 2], SEG_ROWS)

            def make_copy(off, rows):
                return pltpu.make_async_copy(xsl_ref.at[0, pl.ds(pl.multiple_of(src0 + off, SEG_ROWS), rows), :],
                                             xs_ref.at[pl.ds(pl.multiple_of(dst0 + off, SEG_ROWS), rows), :], sem)

            _segment_copies(seg_ref[s0 + 1], SEG_BITS, make_copy, wait)
            return c

        lax.fori_loop(0, N_EXPERTS, body, 0)

    def tails(wait):
        def body(e, c):
            dst0 = pl.multiple_of(tail_ref[2 * e], SEG_ROWS)

            def make_copy(off, rows):
                return pltpu.make_async_copy(zero_s.at[pl.ds(0, rows), :],
                                             xs_ref.at[pl.ds(pl.multiple_of(dst0 + off, SEG_ROWS), rows), :], sem)

            _segment_copies(tail_ref[2 * e + 1], TAIL_BITS, make_copy, wait)
            return c

        lax.fori_loop(0, N_EXPERTS, body, 0)

    def unused_tiles(wait):
        def body(t, c):
            cp = pltpu.make_async_copy(zero_s, xs_ref.at[pl.ds(pl.multiple_of(t * TE, TE), TE), :], sem)
            if wait:
                cp.wait()
            else:
                cp.start()
            return c

        lax.fori_loop(tail_ref[2 * N_EXPERTS], xs_ref.shape[0] // TE, body, 0)

    @pl.when(i == 0)
    def _():
        zero_s[...] = jnp.zeros_like(zero_s)
        tails(False)
        unused_tiles(False)
        tails(True)
        unused_tiles(True)

    segments(False)
    _wait_rows(seg_ref[i * N_EXPERTS * SEG_FIELDS + 3], sem, xsl_ref.at[0], xs_ref)


def _dispatch(seg, tail, xsl, n_rows):
    nt, ls, W = xsl.shape
    return pl.pallas_call(
        _dispatch_kernel,
        out_shape=jax.ShapeDtypeStruct((n_rows, W), xsl.dtype),
        grid_spec=pltpu.PrefetchScalarGridSpec(
            num_scalar_prefetch=2,
            grid=(nt,),
            in_specs=[pl.BlockSpec((1, ls, W), lambda i, sg, tl: (i, 0, 0))],
            out_specs=pl.BlockSpec(memory_space=pl.ANY),
            scratch_shapes=[pltpu.VMEM((TE, W), xsl.dtype), pltpu.SemaphoreType.DMA(())],
        ),
        compiler_params=pltpu.CompilerParams(dimension_semantics=("arbitrary",)),
        name="dispatch",
    )(seg, tail, xsl)


def _experts_kernel(te_ref, nv_ref, xs_ref, wgu_ref, wd_ref, out_ref):
    i = pl.program_id(0)

    @pl.when(i < nv_ref[0])
    def _():
        au = _nn(xs_ref[...], wgu_ref[0])
        a, u = au[:, :EXPERT_FF], au[:, EXPERT_FF:]
        hid = (a * jax.nn.sigmoid(a)) * u
        out_ref[...] = _nn(hid.astype(BF16), wd_ref[0]).astype(BF16)

    @pl.when(i >= nv_ref[0])
    def _():
        out_ref[...] = jnp.zeros_like(out_ref)


def _experts(tile_expert, n_valid, xs, w_gate_up, w_down):
    n_rows, D = xs.shape
    n_tiles = n_rows // TE
    last = lambda i, nv: jnp.minimum(i, nv[0] - 1)
    return pl.pallas_call(
        _experts_kernel,
        out_shape=jax.ShapeDtypeStruct((n_rows, D), BF16),
        grid_spec=pltpu.PrefetchScalarGridSpec(
            num_scalar_prefetch=2,
            grid=(n_tiles,),
            in_specs=[pl.BlockSpec((TE, D), lambda i, te, nv: (last(i, nv), 0)),
                      pl.BlockSpec((1, D, 2 * EXPERT_FF), lambda i, te, nv: (te[last(i, nv)], 0, 0)),
                      pl.BlockSpec((1, EXPERT_FF, D), lambda i, te, nv: (te[last(i, nv)], 0, 0))],
            out_specs=pl.BlockSpec((TE, D), lambda i, te, nv: (i, 0)),
        ),
        compiler_params=pltpu.CompilerParams(dimension_semantics=("arbitrary",), vmem_limit_bytes=VMEM_LIMIT),
        name="experts",
    )(tile_expert, n_valid, xs, w_gate_up, w_down)


def _combine_kernel(seg_ref, ys_ref, x1_ref, rw_ref, gfin_ref, out_ref, buf, sem, *, tm, ls, apply_norm):
    i = pl.program_id(0)
    nt = pl.num_programs(0)

    def segments(tile, wait):
        slot = tile % 2

        def body(e, c):
            s0 = (tile * N_EXPERTS + e) * SEG_FIELDS
            loc0 = pl.multiple_of(seg_ref[s0], SEG_ROWS)
            glob0 = pl.multiple_of(seg_ref[s0 + 2], SEG_ROWS)

            def make_copy(off, rows):
                return pltpu.make_async_copy(ys_ref.at[pl.ds(pl.multiple_of(glob0 + off, SEG_ROWS), rows), :],
                                             buf.at[slot, pl.ds(pl.multiple_of(loc0 + off, SEG_ROWS), rows), :],
                                             sem.at[slot])

            _segment_copies(seg_ref[s0 + 1], SEG_BITS, make_copy, wait)
            return c

        lax.fori_loop(0, N_EXPERTS, body, 0)

    @pl.when(i == 0)
    def _():
        buf[...] = jnp.zeros_like(buf)
        segments(i, False)

    @pl.when(i + 1 < nt)
    def _():
        segments(i + 1, False)

    _wait_rows(seg_ref[i * N_EXPERTS * SEG_FIELDS + 3], sem.at[i % 2], ys_ref, buf.at[i % 2])
    ysl = buf[i % 2]
    cols = jnp.concatenate([rw_ref[0], jnp.zeros((LANES - SUBLANES, tm), F32)], axis=0).T
    srow = lax.broadcasted_iota(jnp.int32, (tm, ls), 1)
    y = x1_ref[...]
    for k in range(2):
        pick = jnp.where(srow == cols[:, 2 + k:3 + k].astype(jnp.int32), 1.0, 0.0).astype(BF16)
        y = y + cols[:, k:k + 1] * _nn(pick, ysl)
    if apply_norm:
        ms = jnp.mean(y * y, axis=-1, keepdims=True)
        y = y * lax.rsqrt(ms + EPS) * gfin_ref[...]
    out_ref[...] = y


def _combine(seg, ys, x1, rw, g_final, apply_norm, tm=MOE_TM):
    T, D = x1.shape
    nt = T // tm
    ls = _local_rows(tm)
    return pl.pallas_call(
        functools.partial(_combine_kernel, tm=tm, ls=ls, apply_norm=apply_norm),
        out_shape=jax.ShapeDtypeStruct((T, D), F32),
        grid_spec=pltpu.PrefetchScalarGridSpec(
            num_scalar_prefetch=1,
            grid=(nt,),
            in_specs=[pl.BlockSpec(memory_space=pl.ANY),
                      pl.BlockSpec((tm, D), lambda i, sg: (i, 0)),
                      pl.BlockSpec((1, SUBLANES, tm), lambda i, sg: (i, 0, 0)),
                      pl.BlockSpec((1, D), lambda i, sg: (0, 0))],
            out_specs=pl.BlockSpec((tm, D), lambda i, sg: (i, 0)),
            scratch_shapes=[pltpu.VMEM((2, ls, D), BF16), pltpu.SemaphoreType.DMA((2,))],
        ),
        compiler_params=pltpu.CompilerParams(dimension_semantics=("arbitrary",), vmem_limit_bytes=VMEM_LIMIT),
        name="combine",
    )(seg, ys, x1, rw, g_final.reshape(1, D))


def _moe_plan(seg, counts, T, tm=MOE_TM):
    nt = T // tm
    n_tiles_max = (2 * T + nt * N_EXPERTS * (SEG_ROWS - 1)) // TE + N_EXPERTS
    total = counts[:, 0]
    tiles = (total + TE - 1) // TE
    ids = jnp.arange(N_EXPERTS)
    tile_end = jnp.sum(jnp.where(ids[None, :] <= ids[:, None], tiles[None, :], 0), axis=1)
    row0 = (tile_end - tiles) * TE
    n8 = seg[:, :, 1] // SEG_ROWS
    segtab = jnp.stack([seg[:, :, 0], n8, seg[:, :, 2] + row0[None, :],
                        jnp.broadcast_to(jnp.sum(n8, axis=1, keepdims=True), n8.shape)],
                       axis=-1).reshape(-1).astype(jnp.int32)
    tail = jnp.concatenate([jnp.stack([row0 + total, (tiles * TE - total) // SEG_ROWS], axis=-1).reshape(-1),
                            tile_end[-1:]]).astype(jnp.int32)
    tile_expert = jnp.minimum(jnp.sum(tile_end[None, :] <= jnp.arange(n_tiles_max)[:, None], axis=1),
                              N_EXPERTS - 1).astype(jnp.int32)
    return segtab, tail, tile_expert, tile_end[-1:].astype(jnp.int32), n_tiles_max * TE


def kernel(x, g_mix, w_in, nsa_pe_k, nsa_cmp_k_w1, nsa_cmp_k_w2, nsa_pe_v, nsa_cmp_v_w1, nsa_cmp_v_w2, rel_bias,
           gla_w_alpha, gla_b_alpha, gla_norm_g, w_branch_a, w_branch_b, w_out, g_ffn, w_router_group,
           b_router_group, w_router_expert, b_router_expert, w_exp_gate, w_exp_up, w_exp_down, g_final):
    B, S, D = x.shape
    T = B * S
    for l in range(w_in.shape[0]):
        (q, kslc, kwin, kvc, misc, qkb, vb, rb, mg, vts, vtw, vtb, wgu16, wd16) = _inproj(
            x, g_mix[l], w_in[l].T, [(w_exp_gate[l], w_exp_up[l]), (w_exp_down[l],)])
        kcb, vcbt = _compress(kvc, nsa_pe_k[l], nsa_cmp_k_w1[l], nsa_cmp_k_w2[l],
                              nsa_pe_v[l], nsa_cmp_v_w1[l], nsa_cmp_v_w2[l])
        ya = _nsa(q, kslc, vts, kwin, vtw, kcb, vcbt, misc, rel_bias)
        yb = _gla(qkb, vb, vtb, misc, rb, gla_w_alpha[l], gla_b_alpha[l], gla_norm_g[l])
        x1, xsl, seg, rw, counts = _outproj(
            ya.reshape(T, -1), yb.reshape(T, -1), mg.reshape(T, -1), x.reshape(T, D),
            w_branch_a[l].astype(BF16), w_branch_b[l].astype(BF16), w_out[l].astype(BF16), g_ffn[l],
            w_router_group[l], b_router_group[l], w_router_expert[l], b_router_expert[l])
        segtab, tail, tile_expert, n_valid, n_rows = _moe_plan(seg, counts, T)
        xs = _dispatch(segtab, tail, xsl, n_rows)
        ys = _experts(tile_expert, n_valid, xs, wgu16, wd16)
        last_layer = l == w_in.shape[0] - 1
        x = _combine(segtab, ys, x1, rw, g_final, apply_norm=last_layer).reshape(B, S, D)
    return x
```

```python
import functools
import math

import numpy as np
import jax
import jax.numpy as jnp
from jax import lax
from jax.experimental import pallas as pl
from jax.experimental.pallas import tpu as pltpu

F32 = jnp.float32
BF16 = jnp.bfloat16

NSA_HEADS = 8
NSA_GROUPS = 2
HPG = NSA_HEADS // NSA_GROUPS
DH = 64
CMP_BLOCK = 32
CMP_STRIDE = 16
CMP_HIDDEN = 128
SLC_BLOCK = 64
SLC_TOPN = 16
WINDOW = 512
GLA_HEADS = 4
GLA_DK = 64
GLA_DV = 128
GLA_RANK = 16
GLA_TAU = 16.0
GLA_CHUNK = 64
REL_BUCKETS = 32
REL_MAX_EXACT = REL_BUCKETS // 2
REL_MAX_DIST = 128
N_GROUPS = 4
EPG = 8
N_EXPERTS = N_GROUPS * EPG
EXPERT_FF = 256
EPS = 1e-6

LANES = 128
SUBLANES = 8
VMEM_LIMIT = 56 * 1024 * 1024

LOG2E = math.log2(math.e)
NEG = -1e30
BIG = float(2.0 ** 100)
QT = 128
SLC_CHUNK_TILES = 2
KV_PAD = WINDOW
SLC_OFF = KV_PAD - (SLC_CHUNK_TILES - 1) * QT
WIN_FAR = WINDOW - QT
TE = 512
MOE_TM = 512
VT_ROWS = DH + 16
NSA_TILES = 2


def _nt(a, b):
    return lax.dot_general(a, b, (((1,), (1,)), ((), ())), preferred_element_type=F32)


def _nn(a, b):
    return jnp.dot(a, b, preferred_element_type=F32)


def _split3(x):
    a = x.astype(BF16)
    r = x - a.astype(F32)
    b = r.astype(BF16)
    c = (r - b.astype(F32)).astype(BF16)
    return a, b, c


def _t5_bucket_np(rel):
    n = np.maximum(rel, 0)
    nf = np.maximum(n, 1).astype(np.float32)
    large = REL_MAX_EXACT + (np.log(nf / np.float32(REL_MAX_EXACT)) / np.float32(math.log(REL_MAX_DIST / REL_MAX_EXACT))
                             * np.float32(REL_BUCKETS - REL_MAX_EXACT)).astype(np.int32)
    return np.where(n < REL_MAX_EXACT, n, np.minimum(large, REL_BUCKETS - 1)).astype(np.int32)


def _inproj_tile(x_ref, g_ref, wq_ref, wkv_ref, wmisc_ref, wb_ref, wrest_ref, kplace_ref, *refs, tm, cast_groups,
                 seq_tile):
    ncast = sum(cast_groups)
    cast_in, refs = refs[:ncast], refs[ncast:]
    (oq, okslc, okwin, okv, omisc, oqkb, ovb, orb, omg, ovts, ovtw, ovtb), cast_out = refs[:12], refs[12:]
    first = 0
    for dst, n in zip(cast_out, cast_groups):
        col = 0
        for src in cast_in[first:first + n]:
            dst[:, :, col:col + src.shape[-1]] = src[...].astype(BF16)
            col += src.shape[-1]
        first += n
    x = x_ref[0]
    ms = jnp.mean(x * x, axis=-1, keepdims=True)
    h = (x * lax.rsqrt(ms + EPS) * g_ref[...]).astype(BF16)

    oq[0] = _nt(h, wq_ref[...]).astype(BF16)
    kv = _nt(h, wkv_ref[...])
    okv[0] = kv[:, 0:256]

    def spread(k):
        return _nn(k.astype(BF16), kplace_ref[...])

    row = lax.broadcasted_iota(jnp.int32, (tm, 256), 0) + seq_tile * tm
    lane = lax.broadcasted_iota(jnp.int32, (tm, 256), 1) % LANES
    onehot = jnp.where(lane - DH == row // SLC_BLOCK, 1.0, 0.0)
    okslc[0] = (spread(kv[:, 256:384]) + onehot).astype(BF16)
    okwin[0] = spread(kv[:, 512:640]).astype(BF16)

    ones_rows = jnp.where(lax.broadcasted_iota(jnp.int32, (VT_ROWS - DH, tm), 0) == 0, 1.0, 0.0)

    def vt_groups(v):
        t = v.T
        return jnp.concatenate([t[:DH], ones_rows, t[DH:], ones_rows], axis=0).astype(BF16)

    ovts[0] = vt_groups(kv[:, 384:512])
    ovtw[0] = vt_groups(kv[:, 640:768])
    omisc[0] = _nt(h, wmisc_ref[...])
    qkv = _nt(h, wb_ref[...])
    oqkb[0] = qkv[:, 0:512]
    ovb[0] = qkv[:, 512:1024].astype(BF16)
    ovtb[0] = qkv[:, 512:1024].T.astype(BF16)
    orb[0] = _nt(h, wrest_ref[0:512, :])
    omg[0] = _nt(h, wrest_ref[512:, :])


W_RUNS = (512, 6 * NSA_GROUPS * DH, 3 * NSA_HEADS, 1024, 16, 2560)
W_RUN0 = tuple(int(v) for v in np.cumsum((0,) + W_RUNS))
GATE_ROWS = W_RUNS[2] + W_RUNS[4]


def _load_inproj_weights(wt_hbm, wq, wkv, wmisc, wb, wrest, stage, gates, sem):
    rows = stage.shape[1]
    chunks = []
    for run, dst, dst0, scale in ((0, wq, 0, DH ** -0.5 * LOG2E), (1, wkv, 0, None), (2, gates, 0, None),
                                  (3, wb, 0, None), (4, gates, W_RUNS[2], None), (5, wrest, 0, None)):
        for r in range(0, W_RUNS[run], rows):
            chunks.append((W_RUN0[run] + r, min(rows, W_RUNS[run] - r), dst, dst0 + r, scale))

    def copy(k):
        src0, n = chunks[k][:2]
        return pltpu.make_async_copy(wt_hbm.at[pl.ds(src0, n)], stage.at[k % 2, pl.ds(0, n)], sem.at[k % 2])

    copy(0).start()
    for k, (_, n, dst, dst0, scale) in enumerate(chunks):
        if k + 1 < len(chunks):
            copy(k + 1).start()
        copy(k).wait()
        w = stage[k % 2, 0:n, :]
        if scale is not None:
            w = w * scale
        dst[dst0:dst0 + n, :] = w.astype(dst.dtype)
    g = gates[0:W_RUNS[2], :]
    z4 = jnp.zeros((HPG, g.shape[1]), F32)
    per_group = [g[(br * NSA_GROUPS + grp) * HPG:(br * NSA_GROUPS + grp + 1) * HPG]
                 for grp in range(NSA_GROUPS) for br in range(3)]
    misc = (per_group[0:3] + [z4] + per_group[3:6] + [z4, gates[W_RUNS[2]:GATE_ROWS, :],
            jnp.zeros((LANES - 48, g.shape[1]), F32)])
    wmisc[...] = jnp.concatenate(misc, axis=0).astype(BF16)


def _inproj_kernel(*refs, tm, cast_groups, batch, tiles):
    s = pl.program_id(0)
    ncast = sum(cast_groups)
    x_ref, g_ref, wt_hbm, kplace_ref = refs[:4]
    cast_in = refs[4:4 + ncast]
    outs = refs[4 + ncast:4 + ncast + 12 + len(cast_groups)]
    weights = refs[-8:-3]
    stage, gates, sem = refs[-3:]
    okslc, okwin, ovts, ovtw = outs[1], outs[2], outs[9], outs[10]

    @pl.when(s == 0)
    def _():
        _load_inproj_weights(wt_hbm, *weights, stage, gates, sem)

    @pl.when(s < batch)
    def _():
        lane = lax.broadcasted_iota(jnp.int32, okslc.shape[1:], 1) % LANES
        pad_keys = jnp.where(lane >= DH, 1.0, 0.0).astype(BF16)
        okslc[0] = pad_keys
        okwin[0] = pad_keys
        ovts[0] = jnp.zeros_like(ovts[0])
        ovtw[0] = jnp.zeros_like(ovtw[0])

    @pl.when(s >= batch)
    def _():
        _inproj_tile(x_ref, g_ref, *weights, kplace_ref, *cast_in, *outs, tm=tm, cast_groups=cast_groups,
                     seq_tile=(s - batch) % tiles)


def _inproj(x, g_mix, wt, to_bf16, tm=KV_PAD):
    B, S, D = x.shape
    assert wt.shape == (W_RUN0[-1], D)
    nsb = S // tm
    nsteps = B * nsb
    t_of = lambda s: jnp.maximum(s - B, 0)
    b_of = lambda s: jnp.where(s < B, s, (s - B) // nsb)
    step_block = lambda shp: pl.BlockSpec((shp[0] // nsteps,) + shp[1:], lambda s: (t_of(s), 0, 0))
    cast_in = [a for grp in to_bf16 for a in grp]
    cast_shapes = [grp[0].shape[:2] + (sum(a.shape[2] for a in grp),) for grp in to_bf16]
    src = np.arange(NSA_GROUPS * DH)
    kplace = np.zeros((NSA_GROUPS * DH, NSA_GROUPS * LANES), np.float32)
    kplace[src, (src // DH) * LANES + src % DH] = 1.0
    kplace = jnp.asarray(kplace, BF16)
    widths = [(512, BF16), (256, BF16), (256, BF16), (256, F32), (128, F32), (512, F32), (512, BF16),
              (512, F32), (2048, F32)]
    out_shape = [jax.ShapeDtypeStruct((B, S, n), dt) for n, dt in widths]
    out_specs = [pl.BlockSpec((1, tm, n), lambda s: (t_of(s) // nsb, t_of(s) % nsb, 0)) for n, _ in widths]
    for rows in (NSA_GROUPS * VT_ROWS, NSA_GROUPS * VT_ROWS, 512):
        out_shape.append(jax.ShapeDtypeStruct((B, rows, S), BF16))
        out_specs.append(pl.BlockSpec((1, rows, tm), lambda s: (t_of(s) // nsb, 0, t_of(s) % nsb)))
    pad_or_tile = lambda s: jnp.where(s < B, 0, (s - B) % nsb + 1)
    for k in (1, 2):
        out_shape[k] = jax.ShapeDtypeStruct((B, KV_PAD + S, widths[k][0]), BF16)
        out_specs[k] = pl.BlockSpec((1, tm, widths[k][0]), lambda s: (b_of(s), pad_or_tile(s), 0))
    for k in (9, 10):
        out_shape[k] = jax.ShapeDtypeStruct((B, NSA_GROUPS * VT_ROWS, KV_PAD + S), BF16)
        out_specs[k] = pl.BlockSpec((1, NSA_GROUPS * VT_ROWS, tm), lambda s: (b_of(s), 0, pad_or_tile(s)))
    return pl.pallas_call(
        functools.partial(_inproj_kernel, tm=tm, cast_groups=tuple(len(grp) for grp in to_bf16), batch=B, tiles=nsb),
        out_shape=out_shape + [jax.ShapeDtypeStruct(shp, BF16) for shp in cast_shapes],
        grid=(B + nsteps,),
        in_specs=[
            pl.BlockSpec((1, tm, D), lambda s: (t_of(s) // nsb, t_of(s) % nsb, 0)),
            pl.BlockSpec((1, D), lambda s: (0, 0)),
            pl.BlockSpec(memory_space=pl.ANY),
            pl.BlockSpec(kplace.shape, lambda s: (0, 0)),
        ] + [step_block(a.shape) for a in cast_in],
        out_specs=out_specs + [step_block(shp) for shp in cast_shapes],
        scratch_shapes=[pltpu.VMEM((W_RUNS[0], D), BF16), pltpu.VMEM((W_RUNS[1], D), BF16),
                        pltpu.VMEM((LANES, D), BF16), pltpu.VMEM((W_RUNS[3], D), BF16),
                        pltpu.VMEM((W_RUNS[5], D), BF16), pltpu.VMEM((2, tm, D), F32),
                        pltpu.VMEM((GATE_ROWS, D), F32), pltpu.SemaphoreType.DMA((2,))],
        compiler_params=pltpu.CompilerParams(dimension_semantics=("arbitrary",),
                                             vmem_limit_bytes=VMEM_LIMIT),
        name="inproj",
    )(x, g_mix.reshape(1, D), wt, kplace, *cast_in)


def _gelu_tanh(x):
    return 0.5 * x * (1.0 + jnp.tanh(math.sqrt(2.0 / math.pi) * (x + 0.044715 * (x * x * x))))


def _compress_kernel(xk_ref, xv_ref, pe_ref, w1_ref, w2k_ref, w2vt_ref, ok_ref, ovt_ref, *, nsub):
    for kind, x_ref in enumerate((xk_ref, xv_ref)):
        top = jnp.zeros((nsub, 2 * CMP_HIDDEN), F32)
        bot = jnp.zeros((nsub, 2 * CMP_HIDDEN), F32)
        for r in range(CMP_STRIDE):
            xr = x_ref[0, pl.ds(r, nsub, stride=CMP_STRIDE), :]
            top = top + _nn((xr + pe_ref[kind, 0, r:r + 1, :]).astype(BF16), w1_ref[kind, 0, r])
            bot = bot + _nn((xr + pe_ref[kind, 1, r:r + 1, :]).astype(BF16), w1_ref[kind, 1, r])
        hid = _gelu_tanh(top + pltpu.roll(bot, shift=nsub - 1, axis=0)).astype(BF16)
        for g in range(NSA_GROUPS):
            hg = hid[:, g * CMP_HIDDEN:(g + 1) * CMP_HIDDEN]
            if kind == 0:
                ok_ref[0, g] = _nn(hg, w2k_ref[...]).astype(BF16)
            else:
                ovt_ref[0, g] = _nt(w2vt_ref[...], hg).astype(BF16)


def _compress(kv_cmp, pe_k, w1k, w2k, pe_v, w1v, w2v):
    B, S, _ = kv_cmp.shape
    nsub = S // CMP_STRIDE
    G = NSA_GROUPS

    def prep(pe, w1):
        pe_t = jnp.tile(pe.reshape(2, CMP_STRIDE, DH), (1, 1, G))
        a = w1.reshape(2, CMP_STRIDE, DH, CMP_HIDDEN)
        z = jnp.zeros_like(a)
        w = jnp.concatenate([jnp.concatenate([a, z], axis=3), jnp.concatenate([z, a], axis=3)], axis=2)
        return pe_t, w.astype(BF16)

    pek, w1kb = prep(pe_k, w1k)
    pev, w1vb = prep(pe_v, w1v)
    pe = jnp.stack([pek, pev])
    w1 = jnp.stack([w1kb, w1vb])
    w2kp = jnp.concatenate([w2k, jnp.zeros_like(w2k)], axis=1).astype(BF16)
    w2vt = w2v.T.astype(BF16)
    full = lambda shp: pl.BlockSpec(shp, lambda b: (0,) * len(shp))
    return pl.pallas_call(
        functools.partial(_compress_kernel, nsub=nsub),
        out_shape=[jax.ShapeDtypeStruct((B, G, nsub, LANES), BF16),
                   jax.ShapeDtypeStruct((B, G, DH, nsub), BF16)],
        grid=(B,),
        in_specs=[pl.BlockSpec((1, S, G * DH), lambda b: (b, 0, 0)), pl.BlockSpec((1, S, G * DH), lambda b: (b, 0, 1)),
                  full(pe.shape), full(w1.shape), full((CMP_HIDDEN, LANES)), full((DH, CMP_HIDDEN))],
        out_specs=[pl.BlockSpec((1, G, nsub, LANES), lambda b: (b, 0, 0, 0)),
                   pl.BlockSpec((1, G, DH, nsub), lambda b: (b, 0, 0, 0))],
        compiler_params=pltpu.CompilerParams(dimension_semantics=("parallel",), vmem_limit_bytes=VMEM_LIMIT),
        name="compress",
    )(kv_cmp, kv_cmp, pe, w1, w2kp, w2vt)


def _bias_kernel(tbl_ref, bkn_ref, bkc_ref, near_ref, cmpb_ref):
    g = pl.program_id(0)
    for h in range(HPG):
        hd = g * HPG + h

        def lookup(bk):
            acc = jnp.full(bk.shape, NEG, F32)
            for b in range(REL_BUCKETS):
                acc = jnp.where(bk == b, tbl_ref[hd, b], acc)
            return acc

        vn = lookup(bkn_ref[...])
        near_ref[0, :, h * QT:(h + 1) * QT] = jnp.where(vn > 0.5 * NEG, (vn - tbl_ref[hd, REL_BUCKETS - 1]) * LOG2E, NEG)
        vc = lookup(bkc_ref[...])
        cmpb_ref[0, :, h * QT:(h + 1) * QT] = jnp.where(vc > 0.5 * NEG, vc * LOG2E, NEG)


def _nsa_bias_tables(rel_bias, seq):
    ql = np.arange(QT)
    ncmp = seq // CMP_STRIDE

    def buckets(rel):
        return jnp.asarray(np.where(rel >= 0, _t5_bucket_np(rel), -1).astype(np.int32))

    nk = SLC_CHUNK_TILES * QT
    bkn = buckets(ql[None, :] + nk - QT - np.arange(nk)[:, None])
    y = np.arange(2 * ncmp)
    bkc = buckets(ql[None, :] - CMP_STRIDE * (y[:, None] - ncmp) - (CMP_BLOCK - 1))
    nql = HPG * QT
    return pl.pallas_call(
        _bias_kernel,
        out_shape=[jax.ShapeDtypeStruct((NSA_GROUPS, nk, nql), F32),
                   jax.ShapeDtypeStruct((NSA_GROUPS, 2 * ncmp, nql), F32)],
        grid=(NSA_GROUPS,),
        in_specs=[pl.BlockSpec(memory_space=pltpu.SMEM),
                  pl.BlockSpec((nk, QT), lambda g: (0, 0)),
                  pl.BlockSpec((2 * ncmp, QT), lambda g: (0, 0))],
        out_specs=[pl.BlockSpec((1, nk, nql), lambda g: (g, 0, 0)),
                   pl.BlockSpec((1, 2 * ncmp, nql), lambda g: (g, 0, 0))],
        compiler_params=pltpu.CompilerParams(dimension_semantics=("parallel",)),
        name="t5bias",
    )(rel_bias.T, bkn, bkc)


def _nsa_kernel(q_ref, kslc_ref, vtslc_ref, kwin_ref, vtwin_ref, kcb_ref, vcbt_ref, misc_ref,
                near_ref, cmpb_ref, wmask_ref, ovt_ref, eye_ref, eye4_ref,
                out_ref, qaug_s, s0_s, s1_s, *, ncmp, nslc):
    G = NSA_GROUPS
    qts = [NSA_TILES * pl.program_id(1) + t for t in range(NSA_TILES)]
    chains = [(t, g) for t in range(NSA_TILES) for g in range(G)]
    nql = HPG * QT
    ck = SLC_CHUNK_TILES * QT
    wk = 2 * QT

    def flash(carry, s, vt_chunk):
        m, acc = carry
        m_new = jnp.maximum(m, jnp.max(s, axis=0, keepdims=True))
        alpha = jnp.exp2(m - m_new)
        p = jnp.exp2((s - m_new).astype(BF16))
        return m_new, alpha * acc + _nn(vt_chunk, p)

    def finish(carry):
        m, acc = carry
        return acc[:DH] * (1.0 / acc[DH:DH + 1])

    init = (jnp.full((1, nql), NEG, F32), jnp.zeros((VT_ROWS, nql), F32))
    halves = [slice(k * nql // 2, (k + 1) * nql // 2) for k in range(2)]
    init_halves = tuple((init[0][:, hs], init[1][:, hs]) for hs in halves)
    ns = [pl.multiple_of(QT * qt, QT) for qt in qts]
    nw = [pl.multiple_of(QT * qt + WIN_FAR, QT) for qt in qts]
    off = [pl.multiple_of(ncmp - (QT // CMP_STRIDE) * qt, SUBLANES) for qt in qts]
    lane = lax.broadcasted_iota(jnp.int32, (nql, LANES), 1)
    win_aug = jnp.where(lane >= DH, -BIG, 0.0).astype(BF16)
    jidx = lax.broadcasted_iota(jnp.int32, (nslc, QT), 0)
    tq = [qt * QT + lax.broadcasted_iota(jnp.int32, (nslc, QT), 1) for qt in qts]
    forced = [(jidx == 0) | (jidx == tq[t] // SLC_BLOCK) | (jidx == tq[t] // SLC_BLOCK - 1) for t in range(NSA_TILES)]
    future = [jidx * SLC_BLOCK > tq[t] for t in range(NSA_TILES)]
    sub = lax.broadcasted_iota(jnp.int32, (SUBLANES, QT), 0)
    ones_lo = jnp.ones((DH, QT), F32)
    eye = eye_ref[...]
    ovt = ovt_ref[...]
    ngrp = nslc // SUBLANES

    def aug(sel01):
        rows = [ones_lo, sel01]
        if LANES - DH - nslc:
            rows.append(jnp.ones((LANES - DH - nslc, QT), F32))
        m01 = _nt(eye, jnp.concatenate(rows, axis=0).astype(BF16))
        return jnp.concatenate([((m01 - 1.0) * BIG).astype(BF16)] * HPG, axis=0)

    cs = range(len(chains))
    kl = [slice(g * LANES, (g + 1) * LANES) for _, g in chains]
    vr = [slice(g * VT_ROWS, (g + 1) * VT_ROWS) for _, g in chains]
    low = lax.broadcasted_iota(jnp.int32, (QT, LANES), 1) < DH

    def head_slot(t, hd):
        pair = q_ref[0, t * QT:(t + 1) * QT, (hd // 2) * LANES:(hd // 2 + 1) * LANES].astype(F32)
        if hd % 2:
            pair = pltpu.roll(pair, shift=DH, axis=1)
        return jnp.where(low, pair, 0.0).astype(BF16)

    q0 = [jnp.concatenate([head_slot(t, g * HPG + h) for h in range(HPG)], axis=0) for t, g in chains]
    qwin = [q0[c] + win_aug for c in cs]

    bc = [cmpb_ref[g, pl.ds(off[t], ncmp), :] for t, g in chains]
    sc = [_nt(kcb_ref[0, chains[c][1]], q0[c]) + bc[c] for c in cs]
    sw = [_nt(kwin_ref[0, pl.ds(ns[chains[c][0]], WIN_FAR), kl[c]], qwin[c]) for c in cs]
    sw = [jnp.concatenate([sw[c][:QT] + wmask_ref[...], sw[c][QT:]], axis=0) for c in cs]

    ecb, rden = [], []
    for c in cs:
        mc = jnp.maximum(jnp.max(sc[c], axis=0, keepdims=True), 0.5 * NEG)
        ec = jnp.exp2(sc[c] - mc)
        rden.append(1.0 / jnp.maximum(jnp.sum(ec, axis=0, keepdims=True), jnp.finfo(F32).tiny))
        ecb.append(ec.astype(BF16))
    wcar = [[flash(init_halves[k], sw[c][:, halves[k]], vtwin_ref[0, vr[c], pl.ds(ns[chains[c][0]], WIN_FAR)])
             for k in range(2)] for c in cs]
    o_c = [_nn(vcbt_ref[0, chains[c][1]], ecb[c]) * rden[c] for c in cs]

    imp = []
    for c in cs:
        t = chains[c][0]
        v = _nn(ovt, ecb[c][:, 0:QT]) * rden[c][:, 0:QT]
        for h in range(1, HPG):
            v = v + _nn(ovt, ecb[c][:, h * QT:(h + 1) * QT]) * rden[c][:, h * QT:(h + 1) * QT]
        imp.append(jnp.where(forced[t], 1e30, jnp.where(future[t], -1e30, v)))
    sw = [_nt(kwin_ref[0, pl.ds(nw[chains[c][0]], wk), kl[c]], qwin[c]) + near_ref[chains[c][1], ck - wk:, :]
          for c in cs]

    grp = [[imp[c][SUBLANES * v:SUBLANES * (v + 1)] for v in range(ngrp)] for c in cs]
    cnt = [[jnp.zeros((SUBLANES, QT), F32) for _ in range(ngrp)] for c in cs]
    for jp in range(nslc):
        v0, r0 = divmod(jp, SUBLANES)
        for c in cs:
            row = jnp.broadcast_to(imp[c][jp:jp + 1, :], (SUBLANES, QT))
            for v in range(ngrp):
                if v < v0:
                    inc = jnp.where(row > grp[c][v], 1.0, 0.0)
                elif v > v0:
                    inc = jnp.where(row >= grp[c][v], 1.0, 0.0)
                else:
                    inc = jnp.where(sub > r0, jnp.where(row >= grp[c][v], 1.0, 0.0),
                                    jnp.where(row > grp[c][v], 1.0, 0.0))
                cnt[c][v] = cnt[c][v] + inc
    o_w = [jnp.concatenate([finish(flash(wcar[c][k], sw[c][:, halves[k]], vtwin_ref[0, vr[c], pl.ds(nw[chains[c][0]], wk)]))
                            for k in range(2)], axis=1) for c in cs]
    for c in cs:
        sel = jnp.concatenate(cnt[c], axis=0) < float(min(SLC_TOPN, nslc))
        sel_near = jnp.where(sel, 1.0, 0.0)
        first_near = (QT // SLC_BLOCK) * (qts[chains[c][0]] - (SLC_CHUNK_TILES - 1))
        sel_far = jnp.where(jidx < first_near, sel_near, 0.0)
        qaug_s[c, 0] = q0[c] + aug(sel_far)
        qaug_s[c, 1] = q0[c] + aug(sel_near)

    gts = [jax.nn.sigmoid(misc_ref[0, t * QT:(t + 1) * QT, :]).T for t in range(NSA_TILES)]
    gates = [[jnp.concatenate([gts[t][16 * g + br * HPG + h:16 * g + br * HPG + h + 1, :] for h in range(HPG)], axis=1)
              for br in range(3)] for t, g in chains]

    n_far = (NSA_TILES * pl.program_id(1)) // SLC_CHUNK_TILES
    n_chunks = n_far + 1

    def rows(c, i):
        return pl.multiple_of(QT * qts[chains[c][0]] - ck * i + SLC_OFF, QT)

    def scores(c, i):
        i = jnp.minimum(i, n_far)
        return _nt(kslc_ref[0, pl.ds(rows(c, i), ck), kl[c]], qaug_s[c, jnp.where(i == 0, 1, 0)])

    def vt(c, i):
        return vtslc_ref[0, vr[c], pl.ds(rows(c, i), ck)]

    for c in cs:
        s0_s[c] = scores(c, 0) + near_ref[chains[c][1]]

    def flash_halves(carry_c, s_ref, c, vt_chunk):
        return tuple(flash(carry_c[k], s_ref[c, :, halves[k]], vt_chunk) for k in range(2))

    def pair_body(p, carry):
        i = 2 * p
        carry = list(carry)
        for t in range(NSA_TILES):
            tc = [c for c in cs if chains[c][0] == t]
            for c in tc:
                s1_s[c] = scores(c, i + 1)
            for c in tc:
                carry[c] = flash_halves(carry[c], s0_s, c, vt(c, i))
            for c in tc:
                s0_s[c] = scores(c, i + 2)
            for c in tc:
                carry[c] = flash_halves(carry[c], s1_s, c, vt(c, i + 1))
        return tuple(carry)

    carry = lax.fori_loop(0, n_chunks // 2, pair_body, (init_halves,) * len(chains))
    carry = lax.cond(n_chunks % 2 == 1,
                     lambda cr: tuple(flash_halves(cr[c], s0_s, c, vt(c, n_far)) for c in cs),
                     lambda cr: cr, carry)
    carry = [tuple(jnp.concatenate([h[k] for h in carry[c]], axis=1) for k in range(2)) for c in cs]

    head = lax.broadcasted_iota(jnp.int32, (DH, nql), 1) // QT
    o_s = [finish(carry[c]) for c in cs]
    o = [gates[c][0] * o_c[c] + gates[c][1] * o_s[c] + gates[c][2] * o_w[c] for c in cs]
    ob = [o[c].astype(BF16) for c in cs]
    blocks = [jnp.concatenate([jnp.where(head == h, ob[c], jnp.zeros_like(ob[c])) for h in range(HPG)], axis=0)
              for c in cs]
    y = [_nt(eye4_ref[...], blocks[c]).astype(BF16) for c in cs]
    for c, (t, g) in enumerate(chains):
        out_ref[0, t * QT:(t + 1) * QT, g * HPG * DH:(g + 1) * HPG * DH] = y[c]


def _nsa(q, kslc, vtslc, kwin, vtwin, kcb, vcbt, misc, rel_bias):
    B, S, _ = q.shape
    G = NSA_GROUPS
    nq = S // QT
    ncmp = S // CMP_STRIDE
    nslc = S // SLC_BLOCK
    nql = HPG * QT
    near, cmpb = _nsa_bias_tables(rel_bias, S)
    wmask = jnp.asarray(np.tile(np.where(np.arange(QT)[:, None] > np.arange(QT)[None, :], 0.0, NEG), (1, HPG)), F32)
    ci = np.arange(ncmp)[None, :] * CMP_STRIDE
    sj = np.arange(nslc)[:, None] * SLC_BLOCK
    ovt = jnp.asarray(((ci < sj + SLC_BLOCK) & (ci + CMP_BLOCK > sj)).astype(np.float32), BF16)
    eye = jnp.eye(QT, dtype=BF16)
    eye4 = jnp.tile(eye, (1, HPG))
    kern = functools.partial(_nsa_kernel, ncmp=ncmp, nslc=nslc)
    nch = NSA_TILES * G
    per_b = lambda shp: pl.BlockSpec(shp, lambda b, i: (b,) + (0,) * (len(shp) - 1))
    full = lambda shp: pl.BlockSpec(shp, lambda b, i: (0,) * len(shp))
    return pl.pallas_call(
        kern,
        out_shape=jax.ShapeDtypeStruct((B, S, NSA_HEADS * DH), BF16),
        grid=(B, nq // NSA_TILES),
        in_specs=[
            pl.BlockSpec((1, NSA_TILES * QT, NSA_HEADS * DH), lambda b, i: (b, i, 0)),
            per_b((1, KV_PAD + S, G * LANES)),
            per_b((1, G * VT_ROWS, KV_PAD + S)),
            per_b((1, KV_PAD + S, G * LANES)),
            per_b((1, G * VT_ROWS, KV_PAD + S)),
            per_b((1, G, ncmp, LANES)),
            per_b((1, G, DH, ncmp)),
            pl.BlockSpec((1, NSA_TILES * QT, LANES), lambda b, i: (b, i, 0)),
            full((G, SLC_CHUNK_TILES * QT, nql)),
            full((G, 2 * ncmp, nql)),
            full((QT, nql)),
            full((nslc, ncmp)),
            full((QT, QT)),
            full((QT, nql)),
        ],
        out_specs=pl.BlockSpec((1, NSA_TILES * QT, NSA_HEADS * DH), lambda b, i: (b, i, 0)),
        scratch_shapes=[pltpu.VMEM((nch, 2, nql, LANES), BF16), pltpu.VMEM((nch, SLC_CHUNK_TILES * QT, nql), F32),
                        pltpu.VMEM((nch, SLC_CHUNK_TILES * QT, nql), F32)],
        compiler_params=pltpu.CompilerParams(dimension_semantics=("parallel", "arbitrary"),
                                             vmem_limit_bytes=VMEM_LIMIT),
        name="nsa",
    )(q, kslc, vtslc, kwin, vtwin, kcb, vcbt, misc, near, cmpb, wmask, ovt, eye, eye4)


GLA_NB = 2


def _gla_kernel(qk_ref, v_ref, vt_ref, misc_ref, r_ref, wal_ref, bal_ref, ng_ref, cum_ref, out_ref,
                state_s, o_s, *, ct, nbatch):
    H, dk, dv, C = GLA_HEADS, GLA_DK, GLA_DV, GLA_CHUNK
    kw = H * dk
    nb = range(nbatch)

    @pl.when(pl.program_id(1) == 0)
    def _():
        state_s[...] = jnp.zeros_like(state_s)

    cum = cum_ref[...]
    q_in, k_in, k_st, decay = [], [], [], []
    for bb in nb:
        z = _nn(misc_ref[bb].astype(BF16), wal_ref[...]) + bal_ref[...]
        log_a = (jnp.minimum(z, 0.0) - jnp.log(1.0 + jnp.exp(-jnp.abs(z)))) * (1.0 / GLA_TAU)
        a1, a2, a3 = _split3(log_a)
        cs = _nn(cum, a1) + _nn(cum, a2) + _nn(cum, a3)
        bc, bl = cs[:ct], cs[ct:]
        q = qk_ref[bb, :, :kw]
        k = qk_ref[bb, :, kw:]
        q_in.append((q * (dk ** -0.5)) * jnp.exp(bc))
        k_in.append((k * jnp.exp(-bc)).astype(BF16))
        k_st.append(k * jnp.exp(bl - bc))
        decay.append(jnp.exp(bl))
    lane_head = lax.broadcasted_iota(jnp.int32, (C, kw), 1) // dk
    rr = lax.broadcasted_iota(jnp.int32, (H * C, C), 0) % C
    cc = lax.broadcasted_iota(jnp.int32, (H * C, C), 1)
    causal = rr >= cc
    pair_row = lax.broadcasted_iota(jnp.int32, (2 * C, kw), 0) // C
    pair_head = lax.broadcasted_iota(jnp.int32, (2 * C, kw), 1) // dk

    for c in range(ct // C):
        r0 = c * C
        p0 = (c // 2) * 2 * C
        for bb in nb:
            qc = q_in[bb][r0:r0 + C]
            qcb = qc.astype(BF16)
            q_heads = jnp.concatenate([jnp.where(lane_head == h, qc, 0.0) for h in range(H)], axis=0).astype(BF16)
            attn = jnp.where(causal, _nt(q_heads, k_in[bb][r0:r0 + C]), 0.0).astype(BF16)
            kst_pair = k_st[bb][p0:p0 + 2 * C]
            dec = decay[bb][r0:r0 + 1]
            for h in range(H):
                st = state_s[bb, h]
                o = _nn(attn[h * C:(h + 1) * C], v_ref[bb, r0:r0 + C, h * dv:(h + 1) * dv])
                o = o + _nt(qcb, st.astype(BF16))
                o_s[bb, r0:r0 + C, h * dv:(h + 1) * dv] = o
                kst_h = jnp.where((pair_row == c % 2) & (pair_head == h), kst_pair, 0.0).astype(BF16)
                state_s[bb, h] = st * dec + _nn(vt_ref[bb, h * dv:(h + 1) * dv, p0:p0 + 2 * C], kst_h)

    for bb in nb:
        for h in range(H):
            oh = o_s[bb, :, h * dv:(h + 1) * dv]
            ms = jnp.mean(oh * oh, axis=-1, keepdims=True)
            r = r_ref[bb, :, h * dv:(h + 1) * dv]
            y = oh * lax.rsqrt(ms + EPS) * ng_ref[:, h * dv:(h + 1) * dv] * (r * jax.nn.sigmoid(r))
            out_ref[bb, :, h * dv:(h + 1) * dv] = y.astype(BF16)


def _gla(qkb, vb, vtb, misc, rb, w_alpha, b_alpha, norm_g, ct=512):
    B, S, _ = qkb.shape
    H, dk, dv, C = GLA_HEADS, GLA_DK, GLA_DV, GLA_CHUNK
    kw, vw = H * dk, H * dv
    nb = GLA_NB if B % GLA_NB == 0 else 1
    wal = jnp.zeros((LANES, kw), F32).at[32:32 + GLA_RANK].set(w_alpha).astype(BF16)
    r = np.arange(ct)
    tri = (r[:, None] // C == r[None, :] // C) & (r[:, None] >= r[None, :])
    tot = r[:, None] // C == r[None, :] // C
    cum = jnp.asarray(np.concatenate([tri, tot], axis=0).astype(np.float32), BF16)
    full = lambda shp: pl.BlockSpec(shp, lambda b, i: (0,) * len(shp))
    return pl.pallas_call(
        functools.partial(_gla_kernel, ct=ct, nbatch=nb),
        out_shape=jax.ShapeDtypeStruct((B, S, vw), BF16),
        grid=(B // nb, S // ct),
        in_specs=[
            pl.BlockSpec((nb, ct, 2 * kw), lambda b, i: (b, i, 0)),
            pl.BlockSpec((nb, ct, vw), lambda b, i: (b, i, 0)),
            pl.BlockSpec((nb, vw, ct), lambda b, i: (b, 0, i)),
            pl.BlockSpec((nb, ct, LANES), lambda b, i: (b, i, 0)),
            pl.BlockSpec((nb, ct, vw), lambda b, i: (b, i, 0)),
            full((LANES, kw)), full((1, kw)), full((1, vw)), full((2 * ct, ct)),
        ],
        out_specs=pl.BlockSpec((nb, ct, vw), lambda b, i: (b, i, 0)),
        scratch_shapes=[pltpu.VMEM((nb, H, dv, kw), F32), pltpu.VMEM((nb, ct, vw), F32)],
        compiler_params=pltpu.CompilerParams(dimension_semantics=("parallel", "arbitrary"),
                                             vmem_limit_bytes=VMEM_LIMIT),
        name="gla",
    )(qkb, vb, vtb, misc, rb, wal, b_alpha.reshape(1, kw), norm_g.reshape(1, vw), cum)


ROUTER_ROWS = 48
EXPERT_ROW0 = 8


SEG_ROWS = 16


def _local_rows(tm):
    return -(-(2 * tm + N_EXPERTS * (SEG_ROWS - 1)) // LANES) * LANES


def _outproj_kernel(ya_ref, yb_ref, mg_ref, x_ref, wa_ref, wb_ref, wo_ref, gf_ref, wr_ref, br_ref,
                    tri_ref, ltri_ref, x1_ref, xsl_ref, seg_ref, rw_ref, cnt_ref, carry_s, *, tm, ls):
    D = x_ref.shape[1]

    @pl.when(pl.program_id(0) == 0)
    def _():
        carry_s[...] = jnp.zeros_like(carry_s)

    ma = _nn(ya_ref[...], wa_ref[...])
    mb = _nn(yb_ref[...], wb_ref[...])
    merged = jax.nn.sigmoid(mg_ref[:, :D]) * ma + jax.nn.sigmoid(mg_ref[:, D:]) * mb
    x1 = x_ref[...] + _nn(merged.astype(BF16), wo_ref[...])
    x1_ref[...] = x1
    ms = jnp.mean(x1 * x1, axis=-1, keepdims=True)
    h2 = x1 * lax.rsqrt(ms + EPS) * gf_ref[...]
    hi = h2.astype(BF16)
    lg = _nt(wr_ref[...], hi) + br_ref[...]
    row8 = lax.broadcasted_iota(jnp.int32, (SUBLANES, tm), 0)
    gl = jnp.where(row8 < N_GROUPS, lg[0:SUBLANES], NEG)
    gmax = jnp.max(gl, axis=0, keepdims=True)
    g_sel = jnp.min(jnp.where(gl == gmax, row8, SUBLANES), axis=0, keepdims=True)
    g_prob = 1.0 / jnp.sum(jnp.where(row8 < N_GROUPS, jnp.exp(gl - gmax), 0.0), axis=0, keepdims=True)
    e_sel = jnp.zeros((EPG, tm), F32)
    for gi in range(N_GROUPS):
        r0 = EXPERT_ROW0 + gi * EPG
        e_sel = e_sel + jnp.where(g_sel == gi, lg[r0:r0 + EPG], 0.0)
    v1 = jnp.max(e_sel, axis=0, keepdims=True)
    i1 = jnp.min(jnp.where(e_sel == v1, row8, EPG), axis=0, keepdims=True)
    rest = jnp.where(row8 == i1, -jnp.inf, e_sel)
    v2 = jnp.max(rest, axis=0, keepdims=True)
    i2 = jnp.min(jnp.where(rest == v2, row8, EPG), axis=0, keepdims=True)
    t = jnp.exp(v2 - v1)
    w1 = g_prob / (1.0 + t)
    w2 = g_prob * t / (1.0 + t)
    e1 = g_sel * EPG + i1
    e2 = g_sel * EPG + i2

    rowe = lax.broadcasted_iota(jnp.int32, (N_EXPERTS, tm), 0)
    oh1 = rowe == e1
    oh2 = rowe == e2
    oh = jnp.where(oh1, 1.0, 0.0) + jnp.where(oh2, 1.0, 0.0)
    pre = _nn(oh.astype(BF16), tri_ref[...])
    cnt = jnp.sum(oh, axis=1, keepdims=True)
    padded = jnp.floor((cnt + (SEG_ROWS - 1)) * (1.0 / SEG_ROWS)) * SEG_ROWS
    c_b = jnp.broadcast_to(padded, (N_EXPERTS, LANES))
    c_hi = jnp.floor(c_b * (1.0 / 16.0))
    c_lo = c_b - 16.0 * c_hi
    base = 16.0 * _nn(ltri_ref[...], c_hi.astype(BF16)) + _nn(ltri_ref[...], c_lo.astype(BF16))
    loc = pre + base[:, 0:1]
    lpos1 = jnp.sum(jnp.where(oh1, loc, 0.0), axis=0, keepdims=True)
    lpos2 = jnp.sum(jnp.where(oh2, loc, 0.0), axis=0, keepdims=True)
    lane = lax.broadcasted_iota(jnp.int32, (N_EXPERTS, LANES), 1)
    seg_ref[0] = jnp.where(lane == 0, base, jnp.where(lane == 1, c_b, carry_s[...])).astype(jnp.int32)
    carry_s[...] = carry_s[...] + c_b
    cnt_ref[...] = carry_s[...].astype(jnp.int32)
    rw_ref[0] = jnp.concatenate([w1, w2, lpos1, lpos2, jnp.zeros((SUBLANES - 4, tm), F32)], axis=0)

    srow = lax.broadcasted_iota(jnp.int32, (ls, tm), 0)
    perm = jnp.where(srow == lpos1.astype(jnp.int32), 1.0, jnp.where(srow == lpos2.astype(jnp.int32), 1.0, 0.0))
    xsorted = _nn(perm.astype(BF16), hi)
    xsl_ref[0] = xsorted.astype(BF16)


def _outproj(ya, yb, mg, x, wa, wb, wo, g_ffn, w_rg, b_rg, w_re, b_re, tm=MOE_TM):
    T, D = x.shape
    nt = T // tm
    wr = jnp.zeros((ROUTER_ROWS, D), F32).at[0:N_GROUPS].set(w_rg.T).at[EXPERT_ROW0:EXPERT_ROW0 + N_EXPERTS].set(w_re.T)
    wr = wr.astype(BF16)
    br = jnp.zeros((ROUTER_ROWS, 1), F32).at[0:N_GROUPS, 0].set(b_rg).at[EXPERT_ROW0:EXPERT_ROW0 + N_EXPERTS, 0].set(b_re)
    r = np.arange(tm)
    tri = jnp.asarray((r[:, None] < r[None, :]).astype(np.float32), BF16)
    re = np.arange(N_EXPERTS)
    ltri = jnp.asarray((re[None, :] < re[:, None]).astype(np.float32), BF16)
    ls = _local_rows(tm)
    row = lambda n: pl.BlockSpec((tm, n), lambda i: (i, 0))
    full = lambda shp: pl.BlockSpec(shp, lambda i: (0,) * len(shp))
    return pl.pallas_call(
        functools.partial(_outproj_kernel, tm=tm, ls=ls),
        out_shape=[jax.ShapeDtypeStruct((T, D), F32), jax.ShapeDtypeStruct((nt, ls, D), BF16),
                   jax.ShapeDtypeStruct((nt, N_EXPERTS, LANES), jnp.int32),
                   jax.ShapeDtypeStruct((nt, SUBLANES, tm), F32),
                   jax.ShapeDtypeStruct((N_EXPERTS, LANES), jnp.int32)],
        grid=(nt,),
        in_specs=[row(ya.shape[1]), row(yb.shape[1]), row(2 * D), row(D),
                  full(wa.shape), full(wb.shape), full(wo.shape), full((1, D)),
                  full((ROUTER_ROWS, D)), full((ROUTER_ROWS, 1)), full((tm, tm)),
                  full((N_EXPERTS, N_EXPERTS))],
        out_specs=[row(D), pl.BlockSpec((1, ls, D), lambda i: (i, 0, 0)),
                   pl.BlockSpec((1, N_EXPERTS, LANES), lambda i: (i, 0, 0)),
                   pl.BlockSpec((1, SUBLANES, tm), lambda i: (i, 0, 0)),
                   full((N_EXPERTS, LANES))],
        scratch_shapes=[pltpu.VMEM((N_EXPERTS, LANES), F32)],
        compiler_params=pltpu.CompilerParams(dimension_semantics=("arbitrary",), vmem_limit_bytes=VMEM_LIMIT),
        name="outproj",
    )(ya, yb, mg, x, wa, wb, wo, g_ffn.reshape(1, D), wr, br, tri, ltri)


SEG_FIELDS = 4
SEG_BITS = (MOE_TM // SEG_ROWS).bit_length()
TILE_BITS = (_local_rows(MOE_TM) // SEG_ROWS).bit_length()
TAIL_BITS = (TE // SEG_ROWS).bit_length() - 1


def _segment_copies(n8, bits, make_copy, wait):
    def arms(lo, hi, off):
        for bit in reversed(range(lo, hi)):
            rows = SEG_ROWS << bit
            take = (n8 >> bit) & 1

            @pl.when(take == 1)
            def _(off=off, rows=rows):
                cp = make_copy(off, rows)
                if wait:
                    cp.wait()
                else:
                    cp.start()

            off = off + take * rows

    low = min(4, bits)
    if bits > low:
        @pl.when((n8 >> low) != 0)
        def _():
            arms(low, bits, 0)

    arms(0, low, ((n8 >> low) << low) * SEG_ROWS)


def _wait_rows(n8, sem, like_src, like_dst):
    def make_copy(off, rows):
        return pltpu.make_async_copy(like_src.at[pl.ds(0, rows), :], like_dst.at[pl.ds(0, rows), :], sem)

    _segment_copies(n8, TILE_BITS, make_copy, True)


def _dispatch_kernel(seg_ref, tail_ref, xsl_ref, xs_ref, zero_s, sem):
    i = pl.program_id(0)

    def segments(wait):
        def body(e, c):
            s0 = (i * N_EXPERTS + e) * SEG_FIELDS
            src0 = pl.multiple_of(seg_ref[s0], SEG_ROWS)
            dst0 = pl.multiple_of(seg_ref[s0 + 2], SEG_ROWS)

            def make_copy(off, rows):
                return pltpu.make_async_copy(xsl_ref.at[0, pl.ds(pl.multiple_of(src0 + off, SEG_ROWS), rows), :],
                                             xs_ref.at[pl.ds(pl.multiple_of(dst0 + off, SEG_ROWS), rows), :], sem)

            _segment_copies(seg_ref[s0 + 1], SEG_BITS, make_copy, wait)
            return c

        lax.fori_loop(0, N_EXPERTS, body, 0)

    def tails(wait):
        def body(e, c):
            dst0 = pl.multiple_of(tail_ref[2 * e], SEG_ROWS)

            def make_copy(off, rows):
                return pltpu.make_async_copy(zero_s.at[pl.ds(0, rows), :],
                                             xs_ref.at[pl.ds(pl.multiple_of(dst0 + off, SEG_ROWS), rows), :], sem)

            _segment_copies(tail_ref[2 * e + 1], TAIL_BITS, make_copy, wait)
            return c

        lax.fori_loop(0, N_EXPERTS, body, 0)

    def unused_tiles(wait):
        def body(t, c):
            cp = pltpu.make_async_copy(zero_s, xs_ref.at[pl.ds(pl.multiple_of(t * TE, TE), TE), :], sem)
            if wait:
                cp.wait()
            else:
                cp.start()
            return c

        lax.fori_loop(tail_ref[2 * N_EXPERTS], xs_ref.shape[0] // TE, body, 0)

    @pl.when(i == 0)
    def _():
        zero_s[...] = jnp.zeros_like(zero_s)
        tails(False)
        unused_tiles(False)
        tails(True)
        unused_tiles(True)

    segments(False)
    _wait_rows(seg_ref[i * N_EXPERTS * SEG_FIELDS + 3], sem, xsl_ref.at[0], xs_ref)


def _dispatch(seg, tail, xsl, n_rows):
    nt, ls, W = xsl.shape
    return pl.pallas_call(
        _dispatch_kernel,
        out_shape=jax.ShapeDtypeStruct((n_rows, W), xsl.dtype),
        grid_spec=pltpu.PrefetchScalarGridSpec(
            num_scalar_prefetch=2,
            grid=(nt,),
            in_specs=[pl.BlockSpec((1, ls, W), lambda i, sg, tl: (i, 0, 0))],
            out_specs=pl.BlockSpec(memory_space=pl.ANY),
            scratch_shapes=[pltpu.VMEM((TE, W), xsl.dtype), pltpu.SemaphoreType.DMA(())],
        ),
        compiler_params=pltpu.CompilerParams(dimension_semantics=("arbitrary",)),
        name="dispatch",
    )(seg, tail, xsl)


def _experts_kernel(te_ref, nv_ref, xs_ref, wgu_ref, wd_ref, out_ref):
    i = pl.program_id(0)

    @pl.when(i < nv_ref[0])
    def _():
        au = _nn(xs_ref[...], wgu_ref[0])
        a, u = au[:, :EXPERT_FF], au[:, EXPERT_FF:]
        hid = (a * jax.nn.sigmoid(a)) * u
        out_ref[...] = _nn(hid.astype(BF16), wd_ref[0]).astype(BF16)

    @pl.when(i >= nv_ref[0])
    def _():
        out_ref[...] = jnp.zeros_like(out_ref)


def _experts(tile_expert, n_valid, xs, w_gate_up, w_down):
    n_rows, D = xs.shape
    n_tiles = n_rows // TE
    last = lambda i, nv: jnp.minimum(i, nv[0] - 1)
    return pl.pallas_call(
        _experts_kernel,
        out_shape=jax.ShapeDtypeStruct((n_rows, D), BF16),
        grid_spec=pltpu.PrefetchScalarGridSpec(
            num_scalar_prefetch=2,
            grid=(n_tiles,),
            in_specs=[pl.BlockSpec((TE, D), lambda i, te, nv: (last(i, nv), 0)),
                      pl.BlockSpec((1, D, 2 * EXPERT_FF), lambda i, te, nv: (te[last(i, nv)], 0, 0)),
                      pl.BlockSpec((1, EXPERT_FF, D), lambda i, te, nv: (te[last(i, nv)], 0, 0))],
            out_specs=pl.BlockSpec((TE, D), lambda i, te, nv: (i, 0)),
        ),
        compiler_params=pltpu.CompilerParams(dimension_semantics=("arbitrary",), vmem_limit_bytes=VMEM_LIMIT),
        name="experts",
    )(tile_expert, n_valid, xs, w_gate_up, w_down)


def _combine_kernel(seg_ref, ys_ref, x1_ref, rw_ref, gfin_ref, out_ref, buf, sem, *, tm, ls, apply_norm):
    i = pl.program_id(0)
    nt = pl.num_programs(0)

    def segments(tile, wait):
        slot = tile % 2

        def body(e, c):
            s0 = (tile * N_EXPERTS + e) * SEG_FIELDS
            loc0 = pl.multiple_of(seg_ref[s0], SEG_ROWS)
            glob0 = pl.multiple_of(seg_ref[s0 + 2], SEG_ROWS)

            def make_copy(off, rows):
                return pltpu.make_async_copy(ys_ref.at[pl.ds(pl.multiple_of(glob0 + off, SEG_ROWS), rows), :],
                                             buf.at[slot, pl.ds(pl.multiple_of(loc0 + off, SEG_ROWS), rows), :],
                                             sem.at[slot])

            _segment_copies(seg_ref[s0 + 1], SEG_BITS, make_copy, wait)
            return c

        lax.fori_loop(0, N_EXPERTS, body, 0)

    @pl.when(i == 0)
    def _():
        buf[...] = jnp.zeros_like(buf)
        segments(i, False)

    @pl.when(i + 1 < nt)
    def _():
        segments(i + 1, False)

    _wait_rows(seg_ref[i * N_EXPERTS * SEG_FIELDS + 3], sem.at[i % 2], ys_ref, buf.at[i % 2])
    ysl = buf[i % 2]
    rw = rw_ref[0]
    pos = rw[2:4].astype(jnp.int32)
    rows_t = lax.broadcasted_iota(jnp.int32, (ls, tm), 0)
    wrow = jnp.sum(jnp.where(rows_t == pos[0:1], rw[0:1], jnp.where(rows_t == pos[1:2], rw[1:2], 0.0)),
                   axis=1, keepdims=True)
    ysw = (ysl.astype(F32) * wrow).astype(BF16)
    cols = jnp.concatenate([rw, jnp.zeros((LANES - SUBLANES, tm), F32)], axis=0).T.astype(jnp.int32)
    srow = lax.broadcasted_iota(jnp.int32, (tm, ls), 1)
    pick = jnp.where((srow == cols[:, 2:3]) | (srow == cols[:, 3:4]), 1.0, 0.0).astype(BF16)
    y = x1_ref[...] + _nn(pick, ysw)
    if apply_norm:
        ms = jnp.mean(y * y, axis=-1, keepdims=True)
        y = y * lax.rsqrt(ms + EPS) * gfin_ref[...]
    out_ref[...] = y


def _combine(seg, ys, x1, rw, g_final, apply_norm, tm=MOE_TM):
    T, D = x1.shape
    nt = T // tm
    ls = _local_rows(tm)
    return pl.pallas_call(
        functools.partial(_combine_kernel, tm=tm, ls=ls, apply_norm=apply_norm),
        out_shape=jax.ShapeDtypeStruct((T, D), F32),
        grid_spec=pltpu.PrefetchScalarGridSpec(
            num_scalar_prefetch=1,
            grid=(nt,),
            in_specs=[pl.BlockSpec(memory_space=pl.ANY),
                      pl.BlockSpec((tm, D), lambda i, sg: (i, 0)),
                      pl.BlockSpec((1, SUBLANES, tm), lambda i, sg: (i, 0, 0)),
                      pl.BlockSpec((1, D), lambda i, sg: (0, 0))],
            out_specs=pl.BlockSpec((tm, D), lambda i, sg: (i, 0)),
            scratch_shapes=[pltpu.VMEM((2, ls, D), BF16), pltpu.SemaphoreType.DMA((2,))],
        ),
        compiler_params=pltpu.CompilerParams(dimension_semantics=("arbitrary",), vmem_limit_bytes=VMEM_LIMIT),
        name="combine",
    )(seg, ys, x1, rw, g_final.reshape(1, D))


def _moe_plan(seg, counts, T, tm=MOE_TM):
    nt = T // tm
    n_tiles_max = (2 * T + nt * N_EXPERTS * (SEG_ROWS - 1)) // TE + N_EXPERTS
    total = counts[:, 0]
    tiles = (total + TE - 1) // TE
    ids = jnp.arange(N_EXPERTS)
    tile_end = jnp.sum(jnp.where(ids[None, :] <= ids[:, None], tiles[None, :], 0), axis=1)
    row0 = (tile_end - tiles) * TE
    n8 = seg[:, :, 1] // SEG_ROWS
    segtab = jnp.stack([seg[:, :, 0], n8, seg[:, :, 2] + row0[None, :],
                        jnp.broadcast_to(jnp.sum(n8, axis=1, keepdims=True), n8.shape)],
                       axis=-1).reshape(-1).astype(jnp.int32)
    tail = jnp.concatenate([jnp.stack([row0 + total, (tiles * TE - total) // SEG_ROWS], axis=-1).reshape(-1),
                            tile_end[-1:]]).astype(jnp.int32)
    tile_expert = jnp.minimum(jnp.sum(tile_end[None, :] <= jnp.arange(n_tiles_max)[:, None], axis=1),
                              N_EXPERTS - 1).astype(jnp.int32)
    return segtab, tail, tile_expert, tile_end[-1:].astype(jnp.int32), n_tiles_max * TE


def kernel(x, g_mix, w_in, nsa_pe_k, nsa_cmp_k_w1, nsa_cmp_k_w2, nsa_pe_v, nsa_cmp_v_w1, nsa_cmp_v_w2, rel_bias,
           gla_w_alpha, gla_b_alpha, gla_norm_g, w_branch_a, w_branch_b, w_out, g_ffn, w_router_group,
           b_router_group, w_router_expert, b_router_expert, w_exp_gate, w_exp_up, w_exp_down, g_final):
    B, S, D = x.shape
    T = B * S
    for l in range(w_in.shape[0]):
        (q, kslc, kwin, kvc, misc, qkb, vb, rb, mg, vts, vtw, vtb, wgu16, wd16) = _inproj(
            x, g_mix[l], w_in[l].T, [(w_exp_gate[l], w_exp_up[l]), (w_exp_down[l],)])
        kcb, vcbt = _compress(kvc, nsa_pe_k[l], nsa_cmp_k_w1[l], nsa_cmp_k_w2[l],
                              nsa_pe_v[l], nsa_cmp_v_w1[l], nsa_cmp_v_w2[l])
        ya = _nsa(q, kslc, vts, kwin, vtw, kcb, vcbt, misc, rel_bias)
        yb = _gla(qkb, vb, vtb, misc, rb, gla_w_alpha[l], gla_b_alpha[l], gla_norm_g[l])
        x1, xsl, seg, rw, counts = _outproj(
            ya.reshape(T, -1), yb.reshape(T, -1), mg.reshape(T, -1), x.reshape(T, D),
            w_branch_a[l].astype(BF16), w_branch_b[l].astype(BF16), w_out[l].astype(BF16), g_ffn[l],
            w_router_group[l], b_router_group[l], w_router_expert[l], b_router_expert[l])
        segtab, tail, tile_expert, n_valid, n_rows = _moe_plan(seg, counts, T)
        xs = _dispatch(segtab, tail, xsl, n_rows)
        ys = _experts(tile_expert, n_valid, xs, wgu16, wd16)
        last_layer = l == w_in.shape[0] - 1
        x = _combine(segtab, ys, x1, rw, g_final, apply_norm=last_layer).reshape(B, S, D)
    return x
```

```python
import functools
import math

import numpy as np
import jax
import jax.numpy as jnp
from jax import lax
from jax.experimental import pallas as pl
from jax.experimental.pallas import tpu as pltpu

F32 = jnp.float32
BF16 = jnp.bfloat16

NSA_HEADS = 8
NSA_GROUPS = 2
HPG = NSA_HEADS // NSA_GROUPS
DH = 64
CMP_BLOCK = 32
CMP_STRIDE = 16
CMP_HIDDEN = 128
SLC_BLOCK = 64
SLC_TOPN = 16
WINDOW = 512
GLA_HEADS = 4
GLA_DK = 64
GLA_DV = 128
GLA_RANK = 16
GLA_TAU = 16.0
GLA_CHUNK = 64
REL_BUCKETS = 32
REL_MAX_EXACT = REL_BUCKETS // 2
REL_MAX_DIST = 128
N_GROUPS = 4
EPG = 8
N_EXPERTS = N_GROUPS * EPG
EXPERT_FF = 256
EPS = 1e-6

LANES = 128
SUBLANES = 8
VMEM_LIMIT = 56 * 1024 * 1024

LOG2E = math.log2(math.e)
NEG = -1e30
BIG = float(2.0 ** 100)
QT = 128
SLC_CHUNK_TILES = 2
KV_PAD = WINDOW
SLC_OFF = KV_PAD - (SLC_CHUNK_TILES - 1) * QT
WIN_FAR = WINDOW - QT
TE = 512
MOE_TM = 512
VT_ROWS = DH + 16
NSA_TILES = 2


def _nt(a, b):
    return lax.dot_general(a, b, (((1,), (1,)), ((), ())), preferred_element_type=F32)


def _nn(a, b):
    return jnp.dot(a, b, preferred_element_type=F32)


def _split3(x):
    a = x.astype(BF16)
    r = x - a.astype(F32)
    b = r.astype(BF16)
    c = (r - b.astype(F32)).astype(BF16)
    return a, b, c


def _t5_bucket_np(rel):
    n = np.maximum(rel, 0)
    nf = np.maximum(n, 1).astype(np.float32)
    large = REL_MAX_EXACT + (np.log(nf / np.float32(REL_MAX_EXACT)) / np.float32(math.log(REL_MAX_DIST / REL_MAX_EXACT))
                             * np.float32(REL_BUCKETS - REL_MAX_EXACT)).astype(np.int32)
    return np.where(n < REL_MAX_EXACT, n, np.minimum(large, REL_BUCKETS - 1)).astype(np.int32)


def _inproj_tile(x_ref, g_ref, wq_ref, wkv_ref, wmisc_ref, wb_ref, wrest_ref, kplace_ref, *refs, tm, cast_groups,
                 seq_tile):
    ncast = sum(cast_groups)
    cast_in, refs = refs[:ncast], refs[ncast:]
    (oq, okslc, okwin, okv, omisc, oqkb, ovb, orb, omg, ovts, ovtw, ovtb), cast_out = refs[:12], refs[12:]
    first = 0
    for dst, n in zip(cast_out, cast_groups):
        col = 0
        for src in cast_in[first:first + n]:
            dst[:, :, col:col + src.shape[-1]] = src[...].astype(BF16)
            col += src.shape[-1]
        first += n
    x = x_ref[0]
    ms = jnp.mean(x * x, axis=-1, keepdims=True)
    h = (x * lax.rsqrt(ms + EPS) * g_ref[...]).astype(BF16)

    oq[0] = _nt(h, wq_ref[...]).astype(BF16)
    kv = _nt(h, wkv_ref[...])
    okv[0] = kv[:, 0:256]

    def spread(k):
        return _nn(k.astype(BF16), kplace_ref[...])

    row = lax.broadcasted_iota(jnp.int32, (tm, 256), 0) + seq_tile * tm
    lane = lax.broadcasted_iota(jnp.int32, (tm, 256), 1) % LANES
    onehot = jnp.where(lane - DH == row // SLC_BLOCK, 1.0, 0.0)
    okslc[0] = (spread(kv[:, 256:384]) + onehot).astype(BF16)
    okwin[0] = spread(kv[:, 512:640]).astype(BF16)

    ones_rows = jnp.where(lax.broadcasted_iota(jnp.int32, (VT_ROWS - DH, tm), 0) == 0, 1.0, 0.0)

    def vt_groups(v):
        t = v.T
        return jnp.concatenate([t[:DH], ones_rows, t[DH:], ones_rows], axis=0).astype(BF16)

    ovts[0] = vt_groups(kv[:, 384:512])
    ovtw[0] = vt_groups(kv[:, 640:768])
    omisc[0] = _nt(h, wmisc_ref[...])
    qkv = _nt(h, wb_ref[...])
    oqkb[0] = qkv[:, 0:512]
    ovb[0] = qkv[:, 512:1024].astype(BF16)
    ovtb[0] = qkv[:, 512:1024].T.astype(BF16)
    orb[0] = _nt(h, wrest_ref[0:512, :])
    omg[0] = _nt(h, wrest_ref[512:, :])


W_RUNS = (512, 6 * NSA_GROUPS * DH, 3 * NSA_HEADS, 1024, 16, 2560)
W_RUN0 = tuple(int(v) for v in np.cumsum((0,) + W_RUNS))
GATE_ROWS = W_RUNS[2] + W_RUNS[4]


def _load_inproj_weights(wt_hbm, wq, wkv, wmisc, wb, wrest, stage, gates, sem):
    rows = stage.shape[1]
    chunks = []
    for run, dst, dst0, scale in ((0, wq, 0, DH ** -0.5 * LOG2E), (1, wkv, 0, None), (2, gates, 0, None),
                                  (3, wb, 0, None), (4, gates, W_RUNS[2], None), (5, wrest, 0, None)):
        for r in range(0, W_RUNS[run], rows):
            chunks.append((W_RUN0[run] + r, min(rows, W_RUNS[run] - r), dst, dst0 + r, scale))

    def copy(k):
        src0, n = chunks[k][:2]
        return pltpu.make_async_copy(wt_hbm.at[pl.ds(src0, n)], stage.at[k % 2, pl.ds(0, n)], sem.at[k % 2])

    copy(0).start()
    for k, (_, n, dst, dst0, scale) in enumerate(chunks):
        if k + 1 < len(chunks):
            copy(k + 1).start()
        copy(k).wait()
        w = stage[k % 2, 0:n, :]
        if scale is not None:
            w = w * scale
        dst[dst0:dst0 + n, :] = w.astype(dst.dtype)
    g = gates[0:W_RUNS[2], :]
    z4 = jnp.zeros((HPG, g.shape[1]), F32)
    per_group = [g[(br * NSA_GROUPS + grp) * HPG:(br * NSA_GROUPS + grp + 1) * HPG]
                 for grp in range(NSA_GROUPS) for br in range(3)]
    misc = (per_group[0:3] + [z4] + per_group[3:6] + [z4, gates[W_RUNS[2]:GATE_ROWS, :],
            jnp.zeros((LANES - 48, g.shape[1]), F32)])
    wmisc[...] = jnp.concatenate(misc, axis=0).astype(BF16)


def _inproj_kernel(*refs, tm, cast_groups, batch, tiles):
    s = pl.program_id(0)
    ncast = sum(cast_groups)
    x_ref, g_ref, wt_hbm, kplace_ref = refs[:4]
    cast_in = refs[4:4 + ncast]
    outs = refs[4 + ncast:4 + ncast + 12 + len(cast_groups)]
    weights = refs[-8:-3]
    stage, gates, sem = refs[-3:]
    okslc, okwin, ovts, ovtw = outs[1], outs[2], outs[9], outs[10]

    @pl.when(s == 0)
    def _():
        _load_inproj_weights(wt_hbm, *weights, stage, gates, sem)

    @pl.when(s < batch)
    def _():
        lane = lax.broadcasted_iota(jnp.int32, okslc.shape[1:], 1) % LANES
        pad_keys = jnp.where(lane >= DH, 1.0, 0.0).astype(BF16)
        okslc[0] = pad_keys
        okwin[0] = pad_keys
        ovts[0] = jnp.zeros_like(ovts[0])
        ovtw[0] = jnp.zeros_like(ovtw[0])

    @pl.when(s >= batch)
    def _():
        _inproj_tile(x_ref, g_ref, *weights, kplace_ref, *cast_in, *outs, tm=tm, cast_groups=cast_groups,
                     seq_tile=(s - batch) % tiles)


def _inproj(x, g_mix, wt, to_bf16, tm=KV_PAD):
    B, S, D = x.shape
    assert wt.shape == (W_RUN0[-1], D)
    nsb = S // tm
    nsteps = B * nsb
    t_of = lambda s: jnp.maximum(s - B, 0)
    b_of = lambda s: jnp.where(s < B, s, (s - B) // nsb)
    step_block = lambda shp: pl.BlockSpec((shp[0] // nsteps,) + shp[1:], lambda s: (t_of(s), 0, 0))
    cast_in = [a for grp in to_bf16 for a in grp]
    cast_shapes = [grp[0].shape[:2] + (sum(a.shape[2] for a in grp),) for grp in to_bf16]
    src = np.arange(NSA_GROUPS * DH)
    kplace = np.zeros((NSA_GROUPS * DH, NSA_GROUPS * LANES), np.float32)
    kplace[src, (src // DH) * LANES + src % DH] = 1.0
    kplace = jnp.asarray(kplace, BF16)
    widths = [(512, BF16), (256, BF16), (256, BF16), (256, F32), (128, F32), (512, F32), (512, BF16),
              (512, F32), (2048, F32)]
    out_shape = [jax.ShapeDtypeStruct((B, S, n), dt) for n, dt in widths]
    out_specs = [pl.BlockSpec((1, tm, n), lambda s: (t_of(s) // nsb, t_of(s) % nsb, 0)) for n, _ in widths]
    for rows in (NSA_GROUPS * VT_ROWS, NSA_GROUPS * VT_ROWS, 512):
        out_shape.append(jax.ShapeDtypeStruct((B, rows, S), BF16))
        out_specs.append(pl.BlockSpec((1, rows, tm), lambda s: (t_of(s) // nsb, 0, t_of(s) % nsb)))
    pad_or_tile = lambda s: jnp.where(s < B, 0, (s - B) % nsb + 1)
    for k in (1, 2):
        out_shape[k] = jax.ShapeDtypeStruct((B, KV_PAD + S, widths[k][0]), BF16)
        out_specs[k] = pl.BlockSpec((1, tm, widths[k][0]), lambda s: (b_of(s), pad_or_tile(s), 0))
    for k in (9, 10):
        out_shape[k] = jax.ShapeDtypeStruct((B, NSA_GROUPS * VT_ROWS, KV_PAD + S), BF16)
        out_specs[k] = pl.BlockSpec((1, NSA_GROUPS * VT_ROWS, tm), lambda s: (b_of(s), 0, pad_or_tile(s)))
    return pl.pallas_call(
        functools.partial(_inproj_kernel, tm=tm, cast_groups=tuple(len(grp) for grp in to_bf16), batch=B, tiles=nsb),
        out_shape=out_shape + [jax.ShapeDtypeStruct(shp, BF16) for shp in cast_shapes],
        grid=(B + nsteps,),
        in_specs=[
            pl.BlockSpec((1, tm, D), lambda s: (t_of(s) // nsb, t_of(s) % nsb, 0)),
            pl.BlockSpec((1, D), lambda s: (0, 0)),
            pl.BlockSpec(memory_space=pl.ANY),
            pl.BlockSpec(kplace.shape, lambda s: (0, 0)),
        ] + [step_block(a.shape) for a in cast_in],
        out_specs=out_specs + [step_block(shp) for shp in cast_shapes],
        scratch_shapes=[pltpu.VMEM((W_RUNS[0], D), BF16), pltpu.VMEM((W_RUNS[1], D), BF16),
                        pltpu.VMEM((LANES, D), BF16), pltpu.VMEM((W_RUNS[3], D), BF16),
                        pltpu.VMEM((W_RUNS[5], D), BF16), pltpu.VMEM((2, tm, D), F32),
                        pltpu.VMEM((GATE_ROWS, D), F32), pltpu.SemaphoreType.DMA((2,))],
        compiler_params=pltpu.CompilerParams(dimension_semantics=("arbitrary",),
                                             vmem_limit_bytes=VMEM_LIMIT),
        name="inproj",
    )(x, g_mix.reshape(1, D), wt, kplace, *cast_in)


def _gelu_tanh(x):
    return 0.5 * x * (1.0 + jnp.tanh(math.sqrt(2.0 / math.pi) * (x + 0.044715 * (x * x * x))))


def _compress_kernel(xk_ref, xv_ref, pe_ref, w1_ref, w2k_ref, w2vt_ref, ok_ref, ovt_ref, *, nsub):
    for kind, x_ref in enumerate((xk_ref, xv_ref)):
        top = jnp.zeros((nsub, 2 * CMP_HIDDEN), F32)
        bot = jnp.zeros((nsub, 2 * CMP_HIDDEN), F32)
        for r in range(CMP_STRIDE):
            xr = x_ref[0, pl.ds(r, nsub, stride=CMP_STRIDE), :]
            top = top + _nn((xr + pe_ref[kind, 0, r:r + 1, :]).astype(BF16), w1_ref[kind, 0, r])
            bot = bot + _nn((xr + pe_ref[kind, 1, r:r + 1, :]).astype(BF16), w1_ref[kind, 1, r])
        hid = _gelu_tanh(top + pltpu.roll(bot, shift=nsub - 1, axis=0)).astype(BF16)
        for g in range(NSA_GROUPS):
            hg = hid[:, g * CMP_HIDDEN:(g + 1) * CMP_HIDDEN]
            if kind == 0:
                ok_ref[0, g] = _nn(hg, w2k_ref[...]).astype(BF16)
            else:
                ovt_ref[0, g] = _nt(w2vt_ref[...], hg).astype(BF16)


def _compress(kv_cmp, pe_k, w1k, w2k, pe_v, w1v, w2v):
    B, S, _ = kv_cmp.shape
    nsub = S // CMP_STRIDE
    G = NSA_GROUPS

    def prep(pe, w1):
        pe_t = jnp.tile(pe.reshape(2, CMP_STRIDE, DH), (1, 1, G))
        a = w1.reshape(2, CMP_STRIDE, DH, CMP_HIDDEN)
        z = jnp.zeros_like(a)
        w = jnp.concatenate([jnp.concatenate([a, z], axis=3), jnp.concatenate([z, a], axis=3)], axis=2)
        return pe_t, w.astype(BF16)

    pek, w1kb = prep(pe_k, w1k)
    pev, w1vb = prep(pe_v, w1v)
    pe = jnp.stack([pek, pev])
    w1 = jnp.stack([w1kb, w1vb])
    w2kp = jnp.concatenate([w2k, jnp.zeros_like(w2k)], axis=1).astype(BF16)
    w2vt = w2v.T.astype(BF16)
    full = lambda shp: pl.BlockSpec(shp, lambda b: (0,) * len(shp))
    return pl.pallas_call(
        functools.partial(_compress_kernel, nsub=nsub),
        out_shape=[jax.ShapeDtypeStruct((B, G, nsub, LANES), BF16),
                   jax.ShapeDtypeStruct((B, G, DH, nsub), BF16)],
        grid=(B,),
        in_specs=[pl.BlockSpec((1, S, G * DH), lambda b: (b, 0, 0)), pl.BlockSpec((1, S, G * DH), lambda b: (b, 0, 1)),
                  full(pe.shape), full(w1.shape), full((CMP_HIDDEN, LANES)), full((DH, CMP_HIDDEN))],
        out_specs=[pl.BlockSpec((1, G, nsub, LANES), lambda b: (b, 0, 0, 0)),
                   pl.BlockSpec((1, G, DH, nsub), lambda b: (b, 0, 0, 0))],
        compiler_params=pltpu.CompilerParams(dimension_semantics=("parallel",), vmem_limit_bytes=VMEM_LIMIT),
        name="compress",
    )(kv_cmp, kv_cmp, pe, w1, w2kp, w2vt)


def _bias_kernel(tbl_ref, bkn_ref, bkc_ref, near_ref, cmpb_ref):
    g = pl.program_id(0)
    for h in range(HPG):
        hd = g * HPG + h

        def lookup(bk):
            acc = jnp.full(bk.shape, NEG, F32)
            for b in range(REL_BUCKETS):
                acc = jnp.where(bk == b, tbl_ref[hd, b], acc)
            return acc

        vn = lookup(bkn_ref[...])
        near_ref[0, :, h * QT:(h + 1) * QT] = jnp.where(vn > 0.5 * NEG, (vn - tbl_ref[hd, REL_BUCKETS - 1]) * LOG2E, NEG)
        vc = lookup(bkc_ref[...])
        cmpb_ref[0, :, h * QT:(h + 1) * QT] = jnp.where(vc > 0.5 * NEG, vc * LOG2E, NEG)


def _nsa_bias_tables(rel_bias, seq):
    ql = np.arange(QT)
    ncmp = seq // CMP_STRIDE

    def buckets(rel):
        return jnp.asarray(np.where(rel >= 0, _t5_bucket_np(rel), -1).astype(np.int32))

    nk = SLC_CHUNK_TILES * QT
    bkn = buckets(ql[None, :] + nk - QT - np.arange(nk)[:, None])
    y = np.arange(2 * ncmp)
    bkc = buckets(ql[None, :] - CMP_STRIDE * (y[:, None] - ncmp) - (CMP_BLOCK - 1))
    nql = HPG * QT
    return pl.pallas_call(
        _bias_kernel,
        out_shape=[jax.ShapeDtypeStruct((NSA_GROUPS, nk, nql), F32),
                   jax.ShapeDtypeStruct((NSA_GROUPS, 2 * ncmp, nql), F32)],
        grid=(NSA_GROUPS,),
        in_specs=[pl.BlockSpec(memory_space=pltpu.SMEM),
                  pl.BlockSpec((nk, QT), lambda g: (0, 0)),
                  pl.BlockSpec((2 * ncmp, QT), lambda g: (0, 0))],
        out_specs=[pl.BlockSpec((1, nk, nql), lambda g: (g, 0, 0)),
                   pl.BlockSpec((1, 2 * ncmp, nql), lambda g: (g, 0, 0))],
        compiler_params=pltpu.CompilerParams(dimension_semantics=("parallel",)),
        name="t5bias",
    )(rel_bias.T, bkn, bkc)


def _nsa_kernel(q_ref, kslc_ref, vtslc_ref, kwin_ref, vtwin_ref, kcb_ref, vcbt_ref, misc_ref,
                near_ref, cmpb_ref, wmask_ref, ovt_ref, eye_ref, eye4_ref,
                out_ref, qaug_s, s0_s, s1_s, *, ncmp, nslc):
    G = NSA_GROUPS
    qts = [NSA_TILES * pl.program_id(1) + t for t in range(NSA_TILES)]
    chains = [(t, g) for t in range(NSA_TILES) for g in range(G)]
    nql = HPG * QT
    ck = SLC_CHUNK_TILES * QT
    wk = 2 * QT

    def flash(carry, s, vt_chunk):
        m, acc = carry
        m_new = jnp.maximum(m, jnp.max(s, axis=0, keepdims=True))
        alpha = jnp.exp2(m - m_new)
        p = jnp.exp2((s - m_new).astype(BF16))
        return m_new, alpha * acc + _nn(vt_chunk, p)

    def finish(carry):
        m, acc = carry
        return acc[:DH] * (1.0 / acc[DH:DH + 1])

    init = (jnp.full((1, nql), NEG, F32), jnp.zeros((VT_ROWS, nql), F32))
    halves = [slice(k * nql // 2, (k + 1) * nql // 2) for k in range(2)]
    init_halves = tuple((init[0][:, hs], init[1][:, hs]) for hs in halves)
    ns = [pl.multiple_of(QT * qt, QT) for qt in qts]
    nw = [pl.multiple_of(QT * qt + WIN_FAR, QT) for qt in qts]
    off = [pl.multiple_of(ncmp - (QT // CMP_STRIDE) * qt, SUBLANES) for qt in qts]
    lane = lax.broadcasted_iota(jnp.int32, (nql, LANES), 1)
    win_aug = jnp.where(lane >= DH, -BIG, 0.0).astype(BF16)
    jidx = lax.broadcasted_iota(jnp.int32, (nslc, QT), 0)
    tq = [qt * QT + lax.broadcasted_iota(jnp.int32, (nslc, QT), 1) for qt in qts]
    forced = [(jidx == 0) | (jidx == tq[t] // SLC_BLOCK) | (jidx == tq[t] // SLC_BLOCK - 1) for t in range(NSA_TILES)]
    future = [jidx * SLC_BLOCK > tq[t] for t in range(NSA_TILES)]
    sub = lax.broadcasted_iota(jnp.int32, (SUBLANES, QT), 0)
    ones_lo = jnp.ones((DH, QT), F32)
    eye = eye_ref[...]
    ovt = ovt_ref[...]
    ngrp = nslc // SUBLANES

    def aug(sel01):
        rows = [ones_lo, sel01]
        if LANES - DH - nslc:
            rows.append(jnp.ones((LANES - DH - nslc, QT), F32))
        m01 = _nt(eye, jnp.concatenate(rows, axis=0).astype(BF16))
        return jnp.concatenate([((m01 - 1.0) * BIG).astype(BF16)] * HPG, axis=0)

    cs = range(len(chains))
    kl = [slice(g * LANES, (g + 1) * LANES) for _, g in chains]
    vr = [slice(g * VT_ROWS, (g + 1) * VT_ROWS) for _, g in chains]
    low = lax.broadcasted_iota(jnp.int32, (QT, LANES), 1) < DH

    def head_slot(t, hd):
        pair = q_ref[0, t * QT:(t + 1) * QT, (hd // 2) * LANES:(hd // 2 + 1) * LANES].astype(F32)
        if hd % 2:
            pair = pltpu.roll(pair, shift=DH, axis=1)
        return jnp.where(low, pair, 0.0).astype(BF16)

    q0 = [jnp.concatenate([head_slot(t, g * HPG + h) for h in range(HPG)], axis=0) for t, g in chains]
    qwin = [q0[c] + win_aug for c in cs]

    bc = [cmpb_ref[g, pl.ds(off[t], ncmp), :] for t, g in chains]
    sc = [_nt(kcb_ref[0, chains[c][1]], q0[c]) + bc[c] for c in cs]
    sw = [_nt(kwin_ref[0, pl.ds(ns[chains[c][0]], WIN_FAR), kl[c]], qwin[c]) for c in cs]
    sw = [jnp.concatenate([sw[c][:QT] + wmask_ref[...], sw[c][QT:]], axis=0) for c in cs]

    ecb, rden = [], []
    for c in cs:
        mc = jnp.maximum(jnp.max(sc[c], axis=0, keepdims=True), 0.5 * NEG)
        ec = jnp.exp2(sc[c] - mc)
        rden.append(1.0 / jnp.maximum(jnp.sum(ec, axis=0, keepdims=True), jnp.finfo(F32).tiny))
        ecb.append(ec.astype(BF16))
    wcar = [[flash(init_halves[k], sw[c][:, halves[k]], vtwin_ref[0, vr[c], pl.ds(ns[chains[c][0]], WIN_FAR)])
             for k in range(2)] for c in cs]
    o_c = [_nn(vcbt_ref[0, chains[c][1]], ecb[c]) * rden[c] for c in cs]

    imp = []
    for c in cs:
        t = chains[c][0]
        v = _nn(ovt, ecb[c][:, 0:QT]) * rden[c][:, 0:QT]
        for h in range(1, HPG):
            v = v + _nn(ovt, ecb[c][:, h * QT:(h + 1) * QT]) * rden[c][:, h * QT:(h + 1) * QT]
        imp.append(jnp.where(forced[t], 1e30, jnp.where(future[t], -1e30, v)))
    sw = [_nt(kwin_ref[0, pl.ds(nw[chains[c][0]], wk), kl[c]], qwin[c]) + near_ref[chains[c][1], ck - wk:, :]
          for c in cs]

    grp = [[imp[c][SUBLANES * v:SUBLANES * (v + 1)] for v in range(ngrp)] for c in cs]
    cnt = [[jnp.zeros((SUBLANES, QT), F32) for _ in range(ngrp)] for c in cs]
    for jp in range(nslc):
        v0, r0 = divmod(jp, SUBLANES)
        for c in cs:
            row = jnp.broadcast_to(imp[c][jp:jp + 1, :], (SUBLANES, QT))
            for v in range(ngrp):
                if v < v0:
                    inc = jnp.where(row > grp[c][v], 1.0, 0.0)
                elif v > v0:
                    inc = jnp.where(row >= grp[c][v], 1.0, 0.0)
                else:
                    inc = jnp.where(sub > r0, jnp.where(row >= grp[c][v], 1.0, 0.0),
                                    jnp.where(row > grp[c][v], 1.0, 0.0))
                cnt[c][v] = cnt[c][v] + inc
    o_w = [jnp.concatenate([finish(flash(wcar[c][k], sw[c][:, halves[k]], vtwin_ref[0, vr[c], pl.ds(nw[chains[c][0]], wk)]))
                            for k in range(2)], axis=1) for c in cs]
    for c in cs:
        sel = jnp.concatenate(cnt[c], axis=0) < float(min(SLC_TOPN, nslc))
        sel_near = jnp.where(sel, 1.0, 0.0)
        first_near = (QT // SLC_BLOCK) * (qts[chains[c][0]] - (SLC_CHUNK_TILES - 1))
        sel_far = jnp.where(jidx < first_near, sel_near, 0.0)
        qaug_s[c, 0] = q0[c] + aug(sel_far)
        qaug_s[c, 1] = q0[c] + aug(sel_near)

    gts = [jax.nn.sigmoid(misc_ref[0, t * QT:(t + 1) * QT, :]).T for t in range(NSA_TILES)]
    gates = [[jnp.concatenate([gts[t][16 * g + br * HPG + h:16 * g + br * HPG + h + 1, :] for h in range(HPG)], axis=1)
              for br in range(3)] for t, g in chains]

    n_far = (NSA_TILES * pl.program_id(1)) // SLC_CHUNK_TILES
    n_chunks = n_far + 1

    def rows(c, i):
        return pl.multiple_of(QT * qts[chains[c][0]] - ck * i + SLC_OFF, QT)

    def scores(c, i):
        i = jnp.minimum(i, n_far)
        return _nt(kslc_ref[0, pl.ds(rows(c, i), ck), kl[c]], qaug_s[c, jnp.where(i == 0, 1, 0)])

    def vt(c, i):
        return vtslc_ref[0, vr[c], pl.ds(rows(c, i), ck)]

    for c in cs:
        s0_s[c] = scores(c, 0) + near_ref[chains[c][1]]

    def flash_halves(carry_c, s_ref, c, vt_chunk):
        return tuple(flash(carry_c[k], s_ref[c, :, halves[k]], vt_chunk) for k in range(2))

    def pair_body(p, carry):
        i = 2 * p
        carry = list(carry)
        for t in range(NSA_TILES):
            tc = [c for c in cs if chains[c][0] == t]
            for c in tc:
                s1_s[c] = scores(c, i + 1)
            for c in tc:
                carry[c] = flash_halves(carry[c], s0_s, c, vt(c, i))
            for c in tc:
                s0_s[c] = scores(c, i + 2)
            for c in tc:
                carry[c] = flash_halves(carry[c], s1_s, c, vt(c, i + 1))
        return tuple(carry)

    carry = lax.fori_loop(0, n_chunks // 2, pair_body, (init_halves,) * len(chains))
    carry = lax.cond(n_chunks % 2 == 1,
                     lambda cr: tuple(flash_halves(cr[c], s0_s, c, vt(c, n_far)) for c in cs),
                     lambda cr: cr, carry)
    carry = [tuple(jnp.concatenate([h[k] for h in carry[c]], axis=1) for k in range(2)) for c in cs]

    head = lax.broadcasted_iota(jnp.int32, (DH, nql), 1) // QT
    o_s = [finish(carry[c]) for c in cs]
    o = [gates[c][0] * o_c[c] + gates[c][1] * o_s[c] + gates[c][2] * o_w[c] for c in cs]
    ob = [o[c].astype(BF16) for c in cs]
    blocks = [jnp.concatenate([jnp.where(head == h, ob[c], jnp.zeros_like(ob[c])) for h in range(HPG)], axis=0)
              for c in cs]
    y = [_nt(eye4_ref[...], blocks[c]).astype(BF16) for c in cs]
    for c, (t, g) in enumerate(chains):
        out_ref[0, t * QT:(t + 1) * QT, g * HPG * DH:(g + 1) * HPG * DH] = y[c]


def _nsa(q, kslc, vtslc, kwin, vtwin, kcb, vcbt, misc, rel_bias):
    B, S, _ = q.shape
    G = NSA_GROUPS
    nq = S // QT
    ncmp = S // CMP_STRIDE
    nslc = S // SLC_BLOCK
    nql = HPG * QT
    near, cmpb = _nsa_bias_tables(rel_bias, S)
    wmask = jnp.asarray(np.tile(np.where(np.arange(QT)[:, None] > np.arange(QT)[None, :], 0.0, NEG), (1, HPG)), F32)
    ci = np.arange(ncmp)[None, :] * CMP_STRIDE
    sj = np.arange(nslc)[:, None] * SLC_BLOCK
    ovt = jnp.asarray(((ci < sj + SLC_BLOCK) & (ci + CMP_BLOCK > sj)).astype(np.float32), BF16)
    eye = jnp.eye(QT, dtype=BF16)
    eye4 = jnp.tile(eye, (1, HPG))
    kern = functools.partial(_nsa_kernel, ncmp=ncmp, nslc=nslc)
    nch = NSA_TILES * G
    per_b = lambda shp: pl.BlockSpec(shp, lambda b, i: (b,) + (0,) * (len(shp) - 1))
    full = lambda shp: pl.BlockSpec(shp, lambda b, i: (0,) * len(shp))
    return pl.pallas_call(
        kern,
        out_shape=jax.ShapeDtypeStruct((B, S, NSA_HEADS * DH), BF16),
        grid=(B, nq // NSA_TILES),
        in_specs=[
            pl.BlockSpec((1, NSA_TILES * QT, NSA_HEADS * DH), lambda b, i: (b, i, 0)),
            per_b((1, KV_PAD + S, G * LANES)),
            per_b((1, G * VT_ROWS, KV_PAD + S)),
            per_b((1, KV_PAD + S, G * LANES)),
            per_b((1, G * VT_ROWS, KV_PAD + S)),
            per_b((1, G, ncmp, LANES)),
            per_b((1, G, DH, ncmp)),
            pl.BlockSpec((1, NSA_TILES * QT, LANES), lambda b, i: (b, i, 0)),
            full((G, SLC_CHUNK_TILES * QT, nql)),
            full((G, 2 * ncmp, nql)),
            full((QT, nql)),
            full((nslc, ncmp)),
            full((QT, QT)),
            full((QT, nql)),
        ],
        out_specs=pl.BlockSpec((1, NSA_TILES * QT, NSA_HEADS * DH), lambda b, i: (b, i, 0)),
        scratch_shapes=[pltpu.VMEM((nch, 2, nql, LANES), BF16), pltpu.VMEM((nch, SLC_CHUNK_TILES * QT, nql), F32),
                        pltpu.VMEM((nch, SLC_CHUNK_TILES * QT, nql), F32)],
        compiler_params=pltpu.CompilerParams(dimension_semantics=("parallel", "arbitrary"),
                                             vmem_limit_bytes=VMEM_LIMIT),
        name="nsa",
    )(q, kslc, vtslc, kwin, vtwin, kcb, vcbt, misc, near, cmpb, wmask, ovt, eye, eye4)


GLA_NB = 2


def _gla_kernel(qk_ref, v_ref, vt_ref, misc_ref, r_ref, wal_ref, bal_ref, ng_ref, cum_ref, out_ref,
                state_s, o_s, *, ct, nbatch):
    H, dk, dv, C = GLA_HEADS, GLA_DK, GLA_DV, GLA_CHUNK
    kw = H * dk
    nb = range(nbatch)

    @pl.when(pl.program_id(1) == 0)
    def _():
        state_s[...] = jnp.zeros_like(state_s)

    cum = cum_ref[...]
    q_in, k_in, k_st, decay = [], [], [], []
    for bb in nb:
        z = _nn(misc_ref[bb].astype(BF16), wal_ref[...]) + bal_ref[...]
        log_a = (jnp.minimum(z, 0.0) - jnp.log(1.0 + jnp.exp(-jnp.abs(z)))) * (1.0 / GLA_TAU)
        a1, a2, a3 = _split3(log_a)
        cs = _nn(cum, a1) + _nn(cum, a2) + _nn(cum, a3)
        bc, bl = cs[:ct], cs[ct:]
        q = qk_ref[bb, :, :kw]
        k = qk_ref[bb, :, kw:]
        q_in.append((q * (dk ** -0.5)) * jnp.exp(bc))
        k_in.append((k * jnp.exp(-bc)).astype(BF16))
        k_st.append(k * jnp.exp(bl - bc))
        decay.append(jnp.exp(bl))
    lane_head = lax.broadcasted_iota(jnp.int32, (C, kw), 1) // dk
    rr = lax.broadcasted_iota(jnp.int32, (H * C, C), 0) % C
    cc = lax.broadcasted_iota(jnp.int32, (H * C, C), 1)
    causal = rr >= cc
    pair_row = lax.broadcasted_iota(jnp.int32, (2 * C, kw), 0) // C
    pair_head = lax.broadcasted_iota(jnp.int32, (2 * C, kw), 1) // dk

    for c in range(ct // C):
        r0 = c * C
        p0 = (c // 2) * 2 * C
        for bb in nb:
            qc = q_in[bb][r0:r0 + C]
            qcb = qc.astype(BF16)
            q_heads = jnp.concatenate([jnp.where(lane_head == h, qc, 0.0) for h in range(H)], axis=0).astype(BF16)
            attn = jnp.where(causal, _nt(q_heads, k_in[bb][r0:r0 + C]), 0.0).astype(BF16)
            kst_pair = k_st[bb][p0:p0 + 2 * C]
            dec = decay[bb][r0:r0 + 1]
            for h in range(H):
                st = state_s[bb, h]
                o = _nn(attn[h * C:(h + 1) * C], v_ref[bb, r0:r0 + C, h * dv:(h + 1) * dv])
                o = o + _nt(qcb, st.astype(BF16))
                o_s[bb, r0:r0 + C, h * dv:(h + 1) * dv] = o
                kst_h = jnp.where((pair_row == c % 2) & (pair_head == h), kst_pair, 0.0).astype(BF16)
                state_s[bb, h] = st * dec + _nn(vt_ref[bb, h * dv:(h + 1) * dv, p0:p0 + 2 * C], kst_h)

    for bb in nb:
        for h in range(H):
            oh = o_s[bb, :, h * dv:(h + 1) * dv]
            ms = jnp.mean(oh * oh, axis=-1, keepdims=True)
            r = r_ref[bb, :, h * dv:(h + 1) * dv]
            y = oh * lax.rsqrt(ms + EPS) * ng_ref[:, h * dv:(h + 1) * dv] * (r * jax.nn.sigmoid(r))
            out_ref[bb, :, h * dv:(h + 1) * dv] = y.astype(BF16)


def _gla(qkb, vb, vtb, misc, rb, w_alpha, b_alpha, norm_g, ct=512):
    B, S, _ = qkb.shape
    H, dk, dv, C = GLA_HEADS, GLA_DK, GLA_DV, GLA_CHUNK
    kw, vw = H * dk, H * dv
    nb = GLA_NB if B % GLA_NB == 0 else 1
    wal = jnp.zeros((LANES, kw), F32).at[32:32 + GLA_RANK].set(w_alpha).astype(BF16)
    r = np.arange(ct)
    tri = (r[:, None] // C == r[None, :] // C) & (r[:, None] >= r[None, :])
    tot = r[:, None] // C == r[None, :] // C
    cum = jnp.asarray(np.concatenate([tri, tot], axis=0).astype(np.float32), BF16)
    full = lambda shp: pl.BlockSpec(shp, lambda b, i: (0,) * len(shp))
    return pl.pallas_call(
        functools.partial(_gla_kernel, ct=ct, nbatch=nb),
        out_shape=jax.ShapeDtypeStruct((B, S, vw), BF16),
        grid=(B // nb, S // ct),
        in_specs=[
            pl.BlockSpec((nb, ct, 2 * kw), lambda b, i: (b, i, 0)),
            pl.BlockSpec((nb, ct, vw), lambda b, i: (b, i, 0)),
            pl.BlockSpec((nb, vw, ct), lambda b, i: (b, 0, i)),
            pl.BlockSpec((nb, ct, LANES), lambda b, i: (b, i, 0)),
            pl.BlockSpec((nb, ct, vw), lambda b, i: (b, i, 0)),
            full((LANES, kw)), full((1, kw)), full((1, vw)), full((2 * ct, ct)),
        ],
        out_specs=pl.BlockSpec((nb, ct, vw), lambda b, i: (b, i, 0)),
        scratch_shapes=[pltpu.VMEM((nb, H, dv, kw), F32), pltpu.VMEM((nb, ct, vw), F32)],
        compiler_params=pltpu.CompilerParams(dimension_semantics=("parallel", "arbitrary"),
                                             vmem_limit_bytes=VMEM_LIMIT),
        name="gla",
    )(qkb, vb, vtb, misc, rb, wal, b_alpha.reshape(1, kw), norm_g.reshape(1, vw), cum)


ROUTER_ROWS = 48
EXPERT_ROW0 = 8


SEG_ROWS = 16


def _local_rows(tm):
    return -(-(2 * tm + N_EXPERTS * (SEG_ROWS - 1)) // LANES) * LANES


def _outproj_kernel(ya_ref, yb_ref, mg_ref, x_ref, wa_ref, wb_ref, wo_ref, gf_ref, wr_ref, br_ref,
                    tri_ref, ltri_ref, x1_ref, xsl_ref, seg_ref, rw_ref, cnt_ref, carry_s, *, tm, ls):
    D = x_ref.shape[1]

    @pl.when(pl.program_id(0) == 0)
    def _():
        carry_s[...] = jnp.zeros_like(carry_s)

    ma = _nn(ya_ref[...], wa_ref[...])
    mb = _nn(yb_ref[...], wb_ref[...])
    merged = jax.nn.sigmoid(mg_ref[:, :D]) * ma + jax.nn.sigmoid(mg_ref[:, D:]) * mb
    x1 = x_ref[...] + _nn(merged.astype(BF16), wo_ref[...])
    x1_ref[...] = x1
    ms = jnp.mean(x1 * x1, axis=-1, keepdims=True)
    h2 = x1 * lax.rsqrt(ms + EPS) * gf_ref[...]
    hi = h2.astype(BF16)
    lg = _nt(wr_ref[...], hi) + br_ref[...]
    row8 = lax.broadcasted_iota(jnp.int32, (SUBLANES, tm), 0)
    gl = jnp.where(row8 < N_GROUPS, lg[0:SUBLANES], NEG)
    gmax = jnp.max(gl, axis=0, keepdims=True)
    g_sel = jnp.min(jnp.where(gl == gmax, row8, SUBLANES), axis=0, keepdims=True)
    g_prob = 1.0 / jnp.sum(jnp.where(row8 < N_GROUPS, jnp.exp(gl - gmax), 0.0), axis=0, keepdims=True)
    e_sel = jnp.zeros((EPG, tm), F32)
    for gi in range(N_GROUPS):
        r0 = EXPERT_ROW0 + gi * EPG
        e_sel = e_sel + jnp.where(g_sel == gi, lg[r0:r0 + EPG], 0.0)
    v1 = jnp.max(e_sel, axis=0, keepdims=True)
    i1 = jnp.min(jnp.where(e_sel == v1, row8, EPG), axis=0, keepdims=True)
    rest = jnp.where(row8 == i1, -jnp.inf, e_sel)
    v2 = jnp.max(rest, axis=0, keepdims=True)
    i2 = jnp.min(jnp.where(rest == v2, row8, EPG), axis=0, keepdims=True)
    t = jnp.exp(v2 - v1)
    w1 = g_prob / (1.0 + t)
    w2 = g_prob * t / (1.0 + t)
    e1 = g_sel * EPG + i1
    e2 = g_sel * EPG + i2

    rowe = lax.broadcasted_iota(jnp.int32, (N_EXPERTS, tm), 0)
    oh1 = rowe == e1
    oh2 = rowe == e2
    oh = jnp.where(oh1, 1.0, 0.0) + jnp.where(oh2, 1.0, 0.0)
    pre = _nn(oh.astype(BF16), tri_ref[...])
    cnt = jnp.sum(oh, axis=1, keepdims=True)
    padded = jnp.floor((cnt + (SEG_ROWS - 1)) * (1.0 / SEG_ROWS)) * SEG_ROWS
    c_b = jnp.broadcast_to(padded, (N_EXPERTS, LANES))
    c_hi = jnp.floor(c_b * (1.0 / 16.0))
    c_lo = c_b - 16.0 * c_hi
    base = 16.0 * _nn(ltri_ref[...], c_hi.astype(BF16)) + _nn(ltri_ref[...], c_lo.astype(BF16))
    loc = pre + base[:, 0:1]
    lpos1 = jnp.sum(jnp.where(oh1, loc, 0.0), axis=0, keepdims=True)
    lpos2 = jnp.sum(jnp.where(oh2, loc, 0.0), axis=0, keepdims=True)
    lane = lax.broadcasted_iota(jnp.int32, (N_EXPERTS, LANES), 1)
    seg_ref[0] = jnp.where(lane == 0, base, jnp.where(lane == 1, c_b, carry_s[...])).astype(jnp.int32)
    carry_s[...] = carry_s[...] + c_b
    cnt_ref[...] = carry_s[...].astype(jnp.int32)
    rw_ref[0] = jnp.concatenate([w1, w2, lpos1, lpos2, jnp.zeros((SUBLANES - 4, tm), F32)], axis=0)

    srow = lax.broadcasted_iota(jnp.int32, (ls, tm), 0)
    perm = jnp.where(srow == lpos1.astype(jnp.int32), 1.0, jnp.where(srow == lpos2.astype(jnp.int32), 1.0, 0.0))
    xsorted = _nn(perm.astype(BF16), hi)
    xsl_ref[0] = xsorted.astype(BF16)


def _outproj(ya, yb, mg, x, wa, wb, wo, g_ffn, w_rg, b_rg, w_re, b_re, tm=MOE_TM):
    T, D = x.shape
    nt = T // tm
    wr = jnp.zeros((ROUTER_ROWS, D), F32).at[0:N_GROUPS].set(w_rg.T).at[EXPERT_ROW0:EXPERT_ROW0 + N_EXPERTS].set(w_re.T)
    wr = wr.astype(BF16)
    br = jnp.zeros((ROUTER_ROWS, 1), F32).at[0:N_GROUPS, 0].set(b_rg).at[EXPERT_ROW0:EXPERT_ROW0 + N_EXPERTS, 0].set(b_re)
    r = np.arange(tm)
    tri = jnp.asarray((r[:, None] < r[None, :]).astype(np.float32), BF16)
    re = np.arange(N_EXPERTS)
    ltri = jnp.asarray((re[None, :] < re[:, None]).astype(np.float32), BF16)
    ls = _local_rows(tm)
    row = lambda n: pl.BlockSpec((tm, n), lambda i: (i, 0))
    full = lambda shp: pl.BlockSpec(shp, lambda i: (0,) * len(shp))
    return pl.pallas_call(
        functools.partial(_outproj_kernel, tm=tm, ls=ls),
        out_shape=[jax.ShapeDtypeStruct((T, D), F32), jax.ShapeDtypeStruct((nt, ls, D), BF16),
                   jax.ShapeDtypeStruct((nt, N_EXPERTS, LANES), jnp.int32),
                   jax.ShapeDtypeStruct((nt, SUBLANES, tm), F32),
                   jax.ShapeDtypeStruct((N_EXPERTS, LANES), jnp.int32)],
        grid=(nt,),
        in_specs=[row(ya.shape[1]), row(yb.shape[1]), row(2 * D), row(D),
                  full(wa.shape), full(wb.shape), full(wo.shape), full((1, D)),
                  full((ROUTER_ROWS, D)), full((ROUTER_ROWS, 1)), full((tm, tm)),
                  full((N_EXPERTS, N_EXPERTS))],
        out_specs=[row(D), pl.BlockSpec((1, ls, D), lambda i: (i, 0, 0)),
                   pl.BlockSpec((1, N_EXPERTS, LANES), lambda i: (i, 0, 0)),
                   pl.BlockSpec((1, SUBLANES, tm), lambda i: (i, 0, 0)),
                   full((N_EXPERTS, LANES))],
        scratch_shapes=[pltpu.VMEM((N_EXPERTS, LANES), F32)],
        compiler_params=pltpu.CompilerParams(dimension_semantics=("arbitrary",), vmem_limit_bytes=VMEM_LIMIT),
        name="outproj",
    )(ya, yb, mg, x, wa, wb, wo, g_ffn.reshape(1, D), wr, br, tri, ltri)


SEG_FIELDS = 4
SEG_BITS = (MOE_TM // SEG_ROWS).bit_length()
TILE_BITS = (_local_rows(MOE_TM) // SEG_ROWS).bit_length()
TAIL_BITS = (TE // SEG_ROWS).bit_length() - 1


def _segment_copies(units, bits, make_copy, wait):
    def arms(lo, hi, off):
        for bit in reversed(range(lo, hi)):
            rows = SEG_ROWS << bit
            take = (units >> bit) & 1

            @pl.when(take == 1)
            def _(off=off, rows=rows):
                cp = make_copy(off, rows)
                if wait:
                    cp.wait()
                else:
                    cp.start()

            off = off + take * rows

    low = min(2, bits)
    if bits > low:
        @pl.when((units >> low) != 0)
        def _():
            arms(low, bits, 0)

    arms(0, low, ((units >> low) << low) * SEG_ROWS)


def _wait_rows(units, sem, like_src, like_dst):
    def make_copy(off, rows):
        return pltpu.make_async_copy(like_src.at[pl.ds(0, rows), :], like_dst.at[pl.ds(0, rows), :], sem)

    _segment_copies(units, TILE_BITS, make_copy, True)


def _dispatch_kernel(seg_ref, tail_ref, xsl_ref, xs_ref, zero_s, sem):
    i = pl.program_id(0)

    def segments(wait):
        def body(e, c):
            s0 = (i * N_EXPERTS + e) * SEG_FIELDS
            src0 = pl.multiple_of(seg_ref[s0], SEG_ROWS)
            dst0 = pl.multiple_of(seg_ref[s0 + 2], SEG_ROWS)

            def make_copy(off, rows):
                return pltpu.make_async_copy(xsl_ref.at[0, pl.ds(pl.multiple_of(src0 + off, SEG_ROWS), rows), :],
                                             xs_ref.at[pl.ds(pl.multiple_of(dst0 + off, SEG_ROWS), rows), :], sem)

            _segment_copies(seg_ref[s0 + 1], SEG_BITS, make_copy, wait)
            return c

        lax.fori_loop(0, N_EXPERTS, body, 0)

    def tails(wait):
        def body(e, c):
            dst0 = pl.multiple_of(tail_ref[2 * e], SEG_ROWS)

            def make_copy(off, rows):
                return pltpu.make_async_copy(zero_s.at[pl.ds(0, rows), :],
                                             xs_ref.at[pl.ds(pl.multiple_of(dst0 + off, SEG_ROWS), rows), :], sem)

            _segment_copies(tail_ref[2 * e + 1], TAIL_BITS, make_copy, wait)
            return c

        lax.fori_loop(0, N_EXPERTS, body, 0)

    def unused_tiles(wait):
        def body(t, c):
            cp = pltpu.make_async_copy(zero_s, xs_ref.at[pl.ds(pl.multiple_of(t * TE, TE), TE), :], sem)
            if wait:
                cp.wait()
            else:
                cp.start()
            return c

        lax.fori_loop(tail_ref[2 * N_EXPERTS], xs_ref.shape[0] // TE, body, 0)

    @pl.when(i == 0)
    def _():
        zero_s[...] = jnp.zeros_like(zero_s)
        tails(False)
        unused_tiles(False)
        tails(True)
        unused_tiles(True)

    segments(False)
    _wait_rows(seg_ref[i * N_EXPERTS * SEG_FIELDS + 3], sem, xsl_ref.at[0], xs_ref)


def _dispatch(seg, tail, xsl, n_rows):
    nt, ls, W = xsl.shape
    return pl.pallas_call(
        _dispatch_kernel,
        out_shape=jax.ShapeDtypeStruct((n_rows, W), xsl.dtype),
        grid_spec=pltpu.PrefetchScalarGridSpec(
            num_scalar_prefetch=2,
            grid=(nt,),
            in_specs=[pl.BlockSpec((1, ls, W), lambda i, sg, tl: (i, 0, 0))],
            out_specs=pl.BlockSpec(memory_space=pl.ANY),
            scratch_shapes=[pltpu.VMEM((TE, W), xsl.dtype), pltpu.SemaphoreType.DMA(())],
        ),
        compiler_params=pltpu.CompilerParams(dimension_semantics=("arbitrary",)),
        name="dispatch",
    )(seg, tail, xsl)


def _experts_kernel(te_ref, nv_ref, xs_ref, wgu_ref, wd_ref, out_ref):
    @pl.when(pl.program_id(0) < nv_ref[0])
    def _():
        au = _nn(xs_ref[...], wgu_ref[0])
        a, u = au[:, :EXPERT_FF], au[:, EXPERT_FF:]
        hid = (a * jax.nn.sigmoid(a)) * u
        out_ref[...] = _nn(hid.astype(BF16), wd_ref[0]).astype(BF16)


def _experts(tile_expert, n_valid, xs, w_gate_up, w_down):
    n_rows, D = xs.shape
    n_tiles = n_rows // TE
    last = lambda i, nv: jnp.minimum(i, nv[0] - 1)
    return pl.pallas_call(
        _experts_kernel,
        out_shape=jax.ShapeDtypeStruct((n_rows, D), BF16),
        input_output_aliases={2: 0},
        grid_spec=pltpu.PrefetchScalarGridSpec(
            num_scalar_prefetch=2,
            grid=(n_tiles,),
            in_specs=[pl.BlockSpec((TE, D), lambda i, te, nv: (last(i, nv), 0)),
                      pl.BlockSpec((1, D, 2 * EXPERT_FF), lambda i, te, nv: (te[last(i, nv)], 0, 0)),
                      pl.BlockSpec((1, EXPERT_FF, D), lambda i, te, nv: (te[last(i, nv)], 0, 0))],
            out_specs=pl.BlockSpec((TE, D), lambda i, te, nv: (last(i, nv), 0)),
        ),
        compiler_params=pltpu.CompilerParams(dimension_semantics=("arbitrary",), vmem_limit_bytes=VMEM_LIMIT),
        name="experts",
    )(tile_expert, n_valid, xs, w_gate_up, w_down)


def _combine_kernel(seg_ref, ys_ref, x1_ref, rw_ref, gfin_ref, out_ref, buf, sem, *, tm, ls, apply_norm):
    i = pl.program_id(0)
    nt = pl.num_programs(0)

    def segments(tile, wait):
        slot = tile % 2

        def body(e, c):
            s0 = (tile * N_EXPERTS + e) * SEG_FIELDS
            loc0 = pl.multiple_of(seg_ref[s0], SEG_ROWS)
            glob0 = pl.multiple_of(seg_ref[s0 + 2], SEG_ROWS)

            def make_copy(off, rows):
                return pltpu.make_async_copy(ys_ref.at[pl.ds(pl.multiple_of(glob0 + off, SEG_ROWS), rows), :],
                                             buf.at[slot, pl.ds(pl.multiple_of(loc0 + off, SEG_ROWS), rows), :],
                                             sem.at[slot])

            _segment_copies(seg_ref[s0 + 1], SEG_BITS, make_copy, wait)
            return c

        lax.fori_loop(0, N_EXPERTS, body, 0)

    @pl.when(i == 0)
    def _():
        buf[...] = jnp.zeros_like(buf)
        segments(i, False)

    @pl.when(i + 1 < nt)
    def _():
        segments(i + 1, False)

    _wait_rows(seg_ref[i * N_EXPERTS * SEG_FIELDS + 3], sem.at[i % 2], ys_ref, buf.at[i % 2])
    ysl = buf[i % 2]
    rw = rw_ref[0]
    pos = rw[2:4].astype(jnp.int32)
    rows_t = lax.broadcasted_iota(jnp.int32, (ls, tm), 0)
    wrow = jnp.sum(jnp.where(rows_t == pos[0:1], rw[0:1], jnp.where(rows_t == pos[1:2], rw[1:2], 0.0)),
                   axis=1, keepdims=True)
    ysw = (ysl.astype(F32) * wrow).astype(BF16)
    cols = jnp.concatenate([rw, jnp.zeros((LANES - SUBLANES, tm), F32)], axis=0).T.astype(jnp.int32)
    srow = lax.broadcasted_iota(jnp.int32, (tm, ls), 1)
    pick = jnp.where((srow == cols[:, 2:3]) | (srow == cols[:, 3:4]), 1.0, 0.0).astype(BF16)
    y = x1_ref[...] + _nn(pick, ysw)
    if apply_norm:
        ms = jnp.mean(y * y, axis=-1, keepdims=True)
        y = y * lax.rsqrt(ms + EPS) * gfin_ref[...]
    out_ref[...] = y


def _combine(seg, ys, x1, rw, g_final, apply_norm, tm=MOE_TM):
    T, D = x1.shape
    nt = T // tm
    ls = _local_rows(tm)
    return pl.pallas_call(
        functools.partial(_combine_kernel, tm=tm, ls=ls, apply_norm=apply_norm),
        out_shape=jax.ShapeDtypeStruct((T, D), F32),
        grid_spec=pltpu.PrefetchScalarGridSpec(
            num_scalar_prefetch=1,
            grid=(nt,),
            in_specs=[pl.BlockSpec(memory_space=pl.ANY),
                      pl.BlockSpec((tm, D), lambda i, sg: (i, 0)),
                      pl.BlockSpec((1, SUBLANES, tm), lambda i, sg: (i, 0, 0)),
                      pl.BlockSpec((1, D), lambda i, sg: (0, 0))],
            out_specs=pl.BlockSpec((tm, D), lambda i, sg: (i, 0)),
            scratch_shapes=[pltpu.VMEM((2, ls, D), BF16), pltpu.SemaphoreType.DMA((2,))],
        ),
        compiler_params=pltpu.CompilerParams(dimension_semantics=("arbitrary",), vmem_limit_bytes=VMEM_LIMIT),
        name="combine",
    )(seg, ys, x1, rw, g_final.reshape(1, D))


def _moe_plan(seg, counts, T, tm=MOE_TM):
    nt = T // tm
    n_tiles_max = (2 * T + nt * N_EXPERTS * (SEG_ROWS - 1)) // TE + N_EXPERTS
    total = counts[:, 0]
    tiles = (total + TE - 1) // TE
    ids = jnp.arange(N_EXPERTS)
    tile_end = jnp.sum(jnp.where(ids[None, :] <= ids[:, None], tiles[None, :], 0), axis=1)
    row0 = (tile_end - tiles) * TE
    units = seg[:, :, 1] // SEG_ROWS
    segtab = jnp.stack([seg[:, :, 0], units, seg[:, :, 2] + row0[None, :],
                        jnp.broadcast_to(jnp.sum(units, axis=1, keepdims=True), units.shape)],
                       axis=-1).reshape(-1).astype(jnp.int32)
    tail = jnp.concatenate([jnp.stack([row0 + total, (tiles * TE - total) // SEG_ROWS], axis=-1).reshape(-1),
                            tile_end[-1:]]).astype(jnp.int32)
    tile_expert = jnp.minimum(jnp.sum(tile_end[None, :] <= jnp.arange(n_tiles_max)[:, None], axis=1),
                              N_EXPERTS - 1).astype(jnp.int32)
    return segtab, tail, tile_expert, tile_end[-1:].astype(jnp.int32), n_tiles_max * TE


def kernel(x, g_mix, w_in, nsa_pe_k, nsa_cmp_k_w1, nsa_cmp_k_w2, nsa_pe_v, nsa_cmp_v_w1, nsa_cmp_v_w2, rel_bias,
           gla_w_alpha, gla_b_alpha, gla_norm_g, w_branch_a, w_branch_b, w_out, g_ffn, w_router_group,
           b_router_group, w_router_expert, b_router_expert, w_exp_gate, w_exp_up, w_exp_down, g_final):
    B, S, D = x.shape
    T = B * S
    for l in range(w_in.shape[0]):
        (q, kslc, kwin, kvc, misc, qkb, vb, rb, mg, vts, vtw, vtb, wgu16, wd16) = _inproj(
            x, g_mix[l], w_in[l].T, [(w_exp_gate[l], w_exp_up[l]), (w_exp_down[l],)])
        kcb, vcbt = _compress(kvc, nsa_pe_k[l], nsa_cmp_k_w1[l], nsa_cmp_k_w2[l],
                              nsa_pe_v[l], nsa_cmp_v_w1[l], nsa_cmp_v_w2[l])
        ya = _nsa(q, kslc, vts, kwin, vtw, kcb, vcbt, misc, rel_bias)
        yb = _gla(qkb, vb, vtb, misc, rb, gla_w_alpha[l], gla_b_alpha[l], gla_norm_g[l])
        x1, xsl, seg, rw, counts = _outproj(
            ya.reshape(T, -1), yb.reshape(T, -1), mg.reshape(T, -1), x.reshape(T, D),
            w_branch_a[l].astype(BF16), w_branch_b[l].astype(BF16), w_out[l].astype(BF16), g_ffn[l],
            w_router_group[l], b_router_group[l], w_router_expert[l], b_router_expert[l])
        segtab, tail, tile_expert, n_valid, n_rows = _moe_plan(seg, counts, T)
        xs = _dispatch(segtab, tail, xsl, n_rows)
        ys = _experts(tile_expert, n_valid, xs, wgu16, wd16)
        last_layer = l == w_in.shape[0] - 1
        x = _combine(segtab, ys, x1, rw, g_final, apply_norm=last_layer).reshape(B, S, D)
    return x
```

```python
import functools
import math

import numpy as np
import jax
import jax.numpy as jnp
from jax import lax
from jax.experimental import pallas as pl
from jax.experimental.pallas import tpu as pltpu

F32 = jnp.float32
BF16 = jnp.bfloat16

NSA_HEADS = 8
NSA_GROUPS = 2
HPG = NSA_HEADS // NSA_GROUPS
DH = 64
CMP_BLOCK = 32
CMP_STRIDE = 16
CMP_HIDDEN = 128
SLC_BLOCK = 64
SLC_TOPN = 16
WINDOW = 512
GLA_HEADS = 4
GLA_DK = 64
GLA_DV = 128
GLA_RANK = 16
GLA_TAU = 16.0
GLA_CHUNK = 64
REL_BUCKETS = 32
REL_MAX_EXACT = REL_BUCKETS // 2
REL_MAX_DIST = 128
N_GROUPS = 4
EPG = 8
N_EXPERTS = N_GROUPS * EPG
EXPERT_FF = 256
EPS = 1e-6

LANES = 128
SUBLANES = 8
VMEM_LIMIT = 56 * 1024 * 1024

LOG2E = math.log2(math.e)
NEG = -1e30
BIG = float(2.0 ** 100)
QT = 128
SLC_CHUNK_TILES = 2
KV_PAD = WINDOW
SLC_OFF = KV_PAD - (SLC_CHUNK_TILES - 1) * QT
WIN_FAR = WINDOW - QT
TE = 512
MOE_TM = 512
VT_ROWS = DH + 16
NSA_TILES = 2


def _nt(a, b):
    return lax.dot_general(a, b, (((1,), (1,)), ((), ())), preferred_element_type=F32)


def _nn(a, b):
    return jnp.dot(a, b, preferred_element_type=F32)


def _split3(x):
    a = x.astype(BF16)
    r = x - a.astype(F32)
    b = r.astype(BF16)
    c = (r - b.astype(F32)).astype(BF16)
    return a, b, c


def _t5_bucket_np(rel):
    n = np.maximum(rel, 0)
    nf = np.maximum(n, 1).astype(np.float32)
    large = REL_MAX_EXACT + (np.log(nf / np.float32(REL_MAX_EXACT)) / np.float32(math.log(REL_MAX_DIST / REL_MAX_EXACT))
                             * np.float32(REL_BUCKETS - REL_MAX_EXACT)).astype(np.int32)
    return np.where(n < REL_MAX_EXACT, n, np.minimum(large, REL_BUCKETS - 1)).astype(np.int32)


def _inproj_tile(x_ref, g_ref, wq_ref, wkv_ref, wmisc_ref, wb_ref, wrest_ref, kplace_ref, *refs, tm, cast_groups,
                 seq_tile):
    ncast = sum(cast_groups)
    cast_in, refs = refs[:ncast], refs[ncast:]
    (oq, okslc, okwin, okv, omisc, oqkb, ovb, orb, omg, ovts, ovtw, ovtb), cast_out = refs[:12], refs[12:]
    first = 0
    for dst, n in zip(cast_out, cast_groups):
        col = 0
        for src in cast_in[first:first + n]:
            dst[:, :, col:col + src.shape[-1]] = src[...].astype(BF16)
            col += src.shape[-1]
        first += n
    x = x_ref[0]
    ms = jnp.mean(x * x, axis=-1, keepdims=True)
    h = (x * lax.rsqrt(ms + EPS) * g_ref[...]).astype(BF16)

    oq[0] = _nt(h, wq_ref[...]).astype(BF16)
    kv = _nt(h, wkv_ref[...])
    okv[0] = kv[:, 0:256]

    def spread(k):
        return _nn(k.astype(BF16), kplace_ref[...])

    row = lax.broadcasted_iota(jnp.int32, (tm, 256), 0) + seq_tile * tm
    lane = lax.broadcasted_iota(jnp.int32, (tm, 256), 1) % LANES
    onehot = jnp.where(lane - DH == row // SLC_BLOCK, 1.0, 0.0)
    okslc[0] = (spread(kv[:, 256:384]) + onehot).astype(BF16)
    okwin[0] = spread(kv[:, 512:640]).astype(BF16)

    ones_rows = jnp.where(lax.broadcasted_iota(jnp.int32, (VT_ROWS - DH, tm), 0) == 0, 1.0, 0.0)

    def vt_groups(v):
        t = v.T
        return jnp.concatenate([t[:DH], ones_rows, t[DH:], ones_rows], axis=0).astype(BF16)

    ovts[0] = vt_groups(kv[:, 384:512])
    ovtw[0] = vt_groups(kv[:, 640:768])
    omisc[0] = _nt(h, wmisc_ref[...])
    qkv = _nt(h, wb_ref[...])
    oqkb[0] = qkv[:, 0:512]
    ovb[0] = qkv[:, 512:1024].astype(BF16)
    ovtb[0] = qkv[:, 512:1024].T.astype(BF16)
    orb[0] = _nt(h, wrest_ref[0:512, :])
    omg[0] = _nt(h, wrest_ref[512:, :])


W_RUNS = (512, 6 * NSA_GROUPS * DH, 3 * NSA_HEADS, 1024, 16, 2560)
W_RUN0 = tuple(int(v) for v in np.cumsum((0,) + W_RUNS))
GATE_ROWS = W_RUNS[2] + W_RUNS[4]


def _load_inproj_weights(wt_hbm, wq, wkv, wmisc, wb, wrest, stage, gates, sem):
    rows = stage.shape[1]
    chunks = []
    for run, dst, dst0, scale in ((0, wq, 0, DH ** -0.5 * LOG2E), (1, wkv, 0, None), (2, gates, 0, None),
                                  (3, wb, 0, None), (4, gates, W_RUNS[2], None), (5, wrest, 0, None)):
        for r in range(0, W_RUNS[run], rows):
            chunks.append((W_RUN0[run] + r, min(rows, W_RUNS[run] - r), dst, dst0 + r, scale))

    def copy(k):
        src0, n = chunks[k][:2]
        return pltpu.make_async_copy(wt_hbm.at[pl.ds(src0, n)], stage.at[k % 2, pl.ds(0, n)], sem.at[k % 2])

    copy(0).start()
    for k, (_, n, dst, dst0, scale) in enumerate(chunks):
        if k + 1 < len(chunks):
            copy(k + 1).start()
        copy(k).wait()
        w = stage[k % 2, 0:n, :]
        if scale is not None:
            w = w * scale
        dst[dst0:dst0 + n, :] = w.astype(dst.dtype)
    g = gates[0:W_RUNS[2], :]
    z4 = jnp.zeros((HPG, g.shape[1]), F32)
    per_group = [g[(br * NSA_GROUPS + grp) * HPG:(br * NSA_GROUPS + grp + 1) * HPG]
                 for grp in range(NSA_GROUPS) for br in range(3)]
    misc = (per_group[0:3] + [z4] + per_group[3:6] + [z4, gates[W_RUNS[2]:GATE_ROWS, :],
            jnp.zeros((LANES - 48, g.shape[1]), F32)])
    wmisc[...] = jnp.concatenate(misc, axis=0).astype(BF16)


def _inproj_kernel(*refs, tm, cast_groups, batch, tiles):
    s = pl.program_id(0)
    ncast = sum(cast_groups)
    x_ref, g_ref, wt_hbm, kplace_ref = refs[:4]
    cast_in = refs[4:4 + ncast]
    outs = refs[4 + ncast:4 + ncast + 12 + len(cast_groups)]
    weights = refs[-8:-3]
    stage, gates, sem = refs[-3:]
    okslc, okwin, ovts, ovtw = outs[1], outs[2], outs[9], outs[10]

    @pl.when(s == 0)
    def _():
        _load_inproj_weights(wt_hbm, *weights, stage, gates, sem)

    @pl.when(s < batch)
    def _():
        lane = lax.broadcasted_iota(jnp.int32, okslc.shape[1:], 1) % LANES
        pad_keys = jnp.where(lane >= DH, 1.0, 0.0).astype(BF16)
        okslc[0] = pad_keys
        okwin[0] = pad_keys
        ovts[0] = jnp.zeros_like(ovts[0])
        ovtw[0] = jnp.zeros_like(ovtw[0])

    @pl.when(s >= batch)
    def _():
        _inproj_tile(x_ref, g_ref, *weights, kplace_ref, *cast_in, *outs, tm=tm, cast_groups=cast_groups,
                     seq_tile=(s - batch) % tiles)


def _inproj(x, g_mix, wt, to_bf16, tm=KV_PAD):
    B, S, D = x.shape
    assert wt.shape == (W_RUN0[-1], D)
    nsb = S // tm
    nsteps = B * nsb
    t_of = lambda s: jnp.maximum(s - B, 0)
    b_of = lambda s: jnp.where(s < B, s, (s - B) // nsb)
    step_block = lambda shp: pl.BlockSpec((shp[0] // nsteps,) + shp[1:], lambda s: (t_of(s), 0, 0))
    cast_in = [a for grp in to_bf16 for a in grp]
    cast_shapes = [grp[0].shape[:2] + (sum(a.shape[2] for a in grp),) for grp in to_bf16]
    src = np.arange(NSA_GROUPS * DH)
    kplace = np.zeros((NSA_GROUPS * DH, NSA_GROUPS * LANES), np.float32)
    kplace[src, (src // DH) * LANES + src % DH] = 1.0
    kplace = jnp.asarray(kplace, BF16)
    widths = [(512, BF16), (256, BF16), (256, BF16), (256, F32), (128, F32), (512, F32), (512, BF16),
              (512, F32), (2048, F32)]
    out_shape = [jax.ShapeDtypeStruct((B, S, n), dt) for n, dt in widths]
    out_specs = [pl.BlockSpec((1, tm, n), lambda s: (t_of(s) // nsb, t_of(s) % nsb, 0)) for n, _ in widths]
    for rows in (NSA_GROUPS * VT_ROWS, NSA_GROUPS * VT_ROWS, 512):
        out_shape.append(jax.ShapeDtypeStruct((B, rows, S), BF16))
        out_specs.append(pl.BlockSpec((1, rows, tm), lambda s: (t_of(s) // nsb, 0, t_of(s) % nsb)))
    pad_or_tile = lambda s: jnp.where(s < B, 0, (s - B) % nsb + 1)
    for k in (1, 2):
        out_shape[k] = jax.ShapeDtypeStruct((B, KV_PAD + S, widths[k][0]), BF16)
        out_specs[k] = pl.BlockSpec((1, tm, widths[k][0]), lambda s: (b_of(s), pad_or_tile(s), 0))
    for k in (9, 10):
        out_shape[k] = jax.ShapeDtypeStruct((B, NSA_GROUPS * VT_ROWS, KV_PAD + S), BF16)
        out_specs[k] = pl.BlockSpec((1, NSA_GROUPS * VT_ROWS, tm), lambda s: (b_of(s), 0, pad_or_tile(s)))
    return pl.pallas_call(
        functools.partial(_inproj_kernel, tm=tm, cast_groups=tuple(len(grp) for grp in to_bf16), batch=B, tiles=nsb),
        out_shape=out_shape + [jax.ShapeDtypeStruct(shp, BF16) for shp in cast_shapes],
        grid=(B + nsteps,),
        in_specs=[
            pl.BlockSpec((1, tm, D), lambda s: (t_of(s) // nsb, t_of(s) % nsb, 0)),
            pl.BlockSpec((1, D), lambda s: (0, 0)),
            pl.BlockSpec(memory_space=pl.ANY),
            pl.BlockSpec(kplace.shape, lambda s: (0, 0)),
        ] + [step_block(a.shape) for a in cast_in],
        out_specs=out_specs + [step_block(shp) for shp in cast_shapes],
        scratch_shapes=[pltpu.VMEM((W_RUNS[0], D), BF16), pltpu.VMEM((W_RUNS[1], D), BF16),
                        pltpu.VMEM((LANES, D), BF16), pltpu.VMEM((W_RUNS[3], D), BF16),
                        pltpu.VMEM((W_RUNS[5], D), BF16), pltpu.VMEM((2, tm, D), F32),
                        pltpu.VMEM((GATE_ROWS, D), F32), pltpu.SemaphoreType.DMA((2,))],
        compiler_params=pltpu.CompilerParams(dimension_semantics=("arbitrary",),
                                             vmem_limit_bytes=VMEM_LIMIT),
        name="inproj",
    )(x, g_mix.reshape(1, D), wt, kplace, *cast_in)


def _gelu_tanh(x):
    return 0.5 * x * (1.0 + jnp.tanh(math.sqrt(2.0 / math.pi) * (x + 0.044715 * (x * x * x))))


def _compress_kernel(xk_ref, xv_ref, pe_ref, w1_ref, w2k_ref, w2vt_ref, ok_ref, ovt_ref, *, nsub):
    for kind, x_ref in enumerate((xk_ref, xv_ref)):
        top = jnp.zeros((nsub, 2 * CMP_HIDDEN), F32)
        bot = jnp.zeros((nsub, 2 * CMP_HIDDEN), F32)
        for r in range(CMP_STRIDE):
            xr = x_ref[0, pl.ds(r, nsub, stride=CMP_STRIDE), :]
            top = top + _nn((xr + pe_ref[kind, 0, r:r + 1, :]).astype(BF16), w1_ref[kind, 0, r])
            bot = bot + _nn((xr + pe_ref[kind, 1, r:r + 1, :]).astype(BF16), w1_ref[kind, 1, r])
        hid = _gelu_tanh(top + pltpu.roll(bot, shift=nsub - 1, axis=0)).astype(BF16)
        for g in range(NSA_GROUPS):
            hg = hid[:, g * CMP_HIDDEN:(g + 1) * CMP_HIDDEN]
            if kind == 0:
                ok_ref[0, g] = _nn(hg, w2k_ref[...]).astype(BF16)
            else:
                ovt_ref[0, g] = _nt(w2vt_ref[...], hg).astype(BF16)


def _compress(kv_cmp, pe_k, w1k, w2k, pe_v, w1v, w2v):
    B, S, _ = kv_cmp.shape
    nsub = S // CMP_STRIDE
    G = NSA_GROUPS

    def prep(pe, w1):
        pe_t = jnp.tile(pe.reshape(2, CMP_STRIDE, DH), (1, 1, G))
        a = w1.reshape(2, CMP_STRIDE, DH, CMP_HIDDEN)
        z = jnp.zeros_like(a)
        w = jnp.concatenate([jnp.concatenate([a, z], axis=3), jnp.concatenate([z, a], axis=3)], axis=2)
        return pe_t, w.astype(BF16)

    pek, w1kb = prep(pe_k, w1k)
    pev, w1vb = prep(pe_v, w1v)
    pe = jnp.stack([pek, pev])
    w1 = jnp.stack([w1kb, w1vb])
    w2kp = jnp.concatenate([w2k, jnp.zeros_like(w2k)], axis=1).astype(BF16)
    w2vt = w2v.T.astype(BF16)
    full = lambda shp: pl.BlockSpec(shp, lambda b: (0,) * len(shp))
    return pl.pallas_call(
        functools.partial(_compress_kernel, nsub=nsub),
        out_shape=[jax.ShapeDtypeStruct((B, G, nsub, LANES), BF16),
                   jax.ShapeDtypeStruct((B, G, DH, nsub), BF16)],
        grid=(B,),
        in_specs=[pl.BlockSpec((1, S, G * DH), lambda b: (b, 0, 0)), pl.BlockSpec((1, S, G * DH), lambda b: (b, 0, 1)),
                  full(pe.shape), full(w1.shape), full((CMP_HIDDEN, LANES)), full((DH, CMP_HIDDEN))],
        out_specs=[pl.BlockSpec((1, G, nsub, LANES), lambda b: (b, 0, 0, 0)),
                   pl.BlockSpec((1, G, DH, nsub), lambda b: (b, 0, 0, 0))],
        compiler_params=pltpu.CompilerParams(dimension_semantics=("parallel",), vmem_limit_bytes=VMEM_LIMIT),
        name="compress",
    )(kv_cmp, kv_cmp, pe, w1, w2kp, w2vt)


def _bias_kernel(tbl_ref, bkn_ref, bkc_ref, near_ref, cmpb_ref):
    g = pl.program_id(0)
    for h in range(HPG):
        hd = g * HPG + h

        def lookup(bk):
            acc = jnp.full(bk.shape, NEG, F32)
            for b in range(REL_BUCKETS):
                acc = jnp.where(bk == b, tbl_ref[hd, b], acc)
            return acc

        vn = lookup(bkn_ref[...])
        near_ref[0, :, h * QT:(h + 1) * QT] = jnp.where(vn > 0.5 * NEG, (vn - tbl_ref[hd, REL_BUCKETS - 1]) * LOG2E, NEG)
        vc = lookup(bkc_ref[...])
        cmpb_ref[0, :, h * QT:(h + 1) * QT] = jnp.where(vc > 0.5 * NEG, vc * LOG2E, NEG)


def _nsa_bias_tables(rel_bias, seq):
    ql = np.arange(QT)
    ncmp = seq // CMP_STRIDE

    def buckets(rel):
        return jnp.asarray(np.where(rel >= 0, _t5_bucket_np(rel), -1).astype(np.int32))

    nk = SLC_CHUNK_TILES * QT
    bkn = buckets(ql[None, :] + nk - QT - np.arange(nk)[:, None])
    y = np.arange(2 * ncmp)
    bkc = buckets(ql[None, :] - CMP_STRIDE * (y[:, None] - ncmp) - (CMP_BLOCK - 1))
    nql = HPG * QT
    return pl.pallas_call(
        _bias_kernel,
        out_shape=[jax.ShapeDtypeStruct((NSA_GROUPS, nk, nql), F32),
                   jax.ShapeDtypeStruct((NSA_GROUPS, 2 * ncmp, nql), F32)],
        grid=(NSA_GROUPS,),
        in_specs=[pl.BlockSpec(memory_space=pltpu.SMEM),
                  pl.BlockSpec((nk, QT), lambda g: (0, 0)),
                  pl.BlockSpec((2 * ncmp, QT), lambda g: (0, 0))],
        out_specs=[pl.BlockSpec((1, nk, nql), lambda g: (g, 0, 0)),
                   pl.BlockSpec((1, 2 * ncmp, nql), lambda g: (g, 0, 0))],
        compiler_params=pltpu.CompilerParams(dimension_semantics=("parallel",)),
        name="t5bias",
    )(rel_bias.T, bkn, bkc)


def _nsa_kernel(q_ref, kslc_ref, vtslc_ref, kwin_ref, vtwin_ref, kcb_ref, vcbt_ref, misc_ref,
                near_ref, cmpb_ref, wmask_ref, ovt_ref, eye_ref, eye4_ref,
                out_ref, qaug_s, s0_s, s1_s, *, ncmp, nslc):
    G = NSA_GROUPS
    qts = [NSA_TILES * pl.program_id(1) + t for t in range(NSA_TILES)]
    chains = [(t, g) for t in range(NSA_TILES) for g in range(G)]
    nql = HPG * QT
    ck = SLC_CHUNK_TILES * QT
    wk = 2 * QT

    def flash(carry, s, vt_chunk):
        m, acc = carry
        m_new = jnp.maximum(m, jnp.max(s, axis=0, keepdims=True))
        alpha = jnp.exp2(m - m_new)
        p = jnp.exp2((s - m_new).astype(BF16))
        return m_new, alpha * acc + _nn(vt_chunk, p)

    def finish(carry):
        m, acc = carry
        return acc[:DH] * (1.0 / acc[DH:DH + 1])

    init = (jnp.full((1, nql), NEG, F32), jnp.zeros((VT_ROWS, nql), F32))
    halves = [slice(k * nql // 2, (k + 1) * nql // 2) for k in range(2)]
    init_halves = tuple((init[0][:, hs], init[1][:, hs]) for hs in halves)
    ns = [pl.multiple_of(QT * qt, QT) for qt in qts]
    nw = [pl.multiple_of(QT * qt + WIN_FAR, QT) for qt in qts]
    off = [pl.multiple_of(ncmp - (QT // CMP_STRIDE) * qt, SUBLANES) for qt in qts]
    lane = lax.broadcasted_iota(jnp.int32, (nql, LANES), 1)
    win_aug = jnp.where(lane >= DH, -BIG, 0.0).astype(BF16)
    jidx = lax.broadcasted_iota(jnp.int32, (nslc, QT), 0)
    tq = [qt * QT + lax.broadcasted_iota(jnp.int32, (nslc, QT), 1) for qt in qts]
    forced = [(jidx == 0) | (jidx == tq[t] // SLC_BLOCK) | (jidx == tq[t] // SLC_BLOCK - 1) for t in range(NSA_TILES)]
    future = [jidx * SLC_BLOCK > tq[t] for t in range(NSA_TILES)]
    sub = lax.broadcasted_iota(jnp.int32, (SUBLANES, QT), 0)
    ones_lo = jnp.ones((DH, QT), F32)
    eye = eye_ref[...]
    ovt = ovt_ref[...]
    ngrp = nslc // SUBLANES

    def aug(sel01):
        rows = [ones_lo, sel01]
        if LANES - DH - nslc:
            rows.append(jnp.ones((LANES - DH - nslc, QT), F32))
        m01 = _nt(eye, jnp.concatenate(rows, axis=0).astype(BF16))
        return jnp.concatenate([((m01 - 1.0) * BIG).astype(BF16)] * HPG, axis=0)

    cs = range(len(chains))
    kl = [slice(g * LANES, (g + 1) * LANES) for _, g in chains]
    vr = [slice(g * VT_ROWS, (g + 1) * VT_ROWS) for _, g in chains]
    low = lax.broadcasted_iota(jnp.int32, (QT, LANES), 1) < DH

    def head_slot(t, hd):
        pair = q_ref[0, t * QT:(t + 1) * QT, (hd // 2) * LANES:(hd // 2 + 1) * LANES].astype(F32)
        if hd % 2:
            pair = pltpu.roll(pair, shift=DH, axis=1)
        return jnp.where(low, pair, 0.0).astype(BF16)

    q0 = [jnp.concatenate([head_slot(t, g * HPG + h) for h in range(HPG)], axis=0) for t, g in chains]
    qwin = [q0[c] + win_aug for c in cs]

    bc = [cmpb_ref[g, pl.ds(off[t], ncmp), :] for t, g in chains]
    sc = [_nt(kcb_ref[0, chains[c][1]], q0[c]) + bc[c] for c in cs]
    sw = [_nt(kwin_ref[0, pl.ds(ns[chains[c][0]], WIN_FAR), kl[c]], qwin[c]) for c in cs]
    sw = [jnp.concatenate([sw[c][:QT] + wmask_ref[...], sw[c][QT:]], axis=0) for c in cs]

    ecb, rden = [], []
    for c in cs:
        mc = jnp.maximum(jnp.max(sc[c], axis=0, keepdims=True), 0.5 * NEG)
        ec = jnp.exp2(sc[c] - mc)
        rden.append(1.0 / jnp.maximum(jnp.sum(ec, axis=0, keepdims=True), jnp.finfo(F32).tiny))
        ecb.append(ec.astype(BF16))
    wcar = [[flash(init_halves[k], sw[c][:, halves[k]], vtwin_ref[0, vr[c], pl.ds(ns[chains[c][0]], WIN_FAR)])
             for k in range(2)] for c in cs]
    o_c = [_nn(vcbt_ref[0, chains[c][1]], ecb[c]) * rden[c] for c in cs]

    imp = []
    for c in cs:
        t = chains[c][0]
        v = _nn(ovt, ecb[c][:, 0:QT]) * rden[c][:, 0:QT]
        for h in range(1, HPG):
            v = v + _nn(ovt, ecb[c][:, h * QT:(h + 1) * QT]) * rden[c][:, h * QT:(h + 1) * QT]
        imp.append(jnp.where(forced[t], 1e30, jnp.where(future[t], -1e30, v)))
    sw = [_nt(kwin_ref[0, pl.ds(nw[chains[c][0]], wk), kl[c]], qwin[c]) + near_ref[chains[c][1], ck - wk:, :]
          for c in cs]

    grp = [[imp[c][SUBLANES * v:SUBLANES * (v + 1)] for v in range(ngrp)] for c in cs]
    cnt = [[jnp.zeros((SUBLANES, QT), F32) for _ in range(ngrp)] for c in cs]
    for jp in range(nslc):
        v0, r0 = divmod(jp, SUBLANES)
        for c in cs:
            row = jnp.broadcast_to(imp[c][jp:jp + 1, :], (SUBLANES, QT))
            for v in range(ngrp):
                if v < v0:
                    inc = jnp.where(row > grp[c][v], 1.0, 0.0)
                elif v > v0:
                    inc = jnp.where(row >= grp[c][v], 1.0, 0.0)
                else:
                    inc = jnp.where(sub > r0, jnp.where(row >= grp[c][v], 1.0, 0.0),
                                    jnp.where(row > grp[c][v], 1.0, 0.0))
                cnt[c][v] = cnt[c][v] + inc
    o_w = [jnp.concatenate([finish(flash(wcar[c][k], sw[c][:, halves[k]], vtwin_ref[0, vr[c], pl.ds(nw[chains[c][0]], wk)]))
                            for k in range(2)], axis=1) for c in cs]
    for c in cs:
        sel = jnp.concatenate(cnt[c], axis=0) < float(min(SLC_TOPN, nslc))
        sel_near = jnp.where(sel, 1.0, 0.0)
        first_near = (QT // SLC_BLOCK) * (qts[chains[c][0]] - (SLC_CHUNK_TILES - 1))
        sel_far = jnp.where(jidx < first_near, sel_near, 0.0)
        qaug_s[c, 0] = q0[c] + aug(sel_far)
        qaug_s[c, 1] = q0[c] + aug(sel_near)

    gts = [jax.nn.sigmoid(misc_ref[0, t * QT:(t + 1) * QT, :]).T for t in range(NSA_TILES)]
    gates = [[jnp.concatenate([gts[t][16 * g + br * HPG + h:16 * g + br * HPG + h + 1, :] for h in range(HPG)], axis=1)
              for br in range(3)] for t, g in chains]

    n_far = (NSA_TILES * pl.program_id(1)) // SLC_CHUNK_TILES
    n_chunks = n_far + 1

    def rows(c, i):
        return pl.multiple_of(QT * qts[chains[c][0]] - ck * i + SLC_OFF, QT)

    def scores(c, i):
        i = jnp.minimum(i, n_far)
        return _nt(kslc_ref[0, pl.ds(rows(c, i), ck), kl[c]], qaug_s[c, jnp.where(i == 0, 1, 0)])

    def vt(c, i):
        return vtslc_ref[0, vr[c], pl.ds(rows(c, i), ck)]

    for c in cs:
        s0_s[c] = scores(c, 0) + near_ref[chains[c][1]]

    def flash_halves(carry_c, s_ref, c, vt_chunk):
        return tuple(flash(carry_c[k], s_ref[c, :, halves[k]], vt_chunk) for k in range(2))

    def pair_body(p, carry):
        i = 2 * p
        carry = list(carry)
        for t in range(NSA_TILES):
            tc = [c for c in cs if chains[c][0] == t]
            for c in tc:
                s1_s[c] = scores(c, i + 1)
            for c in tc:
                carry[c] = flash_halves(carry[c], s0_s, c, vt(c, i))
            for c in tc:
                s0_s[c] = scores(c, i + 2)
            for c in tc:
                carry[c] = flash_halves(carry[c], s1_s, c, vt(c, i + 1))
        return tuple(carry)

    carry = lax.fori_loop(0, n_chunks // 2, pair_body, (init_halves,) * len(chains))
    carry = lax.cond(n_chunks % 2 == 1,
                     lambda cr: tuple(flash_halves(cr[c], s0_s, c, vt(c, n_far)) for c in cs),
                     lambda cr: cr, carry)
    carry = [tuple(jnp.concatenate([h[k] for h in carry[c]], axis=1) for k in range(2)) for c in cs]

    head = lax.broadcasted_iota(jnp.int32, (DH, nql), 1) // QT
    o_s = [finish(carry[c]) for c in cs]
    o = [gates[c][0] * o_c[c] + gates[c][1] * o_s[c] + gates[c][2] * o_w[c] for c in cs]
    ob = [o[c].astype(BF16) for c in cs]
    blocks = [jnp.concatenate([jnp.where(head == h, ob[c], jnp.zeros_like(ob[c])) for h in range(HPG)], axis=0)
              for c in cs]
    y = [_nt(eye4_ref[...], blocks[c]).astype(BF16) for c in cs]
    for c, (t, g) in enumerate(chains):
        out_ref[0, t * QT:(t + 1) * QT, g * HPG * DH:(g + 1) * HPG * DH] = y[c]


def _nsa(q, kslc, vtslc, kwin, vtwin, kcb, vcbt, misc, rel_bias):
    B, S, _ = q.shape
    G = NSA_GROUPS
    nq = S // QT
    ncmp = S // CMP_STRIDE
    nslc = S // SLC_BLOCK
    nql = HPG * QT
    near, cmpb = _nsa_bias_tables(rel_bias, S)
    wmask = jnp.asarray(np.tile(np.where(np.arange(QT)[:, None] > np.arange(QT)[None, :], 0.0, NEG), (1, HPG)), F32)
    ci = np.arange(ncmp)[None, :] * CMP_STRIDE
    sj = np.arange(nslc)[:, None] * SLC_BLOCK
    ovt = jnp.asarray(((ci < sj + SLC_BLOCK) & (ci + CMP_BLOCK > sj)).astype(np.float32), BF16)
    eye = jnp.eye(QT, dtype=BF16)
    eye4 = jnp.tile(eye, (1, HPG))
    kern = functools.partial(_nsa_kernel, ncmp=ncmp, nslc=nslc)
    nch = NSA_TILES * G
    per_b = lambda shp: pl.BlockSpec(shp, lambda b, i: (b,) + (0,) * (len(shp) - 1))
    full = lambda shp: pl.BlockSpec(shp, lambda b, i: (0,) * len(shp))
    return pl.pallas_call(
        kern,
        out_shape=jax.ShapeDtypeStruct((B, S, NSA_HEADS * DH), BF16),
        grid=(B, nq // NSA_TILES),
        in_specs=[
            pl.BlockSpec((1, NSA_TILES * QT, NSA_HEADS * DH), lambda b, i: (b, i, 0)),
            per_b((1, KV_PAD + S, G * LANES)),
            per_b((1, G * VT_ROWS, KV_PAD + S)),
            per_b((1, KV_PAD + S, G * LANES)),
            per_b((1, G * VT_ROWS, KV_PAD + S)),
            per_b((1, G, ncmp, LANES)),
            per_b((1, G, DH, ncmp)),
            pl.BlockSpec((1, NSA_TILES * QT, LANES), lambda b, i: (b, i, 0)),
            full((G, SLC_CHUNK_TILES * QT, nql)),
            full((G, 2 * ncmp, nql)),
            full((QT, nql)),
            full((nslc, ncmp)),
            full((QT, QT)),
            full((QT, nql)),
        ],
        out_specs=pl.BlockSpec((1, NSA_TILES * QT, NSA_HEADS * DH), lambda b, i: (b, i, 0)),
        scratch_shapes=[pltpu.VMEM((nch, 2, nql, LANES), BF16), pltpu.VMEM((nch, SLC_CHUNK_TILES * QT, nql), F32),
                        pltpu.VMEM((nch, SLC_CHUNK_TILES * QT, nql), F32)],
        compiler_params=pltpu.CompilerParams(dimension_semantics=("parallel", "arbitrary"),
                                             vmem_limit_bytes=VMEM_LIMIT),
        name="nsa",
    )(q, kslc, vtslc, kwin, vtwin, kcb, vcbt, misc, near, cmpb, wmask, ovt, eye, eye4)


GLA_NB = 2


def _gla_kernel(qk_ref, v_ref, vt_ref, misc_ref, r_ref, wal_ref, bal_ref, ng_ref, cum_ref, out_ref,
                state_s, o_s, *, ct, nbatch):
    H, dk, dv, C = GLA_HEADS, GLA_DK, GLA_DV, GLA_CHUNK
    kw = H * dk
    nb = range(nbatch)

    @pl.when(pl.program_id(1) == 0)
    def _():
        state_s[...] = jnp.zeros_like(state_s)

    cum = cum_ref[...]
    q_in, k_in, k_st, decay = [], [], [], []
    for bb in nb:
        z = _nn(misc_ref[bb].astype(BF16), wal_ref[...]) + bal_ref[...]
        log_a = (jnp.minimum(z, 0.0) - jnp.log(1.0 + jnp.exp(-jnp.abs(z)))) * (1.0 / GLA_TAU)
        a1, a2, a3 = _split3(log_a)
        cs = _nn(cum, a1) + _nn(cum, a2) + _nn(cum, a3)
        bc, bl = cs[:ct], cs[ct:]
        q = qk_ref[bb, :, :kw]
        k = qk_ref[bb, :, kw:]
        q_in.append((q * (dk ** -0.5)) * jnp.exp(bc))
        k_in.append((k * jnp.exp(-bc)).astype(BF16))
        k_st.append(k * jnp.exp(bl - bc))
        decay.append(jnp.exp(bl))
    lane_head = lax.broadcasted_iota(jnp.int32, (C, kw), 1) // dk
    rr = lax.broadcasted_iota(jnp.int32, (H * C, C), 0) % C
    cc = lax.broadcasted_iota(jnp.int32, (H * C, C), 1)
    causal = rr >= cc
    pair_row = lax.broadcasted_iota(jnp.int32, (2 * C, kw), 0) // C
    pair_head = lax.broadcasted_iota(jnp.int32, (2 * C, kw), 1) // dk

    for c in range(ct // C):
        r0 = c * C
        p0 = (c // 2) * 2 * C
        for bb in nb:
            qc = q_in[bb][r0:r0 + C]
            qcb = qc.astype(BF16)
            q_heads = jnp.concatenate([jnp.where(lane_head == h, qc, 0.0) for h in range(H)], axis=0).astype(BF16)
            attn = jnp.where(causal, _nt(q_heads, k_in[bb][r0:r0 + C]), 0.0).astype(BF16)
            kst_pair = k_st[bb][p0:p0 + 2 * C]
            dec = decay[bb][r0:r0 + 1]
            for h in range(H):
                st = state_s[bb, h]
                o = _nn(attn[h * C:(h + 1) * C], v_ref[bb, r0:r0 + C, h * dv:(h + 1) * dv])
                o = o + _nt(qcb, st.astype(BF16))
                o_s[bb, r0:r0 + C, h * dv:(h + 1) * dv] = o
                kst_h = jnp.where((pair_row == c % 2) & (pair_head == h), kst_pair, 0.0).astype(BF16)
                state_s[bb, h] = st * dec + _nn(vt_ref[bb, h * dv:(h + 1) * dv, p0:p0 + 2 * C], kst_h)

    for bb in nb:
        for h in range(H):
            oh = o_s[bb, :, h * dv:(h + 1) * dv]
            ms = jnp.mean(oh * oh, axis=-1, keepdims=True)
            r = r_ref[bb, :, h * dv:(h + 1) * dv]
            y = oh * lax.rsqrt(ms + EPS) * ng_ref[:, h * dv:(h + 1) * dv] * (r * jax.nn.sigmoid(r))
            out_ref[bb, :, h * dv:(h + 1) * dv] = y.astype(BF16)


def _gla(qkb, vb, vtb, misc, rb, w_alpha, b_alpha, norm_g, ct=512):
    B, S, _ = qkb.shape
    H, dk, dv, C = GLA_HEADS, GLA_DK, GLA_DV, GLA_CHUNK
    kw, vw = H * dk, H * dv
    nb = GLA_NB if B % GLA_NB == 0 else 1
    wal = jnp.zeros((LANES, kw), F32).at[32:32 + GLA_RANK].set(w_alpha).astype(BF16)
    r = np.arange(ct)
    tri = (r[:, None] // C == r[None, :] // C) & (r[:, None] >= r[None, :])
    tot = r[:, None] // C == r[None, :] // C
    cum = jnp.asarray(np.concatenate([tri, tot], axis=0).astype(np.float32), BF16)
    full = lambda shp: pl.BlockSpec(shp, lambda b, i: (0,) * len(shp))
    return pl.pallas_call(
        functools.partial(_gla_kernel, ct=ct, nbatch=nb),
        out_shape=jax.ShapeDtypeStruct((B, S, vw), BF16),
        grid=(B // nb, S // ct),
        in_specs=[
            pl.BlockSpec((nb, ct, 2 * kw), lambda b, i: (b, i, 0)),
            pl.BlockSpec((nb, ct, vw), lambda b, i: (b, i, 0)),
            pl.BlockSpec((nb, vw, ct), lambda b, i: (b, 0, i)),
            pl.BlockSpec((nb, ct, LANES), lambda b, i: (b, i, 0)),
            pl.BlockSpec((nb, ct, vw), lambda b, i: (b, i, 0)),
            full((LANES, kw)), full((1, kw)), full((1, vw)), full((2 * ct, ct)),
        ],
        out_specs=pl.BlockSpec((nb, ct, vw), lambda b, i: (b, i, 0)),
        scratch_shapes=[pltpu.VMEM((nb, H, dv, kw), F32), pltpu.VMEM((nb, ct, vw), F32)],
        compiler_params=pltpu.CompilerParams(dimension_semantics=("parallel", "arbitrary"),
                                             vmem_limit_bytes=VMEM_LIMIT),
        name="gla",
    )(qkb, vb, vtb, misc, rb, wal, b_alpha.reshape(1, kw), norm_g.reshape(1, vw), cum)


ROUTER_ROWS = 48
EXPERT_ROW0 = 8


SEG_ROWS = 16


def _local_rows(tm):
    return -(-(2 * tm + N_EXPERTS * (SEG_ROWS - 1)) // LANES) * LANES


def _outproj_kernel(ya_ref, yb_ref, mg_ref, x_ref, wa_ref, wb_ref, wo_ref, gf_ref, wr_ref, br_ref,
                    tri_ref, ltri_ref, x1_ref, xsl_ref, seg_ref, rw_ref, cnt_ref, carry_s, *, tm, ls):
    D = x_ref.shape[1]

    @pl.when(pl.program_id(0) == 0)
    def _():
        carry_s[...] = jnp.zeros_like(carry_s)

    ma = _nn(ya_ref[...], wa_ref[...])
    mb = _nn(yb_ref[...], wb_ref[...])
    merged = jax.nn.sigmoid(mg_ref[:, :D]) * ma + jax.nn.sigmoid(mg_ref[:, D:]) * mb
    x1 = x_ref[...] + _nn(merged.astype(BF16), wo_ref[...])
    x1_ref[...] = x1
    ms = jnp.mean(x1 * x1, axis=-1, keepdims=True)
    h2 = x1 * lax.rsqrt(ms + EPS) * gf_ref[...]
    hi = h2.astype(BF16)
    lg = _nt(wr_ref[...], hi) + br_ref[...]
    row8 = lax.broadcasted_iota(jnp.int32, (SUBLANES, tm), 0)
    gl = jnp.where(row8 < N_GROUPS, lg[0:SUBLANES], NEG)
    gmax = jnp.max(gl, axis=0, keepdims=True)
    g_sel = jnp.min(jnp.where(gl == gmax, row8, SUBLANES), axis=0, keepdims=True)
    g_prob = 1.0 / jnp.sum(jnp.where(row8 < N_GROUPS, jnp.exp(gl - gmax), 0.0), axis=0, keepdims=True)
    e_sel = jnp.zeros((EPG, tm), F32)
    for gi in range(N_GROUPS):
        r0 = EXPERT_ROW0 + gi * EPG
        e_sel = e_sel + jnp.where(g_sel == gi, lg[r0:r0 + EPG], 0.0)
    v1 = jnp.max(e_sel, axis=0, keepdims=True)
    i1 = jnp.min(jnp.where(e_sel == v1, row8, EPG), axis=0, keepdims=True)
    rest = jnp.where(row8 == i1, -jnp.inf, e_sel)
    v2 = jnp.max(rest, axis=0, keepdims=True)
    i2 = jnp.min(jnp.where(rest == v2, row8, EPG), axis=0, keepdims=True)
    t = jnp.exp(v2 - v1)
    w1 = g_prob / (1.0 + t)
    w2 = g_prob * t / (1.0 + t)
    e1 = g_sel * EPG + i1
    e2 = g_sel * EPG + i2

    rowe = lax.broadcasted_iota(jnp.int32, (N_EXPERTS, tm), 0)
    oh1 = rowe == e1
    oh2 = rowe == e2
    oh = jnp.where(oh1, 1.0, 0.0) + jnp.where(oh2, 1.0, 0.0)
    pre = _nn(oh.astype(BF16), tri_ref[...])
    cnt = jnp.sum(oh, axis=1, keepdims=True)
    padded = jnp.floor((cnt + (SEG_ROWS - 1)) * (1.0 / SEG_ROWS)) * SEG_ROWS
    c_b = jnp.broadcast_to(padded, (N_EXPERTS, LANES))
    c_hi = jnp.floor(c_b * (1.0 / 16.0))
    c_lo = c_b - 16.0 * c_hi
    base = 16.0 * _nn(ltri_ref[...], c_hi.astype(BF16)) + _nn(ltri_ref[...], c_lo.astype(BF16))
    loc = pre + base[:, 0:1]
    lpos1 = jnp.sum(jnp.where(oh1, loc, 0.0), axis=0, keepdims=True)
    lpos2 = jnp.sum(jnp.where(oh2, loc, 0.0), axis=0, keepdims=True)
    lane = lax.broadcasted_iota(jnp.int32, (N_EXPERTS, LANES), 1)
    seg_ref[0] = jnp.where(lane == 0, base, jnp.where(lane == 1, c_b, carry_s[...])).astype(jnp.int32)
    carry_s[...] = carry_s[...] + c_b
    cnt_ref[...] = carry_s[...].astype(jnp.int32)
    rw_ref[0] = jnp.concatenate([w1, w2, lpos1, lpos2, jnp.zeros((SUBLANES - 4, tm), F32)], axis=0)

    srow = lax.broadcasted_iota(jnp.int32, (ls, tm), 0)
    perm = jnp.where((srow == lpos1.astype(jnp.int32)) | (srow == lpos2.astype(jnp.int32)), 1.0, 0.0)
    xsorted = _nn(perm.astype(BF16), hi)
    xsl_ref[0] = xsorted.astype(BF16)


def _outproj(ya, yb, mg, x, wa, wb, wo, g_ffn, w_rg, b_rg, w_re, b_re, tm=MOE_TM):
    T, D = x.shape
    nt = T // tm
    wr = jnp.zeros((ROUTER_ROWS, D), F32).at[0:N_GROUPS].set(w_rg.T).at[EXPERT_ROW0:EXPERT_ROW0 + N_EXPERTS].set(w_re.T)
    wr = wr.astype(BF16)
    br = jnp.zeros((ROUTER_ROWS, 1), F32).at[0:N_GROUPS, 0].set(b_rg).at[EXPERT_ROW0:EXPERT_ROW0 + N_EXPERTS, 0].set(b_re)
    r = np.arange(tm)
    tri = jnp.asarray((r[:, None] < r[None, :]).astype(np.float32), BF16)
    re = np.arange(N_EXPERTS)
    ltri = jnp.asarray((re[None, :] < re[:, None]).astype(np.float32), BF16)
    ls = _local_rows(tm)
    row = lambda n: pl.BlockSpec((tm, n), lambda i: (i, 0))
    full = lambda shp: pl.BlockSpec(shp, lambda i: (0,) * len(shp))
    return pl.pallas_call(
        functools.partial(_outproj_kernel, tm=tm, ls=ls),
        out_shape=[jax.ShapeDtypeStruct((T, D), F32), jax.ShapeDtypeStruct((nt, ls, D), BF16),
                   jax.ShapeDtypeStruct((nt, N_EXPERTS, LANES), jnp.int32),
                   jax.ShapeDtypeStruct((nt, SUBLANES, tm), F32),
                   jax.ShapeDtypeStruct((N_EXPERTS, LANES), jnp.int32)],
        grid=(nt,),
        in_specs=[row(ya.shape[1]), row(yb.shape[1]), row(2 * D), row(D),
                  full(wa.shape), full(wb.shape), full(wo.shape), full((1, D)),
                  full((ROUTER_ROWS, D)), full((ROUTER_ROWS, 1)), full((tm, tm)),
                  full((N_EXPERTS, N_EXPERTS))],
        out_specs=[row(D), pl.BlockSpec((1, ls, D), lambda i: (i, 0, 0)),
                   pl.BlockSpec((1, N_EXPERTS, LANES), lambda i: (i, 0, 0)),
                   pl.BlockSpec((1, SUBLANES, tm), lambda i: (i, 0, 0)),
                   full((N_EXPERTS, LANES))],
        scratch_shapes=[pltpu.VMEM((N_EXPERTS, LANES), F32)],
        compiler_params=pltpu.CompilerParams(dimension_semantics=("arbitrary",), vmem_limit_bytes=VMEM_LIMIT),
        name="outproj",
    )(ya, yb, mg, x, wa, wb, wo, g_ffn.reshape(1, D), wr, br, tri, ltri)


SEG_FIELDS = 4
SEG_BITS = (MOE_TM // SEG_ROWS).bit_length()
TILE_BITS = (_local_rows(MOE_TM) // SEG_ROWS).bit_length()
TAIL_BITS = (TE // SEG_ROWS).bit_length() - 1


def _segment_copies(units, bits, make_copy, wait):
    def arms(lo, hi, off):
        for bit in reversed(range(lo, hi)):
            rows = SEG_ROWS << bit
            take = (units >> bit) & 1

            @pl.when(take == 1)
            def _(off=off, rows=rows):
                cp = make_copy(off, rows)
                if wait:
                    cp.wait()
                else:
                    cp.start()

            off = off + take * rows

    low = min(2, bits)
    if bits > low:
        @pl.when((units >> low) != 0)
        def _():
            arms(low, bits, 0)

    arms(0, low, ((units >> low) << low) * SEG_ROWS)


def _wait_rows(units, sem, like_src, like_dst):
    def make_copy(off, rows):
        return pltpu.make_async_copy(like_src.at[pl.ds(0, rows), :], like_dst.at[pl.ds(0, rows), :], sem)

    _segment_copies(units, TILE_BITS, make_copy, True)


def _dispatch_kernel(seg_ref, tail_ref, xsl_ref, xs_ref, zero_s, sem):
    i = pl.program_id(0)

    def segments(wait):
        def body(e, c):
            s0 = (i * N_EXPERTS + e) * SEG_FIELDS
            src0 = pl.multiple_of(seg_ref[s0], SEG_ROWS)
            dst0 = pl.multiple_of(seg_ref[s0 + 2], SEG_ROWS)

            def make_copy(off, rows):
                return pltpu.make_async_copy(xsl_ref.at[0, pl.ds(pl.multiple_of(src0 + off, SEG_ROWS), rows), :],
                                             xs_ref.at[pl.ds(pl.multiple_of(dst0 + off, SEG_ROWS), rows), :], sem)

            _segment_copies(seg_ref[s0 + 1], SEG_BITS, make_copy, wait)
            return c

        lax.fori_loop(0, N_EXPERTS, body, 0)

    def tails(wait):
        def body(e, c):
            dst0 = pl.multiple_of(tail_ref[2 * e], SEG_ROWS)

            def make_copy(off, rows):
                return pltpu.make_async_copy(zero_s.at[pl.ds(0, rows), :],
                                             xs_ref.at[pl.ds(pl.multiple_of(dst0 + off, SEG_ROWS), rows), :], sem)

            _segment_copies(tail_ref[2 * e + 1], TAIL_BITS, make_copy, wait)
            return c

        lax.fori_loop(0, N_EXPERTS, body, 0)

    def unused_tiles(wait):
        def body(t, c):
            cp = pltpu.make_async_copy(zero_s, xs_ref.at[pl.ds(pl.multiple_of(t * TE, TE), TE), :], sem)
            if wait:
                cp.wait()
            else:
                cp.start()
            return c

        lax.fori_loop(tail_ref[2 * N_EXPERTS], xs_ref.shape[0] // TE, body, 0)

    @pl.when(i == 0)
    def _():
        zero_s[...] = jnp.zeros_like(zero_s)
        tails(False)
        unused_tiles(False)
        tails(True)
        unused_tiles(True)

    segments(False)
    _wait_rows(seg_ref[i * N_EXPERTS * SEG_FIELDS + 3], sem, xsl_ref.at[0], xs_ref)


def _dispatch(seg, tail, xsl, n_rows):
    nt, ls, W = xsl.shape
    return pl.pallas_call(
        _dispatch_kernel,
        out_shape=jax.ShapeDtypeStruct((n_rows, W), xsl.dtype),
        grid_spec=pltpu.PrefetchScalarGridSpec(
            num_scalar_prefetch=2,
            grid=(nt,),
            in_specs=[pl.BlockSpec((1, ls, W), lambda i, sg, tl: (i, 0, 0))],
            out_specs=pl.BlockSpec(memory_space=pl.ANY),
            scratch_shapes=[pltpu.VMEM((TE, W), xsl.dtype), pltpu.SemaphoreType.DMA(())],
        ),
        compiler_params=pltpu.CompilerParams(dimension_semantics=("arbitrary",)),
        name="dispatch",
    )(seg, tail, xsl)


def _experts_kernel(te_ref, nv_ref, xs_ref, wgu_ref, wd_ref, out_ref):
    @pl.when(pl.program_id(0) < nv_ref[0])
    def _():
        au = _nn(xs_ref[...], wgu_ref[0])
        a, u = au[:, :EXPERT_FF], au[:, EXPERT_FF:]
        hid = (a * jax.nn.sigmoid(a)) * u
        out_ref[...] = _nn(hid.astype(BF16), wd_ref[0]).astype(BF16)


def _experts(tile_expert, n_valid, xs, w_gate_up, w_down):
    n_rows, D = xs.shape
    n_tiles = n_rows // TE
    last = lambda i, nv: jnp.minimum(i, nv[0] - 1)
    return pl.pallas_call(
        _experts_kernel,
        out_shape=jax.ShapeDtypeStruct((n_rows, D), BF16),
        input_output_aliases={2: 0},
        grid_spec=pltpu.PrefetchScalarGridSpec(
            num_scalar_prefetch=2,
            grid=(n_tiles,),
            in_specs=[pl.BlockSpec((TE, D), lambda i, te, nv: (last(i, nv), 0)),
                      pl.BlockSpec((1, D, 2 * EXPERT_FF), lambda i, te, nv: (te[last(i, nv)], 0, 0)),
                      pl.BlockSpec((1, EXPERT_FF, D), lambda i, te, nv: (te[last(i, nv)], 0, 0))],
            out_specs=pl.BlockSpec((TE, D), lambda i, te, nv: (last(i, nv), 0)),
        ),
        compiler_params=pltpu.CompilerParams(dimension_semantics=("arbitrary",), vmem_limit_bytes=VMEM_LIMIT),
        name="experts",
    )(tile_expert, n_valid, xs, w_gate_up, w_down)


def _combine_kernel(seg_ref, ys_ref, x1_ref, rw_ref, gfin_ref, out_ref, buf, sem, *, tm, ls, apply_norm):
    i = pl.program_id(0)
    nt = pl.num_programs(0)

    def segments(tile, wait):
        slot = tile % 2

        def body(e, c):
            s0 = (tile * N_EXPERTS + e) * SEG_FIELDS
            loc0 = pl.multiple_of(seg_ref[s0], SEG_ROWS)
            glob0 = pl.multiple_of(seg_ref[s0 + 2], SEG_ROWS)

            def make_copy(off, rows):
                return pltpu.make_async_copy(ys_ref.at[pl.ds(pl.multiple_of(glob0 + off, SEG_ROWS), rows), :],
                                             buf.at[slot, pl.ds(pl.multiple_of(loc0 + off, SEG_ROWS), rows), :],
                                             sem.at[slot])

            _segment_copies(seg_ref[s0 + 1], SEG_BITS, make_copy, wait)
            return c

        lax.fori_loop(0, N_EXPERTS, body, 0)

    @pl.when(i == 0)
    def _():
        buf[...] = jnp.zeros_like(buf)
        segments(i, False)

    @pl.when(i + 1 < nt)
    def _():
        segments(i + 1, False)

    _wait_rows(seg_ref[i * N_EXPERTS * SEG_FIELDS + 3], sem.at[i % 2], ys_ref, buf.at[i % 2])
    ysl = buf[i % 2]
    rw = rw_ref[0]
    pos = rw[2:4].astype(jnp.int32)
    rows_t = lax.broadcasted_iota(jnp.int32, (ls, tm), 0)
    wrow = jnp.sum(jnp.where(rows_t == pos[0:1], rw[0:1], jnp.where(rows_t == pos[1:2], rw[1:2], 0.0)),
                   axis=1, keepdims=True)
    ysw = (ysl.astype(F32) * wrow).astype(BF16)
    cols = jnp.concatenate([rw, jnp.zeros((LANES - SUBLANES, tm), F32)], axis=0).T.astype(jnp.int32)
    srow = lax.broadcasted_iota(jnp.int32, (tm, ls), 1)
    pick = jnp.where((srow == cols[:, 2:3]) | (srow == cols[:, 3:4]), 1.0, 0.0).astype(BF16)
    y = x1_ref[...] + _nn(pick, ysw)
    if apply_norm:
        ms = jnp.mean(y * y, axis=-1, keepdims=True)
        y = y * lax.rsqrt(ms + EPS) * gfin_ref[...]
    out_ref[...] = y


def _combine(seg, ys, x1, rw, g_final, apply_norm, tm=MOE_TM):
    T, D = x1.shape
    nt = T // tm
    ls = _local_rows(tm)
    return pl.pallas_call(
        functools.partial(_combine_kernel, tm=tm, ls=ls, apply_norm=apply_norm),
        out_shape=jax.ShapeDtypeStruct((T, D), F32),
        grid_spec=pltpu.PrefetchScalarGridSpec(
            num_scalar_prefetch=1,
            grid=(nt,),
            in_specs=[pl.BlockSpec(memory_space=pl.ANY),
                      pl.BlockSpec((tm, D), lambda i, sg: (i, 0)),
                      pl.BlockSpec((1, SUBLANES, tm), lambda i, sg: (i, 0, 0)),
                      pl.BlockSpec((1, D), lambda i, sg: (0, 0))],
            out_specs=pl.BlockSpec((tm, D), lambda i, sg: (i, 0)),
            scratch_shapes=[pltpu.VMEM((2, ls, D), BF16), pltpu.SemaphoreType.DMA((2,))],
        ),
        compiler_params=pltpu.CompilerParams(dimension_semantics=("arbitrary",), vmem_limit_bytes=VMEM_LIMIT),
        name="combine",
    )(seg, ys, x1, rw, g_final.reshape(1, D))


def _moe_plan(seg, counts, T, tm=MOE_TM):
    nt = T // tm
    n_tiles_max = (2 * T + nt * N_EXPERTS * (SEG_ROWS - 1)) // TE + N_EXPERTS
    total = counts[:, 0]
    tiles = (total + TE - 1) // TE
    ids = jnp.arange(N_EXPERTS)
    tile_end = jnp.sum(jnp.where(ids[None, :] <= ids[:, None], tiles[None, :], 0), axis=1)
    row0 = (tile_end - tiles) * TE
    units = seg[:, :, 1] // SEG_ROWS
    segtab = jnp.stack([seg[:, :, 0], units, seg[:, :, 2] + row0[None, :],
                        jnp.broadcast_to(jnp.sum(units, axis=1, keepdims=True), units.shape)],
                       axis=-1).reshape(-1).astype(jnp.int32)
    tail = jnp.concatenate([jnp.stack([row0 + total, (tiles * TE - total) // SEG_ROWS], axis=-1).reshape(-1),
                            tile_end[-1:]]).astype(jnp.int32)
    tile_expert = jnp.minimum(jnp.sum(tile_end[None, :] <= jnp.arange(n_tiles_max)[:, None], axis=1),
                              N_EXPERTS - 1).astype(jnp.int32)
    return segtab, tail, tile_expert, tile_end[-1:].astype(jnp.int32), n_tiles_max * TE


def kernel(x, g_mix, w_in, nsa_pe_k, nsa_cmp_k_w1, nsa_cmp_k_w2, nsa_pe_v, nsa_cmp_v_w1, nsa_cmp_v_w2, rel_bias,
           gla_w_alpha, gla_b_alpha, gla_norm_g, w_branch_a, w_branch_b, w_out, g_ffn, w_router_group,
           b_router_group, w_router_expert, b_router_expert, w_exp_gate, w_exp_up, w_exp_down, g_final):
    B, S, D = x.shape
    T = B * S
    for l in range(w_in.shape[0]):
        (q, kslc, kwin, kvc, misc, qkb, vb, rb, mg, vts, vtw, vtb, wgu16, wd16) = _inproj(
            x, g_mix[l], w_in[l].T, [(w_exp_gate[l], w_exp_up[l]), (w_exp_down[l],)])
        kcb, vcbt = _compress(kvc, nsa_pe_k[l], nsa_cmp_k_w1[l], nsa_cmp_k_w2[l],
                              nsa_pe_v[l], nsa_cmp_v_w1[l], nsa_cmp_v_w2[l])
        ya = _nsa(q, kslc, vts, kwin, vtw, kcb, vcbt, misc, rel_bias)
        yb = _gla(qkb, vb, vtb, misc, rb, gla_w_alpha[l], gla_b_alpha[l], gla_norm_g[l])
        x1, xsl, seg, rw, counts = _outproj(
            ya.reshape(T, -1), yb.reshape(T, -1), mg.reshape(T, -1), x.reshape(T, D),
            w_branch_a[l].astype(BF16), w_branch_b[l].astype(BF16), w_out[l].astype(BF16), g_ffn[l],
            w_router_group[l], b_router_group[l], w_router_expert[l], b_router_expert[l])
        segtab, tail, tile_expert, n_valid, n_rows = _moe_plan(seg, counts, T)
        xs = _dispatch(segtab, tail, xsl, n_rows)
        ys = _experts(tile_expert, n_valid, xs, wgu16, wd16)
        last_layer = l == w_in.shape[0] - 1
        x = _combine(segtab, ys, x1, rw, g_final, apply_norm=last_layer).reshape(B, S, D)
    return x
```
